```python
import math
import jax
import jax.numpy as jnp
from jax import lax
import numpy as np

D_MODEL = 1024
BATCH = 8
SEQ = 4096
DEPTH = 4

HEAD_DIM = 64
A_HEADS = D_MODEL // 4 // HEAD_DIM
A_QK_DIM = HEAD_DIM // 2
B_HEADS = D_MODEL // 2 // HEAD_DIM
C_HEADS = D_MODEL // 4 // HEAD_DIM
C_PAIRS = ((128, 1), (512, 4), (2048, 16))
C_GROUPS = len(C_PAIRS)
A_W = A_HEADS * HEAD_DIM
B_W = B_HEADS * HEAD_DIM
C_W = C_HEADS * HEAD_DIM
MIX_W = A_W + B_W + C_W
A_OFF = 0
B_OFF = A_OFF + 3 * A_W
F_OFF = B_OFF + 3 * B_W
C_OFF = F_OFF + B_HEADS
N_IN = C_OFF + 3 * C_GROUPS * C_W
D_FF = 4 * D_MODEL
N_BUCKETS = 32
MAX_DISTANCE = 2048
N_BIAS_HEADS = A_HEADS + C_GROUPS * C_HEADS
Q_BLOCK = 128
NORM_EPS = 1e-6
NEG_INF = -1e30

kernel_name = "hybrid_diff_fox_dilated_trunk"


def rmsnorm(x, g):
    xf = x.astype(jnp.float32)
    y = xf * lax.rsqrt(jnp.mean(xf * xf, axis=-1, keepdims=True) + NORM_EPS)
    return (y * g.astype(jnp.float32)).astype(x.dtype)


def t5_bucket(dist):
    max_exact = N_BUCKETS // 2
    d = jnp.maximum(dist, 0)
    df = jnp.maximum(d, 1).astype(jnp.float32)
    large = max_exact + (jnp.log(df / max_exact) / math.log(MAX_DISTANCE / max_exact)
                         * (N_BUCKETS - max_exact)).astype(jnp.int32)
    large = jnp.minimum(large, N_BUCKETS - 1)
    return jnp.where(d < max_exact, d, large)


def diff_attention(q, k, v, lam, lam_init, sub_g, bias_a):
    bsz, seq = q.shape[0], q.shape[1]
    scale = A_QK_DIM ** -0.5
    kpos = jnp.arange(seq)

    def block(i):
        q0 = i * Q_BLOCK
        qb = lax.dynamic_slice_in_dim(q, q0, Q_BLOCK, axis=1)
        dist = (q0 + jnp.arange(Q_BLOCK))[:, None] - kpos[None, :]
        bias = jnp.transpose(bias_a[t5_bucket(dist)], (2, 0, 1)).astype(jnp.float32)
        logits = jnp.einsum('bqhcd,bkhcd->bchqk', qb, k).astype(jnp.float32) * scale + bias
        logits = jnp.where(dist >= 0, logits, NEG_INF)
        p = jax.nn.softmax(logits, axis=-1)
        a = p[:, 0] - lam * p[:, 1]
        return jnp.einsum('bhqk,bkhd->bqhd', a.astype(v.dtype), v)

    out = lax.map(block, jnp.arange(seq // Q_BLOCK))
    out = jnp.moveaxis(out, 0, 1).reshape(bsz, seq, A_HEADS, HEAD_DIM)
    return rmsnorm(out, sub_g) * (1.0 - lam_init)


def forgetting_attention(q, k, v, f_logit):
    bsz, seq = q.shape[0], q.shape[1]
    scale = HEAD_DIM ** -0.5
    log_f = jax.nn.log_sigmoid(f_logit.astype(jnp.float32))
    cum = jnp.transpose(jnp.cumsum(log_f, axis=1), (0, 2, 1))
    kpos = jnp.arange(seq)

    def block(i):
        q0 = i * Q_BLOCK
        qb = lax.dynamic_slice_in_dim(q, q0, Q_BLOCK, axis=1)
        cq = lax.dynamic_slice_in_dim(cum, q0, Q_BLOCK, axis=2)
        causal = (q0 + jnp.arange(Q_BLOCK))[:, None] >= kpos[None, :]
        decay = cq[:, :, :, None] - cum[:, :, None, :]
        logits = jnp.einsum('bqhd,bkhd->bhqk', qb, k).astype(jnp.float32) * scale + decay
        logits = jnp.where(causal, logits, NEG_INF)
        p = jax.nn.softmax(logits, axis=-1)
        return jnp.einsum('bhqk,bkhd->bqhd', p.astype(v.dtype), v)

    out = lax.map(block, jnp.arange(seq // Q_BLOCK))
    return jnp.moveaxis(out, 0, 1).reshape(bsz, seq, B_HEADS, HEAD_DIM)


def dilated_attention(q, k, v, bias_c):
    bsz, seq = q.shape[0], q.shape[1]
    scale = HEAD_DIM ** -0.5
    qs = [q[:, :, g] for g in range(C_GROUPS)]
    ks = [k[:, :, g] for g in range(C_GROUPS)]
    vs = [v[:, :, g] for g in range(C_GROUPS)]

    def block(i):
        q0 = i * Q_BLOCK
        qpos = q0 + jnp.arange(Q_BLOCK)
        outs, lses = [], []
        for g, (window, dil) in enumerate(C_PAIRS):
            n_keys = window // dil + 1
            dist = jnp.arange(n_keys) * dil
            kidx = qpos[:, None] - dist[None, :]
            valid = kidx >= 0
            kidx = jnp.maximum(kidx, 0)
            kg = jnp.take(ks[g], kidx, axis=1)
            vg = jnp.take(vs[g], kidx, axis=1)
            qg = lax.dynamic_slice_in_dim(qs[g], q0, Q_BLOCK, axis=1)
            bias = bias_c[t5_bucket(dist), g * C_HEADS:(g + 1) * C_HEADS]
            logits = (jnp.einsum('bqhd,bqkhd->bhqk', qg, kg).astype(jnp.float32) * scale
                      + jnp.transpose(bias).astype(jnp.float32)[None, :, None, :])
            logits = jnp.where(valid[None, None], logits, NEG_INF)
            m = jnp.max(logits, axis=-1, keepdims=True)
            e = jnp.exp(logits - m)
            s = jnp.sum(e, axis=-1, keepdims=True)
            outs.append(jnp.einsum('bhqk,bqkhd->bqhd', (e / s).astype(vg.dtype), vg))
            lses.append((m + jnp.log(s))[..., 0])
        alpha = jax.nn.softmax(jnp.stack(lses, axis=0), axis=0)
        alpha = jnp.transpose(alpha, (0, 1, 3, 2))[..., None]
        return jnp.sum(alpha * jnp.stack(outs, axis=0).astype(jnp.float32), axis=0)

    out = lax.map(block, jnp.arange(seq // Q_BLOCK))
    return jnp.moveaxis(out, 0, 1).reshape(bsz, seq, C_HEADS, HEAD_DIM).astype(q.dtype)


def setup_inputs(seed: int = 0) -> dict:
    key = jax.random.key(seed)
    ks = jax.random.split(key, 16)
    L, D = DEPTH, D_MODEL

    def nrm(k, shape, s):
        return jax.random.normal(k, shape, jnp.float32) * s

    return {
        'x': nrm(ks[0], (BATCH, SEQ, D), 1.0),
        'norm1_g': 1.0 + nrm(ks[1], (L, D), 0.02),
        'w_in': nrm(ks[2], (L, D, N_IN), D ** -0.5),
        'b_f': 2.0 + nrm(ks[3], (L, B_HEADS), 0.1),
        'lam_q1': nrm(ks[4], (L, A_QK_DIM), 0.1),
        'lam_k1': nrm(ks[5], (L, A_QK_DIM), 0.1),
        'lam_q2': nrm(ks[6], (L, A_QK_DIM), 0.1),
        'lam_k2': nrm(ks[7], (L, A_QK_DIM), 0.1),
        'diff_norm_g': 1.0 + nrm(ks[8], (L, HEAD_DIM), 0.02),
        'w_o': nrm(ks[9], (L, MIX_W, D), MIX_W ** -0.5),
        'norm2_g': 1.0 + nrm(ks[10], (L, D), 0.02),
        'w_1': nrm(ks[11], (L, D, D_FF), D ** -0.5),
        'w_2': nrm(ks[12], (L, D_FF, D), D_FF ** -0.5),
        'rel_bias': nrm(ks[13], (N_BUCKETS, N_BIAS_HEADS), 0.5),
        'final_g': 1.0 + nrm(ks[14], (D,), 0.02),
    }


def reference(x, norm1_g, w_in, b_f, lam_q1, lam_k1, lam_q2, lam_k2, diff_norm_g,
              w_o, norm2_g, w_1, w_2, rel_bias, final_g):
    bsz, seq = x.shape[0], x.shape[1]
    bias_a = rel_bias[:, :A_HEADS]
    bias_c = rel_bias[:, A_HEADS:]
    for l in range(DEPTH):
        h = rmsnorm(x, norm1_g[l])
        proj = jnp.einsum('bsd,dn->bsn', h, w_in[l])

        qa = proj[..., A_OFF:A_OFF + A_W].reshape(bsz, seq, A_HEADS, 2, A_QK_DIM)
        ka = proj[..., A_OFF + A_W:A_OFF + 2 * A_W].reshape(bsz, seq, A_HEADS, 2, A_QK_DIM)
        va = proj[..., A_OFF + 2 * A_W:B_OFF].reshape(bsz, seq, A_HEADS, HEAD_DIM)
        lam_init = 0.8 - 0.6 * math.exp(-0.3 * l)
        lam = (jnp.exp(jnp.sum(lam_q1[l].astype(jnp.float32) * lam_k1[l].astype(jnp.float32)))
               - jnp.exp(jnp.sum(lam_q2[l].astype(jnp.float32) * lam_k2[l].astype(jnp.float32)))
               + lam_init)
        out_a = diff_attention(qa, ka, va, lam, lam_init, diff_norm_g[l], bias_a)

        qb = proj[..., B_OFF:B_OFF + B_W].reshape(bsz, seq, B_HEADS, HEAD_DIM)
        kb = proj[..., B_OFF + B_W:B_OFF + 2 * B_W].reshape(bsz, seq, B_HEADS, HEAD_DIM)
        vb = proj[..., B_OFF + 2 * B_W:F_OFF].reshape(bsz, seq, B_HEADS, HEAD_DIM)
        f_logit = proj[..., F_OFF:C_OFF] + b_f[l]
        out_b = forgetting_attention(qb, kb, vb, f_logit)

        qc = proj[..., C_OFF:C_OFF + C_GROUPS * C_W].reshape(bsz, seq, C_GROUPS, C_HEADS, HEAD_DIM)
        kc = proj[..., C_OFF + C_GROUPS * C_W:C_OFF + 2 * C_GROUPS * C_W].reshape(bsz, seq, C_GROUPS, C_HEADS, HEAD_DIM)
        vc = proj[..., C_OFF + 2 * C_GROUPS * C_W:N_IN].reshape(bsz, seq, C_GROUPS, C_HEADS, HEAD_DIM)
        out_c = dilated_attention(qc, kc, vc, bias_c)

        mixed = jnp.concatenate([out_a.reshape(bsz, seq, A_W).astype(x.dtype),
                                 out_b.reshape(bsz, seq, B_W).astype(x.dtype),
                                 out_c.reshape(bsz, seq, C_W).astype(x.dtype)], axis=-1)
        x = x + jnp.einsum('bsm,md->bsd', mixed, w_o[l])

        h2 = rmsnorm(x, norm2_g[l])
        u = jnp.square(jax.nn.relu(jnp.einsum('bsd,df->bsf', h2, w_1[l])))
        x = x + jnp.einsum('bsf,fd->bsd', u, w_2[l])
    return rmsnorm(x, final_g)
```

```python
import functools
import math

import jax
import jax.numpy as jnp
import numpy as np
from jax import lax
from jax.experimental import pallas as pl
from jax.experimental.pallas import tpu as pltpu

D_MODEL = 1024
HEAD_DIM = 64
A_HEADS = 4
A_QK_DIM = 32
B_HEADS = 8
C_HEADS = 4
C_PAIRS = ((128, 1), (512, 4), (2048, 16))
C_GROUPS = 3
A_W = 256
B_W = 512
C_W = 256
D_FF = 4096
N_BUCKETS = 32
MAX_DISTANCE = 2048
NORM_EPS = 1e-6
NEG_INF = -1e30

LANES = 128
VMEM_LIMIT = 56 * 1024 * 1024

QA, KA, VA = 0, 256, 512
QB, KB, VB = 768, 1280, 1792
QC, KC, VC = 2304, 3072, 3840
N_MAIN = 4608
F_ROWS = 16

C_TILE = 128
FLASH_TILE = 256


def _bucket_thresholds():
    max_exact = N_BUCKETS // 2
    out = []
    for k in range(1, N_BUCKETS - max_exact):
        t = max_exact * (MAX_DISTANCE / max_exact) ** (k / (N_BUCKETS - max_exact))
        out.append(int(math.ceil(t)))
    return tuple(out)


_THRESHOLDS = _bucket_thresholds()


def _t5_bucket(d):
    big = jnp.full(d.shape, N_BUCKETS // 2, jnp.int32)
    for t in _THRESHOLDS:
        big = big + (d >= t).astype(jnp.int32)
    return jnp.where(d < N_BUCKETS // 2, d, big)


def _bias_lookup(rb_ref, bucket, col):
    val = jnp.zeros(bucket.shape, jnp.float32)
    for b in range(N_BUCKETS):
        val = jnp.where(bucket == b, rb_ref[b, col], val)
    return val


def _bias_a_kernel(rb_ref, out_ref, *, tile):
    delta = pl.program_id(0)
    i = lax.broadcasted_iota(jnp.int32, (tile, tile), 0)
    j = lax.broadcasted_iota(jnp.int32, (tile, tile), 1)
    d = delta * tile + i - j
    bucket = _t5_bucket(d)
    for h in range(A_HEADS):
        val = _bias_lookup(rb_ref, bucket, h)
        out_ref[h, 0] = jnp.where(d >= 0, val, NEG_INF)


def _bias_c_kernel(rb_ref, out_ref):
    g = pl.program_id(0)
    dil = jnp.where(g == 0, C_PAIRS[0][1], jnp.where(g == 1, C_PAIRS[1][1], C_PAIRS[2][1]))
    i = lax.broadcasted_iota(jnp.int32, (C_TILE, 2 * C_TILE), 0)
    c = lax.broadcasted_iota(jnp.int32, (C_TILE, 2 * C_TILE), 1)
    steps = i + C_TILE - c
    valid = (steps >= 0) & (steps <= C_TILE)
    bucket = _t5_bucket(steps * dil)
    for h in range(C_HEADS):
        val = _bias_lookup(rb_ref, bucket, A_HEADS + g * C_HEADS + h)
        out_ref[h, 1] = jnp.where(valid, val, NEG_INF)
        out_ref[h, 0] = jnp.where(valid & (c >= C_TILE), val, NEG_INF)


def _make_bias_tiles(rel_bias, seq):
    tile = FLASH_TILE
    n_delta = seq // tile
    smem = pl.BlockSpec(memory_space=pltpu.SMEM)
    bias_a = pl.pallas_call(
        functools.partial(_bias_a_kernel, tile=tile),
        out_shape=jax.ShapeDtypeStruct((A_HEADS, n_delta, tile, tile), jnp.float32),
        grid=(n_delta,),
        in_specs=[smem],
        out_specs=pl.BlockSpec((A_HEADS, 1, tile, tile), lambda d: (0, d, 0, 0)),
        name="bias_a_tiles",
    )(rel_bias)
    bias_c = pl.pallas_call(
        _bias_c_kernel,
        out_shape=jax.ShapeDtypeStruct((C_GROUPS * C_HEADS, 2, C_TILE, 2 * C_TILE), jnp.float32),
        grid=(C_GROUPS,),
        in_specs=[smem],
        out_specs=pl.BlockSpec((C_HEADS, 2, C_TILE, 2 * C_TILE), lambda g: (g, 0, 0, 0)),
        name="bias_c_tiles",
    )(rel_bias)
    return bias_a, bias_c


def _rms(x, g):
    return x * lax.rsqrt(jnp.mean(x * x, axis=-1, keepdims=True) + NORM_EPS) * g


def _in_proj_kernel(x_ref, g_ref, w_ref, wf_ref, proj_ref, ft_ref, *, chunk):
    h = _rms(x_ref[...], g_ref[0]).astype(jnp.bfloat16)
    for c in range(N_MAIN // chunk):
        cols = slice(c * chunk, (c + 1) * chunk)
        proj_ref[:, cols] = jnp.dot(h, w_ref[0, :, cols],
                                    preferred_element_type=jnp.float32).astype(jnp.bfloat16)
    ft_ref[...] = lax.dot_general(wf_ref[0], h, (((1,), (1,)), ((), ())),
                                  preferred_element_type=jnp.float32)


def _in_proj(x2, norm_g, w_main, w_f, layer, *, tm=512):
    n_tok = x2.shape[0]
    return pl.pallas_call(
        functools.partial(_in_proj_kernel, chunk=512),
        out_shape=(jax.ShapeDtypeStruct((n_tok, N_MAIN), jnp.bfloat16),
                   jax.ShapeDtypeStruct((F_ROWS, n_tok), jnp.float32)),
        grid=(n_tok // tm,),
        in_specs=[
            pl.BlockSpec((tm, D_MODEL), lambda i: (i, 0)),
            pl.BlockSpec((1, 1, D_MODEL), lambda i: (layer, 0, 0)),
            pl.BlockSpec((1, D_MODEL, N_MAIN), lambda i: (layer, 0, 0)),
            pl.BlockSpec((1, F_ROWS, D_MODEL), lambda i: (layer, 0, 0)),
        ],
        out_specs=(pl.BlockSpec((tm, N_MAIN), lambda i: (i, 0)),
                   pl.BlockSpec((F_ROWS, tm), lambda i: (0, i))),
        compiler_params=pltpu.CompilerParams(dimension_semantics=("parallel",),
                                             vmem_limit_bytes=VMEM_LIMIT),
        name="in_proj",
    )(x2, norm_g, w_main, w_f)


def _decay_kernel(ft_ref, bf_ref, cum_ref):
    z = ft_ref[0:B_HEADS, :] + bf_ref[0]
    x = jnp.minimum(z, 0.0) - jnp.log1p(jnp.exp(-jnp.abs(z)))
    seq = x.shape[1]
    pos = lax.broadcasted_iota(jnp.int32, x.shape, 1)
    shift = 1
    while shift < seq:
        x = x + jnp.where(pos >= shift, pltpu.roll(x, shift, 1), 0.0)
        shift *= 2
    cum_ref[0] = x


def _decay(ft, b_f, layer, bsz, seq):
    return pl.pallas_call(
        _decay_kernel,
        out_shape=jax.ShapeDtypeStruct((bsz, B_HEADS, seq), jnp.float32),
        grid=(bsz,),
        in_specs=[pl.BlockSpec((F_ROWS, seq), lambda b: (0, b)),
                  pl.BlockSpec((1, B_HEADS, 1), lambda b: (layer, 0, 0))],
        out_specs=pl.BlockSpec((1, B_HEADS, seq), lambda b: (b, 0, 0)),
        compiler_params=pltpu.CompilerParams(dimension_semantics=("parallel",)),
        name="decay_scan",
    )(ft, b_f)


def _dot_nt(a, b):
    return lax.dot_general(a, b, (((1,), (1,)), ((), ())), preferred_element_type=jnp.float32)


def _flash_body(q_ref, k_ref, v_ref, add_fn, n_sub, tile, m_ref, l_ref, acc_ref, mask_diag):
    qi = pl.program_id(2)
    width = LANES // n_sub
    q = q_ref[0]
    lane = lax.broadcasted_iota(jnp.int32, (1, LANES), 1)
    qs = [jnp.where((lane >= s * width) & (lane < (s + 1) * width), q, jnp.zeros_like(q))
          for s in range(n_sub)]
    m_ref[...] = jnp.full(m_ref.shape, NEG_INF, jnp.float32)
    l_ref[...] = jnp.zeros(l_ref.shape, jnp.float32)
    acc_ref[...] = jnp.zeros(acc_ref.shape, jnp.float32)

    def step(ki, masked):
        k = k_ref[0, pl.ds(pl.multiple_of(ki * tile, tile), tile), :]
        v = v_ref[0, pl.ds(pl.multiple_of(ki * tile, tile), tile), :]
        if masked:
            row = lax.broadcasted_iota(jnp.int32, (tile, tile), 0)
            col = lax.broadcasted_iota(jnp.int32, (tile, tile), 1)
            keep = row >= col
        for s in range(n_sub):
            sc = _dot_nt(qs[s], k) + add_fn(s, ki)
            if masked:
                sc = jnp.where(keep, sc, NEG_INF)
            m_old = m_ref[s]
            m_new = jnp.maximum(m_old, jnp.max(sc, axis=1, keepdims=True))
            alpha = jnp.exp(m_old - m_new)
            p = jnp.exp(sc - m_new)
            l_ref[s] = alpha * l_ref[s] + jnp.sum(p, axis=1, keepdims=True)
            acc_ref[s] = alpha * acc_ref[s] + jnp.dot(p.astype(jnp.bfloat16), v,
                                                      preferred_element_type=jnp.float32)
            m_ref[s] = m_new

    def body(ki, carry):
        step(ki, False)
        return carry

    lax.fori_loop(0, qi, body, 0)
    step(qi, mask_diag)


def _attn_a_kernel(q_ref, k_ref, v_ref, bias_ref, lamv_ref, lami_ref, subg_ref, o_ref,
                   m_ref, l_ref, acc_ref, *, tile):
    qi = pl.program_id(2)

    def add_fn(s, ki):
        return bias_ref[s // 2, qi - ki]

    _flash_body(q_ref, k_ref, v_ref, add_fn, 4, tile, m_ref, l_ref, acc_ref, False)

    lam_init = lami_ref[0]
    lv = lamv_ref[0]
    e1 = jnp.exp(jnp.sum(lv[0:1] * lv[1:2], axis=1, keepdims=True))
    e2 = jnp.exp(jnp.sum(lv[2:3] * lv[3:4], axis=1, keepdims=True))
    lam = e1 - e2 + lam_init
    lane = lax.broadcasted_iota(jnp.int32, (1, LANES), 1)
    first = lane < HEAD_DIM
    o0 = acc_ref[0] / l_ref[0] - lam * (acc_ref[1] / l_ref[1])
    o1 = acc_ref[2] / l_ref[2] - lam * (acc_ref[3] / l_ref[3])
    o = jnp.where(first, o0, o1)
    sq = o * o
    s0 = jnp.sum(jnp.where(first, sq, 0.0), axis=1, keepdims=True)
    s1 = jnp.sum(jnp.where(first, 0.0, sq), axis=1, keepdims=True)
    ms = jnp.where(first, s0, s1) * (1.0 / HEAD_DIM)
    y = o * lax.rsqrt(ms + NORM_EPS) * subg_ref[0] * (1.0 - lam_init)
    o_ref[0] = y.astype(o_ref.dtype)


def _attn_a(proj3, bias_a, lam_vecs, lam_init, sub_g2, layer):
    bsz, seq, _ = proj3.shape
    tile = FLASH_TILE
    n_delta = seq // tile
    cb = lambda off: off // LANES
    return pl.pallas_call(
        functools.partial(_attn_a_kernel, tile=tile),
        out_shape=jax.ShapeDtypeStruct((bsz, seq, A_W), jnp.bfloat16),
        grid=(A_HEADS // 2, bsz, seq // tile),
        in_specs=[
            pl.BlockSpec((1, tile, LANES), lambda hp, b, qi: (b, qi, cb(QA) + hp)),
            pl.BlockSpec((1, seq, LANES), lambda hp, b, qi: (b, 0, cb(KA) + hp)),
            pl.BlockSpec((1, seq, LANES), lambda hp, b, qi: (b, 0, cb(VA) + hp)),
            pl.BlockSpec((2, n_delta, tile, tile), lambda hp, b, qi: (hp, 0, 0, 0)),
            pl.BlockSpec((1, 4, A_QK_DIM), lambda hp, b, qi: (layer, 0, 0)),
            pl.BlockSpec(memory_space=pltpu.SMEM),
            pl.BlockSpec((1, 1, LANES), lambda hp, b, qi: (layer, 0, 0)),
        ],
        out_specs=pl.BlockSpec((1, tile, LANES), lambda hp, b, qi: (b, qi, hp)),
        scratch_shapes=[pltpu.VMEM((4, tile, 1), jnp.float32),
                        pltpu.VMEM((4, tile, 1), jnp.float32),
                        pltpu.VMEM((4, tile, LANES), jnp.float32)],
        compiler_params=pltpu.CompilerParams(
            dimension_semantics=("parallel", "parallel", "parallel"),
            vmem_limit_bytes=VMEM_LIMIT),
        name="attn_a",
    )(proj3, proj3, proj3, bias_a, lam_vecs, lam_init, sub_g2)


def _attn_b_kernel(q_ref, k_ref, v_ref, cum_ref, o_ref, m_ref, l_ref, acc_ref, *, tile):
    def add_fn(s, ki):
        return -cum_ref[0, 0, s:s + 1, pl.ds(pl.multiple_of(ki * tile, tile), tile)]

    _flash_body(q_ref, k_ref, v_ref, add_fn, 2, tile, m_ref, l_ref, acc_ref, True)
    lane = lax.broadcasted_iota(jnp.int32, (1, LANES), 1)
    o = jnp.where(lane < HEAD_DIM, acc_ref[0] / l_ref[0], acc_ref[1] / l_ref[1])
    o_ref[0] = o.astype(o_ref.dtype)


def _attn_b(proj3, cum4):
    bsz, seq, _ = proj3.shape
    tile = FLASH_TILE
    cb = lambda off: off // LANES
    return pl.pallas_call(
        functools.partial(_attn_b_kernel, tile=tile),
        out_shape=jax.ShapeDtypeStruct((bsz, seq, B_W), jnp.bfloat16),
        grid=(B_HEADS // 2, bsz, seq // tile),
        in_specs=[
            pl.BlockSpec((1, tile, LANES), lambda hp, b, qi: (b, qi, cb(QB) + hp)),
            pl.BlockSpec((1, seq, LANES), lambda hp, b, qi: (b, 0, cb(KB) + hp)),
            pl.BlockSpec((1, seq, LANES), lambda hp, b, qi: (b, 0, cb(VB) + hp)),
            pl.BlockSpec((1, 1, 2, seq), lambda hp, b, qi: (b, hp, 0, 0)),
        ],
        out_specs=pl.BlockSpec((1, tile, LANES), lambda hp, b, qi: (b, qi, hp)),
        scratch_shapes=[pltpu.VMEM((2, tile, 1), jnp.float32),
                        pltpu.VMEM((2, tile, 1), jnp.float32),
                        pltpu.VMEM((2, tile, LANES), jnp.float32)],
        compiler_params=pltpu.CompilerParams(
            dimension_semantics=("parallel", "parallel", "parallel"),
            vmem_limit_bytes=VMEM_LIMIT),
        name="attn_b",
    )(proj3, proj3, proj3, cum4)


def _attn_c_kernel(q_ref, k_ref, v_ref, bias_ref, o_ref, lse_ref, *, n_tiles):
    dil = q_ref.shape[1]
    lane = lax.broadcasted_iota(jnp.int32, (1, LANES), 1)
    first = lane < HEAD_DIM

    def body(it, carry):
        r = it // n_tiles
        t = it % n_tiles
        row0 = pl.multiple_of(t * C_TILE, C_TILE)
        ver = jnp.minimum(t, 1)
        for hp in range(2):
            cols = slice(hp * LANES, (hp + 1) * LANES)
            q = q_ref[0, r, pl.ds(row0, C_TILE), cols]
            kw = k_ref[0, r, pl.ds(row0, 2 * C_TILE), cols]
            vw = v_ref[0, r, pl.ds(row0, 2 * C_TILE), cols]
            outs, lses = [], []
            for s in range(2):
                qm = jnp.where(first if s == 0 else jnp.logical_not(first), q, jnp.zeros_like(q))
                sc = _dot_nt(qm, kw) + bias_ref[hp * 2 + s, ver]
                m = jnp.max(sc, axis=1, keepdims=True)
                p = jnp.exp(sc - m)
                l = jnp.sum(p, axis=1, keepdims=True)
                pv = jnp.dot(p.astype(jnp.bfloat16), vw, preferred_element_type=jnp.float32)
                outs.append(pv / l)
                lses.append(m + jnp.log(l))
            o_ref[0, r, pl.ds(row0, C_TILE), cols] = jnp.where(first, outs[0], outs[1])
            lse_ref[0, r, pl.ds(row0, C_TILE), cols] = jnp.where(first, lses[0], lses[1])
        return carry

    lax.fori_loop(0, dil * n_tiles, body, 0)


def _attn_c(qd, kd, vd, bias_c, group):
    bsz, dil, length, _ = qd.shape
    out_sds = jax.ShapeDtypeStruct((bsz, dil, length, C_W), jnp.float32)
    in_blk = pl.BlockSpec((1, dil, length, C_W), lambda b: (b, 0, 0, 0))
    kv_blk = pl.BlockSpec((1, dil, length + C_TILE, C_W), lambda b: (b, 0, 0, 0))
    return pl.pallas_call(
        functools.partial(_attn_c_kernel, n_tiles=length // C_TILE),
        out_shape=(out_sds, out_sds),
        grid=(bsz,),
        in_specs=[in_blk, kv_blk, kv_blk,
                  pl.BlockSpec((C_HEADS, 2, C_TILE, 2 * C_TILE), lambda b: (group, 0, 0, 0))],
        out_specs=(in_blk, in_blk),
        compiler_params=pltpu.CompilerParams(dimension_semantics=("parallel",),
                                             vmem_limit_bytes=VMEM_LIMIT),
        name=f"attn_c{group}",
    )(qd, kd, vd, bias_c)


def _deinterleave(x, dil, pad):
    bsz, seq, w = x.shape
    y = jnp.transpose(x.reshape(bsz, seq // dil, dil, w), (0, 2, 1, 3))
    if pad:
        y = jnp.pad(y, ((0, 0), (0, 0), (pad, 0), (0, 0)))
    return y


def _interleave(y):
    bsz, dil, length, w = y.shape
    return jnp.transpose(y, (0, 2, 1, 3)).reshape(bsz, dil * length, w)


def _mix_mlp_kernel(x_ref, oa_ref, ob_ref, oc0_ref, oc1_ref, oc2_ref, ls0_ref, ls1_ref, ls2_ref,
                    wo_ref, g_ref, w1_ref, w2_ref, out_ref, acc_ref, h_ref):
    f = pl.program_id(1)

    @pl.when(f == 0)
    def _():
        l0, l1, l2 = ls0_ref[...], ls1_ref[...], ls2_ref[...]
        m = jnp.maximum(jnp.maximum(l0, l1), l2)
        e0, e1, e2 = jnp.exp(l0 - m), jnp.exp(l1 - m), jnp.exp(l2 - m)
        oc = (e0 * oc0_ref[...] + e1 * oc1_ref[...] + e2 * oc2_ref[...]) / (e0 + e1 + e2)
        wo = wo_ref.at[0]
        y = jnp.dot(oa_ref[...], wo[0:A_W, :], preferred_element_type=jnp.float32)
        y += jnp.dot(ob_ref[...], wo[A_W:A_W + B_W, :], preferred_element_type=jnp.float32)
        y += jnp.dot(oc.astype(jnp.bfloat16), wo[A_W + B_W:, :],
                     preferred_element_type=jnp.float32)
        x1 = x_ref[...] + y
        acc_ref[...] = x1
        h_ref[...] = _rms(x1, g_ref[0]).astype(jnp.bfloat16)

    u = jnp.dot(h_ref[...], w1_ref[0], preferred_element_type=jnp.float32)
    u = jnp.square(jnp.maximum(u, 0.0)).astype(jnp.bfloat16)
    acc_ref[...] += jnp.dot(u, w2_ref[0], preferred_element_type=jnp.float32)

    @pl.when(f == pl.num_programs(1) - 1)
    def _():
        out_ref[...] = acc_ref[...]


def _mix_mlp(x2, oa, ob, ocs, lses, w_o, norm_g, w_1, w_2, layer, *, tm=512, tf=1024):
    n_tok = x2.shape[0]
    row = lambda w: pl.BlockSpec((tm, w), lambda i, f: (i, 0))
    return pl.pallas_call(
        _mix_mlp_kernel,
        out_shape=jax.ShapeDtypeStruct((n_tok, D_MODEL), jnp.float32),
        grid=(n_tok // tm, D_FF // tf),
        in_specs=[row(D_MODEL), row(A_W), row(B_W)] + [row(C_W)] * 6 + [
            pl.BlockSpec((1, D_MODEL, D_MODEL), lambda i, f: (layer, 0, 0)),
            pl.BlockSpec((1, 1, D_MODEL), lambda i, f: (layer, 0, 0)),
            pl.BlockSpec((1, D_MODEL, tf), lambda i, f: (layer, 0, f)),
            pl.BlockSpec((1, tf, D_MODEL), lambda i, f: (layer, f, 0)),
        ],
        out_specs=row(D_MODEL),
        scratch_shapes=[pltpu.VMEM((tm, D_MODEL), jnp.float32),
                        pltpu.VMEM((tm, D_MODEL), jnp.bfloat16)],
        compiler_params=pltpu.CompilerParams(dimension_semantics=("parallel", "arbitrary"),
                                             vmem_limit_bytes=VMEM_LIMIT),
        name="mix_mlp",
    )(x2, oa, ob, *ocs, *lses, w_o, norm_g, w_1, w_2)


def _final_norm_kernel(x_ref, g_ref, o_ref):
    o_ref[...] = _rms(x_ref[...], g_ref[...])


def _final_norm(x2, g, *, tm=1024):
    n_tok = x2.shape[0]
    return pl.pallas_call(
        _final_norm_kernel,
        out_shape=jax.ShapeDtypeStruct(x2.shape, jnp.float32),
        grid=(n_tok // tm,),
        in_specs=[pl.BlockSpec((tm, D_MODEL), lambda i: (i, 0)),
                  pl.BlockSpec((1, D_MODEL), lambda i: (0, 0))],
        out_specs=pl.BlockSpec((tm, D_MODEL), lambda i: (i, 0)),
        compiler_params=pltpu.CompilerParams(dimension_semantics=("parallel",)),
        name="final_norm",
    )(x2, g)


def _prep_w_in(w_in):
    a0, b0, f0, c0 = 0, 3 * A_W, 3 * A_W + 3 * B_W, 3 * A_W + 3 * B_W + B_HEADS
    sa, sb = A_QK_DIM ** -0.5, HEAD_DIM ** -0.5
    cw = C_GROUPS * C_W
    main = jnp.concatenate([
        w_in[:, :, a0:a0 + A_W] * sa, w_in[:, :, a0 + A_W:b0],
        w_in[:, :, b0:b0 + B_W] * sb, w_in[:, :, b0 + B_W:f0],
        w_in[:, :, c0:c0 + cw] * sb, w_in[:, :, c0 + cw:],
    ], axis=-1).astype(jnp.bfloat16)
    w_f = jnp.transpose(w_in[:, :, f0:c0], (0, 2, 1))
    w_f = jnp.pad(w_f, ((0, 0), (0, F_ROWS - B_HEADS), (0, 0))).astype(jnp.bfloat16)
    return main, w_f


def kernel(x, norm1_g, w_in, b_f, lam_q1, lam_k1, lam_q2, lam_k2, diff_norm_g, w_o, norm2_g,
           w_1, w_2, rel_bias, final_g):
    bsz, seq, _ = x.shape
    depth = w_in.shape[0]
    w_main, w_f = _prep_w_in(w_in)
    w_o16, w_116, w_216 = (w.astype(jnp.bfloat16) for w in (w_o, w_1, w_2))
    lam_vecs = jnp.stack([lam_q1, lam_k1, lam_q2, lam_k2], axis=1)
    sub_g2 = jnp.concatenate([diff_norm_g, diff_norm_g], axis=-1)[:, None, :]
    bias_a, bias_c = _make_bias_tiles(rel_bias, seq)

    x2 = x.reshape(bsz * seq, D_MODEL)
    for l in range(depth):
        proj, ft = _in_proj(x2, norm1_g[:, None, :], w_main, w_f, l)
        proj3 = proj.reshape(bsz, seq, N_MAIN)
        cum = _decay(ft, b_f[:, :, None], l, bsz, seq)
        lam_init = jnp.full((1,), 0.8 - 0.6 * math.exp(-0.3 * l), jnp.float32)
        oa = _attn_a(proj3, bias_a, lam_vecs, lam_init, sub_g2, l)
        ob = _attn_b(proj3, cum.reshape(bsz, B_HEADS // 2, 2, seq))
        ocs, lses = [], []
        for g, (_, dil) in enumerate(C_PAIRS):
            sl = lambda off: proj3[:, :, off + g * C_W: off + (g + 1) * C_W]
            o_g, lse_g = _attn_c(_deinterleave(sl(QC), dil, 0), _deinterleave(sl(KC), dil, C_TILE),
                                 _deinterleave(sl(VC), dil, C_TILE), bias_c, g)
            ocs.append(_interleave(o_g).reshape(bsz * seq, C_W))
            lses.append(_interleave(lse_g).reshape(bsz * seq, C_W))
        x2 = _mix_mlp(x2, oa.reshape(bsz * seq, A_W), ob.reshape(bsz * seq, B_W), ocs, lses,
                      w_o16, norm2_g[:, None, :], w_116, w_216, l)
    out = _final_norm(x2, final_g[None, :])
    return out.reshape(bsz, seq, D_MODEL)
```

```python
import functools
import math

import jax
import jax.numpy as jnp
from jax import lax
from jax.experimental import pallas as pl
from jax.experimental.pallas import tpu as pltpu

D_MODEL = 1024
HEAD_DIM = 64
A_HEADS = 4
A_QK_DIM = 32
B_HEADS = 8
C_HEADS = 4
C_PAIRS = ((128, 1), (512, 4), (2048, 16))
C_GROUPS = 3
A_W = 256
B_W = 512
C_W = 256
D_FF = 4096
N_BUCKETS = 32
MAX_DISTANCE = 2048
NORM_EPS = 1e-6
NEG_INF = -1e30
LOG2E = 1.4426950408889634

LANES = 128
VMEM_LIMIT = 56 * 1024 * 1024

KA, KB, QC, KC, VC = 0, 256, 768, 1536, 2304
N_TM = 3072
QA_T, QB_T, VA_T, VB_T = 0, 256, 768, 1024
N_FM = 1536
F_ROWS = 16
ONES_ROWS = 16
ACC_ROWS = HEAD_DIM + ONES_ROWS

C_TILE = 128
FLASH_TILE = 256


def _bucket_thresholds():
    max_exact = N_BUCKETS // 2
    out = []
    for k in range(1, N_BUCKETS - max_exact):
        t = max_exact * (MAX_DISTANCE / max_exact) ** (k / (N_BUCKETS - max_exact))
        out.append(int(math.ceil(t)))
    return tuple(out)


_THRESHOLDS = _bucket_thresholds()


def _n_near_tiles(tile):
    return -(-(_THRESHOLDS[-1] + tile - 1) // tile)


def _t5_bucket(d):
    big = jnp.full(d.shape, N_BUCKETS // 2, jnp.int32)
    for t in _THRESHOLDS:
        big = big + (d >= t).astype(jnp.int32)
    return jnp.where(d < N_BUCKETS // 2, d, big)


def _bias_lookup(rb_ref, bucket, col):
    val = jnp.zeros(bucket.shape, jnp.float32)
    for b in range(N_BUCKETS):
        val = jnp.where(bucket == b, rb_ref[b, col], val)
    return val


def _bias_a_kernel(rb_ref, out_ref, *, tile):
    delta = pl.program_id(0)
    j = lax.broadcasted_iota(jnp.int32, (tile, tile), 0)
    i = lax.broadcasted_iota(jnp.int32, (tile, tile), 1)
    d = delta * tile + i - j
    bucket = _t5_bucket(d)
    for h in range(A_HEADS):
        val = _bias_lookup(rb_ref, bucket, h) * LOG2E
        out_ref[h, 0] = jnp.where(d >= 0, val, NEG_INF)


def _bias_c_kernel(rb_ref, out_ref):
    g = pl.program_id(0)
    dil = jnp.where(g == 0, C_PAIRS[0][1], jnp.where(g == 1, C_PAIRS[1][1], C_PAIRS[2][1]))
    i = lax.broadcasted_iota(jnp.int32, (C_TILE, 2 * C_TILE), 0)
    c = lax.broadcasted_iota(jnp.int32, (C_TILE, 2 * C_TILE), 1)
    steps = i + C_TILE - c
    valid = (steps >= 0) & (steps <= C_TILE)
    bucket = _t5_bucket(steps * dil)
    for h in range(C_HEADS):
        val = _bias_lookup(rb_ref, bucket, A_HEADS + g * C_HEADS + h)
        out_ref[h, 1] = jnp.where(valid, val, NEG_INF)
        out_ref[h, 0] = jnp.where(valid & (c >= C_TILE), val, NEG_INF)


def _make_bias_tiles(rel_bias):
    tile = FLASH_TILE
    n_bias = _n_near_tiles(tile) + 1
    smem = pl.BlockSpec(memory_space=pltpu.SMEM)
    bias_a = pl.pallas_call(
        functools.partial(_bias_a_kernel, tile=tile),
        out_shape=jax.ShapeDtypeStruct((A_HEADS, n_bias, tile, tile), jnp.float32),
        grid=(n_bias,),
        in_specs=[smem],
        out_specs=pl.BlockSpec((A_HEADS, 1, tile, tile), lambda d: (0, d, 0, 0)),
        name="bias_a_tiles",
    )(rel_bias)
    bias_c = pl.pallas_call(
        _bias_c_kernel,
        out_shape=jax.ShapeDtypeStruct((C_GROUPS * C_HEADS, 2, C_TILE, 2 * C_TILE), jnp.float32),
        grid=(C_GROUPS,),
        in_specs=[smem],
        out_specs=pl.BlockSpec((C_HEADS, 2, C_TILE, 2 * C_TILE), lambda g: (g, 0, 0, 0)),
        name="bias_c_tiles",
    )(rel_bias)
    return bias_a, bias_c


def _rms(x, g):
    return x * lax.rsqrt(jnp.mean(x * x, axis=-1, keepdims=True) + NORM_EPS) * g


def _dot_nt(a, b):
    return lax.dot_general(a, b, (((1,), (1,)), ((), ())), preferred_element_type=jnp.float32)


def _in_proj_kernel(x_ref, g_ref, wtm_ref, wfm_ref, wf_ref, tm_ref, fm_ref, ft_ref, *, chunk):
    h = _rms(x_ref[...], g_ref[0]).astype(jnp.bfloat16)
    for c in range(N_TM // chunk):
        cols = slice(c * chunk, (c + 1) * chunk)
        tm_ref[:, cols] = jnp.dot(h, wtm_ref[0, :, cols],
                                  preferred_element_type=jnp.float32).astype(jnp.bfloat16)
    for c in range(N_FM // chunk):
        rows = slice(c * chunk, (c + 1) * chunk)
        fm_ref[rows, :] = _dot_nt(wfm_ref[0, rows, :], h).astype(jnp.bfloat16)
    ft_ref[...] = _dot_nt(wf_ref[0], h)


def _in_proj(x2, norm_g, w_tm, w_fm, w_f, layer, *, tm=512):
    n_tok = x2.shape[0]
    return pl.pallas_call(
        functools.partial(_in_proj_kernel, chunk=512),
        out_shape=(jax.ShapeDtypeStruct((n_tok, N_TM), jnp.bfloat16),
                   jax.ShapeDtypeStruct((N_FM, n_tok), jnp.bfloat16),
                   jax.ShapeDtypeStruct((F_ROWS, n_tok), jnp.float32)),
        grid=(n_tok // tm,),
        in_specs=[
            pl.BlockSpec((tm, D_MODEL), lambda i: (i, 0)),
            pl.BlockSpec((1, 1, D_MODEL), lambda i: (layer, 0, 0)),
            pl.BlockSpec((1, D_MODEL, N_TM), lambda i: (layer, 0, 0)),
            pl.BlockSpec((1, N_FM, D_MODEL), lambda i: (layer, 0, 0)),
            pl.BlockSpec((1, F_ROWS, D_MODEL), lambda i: (layer, 0, 0)),
        ],
        out_specs=(pl.BlockSpec((tm, N_TM), lambda i: (i, 0)),
                   pl.BlockSpec((N_FM, tm), lambda i: (0, i)),
                   pl.BlockSpec((F_ROWS, tm), lambda i: (0, i))),
        compiler_params=pltpu.CompilerParams(dimension_semantics=("parallel",),
                                             vmem_limit_bytes=VMEM_LIMIT),
        name="in_proj",
    )(x2, norm_g, w_tm, w_fm, w_f)


def _decay_kernel(ft_ref, bf_ref, ckp_ref):
    z = ft_ref[0:B_HEADS, :] + bf_ref[0]
    x = jnp.minimum(z, 0.0) - jnp.log1p(jnp.exp(-jnp.abs(z)))
    seq = x.shape[1]
    pos = lax.broadcasted_iota(jnp.int32, x.shape, 1)
    shift = 1
    while shift < seq:
        x = x + jnp.where(pos >= shift, pltpu.roll(x, shift, 1), 0.0)
        shift *= 2
    c = x * (-LOG2E)
    hi = c.astype(jnp.bfloat16).astype(jnp.float32)
    mid = (c - hi).astype(jnp.bfloat16).astype(jnp.float32)
    lo = c - hi - mid
    stacked = jnp.concatenate(
        [hi, mid, lo, jnp.zeros((LANES - 3 * B_HEADS, seq), jnp.float32)], axis=0)
    ckp_ref[0] = jnp.transpose(stacked).astype(jnp.bfloat16)


def _decay(ft, b_f, layer, bsz, seq):
    return pl.pallas_call(
        _decay_kernel,
        out_shape=jax.ShapeDtypeStruct((bsz, seq, LANES), jnp.bfloat16),
        grid=(bsz,),
        in_specs=[pl.BlockSpec((F_ROWS, seq), lambda b: (0, b)),
                  pl.BlockSpec((1, B_HEADS, 1), lambda b: (layer, 0, 0))],
        out_specs=pl.BlockSpec((1, seq, LANES), lambda b: (b, 0, 0)),
        compiler_params=pltpu.CompilerParams(dimension_semantics=("parallel",)),
        name="decay_scan",
    )(ft, b_f)


def _colmax(s):
    rows = s.shape[0]
    while rows > 8:
        rows //= 2
        s = jnp.maximum(s[:rows], s[rows:2 * rows])
    return jnp.max(s, axis=0, keepdims=True)


def _online_step(s, shift, v_aug, m_ref, acc_ref, idx):
    m_old = m_ref[idx]
    m_new = jnp.maximum(m_old, _colmax(s) + shift)
    alpha = jnp.exp2(m_old - m_new)
    p = jnp.exp2(s - (m_new - shift)).astype(jnp.bfloat16)
    acc_ref[idx] = alpha * acc_ref[idx] + jnp.dot(v_aug, p, preferred_element_type=jnp.float32)
    m_ref[idx] = m_new


def _v_aug(vt_ref, head, k0, tile):
    v = vt_ref[head * HEAD_DIM:(head + 1) * HEAD_DIM, pl.ds(k0, tile)]
    return jnp.concatenate([v, jnp.ones((ONES_ROWS, tile), jnp.bfloat16)], axis=0)


def _attn_a_kernel(qt_ref, k_ref, vt_ref, bias_ref, lamv_ref, lami_ref, subg_ref, o_ref,
                   m_ref, acc_ref, *, tile, n_near):
    qi = pl.program_id(2)
    qt = qt_ref[...].astype(jnp.float32)
    row = lax.broadcasted_iota(jnp.int32, qt.shape, 0)
    qs = [jnp.where((row >= s * A_QK_DIM) & (row < (s + 1) * A_QK_DIM), qt, 0.0)
          .astype(jnp.bfloat16) for s in range(4)]
    m_ref[...] = jnp.full(m_ref.shape, NEG_INF, jnp.float32)
    acc_ref[...] = jnp.zeros(acc_ref.shape, jnp.float32)
    n_far = jnp.maximum(qi - (n_near - 1), 0)

    def scores(ki):
        k = k_ref[0, pl.ds(pl.multiple_of(ki * tile, tile), tile), :]
        return tuple(jnp.dot(k, qs[s], preferred_element_type=jnp.float32) for s in range(4))

    def consume(ki, sc, is_far):
        k0 = pl.multiple_of(ki * tile, tile)
        for s in range(4):
            if is_far:
                x, shift = sc[s], bias_ref[s // 2, n_near, 0:1, 0:1]
            else:
                x, shift = sc[s] + bias_ref[s // 2, qi - ki], 0.0
            _online_step(x, shift, _v_aug(vt_ref, s // 2, k0, tile), m_ref, acc_ref, s)

    def far(ki, sc):
        nxt = scores(ki + 1)
        consume(ki, sc, True)
        return nxt

    def near(ki, sc):
        nxt = scores(ki + 1)
        consume(ki, sc, False)
        return nxt

    sc = lax.fori_loop(0, n_far, far, scores(0))
    sc = lax.fori_loop(n_far, qi, near, sc)
    consume(qi, sc, False)

    lam_init = lami_ref[0]
    lv = lamv_ref[0]
    e1 = jnp.exp(jnp.sum(lv[0:1] * lv[1:2], axis=1, keepdims=True))
    e2 = jnp.exp(jnp.sum(lv[2:3] * lv[3:4], axis=1, keepdims=True))
    lam = e1 - e2 + lam_init
    heads = []
    for h in range(2):
        a0, a1 = acc_ref[2 * h], acc_ref[2 * h + 1]
        o = (a0[0:HEAD_DIM] / a0[HEAD_DIM:HEAD_DIM + 1]
             - lam * (a1[0:HEAD_DIM] / a1[HEAD_DIM:HEAD_DIM + 1]))
        ms = jnp.mean(o * o, axis=0, keepdims=True)
        heads.append(o * lax.rsqrt(ms + NORM_EPS) * subg_ref[0] * (1.0 - lam_init))
    o_ref[0] = jnp.transpose(jnp.concatenate(heads, axis=0)).astype(o_ref.dtype)


def _attn_a(proj_tm3, proj_fm, bias_a, lam_vecs, lam_init, sub_g, layer):
    bsz, seq, _ = proj_tm3.shape
    tile = FLASH_TILE
    n_near = _n_near_tiles(tile)
    n_q = seq // tile
    return pl.pallas_call(
        functools.partial(_attn_a_kernel, tile=tile, n_near=n_near),
        out_shape=jax.ShapeDtypeStruct((bsz, seq, A_W), jnp.bfloat16),
        grid=(A_HEADS // 2, bsz, n_q),
        in_specs=[
            pl.BlockSpec((LANES, tile), lambda hp, b, qi: (QA_T // LANES + hp, b * n_q + qi)),
            pl.BlockSpec((1, seq, LANES), lambda hp, b, qi: (b, 0, KA // LANES + hp)),
            pl.BlockSpec((LANES, seq), lambda hp, b, qi: (VA_T // LANES + hp, b)),
            pl.BlockSpec((2, n_near + 1, tile, tile), lambda hp, b, qi: (hp, 0, 0, 0)),
            pl.BlockSpec((1, 4, A_QK_DIM), lambda hp, b, qi: (layer, 0, 0)),
            pl.BlockSpec(memory_space=pltpu.SMEM),
            pl.BlockSpec((1, HEAD_DIM, 1), lambda hp, b, qi: (layer, 0, 0)),
        ],
        out_specs=pl.BlockSpec((1, tile, LANES), lambda hp, b, qi: (b, qi, hp)),
        scratch_shapes=[pltpu.VMEM((4, 1, tile), jnp.float32),
                        pltpu.VMEM((4, ACC_ROWS, tile), jnp.float32)],
        compiler_params=pltpu.CompilerParams(
            dimension_semantics=("parallel", "parallel", "parallel"),
            vmem_limit_bytes=VMEM_LIMIT),
        name="attn_a",
    )(proj_fm, proj_tm3, proj_fm, bias_a, lam_vecs, lam_init, sub_g)


B_STEP_HEADS = 4


def _attn_b_kernel(qt_ref, k_ref, ckp_ref, vt_ref, o_ref, m_ref, acc_ref, *, tile):
    hg = pl.program_id(0)
    qi = pl.program_id(2)
    n_h = B_STEP_HEADS
    qt = qt_ref[...].astype(jnp.float32)
    row = lax.broadcasted_iota(jnp.int32, (LANES, tile), 0)
    q_aug = []
    for s in range(n_h):
        blk = qt[(s // 2) * LANES:(s // 2 + 1) * LANES]
        lo = (s % 2) * HEAD_DIM
        q_s = jnp.where((row >= lo) & (row < lo + HEAD_DIM), blk, 0.0)
        pick = (row < 3 * B_HEADS) & ((row % B_HEADS) == n_h * hg + s)
        ones = jnp.where(pick, 1.0, 0.0)
        q_aug.append(jnp.concatenate([q_s, ones], axis=0).astype(jnp.bfloat16))
    m_ref[...] = jnp.full(m_ref.shape, NEG_INF, jnp.float32)
    acc_ref[...] = jnp.zeros(acc_ref.shape, jnp.float32)

    def scores(ki):
        k0 = pl.multiple_of(ki * tile, tile)
        ck = ckp_ref[0, pl.ds(k0, tile), :]
        out = []
        for s in range(n_h):
            k = k_ref[0, pl.ds(k0, tile), (s // 2) * LANES:(s // 2 + 1) * LANES]
            out.append(jnp.dot(jnp.concatenate([k, ck], axis=1), q_aug[s],
                               preferred_element_type=jnp.float32))
        return tuple(out)

    def consume(ki, sc, masked):
        k0 = pl.multiple_of(ki * tile, tile)
        if masked:
            kk = lax.broadcasted_iota(jnp.int32, (tile, tile), 0)
            qq = lax.broadcasted_iota(jnp.int32, (tile, tile), 1)
            keep = kk <= qq
        for s in range(n_h):
            x = jnp.where(keep, sc[s], NEG_INF) if masked else sc[s]
            _online_step(x, 0.0, _v_aug(vt_ref, s, k0, tile), m_ref, acc_ref, s)

    def body(ki, sc):
        nxt = scores(ki + 1)
        consume(ki, sc, False)
        return nxt

    sc = lax.fori_loop(0, qi, body, scores(0))
    consume(qi, sc, True)
    heads = [acc_ref[s][0:HEAD_DIM] / acc_ref[s][HEAD_DIM:HEAD_DIM + 1] for s in range(n_h)]
    o_ref[0] = jnp.transpose(jnp.concatenate(heads, axis=0)).astype(o_ref.dtype)


def _attn_b(proj_tm3, proj_fm, ckp):
    bsz, seq, _ = proj_tm3.shape
    tile = FLASH_TILE
    n_q = seq // tile
    width = B_STEP_HEADS * HEAD_DIM
    return pl.pallas_call(
        functools.partial(_attn_b_kernel, tile=tile),
        out_shape=jax.ShapeDtypeStruct((bsz, seq, B_W), jnp.bfloat16),
        grid=(B_HEADS // B_STEP_HEADS, bsz, n_q),
        in_specs=[
            pl.BlockSpec((width, tile), lambda hg, b, qi: (QB_T // width + hg, b * n_q + qi)),
            pl.BlockSpec((1, seq, width), lambda hg, b, qi: (b, 0, KB // width + hg)),
            pl.BlockSpec((1, seq, LANES), lambda hg, b, qi: (b, 0, 0)),
            pl.BlockSpec((width, seq), lambda hg, b, qi: (VB_T // width + hg, b)),
        ],
        out_specs=pl.BlockSpec((1, tile, width), lambda hg, b, qi: (b, qi, hg)),
        scratch_shapes=[pltpu.VMEM((B_STEP_HEADS, 1, tile), jnp.float32),
                        pltpu.VMEM((B_STEP_HEADS, ACC_ROWS, tile), jnp.float32)],
        compiler_params=pltpu.CompilerParams(
            dimension_semantics=("parallel", "parallel", "parallel"),
            vmem_limit_bytes=VMEM_LIMIT),
        name="attn_b",
    )(proj_fm, proj_tm3, ckp, proj_fm)


def _attn_c_kernel(q_ref, k_ref, v_ref, bias_ref, o_ref, lse_ref, *, n_tiles):
    dil = q_ref.shape[1]
    lane = lax.broadcasted_iota(jnp.int32, (1, LANES), 1)
    first = lane < HEAD_DIM

    def body(it, carry):
        r = it // n_tiles
        t = it % n_tiles
        row0 = pl.multiple_of(t * C_TILE, C_TILE)
        ver = jnp.minimum(t, 1)
        for hp in range(2):
            cols = slice(hp * LANES, (hp + 1) * LANES)
            q = q_ref[0, r, pl.ds(row0, C_TILE), cols]
            kw = k_ref[0, r, pl.ds(row0, 2 * C_TILE), cols]
            vw = v_ref[0, r, pl.ds(row0, 2 * C_TILE), cols]
            outs, lses = [], []
            for s in range(2):
                qm = jnp.where(first if s == 0 else jnp.logical_not(first), q, jnp.zeros_like(q))
                sc = _dot_nt(qm, kw) + bias_ref[hp * 2 + s, ver]
                m = jnp.max(sc, axis=1, keepdims=True)
                p = jnp.exp(sc - m)
                l = jnp.sum(p, axis=1, keepdims=True)
                pv = jnp.dot(p.astype(jnp.bfloat16), vw, preferred_element_type=jnp.float32)
                outs.append(pv / l)
                lses.append(m + jnp.log(l))
            o_ref[0, r, pl.ds(row0, C_TILE), cols] = jnp.where(first, outs[0], outs[1])
            lse_ref[0, r, pl.ds(row0, C_TILE), cols] = jnp.where(first, lses[0], lses[1])
        return carry

    lax.fori_loop(0, dil * n_tiles, body, 0)


def _attn_c(qd, kd, vd, bias_c, group):
    bsz, dil, length, _ = qd.shape
    out_sds = jax.ShapeDtypeStruct((bsz, dil, length, C_W), jnp.float32)
    in_blk = pl.BlockSpec((1, dil, length, C_W), lambda b: (b, 0, 0, 0))
    kv_blk = pl.BlockSpec((1, dil, length + C_TILE, C_W), lambda b: (b, 0, 0, 0))
    return pl.pallas_call(
        functools.partial(_attn_c_kernel, n_tiles=length // C_TILE),
        out_shape=(out_sds, out_sds),
        grid=(bsz,),
        in_specs=[in_blk, kv_blk, kv_blk,
                  pl.BlockSpec((C_HEADS, 2, C_TILE, 2 * C_TILE), lambda b: (group, 0, 0, 0))],
        out_specs=(in_blk, in_blk),
        compiler_params=pltpu.CompilerParams(dimension_semantics=("parallel",),
                                             vmem_limit_bytes=VMEM_LIMIT),
        name=f"attn_c{group}",
    )(qd, kd, vd, bias_c)


def _deinterleave(x, dil, pad):
    bsz, seq, w = x.shape
    y = jnp.transpose(x.reshape(bsz, seq // dil, dil, w), (0, 2, 1, 3))
    if pad:
        y = jnp.pad(y, ((0, 0), (0, 0), (pad, 0), (0, 0)))
    return y


def _interleave(y):
    bsz, dil, length, w = y.shape
    return jnp.transpose(y, (0, 2, 1, 3)).reshape(bsz, dil * length, w)


def _mix_mlp_kernel(x_ref, oa_ref, ob_ref, oc0_ref, oc1_ref, oc2_ref, ls0_ref, ls1_ref, ls2_ref,
                    wo_ref, g_ref, w1_ref, w2_ref, out_ref, acc_ref, h_ref):
    f = pl.program_id(1)

    @pl.when(f == 0)
    def _():
        l0, l1, l2 = ls0_ref[...], ls1_ref[...], ls2_ref[...]
        m = jnp.maximum(jnp.maximum(l0, l1), l2)
        e0, e1, e2 = jnp.exp(l0 - m), jnp.exp(l1 - m), jnp.exp(l2 - m)
        oc = (e0 * oc0_ref[...] + e1 * oc1_ref[...] + e2 * oc2_ref[...]) / (e0 + e1 + e2)
        wo = wo_ref.at[0]
        y = jnp.dot(oa_ref[...], wo[0:A_W, :], preferred_element_type=jnp.float32)
        y += jnp.dot(ob_ref[...], wo[A_W:A_W + B_W, :], preferred_element_type=jnp.float32)
        y += jnp.dot(oc.astype(jnp.bfloat16), wo[A_W + B_W:, :],
                     preferred_element_type=jnp.float32)
        x1 = x_ref[...] + y
        acc_ref[...] = x1
        h_ref[...] = _rms(x1, g_ref[0]).astype(jnp.bfloat16)

    u = jnp.dot(h_ref[...], w1_ref[0], preferred_element_type=jnp.float32)
    u = jnp.square(jnp.maximum(u, 0.0)).astype(jnp.bfloat16)
    acc_ref[...] += jnp.dot(u, w2_ref[0], preferred_element_type=jnp.float32)

    @pl.when(f == pl.num_programs(1) - 1)
    def _():
        out_ref[...] = acc_ref[...]


def _mix_mlp(x2, oa, ob, ocs, lses, w_o, norm_g, w_1, w_2, layer, *, tm=512, tf=1024):
    n_tok = x2.shape[0]
    row = lambda w: pl.BlockSpec((tm, w), lambda i, f: (i, 0))
    return pl.pallas_call(
        _mix_mlp_kernel,
        out_shape=jax.ShapeDtypeStruct((n_tok, D_MODEL), jnp.float32),
        grid=(n_tok // tm, D_FF // tf),
        in_specs=[row(D_MODEL), row(A_W), row(B_W)] + [row(C_W)] * 6 + [
            pl.BlockSpec((1, D_MODEL, D_MODEL), lambda i, f: (layer, 0, 0)),
            pl.BlockSpec((1, 1, D_MODEL), lambda i, f: (layer, 0, 0)),
            pl.BlockSpec((1, D_MODEL, tf), lambda i, f: (layer, 0, f)),
            pl.BlockSpec((1, tf, D_MODEL), lambda i, f: (layer, f, 0)),
        ],
        out_specs=row(D_MODEL),
        scratch_shapes=[pltpu.VMEM((tm, D_MODEL), jnp.float32),
                        pltpu.VMEM((tm, D_MODEL), jnp.bfloat16)],
        compiler_params=pltpu.CompilerParams(dimension_semantics=("parallel", "arbitrary"),
                                             vmem_limit_bytes=VMEM_LIMIT),
        name="mix_mlp",
    )(x2, oa, ob, *ocs, *lses, w_o, norm_g, w_1, w_2)


def _final_norm_kernel(x_ref, g_ref, o_ref):
    o_ref[...] = _rms(x_ref[...], g_ref[...])


def _final_norm(x2, g, *, tm=1024):
    n_tok = x2.shape[0]
    return pl.pallas_call(
        _final_norm_kernel,
        out_shape=jax.ShapeDtypeStruct(x2.shape, jnp.float32),
        grid=(n_tok // tm,),
        in_specs=[pl.BlockSpec((tm, D_MODEL), lambda i: (i, 0)),
                  pl.BlockSpec((1, D_MODEL), lambda i: (0, 0))],
        out_specs=pl.BlockSpec((tm, D_MODEL), lambda i: (i, 0)),
        compiler_params=pltpu.CompilerParams(dimension_semantics=("parallel",)),
        name="final_norm",
    )(x2, g)


def _prep_w_in(w_in):
    a0, b0, f0, c0 = 0, 3 * A_W, 3 * A_W + 3 * B_W, 3 * A_W + 3 * B_W + B_HEADS
    sa, sb = A_QK_DIM ** -0.5, HEAD_DIM ** -0.5
    cw = C_GROUPS * C_W
    w_tm = jnp.concatenate([
        w_in[:, :, a0 + A_W:a0 + 2 * A_W], w_in[:, :, b0 + B_W:b0 + 2 * B_W],
        w_in[:, :, c0:c0 + cw] * sb, w_in[:, :, c0 + cw:],
    ], axis=-1).astype(jnp.bfloat16)
    w_fm = jnp.concatenate([
        w_in[:, :, a0:a0 + A_W] * (sa * LOG2E), w_in[:, :, b0:b0 + B_W] * (sb * LOG2E),
        w_in[:, :, a0 + 2 * A_W:b0], w_in[:, :, b0 + 2 * B_W:f0],
    ], axis=-1)
    w_fm = jnp.transpose(w_fm, (0, 2, 1)).astype(jnp.bfloat16)
    w_f = jnp.transpose(w_in[:, :, f0:c0], (0, 2, 1))
    w_f = jnp.pad(w_f, ((0, 0), (0, F_ROWS - B_HEADS), (0, 0))).astype(jnp.bfloat16)
    return w_tm, w_fm, w_f


def kernel(x, norm1_g, w_in, b_f, lam_q1, lam_k1, lam_q2, lam_k2, diff_norm_g, w_o, norm2_g,
           w_1, w_2, rel_bias, final_g):
    bsz, seq, _ = x.shape
    depth = w_in.shape[0]
    w_tm, w_fm, w_f = _prep_w_in(w_in)
    w_o16, w_116, w_216 = (w.astype(jnp.bfloat16) for w in (w_o, w_1, w_2))
    lam_vecs = jnp.stack([lam_q1, lam_k1, lam_q2, lam_k2], axis=1)
    bias_a, bias_c = _make_bias_tiles(rel_bias)

    x2 = x.reshape(bsz * seq, D_MODEL)
    for l in range(depth):
        proj_tm, proj_fm, ft = _in_proj(x2, norm1_g[:, None, :], w_tm, w_fm, w_f, l)
        proj3 = proj_tm.reshape(bsz, seq, N_TM)
        ckp = _decay(ft, b_f[:, :, None], l, bsz, seq)
        lam_init = jnp.full((1,), 0.8 - 0.6 * math.exp(-0.3 * l), jnp.float32)
        oa = _attn_a(proj3, proj_fm, bias_a, lam_vecs, lam_init, diff_norm_g[:, :, None], l)
        ob = _attn_b(proj3, proj_fm, ckp)
        ocs, lses = [], []
        for g, (_, dil) in enumerate(C_PAIRS):
            sl = lambda off: proj3[:, :, off + g * C_W: off + (g + 1) * C_W]
            o_g, lse_g = _attn_c(_deinterleave(sl(QC), dil, 0), _deinterleave(sl(KC), dil, C_TILE),
                                 _deinterleave(sl(VC), dil, C_TILE), bias_c, g)
            ocs.append(_interleave(o_g).reshape(bsz * seq, C_W))
            lses.append(_interleave(lse_g).reshape(bsz * seq, C_W))
        x2 = _mix_mlp(x2, oa.reshape(bsz * seq, A_W), ob.reshape(bsz * seq, B_W), ocs, lses,
                      w_o16, norm2_g[:, None, :], w_116, w_216, l)
    out = _final_norm(x2, final_g[None, :])
    return out.reshape(bsz, seq, D_MODEL)
```

```python
import functools
import math

import jax
import jax.numpy as jnp
from jax import lax
from jax.experimental import pallas as pl
from jax.experimental.pallas import tpu as pltpu

D_MODEL = 1024
HEAD_DIM = 64
A_HEADS = 4
A_QK_DIM = 32
B_HEADS = 8
C_HEADS = 4
C_PAIRS = ((128, 1), (512, 4), (2048, 16))
C_GROUPS = 3
A_W = 256
B_W = 512
C_W = 256
D_FF = 4096
N_BUCKETS = 32
MAX_DISTANCE = 2048
NORM_EPS = 1e-6
NEG_INF = -1e30
LOG2E = 1.4426950408889634

LANES = 128
VMEM_LIMIT = 56 * 1024 * 1024

KA, KB, QC, KC, VC = 0, 256, 768, 1536, 2304
N_TM = 3072
QA_T, QB_T, VA_T, VB_T = 0, 256, 768, 1024
N_FM = 1536
F_ROWS = 16
ONES_ROWS = 16
ACC_ROWS = HEAD_DIM + ONES_ROWS

C_TILE = 128
FLASH_TILE = 256


def _bucket_thresholds():
    max_exact = N_BUCKETS // 2
    out = []
    for k in range(1, N_BUCKETS - max_exact):
        t = max_exact * (MAX_DISTANCE / max_exact) ** (k / (N_BUCKETS - max_exact))
        out.append(int(math.ceil(t)))
    return tuple(out)


_THRESHOLDS = _bucket_thresholds()


def _n_near_tiles(tile):
    return -(-(_THRESHOLDS[-1] + tile - 1) // tile)


def _t5_bucket(d):
    big = jnp.full(d.shape, N_BUCKETS // 2, jnp.int32)
    for t in _THRESHOLDS:
        big = big + (d >= t).astype(jnp.int32)
    return jnp.where(d < N_BUCKETS // 2, d, big)


def _bias_lookup(rb_ref, bucket, col):
    val = jnp.zeros(bucket.shape, jnp.float32)
    for b in range(N_BUCKETS):
        val = jnp.where(bucket == b, rb_ref[b, col], val)
    return val


def _bias_a_kernel(rb_ref, out_ref, *, tile):
    delta = pl.program_id(0)
    j = lax.broadcasted_iota(jnp.int32, (tile, tile), 0)
    i = lax.broadcasted_iota(jnp.int32, (tile, tile), 1)
    d = delta * tile + i - j
    bucket = _t5_bucket(d)
    for h in range(A_HEADS):
        val = _bias_lookup(rb_ref, bucket, h) * LOG2E
        out_ref[h, 0] = jnp.where(d >= 0, val, NEG_INF)


def _bias_c_kernel(rb_ref, out_ref):
    g = pl.program_id(0)
    dil = jnp.where(g == 0, C_PAIRS[0][1], jnp.where(g == 1, C_PAIRS[1][1], C_PAIRS[2][1]))
    i = lax.broadcasted_iota(jnp.int32, (C_TILE, 2 * C_TILE), 0)
    c = lax.broadcasted_iota(jnp.int32, (C_TILE, 2 * C_TILE), 1)
    steps = i + C_TILE - c
    valid = (steps >= 0) & (steps <= C_TILE)
    bucket = _t5_bucket(steps * dil)
    for h in range(C_HEADS):
        val = _bias_lookup(rb_ref, bucket, A_HEADS + g * C_HEADS + h)
        out_ref[h, 1] = jnp.where(valid, val, NEG_INF)
        out_ref[h, 0] = jnp.where(valid & (c >= C_TILE), val, NEG_INF)


def _make_bias_tiles(rel_bias):
    tile = FLASH_TILE
    n_bias = _n_near_tiles(tile) + 1
    smem = pl.BlockSpec(memory_space=pltpu.SMEM)
    bias_a = pl.pallas_call(
        functools.partial(_bias_a_kernel, tile=tile),
        out_shape=jax.ShapeDtypeStruct((A_HEADS, n_bias, tile, tile), jnp.float32),
        grid=(n_bias,),
        in_specs=[smem],
        out_specs=pl.BlockSpec((A_HEADS, 1, tile, tile), lambda d: (0, d, 0, 0)),
        name="bias_a_tiles",
    )(rel_bias)
    bias_c = pl.pallas_call(
        _bias_c_kernel,
        out_shape=jax.ShapeDtypeStruct((C_GROUPS * C_HEADS, 2, C_TILE, 2 * C_TILE), jnp.float32),
        grid=(C_GROUPS,),
        in_specs=[smem],
        out_specs=pl.BlockSpec((C_HEADS, 2, C_TILE, 2 * C_TILE), lambda g: (g, 0, 0, 0)),
        name="bias_c_tiles",
    )(rel_bias)
    return bias_a, bias_c


def _rms(x, g):
    return x * lax.rsqrt(jnp.mean(x * x, axis=-1, keepdims=True) + NORM_EPS) * g


def _dot_nt(a, b):
    return lax.dot_general(a, b, (((1,), (1,)), ((), ())), preferred_element_type=jnp.float32)


def _in_proj_kernel(x_ref, g_ref, wtm_ref, wfm_ref, wf_ref, tm_ref, fm_ref, ft_ref, *, chunk):
    h = _rms(x_ref[...], g_ref[0]).astype(jnp.bfloat16)
    for c in range(N_TM // chunk):
        cols = slice(c * chunk, (c + 1) * chunk)
        tm_ref[:, cols] = jnp.dot(h, wtm_ref[0, :, cols],
                                  preferred_element_type=jnp.float32).astype(jnp.bfloat16)
    for c in range(N_FM // chunk):
        rows = slice(c * chunk, (c + 1) * chunk)
        fm_ref[rows, :] = _dot_nt(wfm_ref[0, rows, :], h).astype(jnp.bfloat16)
    ft_ref[...] = _dot_nt(wf_ref[0], h)


def _in_proj(x2, norm_g, w_tm, w_fm, w_f, layer, *, tm=512):
    n_tok = x2.shape[0]
    return pl.pallas_call(
        functools.partial(_in_proj_kernel, chunk=512),
        out_shape=(jax.ShapeDtypeStruct((n_tok, N_TM), jnp.bfloat16),
                   jax.ShapeDtypeStruct((N_FM, n_tok), jnp.bfloat16),
                   jax.ShapeDtypeStruct((F_ROWS, n_tok), jnp.float32)),
        grid=(n_tok // tm,),
        in_specs=[
            pl.BlockSpec((tm, D_MODEL), lambda i: (i, 0)),
            pl.BlockSpec((1, 1, D_MODEL), lambda i: (layer, 0, 0)),
            pl.BlockSpec((1, D_MODEL, N_TM), lambda i: (layer, 0, 0)),
            pl.BlockSpec((1, N_FM, D_MODEL), lambda i: (layer, 0, 0)),
            pl.BlockSpec((1, F_ROWS, D_MODEL), lambda i: (layer, 0, 0)),
        ],
        out_specs=(pl.BlockSpec((tm, N_TM), lambda i: (i, 0)),
                   pl.BlockSpec((N_FM, tm), lambda i: (0, i)),
                   pl.BlockSpec((F_ROWS, tm), lambda i: (0, i))),
        compiler_params=pltpu.CompilerParams(dimension_semantics=("parallel",),
                                             vmem_limit_bytes=VMEM_LIMIT),
        name="in_proj",
    )(x2, norm_g, w_tm, w_fm, w_f)


def _decay_kernel(ft_ref, bf_ref, ckp_ref):
    z = ft_ref[0:B_HEADS, :] + bf_ref[0]
    x = jnp.minimum(z, 0.0) - jnp.log1p(jnp.exp(-jnp.abs(z)))
    seq = x.shape[1]
    pos = lax.broadcasted_iota(jnp.int32, x.shape, 1)
    shift = 1
    while shift < seq:
        x = x + jnp.where(pos >= shift, pltpu.roll(x, shift, 1), 0.0)
        shift *= 2
    c = x * (-LOG2E)
    hi = c.astype(jnp.bfloat16).astype(jnp.float32)
    mid = (c - hi).astype(jnp.bfloat16).astype(jnp.float32)
    lo = c - hi - mid
    stacked = jnp.concatenate(
        [hi, mid, lo, jnp.zeros((LANES - 3 * B_HEADS, seq), jnp.float32)], axis=0)
    ckp_ref[0] = jnp.transpose(stacked).astype(jnp.bfloat16)


def _decay(ft, b_f, layer, bsz, seq):
    return pl.pallas_call(
        _decay_kernel,
        out_shape=jax.ShapeDtypeStruct((bsz, seq, LANES), jnp.bfloat16),
        grid=(bsz,),
        in_specs=[pl.BlockSpec((F_ROWS, seq), lambda b: (0, b)),
                  pl.BlockSpec((1, B_HEADS, 1), lambda b: (layer, 0, 0))],
        out_specs=pl.BlockSpec((1, seq, LANES), lambda b: (b, 0, 0)),
        compiler_params=pltpu.CompilerParams(dimension_semantics=("parallel",)),
        name="decay_scan",
    )(ft, b_f)


def _colmax(s):
    rows = s.shape[0]
    while rows > 8:
        rows //= 2
        s = jnp.maximum(s[:rows], s[rows:2 * rows])
    return jnp.max(s, axis=0, keepdims=True)


def _online_step(s, smax, v_aug, m_ref, acc_ref, idx):
    m_old = m_ref[idx]
    m_new = jnp.maximum(m_old, smax)
    alpha = jnp.exp2(m_old - m_new)
    p = jnp.exp2(s - m_new).astype(jnp.bfloat16)
    acc_ref[idx] = alpha * acc_ref[idx] + jnp.dot(v_aug, p, preferred_element_type=jnp.float32)
    m_ref[idx] = m_new


def _v_aug(vt_ref, head, k0, tile):
    v = vt_ref[head * HEAD_DIM:(head + 1) * HEAD_DIM, pl.ds(k0, tile)]
    return jnp.concatenate([v, jnp.ones((ONES_ROWS, tile), jnp.bfloat16)], axis=0)


def _flash_pipeline(qi, produce, consume):
    produce(0, 0)

    def pair(j, carry):
        k = 2 * j
        produce(k + 1, 1)
        consume(k, 0, False)
        produce(k + 2, 0)
        consume(k + 1, 1, False)
        return carry

    lax.fori_loop(0, qi // 2, pair, 0)

    @pl.when(qi % 2 == 0)
    def _():
        consume(qi, 0, True)

    @pl.when(qi % 2 == 1)
    def _():
        produce(qi, 1)
        consume(qi - 1, 0, False)
        consume(qi, 1, True)


def _attn_a_kernel(qt_ref, k_ref, vt_ref, bias_ref, lamv_ref, lami_ref, subg_ref, o_ref,
                   m_ref, acc_ref, s_ref, smax_ref, *, tile, n_near):
    qi = pl.program_id(2)
    qt = qt_ref[...].astype(jnp.float32)
    row = lax.broadcasted_iota(jnp.int32, qt.shape, 0)
    qs = [jnp.where((row >= s * A_QK_DIM) & (row < (s + 1) * A_QK_DIM), qt, 0.0)
          .astype(jnp.bfloat16) for s in range(4)]
    m_ref[...] = jnp.full(m_ref.shape, NEG_INF, jnp.float32)
    acc_ref[...] = jnp.zeros(acc_ref.shape, jnp.float32)

    def produce(ki, slot):
        k = k_ref[0, pl.ds(pl.multiple_of(ki * tile, tile), tile), :]
        delta = jnp.minimum(qi - ki, n_near)
        for s in range(4):
            x = jnp.dot(k, qs[s], preferred_element_type=jnp.float32) + bias_ref[s // 2, delta]
            s_ref[slot, s] = x
            smax_ref[slot, s] = _colmax(x)

    def consume(ki, slot, last):
        k0 = pl.multiple_of(ki * tile, tile)
        for s in range(4):
            _online_step(s_ref[slot, s], smax_ref[slot, s], _v_aug(vt_ref, s // 2, k0, tile),
                         m_ref, acc_ref, s)

    _flash_pipeline(qi, produce, consume)

    lam_init = lami_ref[0]
    lv = lamv_ref[0]
    e1 = jnp.exp(jnp.sum(lv[0:1] * lv[1:2], axis=1, keepdims=True))
    e2 = jnp.exp(jnp.sum(lv[2:3] * lv[3:4], axis=1, keepdims=True))
    lam = e1 - e2 + lam_init
    heads = []
    for h in range(2):
        a0, a1 = acc_ref[2 * h], acc_ref[2 * h + 1]
        o = (a0[0:HEAD_DIM] / a0[HEAD_DIM:HEAD_DIM + 1]
             - lam * (a1[0:HEAD_DIM] / a1[HEAD_DIM:HEAD_DIM + 1]))
        ms = jnp.mean(o * o, axis=0, keepdims=True)
        heads.append(o * lax.rsqrt(ms + NORM_EPS) * subg_ref[0] * (1.0 - lam_init))
    o_ref[0] = jnp.transpose(jnp.concatenate(heads, axis=0)).astype(o_ref.dtype)


def _attn_a(proj_tm3, proj_fm, bias_a, lam_vecs, lam_init, sub_g, layer):
    bsz, seq, _ = proj_tm3.shape
    tile = FLASH_TILE
    n_near = _n_near_tiles(tile)
    n_q = seq // tile
    return pl.pallas_call(
        functools.partial(_attn_a_kernel, tile=tile, n_near=n_near),
        out_shape=jax.ShapeDtypeStruct((bsz, seq, A_W), jnp.bfloat16),
        grid=(A_HEADS // 2, bsz, n_q),
        in_specs=[
            pl.BlockSpec((LANES, tile), lambda hp, b, qi: (QA_T // LANES + hp, b * n_q + qi)),
            pl.BlockSpec((1, seq, LANES), lambda hp, b, qi: (b, 0, KA // LANES + hp)),
            pl.BlockSpec((LANES, seq), lambda hp, b, qi: (VA_T // LANES + hp, b)),
            pl.BlockSpec((2, n_near + 1, tile, tile), lambda hp, b, qi: (hp, 0, 0, 0)),
            pl.BlockSpec((1, 4, A_QK_DIM), lambda hp, b, qi: (layer, 0, 0)),
            pl.BlockSpec(memory_space=pltpu.SMEM),
            pl.BlockSpec((1, HEAD_DIM, 1), lambda hp, b, qi: (layer, 0, 0)),
        ],
        out_specs=pl.BlockSpec((1, tile, LANES), lambda hp, b, qi: (b, qi, hp)),
        scratch_shapes=[pltpu.VMEM((4, 1, tile), jnp.float32),
                        pltpu.VMEM((4, ACC_ROWS, tile), jnp.float32),
                        pltpu.VMEM((2, 4, tile, tile), jnp.float32),
                        pltpu.VMEM((2, 4, 1, tile), jnp.float32)],
        compiler_params=pltpu.CompilerParams(
            dimension_semantics=("parallel", "parallel", "parallel"),
            vmem_limit_bytes=VMEM_LIMIT),
        name="attn_a",
    )(proj_fm, proj_tm3, proj_fm, bias_a, lam_vecs, lam_init, sub_g)


B_STEP_HEADS = 4


def _attn_b_kernel(qt_ref, k_ref, ckp_ref, vt_ref, o_ref, m_ref, acc_ref, s_ref, smax_ref, *,
                   tile):
    hg = pl.program_id(0)
    qi = pl.program_id(2)
    n_h = B_STEP_HEADS
    qt = qt_ref[...].astype(jnp.float32)
    row = lax.broadcasted_iota(jnp.int32, (LANES, tile), 0)
    q_aug = []
    for s in range(n_h):
        blk = qt[(s // 2) * LANES:(s // 2 + 1) * LANES]
        lo = (s % 2) * HEAD_DIM
        q_s = jnp.where((row >= lo) & (row < lo + HEAD_DIM), blk, 0.0)
        pick = (row < 3 * B_HEADS) & ((row % B_HEADS) == n_h * hg + s)
        ones = jnp.where(pick, 1.0, 0.0)
        q_aug.append(jnp.concatenate([q_s, ones], axis=0).astype(jnp.bfloat16))
    m_ref[...] = jnp.full(m_ref.shape, NEG_INF, jnp.float32)
    acc_ref[...] = jnp.zeros(acc_ref.shape, jnp.float32)

    def produce(ki, slot):
        k0 = pl.multiple_of(ki * tile, tile)
        ck = ckp_ref[0, pl.ds(k0, tile), :]
        for s in range(n_h):
            k = k_ref[0, pl.ds(k0, tile), (s // 2) * LANES:(s // 2 + 1) * LANES]
            x = jnp.dot(jnp.concatenate([k, ck], axis=1), q_aug[s],
                        preferred_element_type=jnp.float32)
            s_ref[slot, s] = x
            smax_ref[slot, s] = _colmax(x)

    def consume(ki, slot, last):
        k0 = pl.multiple_of(ki * tile, tile)
        if last:
            kk = lax.broadcasted_iota(jnp.int32, (tile, tile), 0)
            qq = lax.broadcasted_iota(jnp.int32, (tile, tile), 1)
            keep = kk <= qq
        for s in range(n_h):
            x, xmax = s_ref[slot, s], smax_ref[slot, s]
            if last:
                x = jnp.where(keep, x, NEG_INF)
                xmax = _colmax(x)
            _online_step(x, xmax, _v_aug(vt_ref, s, k0, tile), m_ref, acc_ref, s)

    _flash_pipeline(qi, produce, consume)
    heads = [acc_ref[s][0:HEAD_DIM] / acc_ref[s][HEAD_DIM:HEAD_DIM + 1] for s in range(n_h)]
    o_ref[0] = jnp.transpose(jnp.concatenate(heads, axis=0)).astype(o_ref.dtype)


def _attn_b(proj_tm3, proj_fm, ckp):
    bsz, seq, _ = proj_tm3.shape
    tile = FLASH_TILE
    n_q = seq // tile
    width = B_STEP_HEADS * HEAD_DIM
    return pl.pallas_call(
        functools.partial(_attn_b_kernel, tile=tile),
        out_shape=jax.ShapeDtypeStruct((bsz, seq, B_W), jnp.bfloat16),
        grid=(B_HEADS // B_STEP_HEADS, bsz, n_q),
        in_specs=[
            pl.BlockSpec((width, tile), lambda hg, b, qi: (QB_T // width + hg, b * n_q + qi)),
            pl.BlockSpec((1, seq, width), lambda hg, b, qi: (b, 0, KB // width + hg)),
            pl.BlockSpec((1, seq, LANES), lambda hg, b, qi: (b, 0, 0)),
            pl.BlockSpec((width, seq), lambda hg, b, qi: (VB_T // width + hg, b)),
        ],
        out_specs=pl.BlockSpec((1, tile, width), lambda hg, b, qi: (b, qi, hg)),
        scratch_shapes=[pltpu.VMEM((B_STEP_HEADS, 1, tile), jnp.float32),
                        pltpu.VMEM((B_STEP_HEADS, ACC_ROWS, tile), jnp.float32),
                        pltpu.VMEM((2, B_STEP_HEADS, tile, tile), jnp.float32),
                        pltpu.VMEM((2, B_STEP_HEADS, 1, tile), jnp.float32)],
        compiler_params=pltpu.CompilerParams(
            dimension_semantics=("parallel", "parallel", "parallel"),
            vmem_limit_bytes=VMEM_LIMIT),
        name="attn_b",
    )(proj_fm, proj_tm3, ckp, proj_fm)


def _attn_c_kernel(q_ref, k_ref, v_ref, bias_ref, o_ref, lse_ref, *, n_tiles):
    dil = q_ref.shape[1]
    lane = lax.broadcasted_iota(jnp.int32, (1, LANES), 1)
    first = lane < HEAD_DIM

    def body(it, carry):
        r = it // n_tiles
        t = it % n_tiles
        row0 = pl.multiple_of(t * C_TILE, C_TILE)
        ver = jnp.minimum(t, 1)
        for hp in range(2):
            cols = slice(hp * LANES, (hp + 1) * LANES)
            q = q_ref[0, r, pl.ds(row0, C_TILE), cols]
            kw = k_ref[0, r, pl.ds(row0, 2 * C_TILE), cols]
            vw = v_ref[0, r, pl.ds(row0, 2 * C_TILE), cols]
            outs, lses = [], []
            for s in range(2):
                qm = jnp.where(first if s == 0 else jnp.logical_not(first), q, jnp.zeros_like(q))
                sc = _dot_nt(qm, kw) + bias_ref[hp * 2 + s, ver]
                m = jnp.max(sc, axis=1, keepdims=True)
                p = jnp.exp(sc - m)
                l = jnp.sum(p, axis=1, keepdims=True)
                pv = jnp.dot(p.astype(jnp.bfloat16), vw, preferred_element_type=jnp.float32)
                outs.append(pv / l)
                lses.append(m + jnp.log(l))
            o_ref[0, r, pl.ds(row0, C_TILE), cols] = jnp.where(first, outs[0], outs[1])
            lse_ref[0, r, pl.ds(row0, C_TILE), cols] = jnp.where(first, lses[0], lses[1])
        return carry

    lax.fori_loop(0, dil * n_tiles, body, 0)


def _attn_c(qd, kd, vd, bias_c, group):
    bsz, dil, length, _ = qd.shape
    out_sds = jax.ShapeDtypeStruct((bsz, dil, length, C_W), jnp.float32)
    in_blk = pl.BlockSpec((1, dil, length, C_W), lambda b: (b, 0, 0, 0))
    kv_blk = pl.BlockSpec((1, dil, length + C_TILE, C_W), lambda b: (b, 0, 0, 0))
    return pl.pallas_call(
        functools.partial(_attn_c_kernel, n_tiles=length // C_TILE),
        out_shape=(out_sds, out_sds),
        grid=(bsz,),
        in_specs=[in_blk, kv_blk, kv_blk,
                  pl.BlockSpec((C_HEADS, 2, C_TILE, 2 * C_TILE), lambda b: (group, 0, 0, 0))],
        out_specs=(in_blk, in_blk),
        compiler_params=pltpu.CompilerParams(dimension_semantics=("parallel",),
                                             vmem_limit_bytes=VMEM_LIMIT),
        name=f"attn_c{group}",
    )(qd, kd, vd, bias_c)


def _deinterleave(x, dil, pad):
    bsz, seq, w = x.shape
    y = jnp.transpose(x.reshape(bsz, seq // dil, dil, w), (0, 2, 1, 3))
    if pad:
        y = jnp.pad(y, ((0, 0), (0, 0), (pad, 0), (0, 0)))
    return y


def _interleave(y):
    bsz, dil, length, w = y.shape
    return jnp.transpose(y, (0, 2, 1, 3)).reshape(bsz, dil * length, w)


def _mix_mlp_kernel(x_ref, oa_ref, ob_ref, oc0_ref, oc1_ref, oc2_ref, ls0_ref, ls1_ref, ls2_ref,
                    wo_ref, g_ref, w1_ref, w2_ref, out_ref, acc_ref, h_ref):
    f = pl.program_id(1)

    @pl.when(f == 0)
    def _():
        l0, l1, l2 = ls0_ref[...], ls1_ref[...], ls2_ref[...]
        m = jnp.maximum(jnp.maximum(l0, l1), l2)
        e0, e1, e2 = jnp.exp(l0 - m), jnp.exp(l1 - m), jnp.exp(l2 - m)
        oc = (e0 * oc0_ref[...] + e1 * oc1_ref[...] + e2 * oc2_ref[...]) / (e0 + e1 + e2)
        wo = wo_ref.at[0]
        y = jnp.dot(oa_ref[...], wo[0:A_W, :], preferred_element_type=jnp.float32)
        y += jnp.dot(ob_ref[...], wo[A_W:A_W + B_W, :], preferred_element_type=jnp.float32)
        y += jnp.dot(oc.astype(jnp.bfloat16), wo[A_W + B_W:, :],
                     preferred_element_type=jnp.float32)
        x1 = x_ref[...] + y
        acc_ref[...] = x1
        h_ref[...] = _rms(x1, g_ref[0]).astype(jnp.bfloat16)

    u = jnp.dot(h_ref[...], w1_ref[0], preferred_element_type=jnp.float32)
    u = jnp.square(jnp.maximum(u, 0.0)).astype(jnp.bfloat16)
    acc_ref[...] += jnp.dot(u, w2_ref[0], preferred_element_type=jnp.float32)

    @pl.when(f == pl.num_programs(1) - 1)
    def _():
        out_ref[...] = acc_ref[...]


def _mix_mlp(x2, oa, ob, ocs, lses, w_o, norm_g, w_1, w_2, layer, *, tm=512, tf=1024):
    n_tok = x2.shape[0]
    row = lambda w: pl.BlockSpec((tm, w), lambda i, f: (i, 0))
    return pl.pallas_call(
        _mix_mlp_kernel,
        out_shape=jax.ShapeDtypeStruct((n_tok, D_MODEL), jnp.float32),
        grid=(n_tok // tm, D_FF // tf),
        in_specs=[row(D_MODEL), row(A_W), row(B_W)] + [row(C_W)] * 6 + [
            pl.BlockSpec((1, D_MODEL, D_MODEL), lambda i, f: (layer, 0, 0)),
            pl.BlockSpec((1, 1, D_MODEL), lambda i, f: (layer, 0, 0)),
            pl.BlockSpec((1, D_MODEL, tf), lambda i, f: (layer, 0, f)),
            pl.BlockSpec((1, tf, D_MODEL), lambda i, f: (layer, f, 0)),
        ],
        out_specs=row(D_MODEL),
        scratch_shapes=[pltpu.VMEM((tm, D_MODEL), jnp.float32),
                        pltpu.VMEM((tm, D_MODEL), jnp.bfloat16)],
        compiler_params=pltpu.CompilerParams(dimension_semantics=("parallel", "arbitrary"),
                                             vmem_limit_bytes=VMEM_LIMIT),
        name="mix_mlp",
    )(x2, oa, ob, *ocs, *lses, w_o, norm_g, w_1, w_2)


def _final_norm_kernel(x_ref, g_ref, o_ref):
    o_ref[...] = _rms(x_ref[...], g_ref[...])


def _final_norm(x2, g, *, tm=1024):
    n_tok = x2.shape[0]
    return pl.pallas_call(
        _final_norm_kernel,
        out_shape=jax.ShapeDtypeStruct(x2.shape, jnp.float32),
        grid=(n_tok // tm,),
        in_specs=[pl.BlockSpec((tm, D_MODEL), lambda i: (i, 0)),
                  pl.BlockSpec((1, D_MODEL), lambda i: (0, 0))],
        out_specs=pl.BlockSpec((tm, D_MODEL), lambda i: (i, 0)),
        compiler_params=pltpu.CompilerParams(dimension_semantics=("parallel",)),
        name="final_norm",
    )(x2, g)


def _prep_w_in(w_in):
    a0, b0, f0, c0 = 0, 3 * A_W, 3 * A_W + 3 * B_W, 3 * A_W + 3 * B_W + B_HEADS
    sa, sb = A_QK_DIM ** -0.5, HEAD_DIM ** -0.5
    cw = C_GROUPS * C_W
    w_tm = jnp.concatenate([
        w_in[:, :, a0 + A_W:a0 + 2 * A_W], w_in[:, :, b0 + B_W:b0 + 2 * B_W],
        w_in[:, :, c0:c0 + cw] * sb, w_in[:, :, c0 + cw:],
    ], axis=-1).astype(jnp.bfloat16)
    w_fm = jnp.concatenate([
        w_in[:, :, a0:a0 + A_W] * (sa * LOG2E), w_in[:, :, b0:b0 + B_W] * (sb * LOG2E),
        w_in[:, :, a0 + 2 * A_W:b0], w_in[:, :, b0 + 2 * B_W:f0],
    ], axis=-1)
    w_fm = jnp.transpose(w_fm, (0, 2, 1)).astype(jnp.bfloat16)
    w_f = jnp.transpose(w_in[:, :, f0:c0], (0, 2, 1))
    w_f = jnp.pad(w_f, ((0, 0), (0, F_ROWS - B_HEADS), (0, 0))).astype(jnp.bfloat16)
    return w_tm, w_fm, w_f


def kernel(x, norm1_g, w_in, b_f, lam_q1, lam_k1, lam_q2, lam_k2, diff_norm_g, w_o, norm2_g,
           w_1, w_2, rel_bias, final_g):
    bsz, seq, _ = x.shape
    depth = w_in.shape[0]
    w_tm, w_fm, w_f = _prep_w_in(w_in)
    w_o16, w_116, w_216 = (w.astype(jnp.bfloat16) for w in (w_o, w_1, w_2))
    lam_vecs = jnp.stack([lam_q1, lam_k1, lam_q2, lam_k2], axis=1)
    bias_a, bias_c = _make_bias_tiles(rel_bias)

    x2 = x.reshape(bsz * seq, D_MODEL)
    for l in range(depth):
        proj_tm, proj_fm, ft = _in_proj(x2, norm1_g[:, None, :], w_tm, w_fm, w_f, l)
        proj3 = proj_tm.reshape(bsz, seq, N_TM)
        ckp = _decay(ft, b_f[:, :, None], l, bsz, seq)
        lam_init = jnp.full((1,), 0.8 - 0.6 * math.exp(-0.3 * l), jnp.float32)
        oa = _attn_a(proj3, proj_fm, bias_a, lam_vecs, lam_init, diff_norm_g[:, :, None], l)
        ob = _attn_b(proj3, proj_fm, ckp)
        ocs, lses = [], []
        for g, (_, dil) in enumerate(C_PAIRS):
            sl = lambda off: proj3[:, :, off + g * C_W: off + (g + 1) * C_W]
            o_g, lse_g = _attn_c(_deinterleave(sl(QC), dil, 0), _deinterleave(sl(KC), dil, C_TILE),
                                 _deinterleave(sl(VC), dil, C_TILE), bias_c, g)
            ocs.append(_interleave(o_g).reshape(bsz * seq, C_W))
            lses.append(_interleave(lse_g).reshape(bsz * seq, C_W))
        x2 = _mix_mlp(x2, oa.reshape(bsz * seq, A_W), ob.reshape(bsz * seq, B_W), ocs, lses,
                      w_o16, norm2_g[:, None, :], w_116, w_216, l)
    out = _final_norm(x2, final_g[None, :])
    return out.reshape(bsz, seq, D_MODEL)
```

```python
import functools
import math

import jax
import jax.numpy as jnp
from jax import lax
from jax.experimental import pallas as pl
from jax.experimental.pallas import tpu as pltpu

D_MODEL = 1024
HEAD_DIM = 64
A_HEADS = 4
A_QK_DIM = 32
B_HEADS = 8
C_HEADS = 4
C_PAIRS = ((128, 1), (512, 4), (2048, 16))
C_GROUPS = 3
A_W = 256
B_W = 512
C_W = 256
D_FF = 4096
N_BUCKETS = 32
MAX_DISTANCE = 2048
NORM_EPS = 1e-6
NEG_INF = -1e30
LOG2E = 1.4426950408889634

LANES = 128
VMEM_LIMIT = 56 * 1024 * 1024

KA, KB = 0, 256
N_TM = 768
QA_T, QB_T, VA_T, VB_T = 0, 256, 768, 1024
N_FM = 1536
N_C = 2304
N_C_SLABS = N_C // 128
F_ROWS = 16
ONES_ROWS = 16
ACC_ROWS = HEAD_DIM + ONES_ROWS

C_TILE = 128
C_UNROLL = 8
FLASH_TILE = 256


def _bucket_thresholds():
    max_exact = N_BUCKETS // 2
    out = []
    for k in range(1, N_BUCKETS - max_exact):
        t = max_exact * (MAX_DISTANCE / max_exact) ** (k / (N_BUCKETS - max_exact))
        out.append(int(math.ceil(t)))
    return tuple(out)


_THRESHOLDS = _bucket_thresholds()


def _n_near_tiles(tile):
    return -(-(_THRESHOLDS[-1] + tile - 1) // tile)


def _t5_bucket(d):
    big = jnp.full(d.shape, N_BUCKETS // 2, jnp.int32)
    for t in _THRESHOLDS:
        big = big + (d >= t).astype(jnp.int32)
    return jnp.where(d < N_BUCKETS // 2, d, big)


def _bias_lookup(rb_ref, bucket, col):
    val = jnp.zeros(bucket.shape, jnp.float32)
    for b in range(N_BUCKETS):
        val = jnp.where(bucket == b, rb_ref[b, col], val)
    return val


def _bias_a_kernel(rb_ref, out_ref, *, tile):
    delta = pl.program_id(0)
    j = lax.broadcasted_iota(jnp.int32, (tile, tile), 0)
    i = lax.broadcasted_iota(jnp.int32, (tile, tile), 1)
    d = delta * tile + i - j
    bucket = _t5_bucket(d)
    for h in range(A_HEADS):
        val = _bias_lookup(rb_ref, bucket, h) * LOG2E
        out_ref[h, 0] = jnp.where(d >= 0, val, NEG_INF)


def _bias_c_kernel(rb_ref, out_ref):
    g = pl.program_id(0)
    dil = jnp.where(g == 0, C_PAIRS[0][1], jnp.where(g == 1, C_PAIRS[1][1], C_PAIRS[2][1]))
    c = lax.broadcasted_iota(jnp.int32, (2 * C_TILE, C_TILE), 0)
    i = lax.broadcasted_iota(jnp.int32, (2 * C_TILE, C_TILE), 1)
    steps = i + C_TILE - c
    valid = (steps >= 0) & (steps <= C_TILE)
    bucket = _t5_bucket(steps * dil)
    for h in range(C_HEADS):
        val = _bias_lookup(rb_ref, bucket, A_HEADS + g * C_HEADS + h) * LOG2E
        out_ref[h, 1] = jnp.where(valid, val, NEG_INF)
        out_ref[h, 0] = jnp.where(valid & (c >= C_TILE), val, NEG_INF)


def _make_bias_tiles(rel_bias):
    tile = FLASH_TILE
    n_bias = _n_near_tiles(tile) + 1
    smem = pl.BlockSpec(memory_space=pltpu.SMEM)
    bias_a = pl.pallas_call(
        functools.partial(_bias_a_kernel, tile=tile),
        out_shape=jax.ShapeDtypeStruct((A_HEADS, n_bias, tile, tile), jnp.float32),
        grid=(n_bias,),
        in_specs=[smem],
        out_specs=pl.BlockSpec((A_HEADS, 1, tile, tile), lambda d: (0, d, 0, 0)),
        name="bias_a_tiles",
    )(rel_bias)
    bias_c = pl.pallas_call(
        _bias_c_kernel,
        out_shape=jax.ShapeDtypeStruct((C_GROUPS * C_HEADS, 2, 2 * C_TILE, C_TILE), jnp.float32),
        grid=(C_GROUPS,),
        in_specs=[smem],
        out_specs=pl.BlockSpec((C_HEADS, 2, 2 * C_TILE, C_TILE), lambda g: (g, 0, 0, 0)),
        name="bias_c_tiles",
    )(rel_bias)
    return bias_a, bias_c


def _rms(x, g):
    return x * lax.rsqrt(jnp.mean(x * x, axis=-1, keepdims=True) + NORM_EPS) * g


def _dot_nt(a, b):
    return lax.dot_general(a, b, (((1,), (1,)), ((), ())), preferred_element_type=jnp.float32)


def _in_proj_kernel(x_ref, g_ref, wtm_ref, wfm_ref, wf_ref, wc_ref, tm_ref, fm_ref, ft_ref, c_ref):
    h = _rms(x_ref[...], g_ref[0]).astype(jnp.bfloat16)
    tm_ref[...] = jnp.dot(h, wtm_ref[0], preferred_element_type=jnp.float32).astype(jnp.bfloat16)
    for c in range(N_FM // 512):
        rows = slice(c * 512, (c + 1) * 512)
        fm_ref[rows, :] = _dot_nt(wfm_ref[0, rows, :], h).astype(jnp.bfloat16)
    ft_ref[...] = _dot_nt(wf_ref[0], h)
    for c in range(N_C // 256):
        pc = jnp.dot(h, wc_ref[0, :, c * 256:(c + 1) * 256], preferred_element_type=jnp.float32)
        c_ref[2 * c] = pc[:, :LANES]
        c_ref[2 * c + 1] = pc[:, LANES:]


def _in_proj(x2, norm_g, w_tm, w_fm, w_f, w_c, layer, *, tm=512):
    n_tok = x2.shape[0]
    return pl.pallas_call(
        _in_proj_kernel,
        out_shape=(jax.ShapeDtypeStruct((n_tok, N_TM), jnp.bfloat16),
                   jax.ShapeDtypeStruct((N_FM, n_tok), jnp.bfloat16),
                   jax.ShapeDtypeStruct((F_ROWS, n_tok), jnp.float32),
                   jax.ShapeDtypeStruct((N_C_SLABS, n_tok, LANES), jnp.float32)),
        grid=(n_tok // tm,),
        in_specs=[
            pl.BlockSpec((tm, D_MODEL), lambda i: (i, 0)),
            pl.BlockSpec((1, 1, D_MODEL), lambda i: (layer, 0, 0)),
            pl.BlockSpec((1, D_MODEL, N_TM), lambda i: (layer, 0, 0)),
            pl.BlockSpec((1, N_FM, D_MODEL), lambda i: (layer, 0, 0)),
            pl.BlockSpec((1, F_ROWS, D_MODEL), lambda i: (layer, 0, 0)),
            pl.BlockSpec((1, D_MODEL, N_C), lambda i: (layer, 0, 0)),
        ],
        out_specs=(pl.BlockSpec((tm, N_TM), lambda i: (i, 0)),
                   pl.BlockSpec((N_FM, tm), lambda i: (0, i)),
                   pl.BlockSpec((F_ROWS, tm), lambda i: (0, i)),
                   pl.BlockSpec((N_C_SLABS, tm, LANES), lambda i: (0, i, 0))),
        compiler_params=pltpu.CompilerParams(dimension_semantics=("parallel",),
                                             vmem_limit_bytes=VMEM_LIMIT),
        name="in_proj",
    )(x2, norm_g, w_tm, w_fm, w_f, w_c)


def _decay_kernel(ft_ref, bf_ref, ckp_ref):
    z = ft_ref[0:B_HEADS, :] + bf_ref[0]
    x = jnp.minimum(z, 0.0) - jnp.log1p(jnp.exp(-jnp.abs(z)))
    seq = x.shape[1]
    pos = lax.broadcasted_iota(jnp.int32, x.shape, 1)
    shift = 1
    while shift < seq:
        x = x + jnp.where(pos >= shift, pltpu.roll(x, shift, 1), 0.0)
        shift *= 2
    c = x * (-LOG2E)
    hi = c.astype(jnp.bfloat16).astype(jnp.float32)
    mid = (c - hi).astype(jnp.bfloat16).astype(jnp.float32)
    lo = c - hi - mid
    stacked = jnp.concatenate(
        [hi, mid, lo, jnp.zeros((LANES - 3 * B_HEADS, seq), jnp.float32)], axis=0)
    ckp_ref[0] = jnp.transpose(stacked).astype(jnp.bfloat16)


def _decay(ft, b_f, layer, bsz, seq):
    return pl.pallas_call(
        _decay_kernel,
        out_shape=jax.ShapeDtypeStruct((bsz, seq, LANES), jnp.bfloat16),
        grid=(bsz,),
        in_specs=[pl.BlockSpec((F_ROWS, seq), lambda b: (0, b)),
                  pl.BlockSpec((1, B_HEADS, 1), lambda b: (layer, 0, 0))],
        out_specs=pl.BlockSpec((1, seq, LANES), lambda b: (b, 0, 0)),
        compiler_params=pltpu.CompilerParams(dimension_semantics=("parallel",)),
        name="decay_scan",
    )(ft, b_f)


def _colmax(s):
    rows = s.shape[0]
    while rows > 8:
        rows //= 2
        s = jnp.maximum(s[:rows], s[rows:2 * rows])
    return jnp.max(s, axis=0, keepdims=True)


def _online_step(s, smax, v_aug, m_ref, acc_ref, idx):
    m_old = m_ref[idx]
    m_new = jnp.maximum(m_old, smax)
    alpha = jnp.exp2(m_old - m_new)
    p = jnp.exp2(s - m_new).astype(jnp.bfloat16)
    acc_ref[idx] = alpha * acc_ref[idx] + jnp.dot(v_aug, p, preferred_element_type=jnp.float32)
    m_ref[idx] = m_new


def _v_aug(vt_ref, head, k0, tile):
    v = vt_ref[head * HEAD_DIM:(head + 1) * HEAD_DIM, pl.ds(k0, tile)]
    return jnp.concatenate([v, jnp.ones((ONES_ROWS, tile), jnp.bfloat16)], axis=0)


def _flash_pipeline(qi, produce, consume):
    produce(0, 0)

    def pair(j, carry):
        k = 2 * j
        produce(k + 1, 1)
        consume(k, 0, False)
        produce(k + 2, 0)
        consume(k + 1, 1, False)
        return carry

    lax.fori_loop(0, qi // 2, pair, 0)

    @pl.when(qi % 2 == 0)
    def _():
        consume(qi, 0, True)

    @pl.when(qi % 2 == 1)
    def _():
        produce(qi, 1)
        consume(qi - 1, 0, False)
        consume(qi, 1, True)


def _attn_a_kernel(qt_ref, k_ref, vt_ref, bias_ref, lamv_ref, lami_ref, subg_ref, o_ref,
                   m_ref, acc_ref, s_ref, smax_ref, *, tile, n_near):
    qi = pl.program_id(2)
    qt = qt_ref[...].astype(jnp.float32)
    row = lax.broadcasted_iota(jnp.int32, qt.shape, 0)
    qs = [jnp.where((row >= s * A_QK_DIM) & (row < (s + 1) * A_QK_DIM), qt, 0.0)
          .astype(jnp.bfloat16) for s in range(4)]
    m_ref[...] = jnp.full(m_ref.shape, NEG_INF, jnp.float32)
    acc_ref[...] = jnp.zeros(acc_ref.shape, jnp.float32)

    def produce(ki, slot):
        k = k_ref[0, pl.ds(pl.multiple_of(ki * tile, tile), tile), :]
        delta = jnp.minimum(qi - ki, n_near)
        for s in range(4):
            x = jnp.dot(k, qs[s], preferred_element_type=jnp.float32) + bias_ref[s // 2, delta]
            s_ref[slot, s] = x
            smax_ref[slot, s] = _colmax(x)

    def consume(ki, slot, last):
        k0 = pl.multiple_of(ki * tile, tile)
        for s in range(4):
            _online_step(s_ref[slot, s], smax_ref[slot, s], _v_aug(vt_ref, s // 2, k0, tile),
                         m_ref, acc_ref, s)

    _flash_pipeline(qi, produce, consume)

    lam_init = lami_ref[0]
    lv = lamv_ref[0]
    e1 = jnp.exp(jnp.sum(lv[0:1] * lv[1:2], axis=1, keepdims=True))
    e2 = jnp.exp(jnp.sum(lv[2:3] * lv[3:4], axis=1, keepdims=True))
    lam = e1 - e2 + lam_init
    heads = []
    for h in range(2):
        a0, a1 = acc_ref[2 * h], acc_ref[2 * h + 1]
        o = (a0[0:HEAD_DIM] / a0[HEAD_DIM:HEAD_DIM + 1]
             - lam * (a1[0:HEAD_DIM] / a1[HEAD_DIM:HEAD_DIM + 1]))
        ms = jnp.mean(o * o, axis=0, keepdims=True)
        heads.append(o * lax.rsqrt(ms + NORM_EPS) * subg_ref[0] * (1.0 - lam_init))
    o_ref[0] = jnp.transpose(jnp.concatenate(heads, axis=0)).astype(o_ref.dtype)


def _attn_a(proj_tm3, proj_fm, bias_a, lam_vecs, lam_init, sub_g, layer):
    bsz, seq, _ = proj_tm3.shape
    tile = FLASH_TILE
    n_near = _n_near_tiles(tile)
    n_q = seq // tile
    return pl.pallas_call(
        functools.partial(_attn_a_kernel, tile=tile, n_near=n_near),
        out_shape=jax.ShapeDtypeStruct((bsz, seq, A_W), jnp.bfloat16),
        grid=(A_HEADS // 2, bsz, n_q),
        in_specs=[
            pl.BlockSpec((LANES, tile), lambda hp, b, qi: (QA_T // LANES + hp, b * n_q + qi)),
            pl.BlockSpec((1, seq, LANES), lambda hp, b, qi: (b, 0, KA // LANES + hp)),
            pl.BlockSpec((LANES, seq), lambda hp, b, qi: (VA_T // LANES + hp, b)),
            pl.BlockSpec((2, n_near + 1, tile, tile), lambda hp, b, qi: (hp, 0, 0, 0)),
            pl.BlockSpec((1, 4, A_QK_DIM), lambda hp, b, qi: (layer, 0, 0)),
            pl.BlockSpec(memory_space=pltpu.SMEM),
            pl.BlockSpec((1, HEAD_DIM, 1), lambda hp, b, qi: (layer, 0, 0)),
        ],
        out_specs=pl.BlockSpec((1, tile, LANES), lambda hp, b, qi: (b, qi, hp)),
        scratch_shapes=[pltpu.VMEM((4, 1, tile), jnp.float32),
                        pltpu.VMEM((4, ACC_ROWS, tile), jnp.float32),
                        pltpu.VMEM((2, 4, tile, tile), jnp.float32),
                        pltpu.VMEM((2, 4, 1, tile), jnp.float32)],
        compiler_params=pltpu.CompilerParams(
            dimension_semantics=("parallel", "parallel", "parallel"),
            vmem_limit_bytes=VMEM_LIMIT),
        name="attn_a",
    )(proj_fm, proj_tm3, proj_fm, bias_a, lam_vecs, lam_init, sub_g)


B_STEP_HEADS = 4


def _attn_b_kernel(qt_ref, k_ref, ckp_ref, vt_ref, o_ref, m_ref, acc_ref, s_ref, smax_ref, *,
                   tile):
    hg = pl.program_id(0)
    qi = pl.program_id(2)
    n_h = B_STEP_HEADS
    qt = qt_ref[...].astype(jnp.float32)
    row = lax.broadcasted_iota(jnp.int32, (LANES, tile), 0)
    q_aug = []
    for s in range(n_h):
        blk = qt[(s // 2) * LANES:(s // 2 + 1) * LANES]
        lo = (s % 2) * HEAD_DIM
        q_s = jnp.where((row >= lo) & (row < lo + HEAD_DIM), blk, 0.0)
        pick = (row < 3 * B_HEADS) & ((row % B_HEADS) == n_h * hg + s)
        ones = jnp.where(pick, 1.0, 0.0)
        q_aug.append(jnp.concatenate([q_s, ones], axis=0).astype(jnp.bfloat16))
    m_ref[...] = jnp.full(m_ref.shape, NEG_INF, jnp.float32)
    acc_ref[...] = jnp.zeros(acc_ref.shape, jnp.float32)

    def produce(ki, slot):
        k0 = pl.multiple_of(ki * tile, tile)
        ck = ckp_ref[0, pl.ds(k0, tile), :]
        for s in range(n_h):
            k = k_ref[0, pl.ds(k0, tile), (s // 2) * LANES:(s // 2 + 1) * LANES]
            x = jnp.dot(jnp.concatenate([k, ck], axis=1), q_aug[s],
                        preferred_element_type=jnp.float32)
            s_ref[slot, s] = x
            smax_ref[slot, s] = _colmax(x)

    def consume(ki, slot, last):
        k0 = pl.multiple_of(ki * tile, tile)
        if last:
            kk = lax.broadcasted_iota(jnp.int32, (tile, tile), 0)
            qq = lax.broadcasted_iota(jnp.int32, (tile, tile), 1)
            keep = kk <= qq
        for s in range(n_h):
            x, xmax = s_ref[slot, s], smax_ref[slot, s]
            if last:
                x = jnp.where(keep, x, NEG_INF)
                xmax = _colmax(x)
            _online_step(x, xmax, _v_aug(vt_ref, s, k0, tile), m_ref, acc_ref, s)

    _flash_pipeline(qi, produce, consume)
    heads = [acc_ref[s][0:HEAD_DIM] / acc_ref[s][HEAD_DIM:HEAD_DIM + 1] for s in range(n_h)]
    o_ref[0] = jnp.transpose(jnp.concatenate(heads, axis=0)).astype(o_ref.dtype)


def _attn_b(proj_tm3, proj_fm, ckp):
    bsz, seq, _ = proj_tm3.shape
    tile = FLASH_TILE
    n_q = seq // tile
    width = B_STEP_HEADS * HEAD_DIM
    return pl.pallas_call(
        functools.partial(_attn_b_kernel, tile=tile),
        out_shape=jax.ShapeDtypeStruct((bsz, seq, B_W), jnp.bfloat16),
        grid=(B_HEADS // B_STEP_HEADS, bsz, n_q),
        in_specs=[
            pl.BlockSpec((width, tile), lambda hg, b, qi: (QB_T // width + hg, b * n_q + qi)),
            pl.BlockSpec((1, seq, width), lambda hg, b, qi: (b, 0, KB // width + hg)),
            pl.BlockSpec((1, seq, LANES), lambda hg, b, qi: (b, 0, 0)),
            pl.BlockSpec((width, seq), lambda hg, b, qi: (VB_T // width + hg, b)),
        ],
        out_specs=pl.BlockSpec((1, tile, width), lambda hg, b, qi: (b, qi, hg)),
        scratch_shapes=[pltpu.VMEM((B_STEP_HEADS, 1, tile), jnp.float32),
                        pltpu.VMEM((B_STEP_HEADS, ACC_ROWS, tile), jnp.float32),
                        pltpu.VMEM((2, B_STEP_HEADS, tile, tile), jnp.float32),
                        pltpu.VMEM((2, B_STEP_HEADS, 1, tile), jnp.float32)],
        compiler_params=pltpu.CompilerParams(
            dimension_semantics=("parallel", "parallel", "parallel"),
            vmem_limit_bytes=VMEM_LIMIT),
        name="attn_b",
    )(proj_fm, proj_tm3, ckp, proj_fm)


def _attn_c_kernel(q_ref, k_ref, v_ref, bias_ref, o_ref, qd_ref, kd_ref, vtd_ref, onat_ref,
                   lnat_ref, *, seq):
    g = pl.program_id(2)
    n_chunks = seq // C_TILE
    lane = lax.broadcasted_iota(jnp.int32, (1, LANES), 1)
    first = lane < HEAD_DIM

    def run_group(group, dil):
        n_tiles = seq // dil // C_TILE

        def src_rows(i):
            r, t = i // n_tiles, i % n_tiles
            return pl.ds(r + t * (C_TILE * dil), C_TILE, stride=dil), t

        kd_ref[0:C_TILE, :] = jnp.zeros((C_TILE, LANES), jnp.bfloat16)
        vtd_ref[:, 0:C_TILE] = jnp.zeros((LANES, C_TILE), jnp.bfloat16)

        def gather(j, carry):
            ids = [j * C_UNROLL + u for u in range(C_UNROLL)]
            vts = [jnp.transpose(v_ref[src_rows(i)[0], :]) for i in ids]
            for i, vt in zip(ids, vts):
                rows, _ = src_rows(i)
                dst = pl.multiple_of(i * C_TILE, C_TILE)
                qd_ref[pl.ds(dst, C_TILE), :] = q_ref[rows, :].astype(jnp.bfloat16)
                kd_ref[pl.ds(dst + C_TILE, C_TILE), :] = k_ref[rows, :].astype(jnp.bfloat16)
                vtd_ref[:, pl.ds(dst + C_TILE, C_TILE)] = vt.astype(jnp.bfloat16)
            return carry

        lax.fori_loop(0, n_chunks // C_UNROLL, gather, 0)

        def tiles(j, carry):
            ids = [j * C_UNROLL + u for u in range(C_UNROLL)]
            row0s = [pl.multiple_of(i * C_TILE, C_TILE) for i in ids]
            scores = []
            for row0 in row0s:
                q = qd_ref[pl.ds(row0, C_TILE), :]
                q2 = jnp.concatenate([jnp.where(first, q, jnp.zeros_like(q)),
                                      jnp.where(first, jnp.zeros_like(q), q)], axis=0)
                scores.append(_dot_nt(kd_ref[pl.ds(row0, 2 * C_TILE), :], q2))
            probs = []
            for i, st in zip(ids, scores):
                ver = jnp.minimum(i % n_tiles, 1)
                for h in range(2):
                    x = st[:, h * C_TILE:(h + 1) * C_TILE] + bias_ref[h, ver]
                    m = _colmax(x)
                    probs.append((m, jnp.exp2(x - m).astype(jnp.bfloat16)))
            pvs = []
            for u, row0 in enumerate(row0s):
                for h in range(2):
                    v_aug = jnp.concatenate(
                        [vtd_ref[h * HEAD_DIM:(h + 1) * HEAD_DIM, pl.ds(row0, 2 * C_TILE)],
                         jnp.ones((ONES_ROWS, 2 * C_TILE), jnp.bfloat16)], axis=0)
                    pvs.append(jnp.dot(v_aug, probs[2 * u + h][1],
                                       preferred_element_type=jnp.float32))
            for u, i in enumerate(ids):
                outs, lses = [], []
                for h in range(2):
                    pv, m = pvs[2 * u + h], probs[2 * u + h][0]
                    l = pv[HEAD_DIM:HEAD_DIM + 1]
                    outs.append(pv[0:HEAD_DIM] / l)
                    lses.append(jnp.broadcast_to(m + jnp.log2(l), (HEAD_DIM, C_TILE)))
                rows, _ = src_rows(i)
                onat_ref[group, rows, :] = jnp.transpose(jnp.concatenate(outs, axis=0))
                lnat_ref[group, rows, :] = jnp.transpose(jnp.concatenate(lses, axis=0))
            return carry

        lax.fori_loop(0, n_chunks // C_UNROLL, tiles, 0)

    for group, (_, dil) in enumerate(C_PAIRS):
        pl.when(g == group)(functools.partial(run_group, group, dil))

    @pl.when(g == C_GROUPS - 1)
    def _():
        rows_per = 512

        def combine(c, carry):
            rows = pl.ds(pl.multiple_of(c * rows_per, rows_per), rows_per)
            l0, l1, l2 = lnat_ref[0, rows, :], lnat_ref[1, rows, :], lnat_ref[2, rows, :]
            m = jnp.maximum(jnp.maximum(l0, l1), l2)
            e0, e1, e2 = jnp.exp2(l0 - m), jnp.exp2(l1 - m), jnp.exp2(l2 - m)
            num = e0 * onat_ref[0, rows, :] + e1 * onat_ref[1, rows, :] + e2 * onat_ref[2, rows, :]
            o_ref[0, rows, :] = (num / (e0 + e1 + e2)).astype(o_ref.dtype)
            return carry

        lax.fori_loop(0, seq // rows_per, combine, 0)


def _attn_c(proj_c, bias_c, bsz, seq):
    slab = lambda role: pl.BlockSpec(
        (None, seq, LANES), lambda b, hp, g: (role * 2 * C_GROUPS + g * 2 + hp, b, 0))
    return pl.pallas_call(
        functools.partial(_attn_c_kernel, seq=seq),
        out_shape=jax.ShapeDtypeStruct((bsz, seq, C_W), jnp.bfloat16),
        grid=(bsz, C_HEADS // 2, C_GROUPS),
        in_specs=[slab(0), slab(1), slab(2),
                  pl.BlockSpec((2, 2, 2 * C_TILE, C_TILE), lambda b, hp, g: (g * 2 + hp, 0, 0, 0))],
        out_specs=pl.BlockSpec((1, seq, LANES), lambda b, hp, g: (b, 0, hp)),
        scratch_shapes=[pltpu.VMEM((seq, LANES), jnp.bfloat16),
                        pltpu.VMEM((seq + C_TILE, LANES), jnp.bfloat16),
                        pltpu.VMEM((LANES, seq + C_TILE), jnp.bfloat16),
                        pltpu.VMEM((C_GROUPS, seq, LANES), jnp.float32),
                        pltpu.VMEM((C_GROUPS, seq, LANES), jnp.float32)],
        compiler_params=pltpu.CompilerParams(
            dimension_semantics=("parallel", "parallel", "arbitrary"),
            vmem_limit_bytes=VMEM_LIMIT),
        name="attn_c",
    )(proj_c, proj_c, proj_c, bias_c)


def _mix_mlp_kernel(x_ref, oa_ref, ob_ref, oc_ref, wo_ref, g_ref, w1_ref, w2_ref, out_ref,
                    acc_ref, h_ref):
    f = pl.program_id(1)

    @pl.when(f == 0)
    def _():
        wo = wo_ref.at[0]
        y = jnp.dot(oa_ref[...], wo[0:A_W, :], preferred_element_type=jnp.float32)
        y += jnp.dot(ob_ref[...], wo[A_W:A_W + B_W, :], preferred_element_type=jnp.float32)
        y += jnp.dot(oc_ref[...], wo[A_W + B_W:, :], preferred_element_type=jnp.float32)
        x1 = x_ref[...] + y
        acc_ref[...] = x1
        h_ref[...] = _rms(x1, g_ref[0]).astype(jnp.bfloat16)

    u = jnp.dot(h_ref[...], w1_ref[0], preferred_element_type=jnp.float32)
    u = jnp.square(jnp.maximum(u, 0.0)).astype(jnp.bfloat16)
    acc_ref[...] += jnp.dot(u, w2_ref[0], preferred_element_type=jnp.float32)

    @pl.when(f == pl.num_programs(1) - 1)
    def _():
        out_ref[...] = acc_ref[...]


def _mix_mlp(x2, oa, ob, oc, w_o, norm_g, w_1, w_2, layer, *, tm=1024, tf=512):
    n_tok = x2.shape[0]
    row = lambda w: pl.BlockSpec((tm, w), lambda i, f: (i, 0))
    return pl.pallas_call(
        _mix_mlp_kernel,
        out_shape=jax.ShapeDtypeStruct((n_tok, D_MODEL), jnp.float32),
        grid=(n_tok // tm, D_FF // tf),
        in_specs=[row(D_MODEL), row(A_W), row(B_W), row(C_W),
                  pl.BlockSpec((1, D_MODEL, D_MODEL), lambda i, f: (layer, 0, 0)),
                  pl.BlockSpec((1, 1, D_MODEL), lambda i, f: (layer, 0, 0)),
                  pl.BlockSpec((1, D_MODEL, tf), lambda i, f: (layer, 0, f)),
                  pl.BlockSpec((1, tf, D_MODEL), lambda i, f: (layer, f, 0))],
        out_specs=row(D_MODEL),
        scratch_shapes=[pltpu.VMEM((tm, D_MODEL), jnp.float32),
                        pltpu.VMEM((tm, D_MODEL), jnp.bfloat16)],
        compiler_params=pltpu.CompilerParams(dimension_semantics=("parallel", "arbitrary"),
                                             vmem_limit_bytes=VMEM_LIMIT),
        name="mix_mlp",
    )(x2, oa, ob, oc, w_o, norm_g, w_1, w_2)


def _final_norm_kernel(x_ref, g_ref, o_ref):
    o_ref[...] = _rms(x_ref[...], g_ref[...])


def _final_norm(x2, g, *, tm=1024):
    n_tok = x2.shape[0]
    return pl.pallas_call(
        _final_norm_kernel,
        out_shape=jax.ShapeDtypeStruct(x2.shape, jnp.float32),
        grid=(n_tok // tm,),
        in_specs=[pl.BlockSpec((tm, D_MODEL), lambda i: (i, 0)),
                  pl.BlockSpec((1, D_MODEL), lambda i: (0, 0))],
        out_specs=pl.BlockSpec((tm, D_MODEL), lambda i: (i, 0)),
        compiler_params=pltpu.CompilerParams(dimension_semantics=("parallel",)),
        name="final_norm",
    )(x2, g)


def _prep_w_in(w_in):
    a0, b0, f0, c0 = 0, 3 * A_W, 3 * A_W + 3 * B_W, 3 * A_W + 3 * B_W + B_HEADS
    sa, sb = A_QK_DIM ** -0.5 * LOG2E, HEAD_DIM ** -0.5 * LOG2E
    cw = C_GROUPS * C_W
    w_tm = jnp.concatenate([w_in[:, :, a0 + A_W:a0 + 2 * A_W],
                            w_in[:, :, b0 + B_W:b0 + 2 * B_W]], axis=-1).astype(jnp.bfloat16)
    w_fm = jnp.concatenate([
        w_in[:, :, a0:a0 + A_W] * sa, w_in[:, :, b0:b0 + B_W] * sb,
        w_in[:, :, a0 + 2 * A_W:b0], w_in[:, :, b0 + 2 * B_W:f0],
    ], axis=-1)
    w_fm = jnp.transpose(w_fm, (0, 2, 1)).astype(jnp.bfloat16)
    w_f = jnp.transpose(w_in[:, :, f0:c0], (0, 2, 1))
    w_f = jnp.pad(w_f, ((0, 0), (0, F_ROWS - B_HEADS), (0, 0))).astype(jnp.bfloat16)
    w_c = jnp.concatenate([w_in[:, :, c0:c0 + cw] * sb, w_in[:, :, c0 + cw:]],
                          axis=-1).astype(jnp.bfloat16)
    return w_tm, w_fm, w_f, w_c


def kernel(x, norm1_g, w_in, b_f, lam_q1, lam_k1, lam_q2, lam_k2, diff_norm_g, w_o, norm2_g,
           w_1, w_2, rel_bias, final_g):
    bsz, seq, _ = x.shape
    depth = w_in.shape[0]
    w_tm, w_fm, w_f, w_c = _prep_w_in(w_in)
    w_o16, w_116, w_216 = (w.astype(jnp.bfloat16) for w in (w_o, w_1, w_2))
    lam_vecs = jnp.stack([lam_q1, lam_k1, lam_q2, lam_k2], axis=1)
    bias_a, bias_c = _make_bias_tiles(rel_bias)

    x2 = x.reshape(bsz * seq, D_MODEL)
    for l in range(depth):
        proj_tm, proj_fm, ft, proj_c = _in_proj(x2, norm1_g[:, None, :], w_tm, w_fm, w_f, w_c, l)
        proj3 = proj_tm.reshape(bsz, seq, N_TM)
        ckp = _decay(ft, b_f[:, :, None], l, bsz, seq)
        lam_init = jnp.full((1,), 0.8 - 0.6 * math.exp(-0.3 * l), jnp.float32)
        oa = _attn_a(proj3, proj_fm, bias_a, lam_vecs, lam_init, diff_norm_g[:, :, None], l)
        ob = _attn_b(proj3, proj_fm, ckp)
        oc = _attn_c(proj_c, bias_c, bsz, seq)
        x2 = _mix_mlp(x2, oa.reshape(bsz * seq, A_W), ob.reshape(bsz * seq, B_W),
                      oc.reshape(bsz * seq, C_W), w_o16, norm2_g[:, None, :], w_116, w_216, l)
    out = _final_norm(x2, final_g[None, :])
    return out.reshape(bsz, seq, D_MODEL)
```

```python
import functools
import math

import jax
import jax.numpy as jnp
from jax import lax
from jax.experimental import pallas as pl
from jax.experimental.pallas import tpu as pltpu

D_MODEL = 1024
HEAD_DIM = 64
A_HEADS = 4
A_QK_DIM = 32
B_HEADS = 8
C_HEADS = 4
C_PAIRS = ((128, 1), (512, 4), (2048, 16))
C_GROUPS = 3
A_W = 256
B_W = 512
C_W = 256
D_FF = 4096
N_BUCKETS = 32
MAX_DISTANCE = 2048
NORM_EPS = 1e-6
NEG_INF = -1e30
LOG2E = 1.4426950408889634

LANES = 128
VMEM_LIMIT = 56 * 1024 * 1024

KB, KA = 0, 512
N_TM = 768
QB_T, VB_T, QA_T, VA_T = 0, 512, 1024, 1280
N_FM = 1536
N_C = 2304
N_C_SLABS = N_C // 128
F_ROWS = 16
ONES_ROWS = 16
ACC_ROWS = HEAD_DIM + ONES_ROWS

C_TILE = 128
C_UNROLL = 8
FLASH_TILE = 256


def _bucket_thresholds():
    max_exact = N_BUCKETS // 2
    out = []
    for k in range(1, N_BUCKETS - max_exact):
        t = max_exact * (MAX_DISTANCE / max_exact) ** (k / (N_BUCKETS - max_exact))
        out.append(int(math.ceil(t)))
    return tuple(out)


_THRESHOLDS = _bucket_thresholds()


def _n_near_tiles(tile):
    return -(-(_THRESHOLDS[-1] + tile - 1) // tile)


def _t5_bucket(d):
    big = jnp.full(d.shape, N_BUCKETS // 2, jnp.int32)
    for t in _THRESHOLDS:
        big = big + (d >= t).astype(jnp.int32)
    return jnp.where(d < N_BUCKETS // 2, d, big)


def _bias_lookup(rb_ref, bucket, col):
    val = jnp.zeros(bucket.shape, jnp.float32)
    for b in range(N_BUCKETS):
        val = jnp.where(bucket == b, rb_ref[b, col], val)
    return val


def _bias_a_kernel(rb_ref, out_ref, *, tile):
    delta = pl.program_id(0)
    j = lax.broadcasted_iota(jnp.int32, (tile, tile), 0)
    i = lax.broadcasted_iota(jnp.int32, (tile, tile), 1)
    d = delta * tile + i - j
    bucket = _t5_bucket(d)
    for h in range(A_HEADS):
        val = _bias_lookup(rb_ref, bucket, h) * LOG2E
        out_ref[h, 0] = jnp.where(d >= 0, val, NEG_INF)


def _bias_c_kernel(rb_ref, out_ref):
    g = pl.program_id(0)
    dil = jnp.where(g == 0, C_PAIRS[0][1], jnp.where(g == 1, C_PAIRS[1][1], C_PAIRS[2][1]))
    c = lax.broadcasted_iota(jnp.int32, (2 * C_TILE, C_TILE), 0)
    i = lax.broadcasted_iota(jnp.int32, (2 * C_TILE, C_TILE), 1)
    steps = i + C_TILE - c
    valid = (steps >= 0) & (steps <= C_TILE)
    bucket = _t5_bucket(steps * dil)
    for h in range(C_HEADS):
        val = _bias_lookup(rb_ref, bucket, A_HEADS + g * C_HEADS + h) * LOG2E
        out_ref[h, 1] = jnp.where(valid, val, NEG_INF)
        out_ref[h, 0] = jnp.where(valid & (c >= C_TILE), val, NEG_INF)


def _make_bias_tiles(rel_bias):
    tile = FLASH_TILE
    n_bias = _n_near_tiles(tile) + 1
    smem = pl.BlockSpec(memory_space=pltpu.SMEM)
    bias_a = pl.pallas_call(
        functools.partial(_bias_a_kernel, tile=tile),
        out_shape=jax.ShapeDtypeStruct((A_HEADS, n_bias, tile, tile), jnp.float32),
        grid=(n_bias,),
        in_specs=[smem],
        out_specs=pl.BlockSpec((A_HEADS, 1, tile, tile), lambda d: (0, d, 0, 0)),
        name="bias_a_tiles",
    )(rel_bias)
    bias_c = pl.pallas_call(
        _bias_c_kernel,
        out_shape=jax.ShapeDtypeStruct((C_GROUPS * C_HEADS, 2, 2 * C_TILE, C_TILE), jnp.float32),
        grid=(C_GROUPS,),
        in_specs=[smem],
        out_specs=pl.BlockSpec((C_HEADS, 2, 2 * C_TILE, C_TILE), lambda g: (g, 0, 0, 0)),
        name="bias_c_tiles",
    )(rel_bias)
    return bias_a, bias_c


def _rms(x, g):
    return x * lax.rsqrt(jnp.mean(x * x, axis=-1, keepdims=True) + NORM_EPS) * g


def _dot_nt(a, b):
    return lax.dot_general(a, b, (((1,), (1,)), ((), ())), preferred_element_type=jnp.float32)


def _in_proj_kernel(x_ref, g_ref, wtm_ref, wfm_ref, wf_ref, wc_ref, tm_ref, fm_ref, ft_ref, c_ref):
    h = _rms(x_ref[...], g_ref[0]).astype(jnp.bfloat16)
    tm_ref[...] = jnp.dot(h, wtm_ref[0], preferred_element_type=jnp.float32).astype(jnp.bfloat16)
    for c in range(N_FM // 512):
        rows = slice(c * 512, (c + 1) * 512)
        fm_ref[rows, :] = _dot_nt(wfm_ref[0, rows, :], h).astype(jnp.bfloat16)
    ft_ref[...] = _dot_nt(wf_ref[0], h)
    for c in range(N_C // 256):
        pc = jnp.dot(h, wc_ref[0, :, c * 256:(c + 1) * 256], preferred_element_type=jnp.float32)
        c_ref[2 * c] = pc[:, :LANES]
        c_ref[2 * c + 1] = pc[:, LANES:]


def _in_proj(x2, norm_g, w_tm, w_fm, w_f, w_c, layer, *, tm=512):
    n_tok = x2.shape[0]
    return pl.pallas_call(
        _in_proj_kernel,
        out_shape=(jax.ShapeDtypeStruct((n_tok, N_TM), jnp.bfloat16),
                   jax.ShapeDtypeStruct((N_FM, n_tok), jnp.bfloat16),
                   jax.ShapeDtypeStruct((F_ROWS, n_tok), jnp.float32),
                   jax.ShapeDtypeStruct((N_C_SLABS, n_tok, LANES), jnp.float32)),
        grid=(n_tok // tm,),
        in_specs=[
            pl.BlockSpec((tm, D_MODEL), lambda i: (i, 0)),
            pl.BlockSpec((1, 1, D_MODEL), lambda i: (layer, 0, 0)),
            pl.BlockSpec((1, D_MODEL, N_TM), lambda i: (layer, 0, 0)),
            pl.BlockSpec((1, N_FM, D_MODEL), lambda i: (layer, 0, 0)),
            pl.BlockSpec((1, F_ROWS, D_MODEL), lambda i: (layer, 0, 0)),
            pl.BlockSpec((1, D_MODEL, N_C), lambda i: (layer, 0, 0)),
        ],
        out_specs=(pl.BlockSpec((tm, N_TM), lambda i: (i, 0)),
                   pl.BlockSpec((N_FM, tm), lambda i: (0, i)),
                   pl.BlockSpec((F_ROWS, tm), lambda i: (0, i)),
                   pl.BlockSpec((N_C_SLABS, tm, LANES), lambda i: (0, i, 0))),
        compiler_params=pltpu.CompilerParams(dimension_semantics=("parallel",),
                                             vmem_limit_bytes=VMEM_LIMIT),
        name="in_proj",
    )(x2, norm_g, w_tm, w_fm, w_f, w_c)


def _decay_kernel(ft_ref, bf_ref, ckp_ref):
    z = ft_ref[0:B_HEADS, :] + bf_ref[0]
    x = jnp.minimum(z, 0.0) - jnp.log1p(jnp.exp(-jnp.abs(z)))
    seq = x.shape[1]
    pos = lax.broadcasted_iota(jnp.int32, x.shape, 1)
    shift = 1
    while shift < seq:
        x = x + jnp.where(pos >= shift, pltpu.roll(x, shift, 1), 0.0)
        shift *= 2
    c = x * (-LOG2E)
    hi = c.astype(jnp.bfloat16).astype(jnp.float32)
    mid = (c - hi).astype(jnp.bfloat16).astype(jnp.float32)
    lo = c - hi - mid
    stacked = jnp.concatenate(
        [hi, mid, lo, jnp.zeros((LANES - 3 * B_HEADS, seq), jnp.float32)], axis=0)
    ckp_ref[0] = jnp.transpose(stacked).astype(jnp.bfloat16)


def _decay(ft, b_f, layer, bsz, seq):
    return pl.pallas_call(
        _decay_kernel,
        out_shape=jax.ShapeDtypeStruct((bsz, seq, LANES), jnp.bfloat16),
        grid=(bsz,),
        in_specs=[pl.BlockSpec((F_ROWS, seq), lambda b: (0, b)),
                  pl.BlockSpec((1, B_HEADS, 1), lambda b: (layer, 0, 0))],
        out_specs=pl.BlockSpec((1, seq, LANES), lambda b: (b, 0, 0)),
        compiler_params=pltpu.CompilerParams(dimension_semantics=("parallel",)),
        name="decay_scan",
    )(ft, b_f)


def _colmax(s):
    rows = s.shape[0]
    while rows > 8:
        rows //= 2
        s = jnp.maximum(s[:rows], s[rows:2 * rows])
    return jnp.max(s, axis=0, keepdims=True)


def _online_step(s, smax, v_aug, m_ref, acc_ref, idx):
    m_old = m_ref[idx]
    m_new = jnp.maximum(m_old, smax)
    alpha = jnp.exp2(m_old - m_new)
    p = jnp.exp2(s - m_new).astype(jnp.bfloat16)
    acc_ref[idx] = alpha * acc_ref[idx] + jnp.dot(v_aug, p, preferred_element_type=jnp.float32)
    m_ref[idx] = m_new


def _v_aug(vt_ref, head, k0, tile):
    v = vt_ref[head * HEAD_DIM:(head + 1) * HEAD_DIM, pl.ds(k0, tile)]
    return jnp.concatenate([v, jnp.ones((ONES_ROWS, tile), jnp.bfloat16)], axis=0)


def _flash_pipeline(qi, produce, consume):
    produce(0, 0)

    def pair(j, carry):
        k = 2 * j
        produce(k + 1, 1)
        consume(k, 0, False)
        produce(k + 2, 0)
        consume(k + 1, 1, False)
        return carry

    lax.fori_loop(0, qi // 2, pair, 0)

    @pl.when(qi % 2 == 0)
    def _():
        consume(qi, 0, True)

    @pl.when(qi % 2 == 1)
    def _():
        produce(qi, 1)
        consume(qi - 1, 0, False)
        consume(qi, 1, True)


def _attn_a_kernel(qt_ref, k_ref, vt_ref, bias_ref, lamv_ref, lami_ref, subg_ref, o_ref,
                   m_ref, acc_ref, s_ref, smax_ref, *, tile, n_near):
    qi = pl.program_id(1)
    n_sub = 2 * A_HEADS
    qt = qt_ref[...].astype(jnp.float32)
    row = lax.broadcasted_iota(jnp.int32, (LANES, tile), 0)
    qs = []
    for s in range(n_sub):
        blk = qt[(s // 4) * LANES:(s // 4 + 1) * LANES]
        lo = (s % 4) * A_QK_DIM
        qs.append(jnp.where((row >= lo) & (row < lo + A_QK_DIM), blk, 0.0).astype(jnp.bfloat16))
    m_ref[...] = jnp.full(m_ref.shape, NEG_INF, jnp.float32)
    acc_ref[...] = jnp.zeros(acc_ref.shape, jnp.float32)

    def produce(ki, slot):
        k0 = pl.multiple_of(ki * tile, tile)
        delta = jnp.minimum(qi - ki, n_near)
        for s in range(n_sub):
            k = k_ref[0, pl.ds(k0, tile), (s // 4) * LANES:(s // 4 + 1) * LANES]
            x = jnp.dot(k, qs[s], preferred_element_type=jnp.float32) + bias_ref[s // 2, delta]
            s_ref[slot, s] = x
            smax_ref[slot, s] = _colmax(x)

    def consume(ki, slot, last):
        k0 = pl.multiple_of(ki * tile, tile)
        for s in range(n_sub):
            _online_step(s_ref[slot, s], smax_ref[slot, s], _v_aug(vt_ref, s // 2, k0, tile),
                         m_ref, acc_ref, s)

    _flash_pipeline(qi, produce, consume)

    lam_init = lami_ref[0]
    lv = lamv_ref[0]
    e1 = jnp.exp(jnp.sum(lv[0:1] * lv[1:2], axis=1, keepdims=True))
    e2 = jnp.exp(jnp.sum(lv[2:3] * lv[3:4], axis=1, keepdims=True))
    lam = e1 - e2 + lam_init
    heads = []
    for h in range(A_HEADS):
        a0, a1 = acc_ref[2 * h], acc_ref[2 * h + 1]
        o = (a0[0:HEAD_DIM] / a0[HEAD_DIM:HEAD_DIM + 1]
             - lam * (a1[0:HEAD_DIM] / a1[HEAD_DIM:HEAD_DIM + 1]))
        ms = jnp.mean(o * o, axis=0, keepdims=True)
        heads.append(o * lax.rsqrt(ms + NORM_EPS) * subg_ref[0] * (1.0 - lam_init))
    o_ref[0] = jnp.transpose(jnp.concatenate(heads, axis=0)).astype(o_ref.dtype)


def _attn_a(proj_tm3, proj_fm, bias_a, lam_vecs, lam_init, sub_g, layer):
    bsz, seq, _ = proj_tm3.shape
    tile = FLASH_TILE
    n_near = _n_near_tiles(tile)
    n_q = seq // tile
    return pl.pallas_call(
        functools.partial(_attn_a_kernel, tile=tile, n_near=n_near),
        out_shape=jax.ShapeDtypeStruct((bsz, seq, A_W), jnp.bfloat16),
        grid=(bsz, n_q),
        in_specs=[
            pl.BlockSpec((A_W, tile), lambda b, qi: (QA_T // A_W, b * n_q + qi)),
            pl.BlockSpec((1, seq, A_W), lambda b, qi: (b, 0, KA // A_W)),
            pl.BlockSpec((A_W, seq), lambda b, qi: (VA_T // A_W, b)),
            pl.BlockSpec((A_HEADS, n_near + 1, tile, tile), lambda b, qi: (0, 0, 0, 0)),
            pl.BlockSpec((1, 4, A_QK_DIM), lambda b, qi: (layer, 0, 0)),
            pl.BlockSpec(memory_space=pltpu.SMEM),
            pl.BlockSpec((1, HEAD_DIM, 1), lambda b, qi: (layer, 0, 0)),
        ],
        out_specs=pl.BlockSpec((1, tile, A_W), lambda b, qi: (b, qi, 0)),
        scratch_shapes=[pltpu.VMEM((2 * A_HEADS, 1, tile), jnp.float32),
                        pltpu.VMEM((2 * A_HEADS, ACC_ROWS, tile), jnp.float32),
                        pltpu.VMEM((2, 2 * A_HEADS, tile, tile), jnp.float32),
                        pltpu.VMEM((2, 2 * A_HEADS, 1, tile), jnp.float32)],
        compiler_params=pltpu.CompilerParams(
            dimension_semantics=("parallel", "parallel"),
            vmem_limit_bytes=VMEM_LIMIT),
        name="attn_a",
    )(proj_fm, proj_tm3, proj_fm, bias_a, lam_vecs, lam_init, sub_g)


def _attn_b_kernel(qt_ref, k_ref, ckp_ref, vt_ref, o_ref, m_ref, acc_ref, s_ref, smax_ref, *,
                   tile):
    qi = pl.program_id(1)
    n_h = B_HEADS
    qt = qt_ref[...].astype(jnp.float32)
    row = lax.broadcasted_iota(jnp.int32, (LANES, tile), 0)
    q_aug = []
    for s in range(n_h):
        blk = qt[(s // 2) * LANES:(s // 2 + 1) * LANES]
        lo = (s % 2) * HEAD_DIM
        q_s = jnp.where((row >= lo) & (row < lo + HEAD_DIM), blk, 0.0)
        pick = (row < 3 * B_HEADS) & ((row % B_HEADS) == s)
        ones = jnp.where(pick, 1.0, 0.0)
        q_aug.append(jnp.concatenate([q_s, ones], axis=0).astype(jnp.bfloat16))
    m_ref[...] = jnp.full(m_ref.shape, NEG_INF, jnp.float32)
    acc_ref[...] = jnp.zeros(acc_ref.shape, jnp.float32)

    def produce(ki, slot):
        k0 = pl.multiple_of(ki * tile, tile)
        ck = ckp_ref[0, pl.ds(k0, tile), :]
        for s in range(n_h):
            k = k_ref[0, pl.ds(k0, tile), (s // 2) * LANES:(s // 2 + 1) * LANES]
            x = jnp.dot(jnp.concatenate([k, ck], axis=1), q_aug[s],
                        preferred_element_type=jnp.float32)
            s_ref[slot, s] = x
            smax_ref[slot, s] = _colmax(x)

    def consume(ki, slot, last):
        k0 = pl.multiple_of(ki * tile, tile)
        if last:
            kk = lax.broadcasted_iota(jnp.int32, (tile, tile), 0)
            qq = lax.broadcasted_iota(jnp.int32, (tile, tile), 1)
            keep = kk <= qq
        for s in range(n_h):
            x, xmax = s_ref[slot, s], smax_ref[slot, s]
            if last:
                x = jnp.where(keep, x, NEG_INF)
                xmax = _colmax(x)
            _online_step(x, xmax, _v_aug(vt_ref, s, k0, tile), m_ref, acc_ref, s)

    _flash_pipeline(qi, produce, consume)
    heads = [acc_ref[s][0:HEAD_DIM] / acc_ref[s][HEAD_DIM:HEAD_DIM + 1] for s in range(n_h)]
    o_ref[0] = jnp.transpose(jnp.concatenate(heads, axis=0)).astype(o_ref.dtype)


def _attn_b(proj_tm3, proj_fm, ckp):
    bsz, seq, _ = proj_tm3.shape
    tile = FLASH_TILE
    n_q = seq // tile
    return pl.pallas_call(
        functools.partial(_attn_b_kernel, tile=tile),
        out_shape=jax.ShapeDtypeStruct((bsz, seq, B_W), jnp.bfloat16),
        grid=(bsz, n_q),
        in_specs=[
            pl.BlockSpec((B_W, tile), lambda b, qi: (QB_T // B_W, b * n_q + qi)),
            pl.BlockSpec((1, seq, B_W), lambda b, qi: (b, 0, KB // B_W)),
            pl.BlockSpec((1, seq, LANES), lambda b, qi: (b, 0, 0)),
            pl.BlockSpec((B_W, seq), lambda b, qi: (VB_T // B_W, b)),
        ],
        out_specs=pl.BlockSpec((1, tile, B_W), lambda b, qi: (b, qi, 0)),
        scratch_shapes=[pltpu.VMEM((B_HEADS, 1, tile), jnp.float32),
                        pltpu.VMEM((B_HEADS, ACC_ROWS, tile), jnp.float32),
                        pltpu.VMEM((2, B_HEADS, tile, tile), jnp.float32),
                        pltpu.VMEM((2, B_HEADS, 1, tile), jnp.float32)],
        compiler_params=pltpu.CompilerParams(
            dimension_semantics=("parallel", "parallel"),
            vmem_limit_bytes=VMEM_LIMIT),
        name="attn_b",
    )(proj_fm, proj_tm3, ckp, proj_fm)


def _attn_c_kernel(q_ref, k_ref, v_ref, bias_ref, o_ref, qd_ref, kd_ref, vtd_ref, onat_ref,
                   lnat_ref, *, seq):
    g = pl.program_id(2)
    n_chunks = seq // C_TILE
    lane = lax.broadcasted_iota(jnp.int32, (1, LANES), 1)
    first = lane < HEAD_DIM

    def run_group(group, dil):
        n_tiles = seq // dil // C_TILE

        def src_rows(i):
            r, t = i // n_tiles, i % n_tiles
            return pl.ds(r + t * (C_TILE * dil), C_TILE, stride=dil), t

        kd_ref[0:C_TILE, :] = jnp.zeros((C_TILE, LANES), jnp.bfloat16)
        vtd_ref[:, 0:C_TILE] = jnp.zeros((LANES, C_TILE), jnp.bfloat16)

        def gather(j, carry):
            ids = [j * C_UNROLL + u for u in range(C_UNROLL)]
            vts = [jnp.transpose(v_ref[src_rows(i)[0], :]) for i in ids]
            for i, vt in zip(ids, vts):
                rows, _ = src_rows(i)
                dst = pl.multiple_of(i * C_TILE, C_TILE)
                qd_ref[pl.ds(dst, C_TILE), :] = q_ref[rows, :].astype(jnp.bfloat16)
                kd_ref[pl.ds(dst + C_TILE, C_TILE), :] = k_ref[rows, :].astype(jnp.bfloat16)
                vtd_ref[:, pl.ds(dst + C_TILE, C_TILE)] = vt.astype(jnp.bfloat16)
            return carry

        lax.fori_loop(0, n_chunks // C_UNROLL, gather, 0)

        def tiles(j, carry):
            ids = [j * C_UNROLL + u for u in range(C_UNROLL)]
            row0s = [pl.multiple_of(i * C_TILE, C_TILE) for i in ids]
            scores = []
            for row0 in row0s:
                q = qd_ref[pl.ds(row0, C_TILE), :]
                q2 = jnp.concatenate([jnp.where(first, q, jnp.zeros_like(q)),
                                      jnp.where(first, jnp.zeros_like(q), q)], axis=0)
                scores.append(_dot_nt(kd_ref[pl.ds(row0, 2 * C_TILE), :], q2))
            probs = []
            for i, st in zip(ids, scores):
                ver = jnp.minimum(i % n_tiles, 1)
                for h in range(2):
                    x = st[:, h * C_TILE:(h + 1) * C_TILE] + bias_ref[h, ver]
                    m = _colmax(x)
                    probs.append((m, jnp.exp2(x - m).astype(jnp.bfloat16)))
            pvs = []
            for u, row0 in enumerate(row0s):
                for h in range(2):
                    v_aug = jnp.concatenate(
                        [vtd_ref[h * HEAD_DIM:(h + 1) * HEAD_DIM, pl.ds(row0, 2 * C_TILE)],
                         jnp.ones((ONES_ROWS, 2 * C_TILE), jnp.bfloat16)], axis=0)
                    pvs.append(jnp.dot(v_aug, probs[2 * u + h][1],
                                       preferred_element_type=jnp.float32))
            for u, i in enumerate(ids):
                outs, lses = [], []
                for h in range(2):
                    pv, m = pvs[2 * u + h], probs[2 * u + h][0]
                    l = pv[HEAD_DIM:HEAD_DIM + 1]
                    outs.append(pv[0:HEAD_DIM] / l)
                    lses.append(jnp.broadcast_to(m + jnp.log2(l), (HEAD_DIM, C_TILE)))
                rows, _ = src_rows(i)
                onat_ref[group, rows, :] = jnp.transpose(jnp.concatenate(outs, axis=0))
                lnat_ref[group, rows, :] = jnp.transpose(jnp.concatenate(lses, axis=0))
            return carry

        lax.fori_loop(0, n_chunks // C_UNROLL, tiles, 0)

    for group, (_, dil) in enumerate(C_PAIRS):
        pl.when(g == group)(functools.partial(run_group, group, dil))

    @pl.when(g == C_GROUPS - 1)
    def _():
        rows_per = 512

        def combine(c, carry):
            rows = pl.ds(pl.multiple_of(c * rows_per, rows_per), rows_per)
            l0, l1, l2 = lnat_ref[0, rows, :], lnat_ref[1, rows, :], lnat_ref[2, rows, :]
            m = jnp.maximum(jnp.maximum(l0, l1), l2)
            e0, e1, e2 = jnp.exp2(l0 - m), jnp.exp2(l1 - m), jnp.exp2(l2 - m)
            num = e0 * onat_ref[0, rows, :] + e1 * onat_ref[1, rows, :] + e2 * onat_ref[2, rows, :]
            o_ref[0, rows, :] = (num / (e0 + e1 + e2)).astype(o_ref.dtype)
            return carry

        lax.fori_loop(0, seq // rows_per, combine, 0)


def _attn_c(proj_c, bias_c, bsz, seq):
    slab = lambda role: pl.BlockSpec(
        (None, seq, LANES), lambda b, hp, g: (role * 2 * C_GROUPS + g * 2 + hp, b, 0))
    return pl.pallas_call(
        functools.partial(_attn_c_kernel, seq=seq),
        out_shape=jax.ShapeDtypeStruct((bsz, seq, C_W), jnp.bfloat16),
        grid=(bsz, C_HEADS // 2, C_GROUPS),
        in_specs=[slab(0), slab(1), slab(2),
                  pl.BlockSpec((2, 2, 2 * C_TILE, C_TILE), lambda b, hp, g: (g * 2 + hp, 0, 0, 0))],
        out_specs=pl.BlockSpec((1, seq, LANES), lambda b, hp, g: (b, 0, hp)),
        scratch_shapes=[pltpu.VMEM((seq, LANES), jnp.bfloat16),
                        pltpu.VMEM((seq + C_TILE, LANES), jnp.bfloat16),
                        pltpu.VMEM((LANES, seq + C_TILE), jnp.bfloat16),
                        pltpu.VMEM((C_GROUPS, seq, LANES), jnp.float32),
                        pltpu.VMEM((C_GROUPS, seq, LANES), jnp.float32)],
        compiler_params=pltpu.CompilerParams(
            dimension_semantics=("parallel", "parallel", "arbitrary"),
            vmem_limit_bytes=VMEM_LIMIT),
        name="attn_c",
    )(proj_c, proj_c, proj_c, bias_c)


def _mix_mlp_kernel(x_ref, oa_ref, ob_ref, oc_ref, wo_ref, g_ref, w1_ref, w2_ref, fg_ref, out_ref,
                    acc_ref, h_ref, *, final):
    f = pl.program_id(1)

    @pl.when(f == 0)
    def _():
        wo = wo_ref.at[0]
        y = jnp.dot(oa_ref[...], wo[0:A_W, :], preferred_element_type=jnp.float32)
        y += jnp.dot(ob_ref[...], wo[A_W:A_W + B_W, :], preferred_element_type=jnp.float32)
        y += jnp.dot(oc_ref[...], wo[A_W + B_W:, :], preferred_element_type=jnp.float32)
        x1 = x_ref[...] + y
        acc_ref[...] = x1
        h_ref[...] = _rms(x1, g_ref[0]).astype(jnp.bfloat16)

    u = jnp.dot(h_ref[...], w1_ref[0], preferred_element_type=jnp.float32)
    u = jnp.square(jnp.maximum(u, 0.0)).astype(jnp.bfloat16)
    acc_ref[...] += jnp.dot(u, w2_ref[0], preferred_element_type=jnp.float32)

    @pl.when(f == pl.num_programs(1) - 1)
    def _():
        out_ref[...] = _rms(acc_ref[...], fg_ref[...]) if final else acc_ref[...]


def _mix_mlp(x2, oa, ob, oc, w_o, norm_g, w_1, w_2, final_g, layer, final, *, tm=1024, tf=1024):
    n_tok = x2.shape[0]
    row = lambda w: pl.BlockSpec((tm, w), lambda i, f: (i, 0))
    return pl.pallas_call(
        functools.partial(_mix_mlp_kernel, final=final),
        out_shape=jax.ShapeDtypeStruct((n_tok, D_MODEL), jnp.float32),
        grid=(n_tok // tm, D_FF // tf),
        in_specs=[row(D_MODEL), row(A_W), row(B_W), row(C_W),
                  pl.BlockSpec((1, D_MODEL, D_MODEL), lambda i, f: (layer, 0, 0)),
                  pl.BlockSpec((1, 1, D_MODEL), lambda i, f: (layer, 0, 0)),
                  pl.BlockSpec((1, D_MODEL, tf), lambda i, f: (layer, 0, f)),
                  pl.BlockSpec((1, tf, D_MODEL), lambda i, f: (layer, f, 0)),
                  pl.BlockSpec((1, D_MODEL), lambda i, f: (0, 0))],
        out_specs=row(D_MODEL),
        scratch_shapes=[pltpu.VMEM((tm, D_MODEL), jnp.float32),
                        pltpu.VMEM((tm, D_MODEL), jnp.bfloat16)],
        compiler_params=pltpu.CompilerParams(dimension_semantics=("parallel", "arbitrary"),
                                             vmem_limit_bytes=VMEM_LIMIT),
        name="mix_mlp",
    )(x2, oa, ob, oc, w_o, norm_g, w_1, w_2, final_g)


def _prep_w_in(w_in):
    a0, b0, f0, c0 = 0, 3 * A_W, 3 * A_W + 3 * B_W, 3 * A_W + 3 * B_W + B_HEADS
    sa, sb = A_QK_DIM ** -0.5 * LOG2E, HEAD_DIM ** -0.5 * LOG2E
    cw = C_GROUPS * C_W
    w_tm = jnp.concatenate([w_in[:, :, b0 + B_W:b0 + 2 * B_W],
                            w_in[:, :, a0 + A_W:a0 + 2 * A_W]], axis=-1).astype(jnp.bfloat16)
    w_fm = jnp.concatenate([
        w_in[:, :, b0:b0 + B_W] * sb, w_in[:, :, b0 + 2 * B_W:f0],
        w_in[:, :, a0:a0 + A_W] * sa, w_in[:, :, a0 + 2 * A_W:b0],
    ], axis=-1)
    w_fm = jnp.transpose(w_fm, (0, 2, 1)).astype(jnp.bfloat16)
    w_f = jnp.transpose(w_in[:, :, f0:c0], (0, 2, 1))
    w_f = jnp.pad(w_f, ((0, 0), (0, F_ROWS - B_HEADS), (0, 0))).astype(jnp.bfloat16)
    w_c = jnp.concatenate([w_in[:, :, c0:c0 + cw] * sb, w_in[:, :, c0 + cw:]],
                          axis=-1).astype(jnp.bfloat16)
    return w_tm, w_fm, w_f, w_c


def kernel(x, norm1_g, w_in, b_f, lam_q1, lam_k1, lam_q2, lam_k2, diff_norm_g, w_o, norm2_g,
           w_1, w_2, rel_bias, final_g):
    bsz, seq, _ = x.shape
    depth = w_in.shape[0]
    w_tm, w_fm, w_f, w_c = _prep_w_in(w_in)
    w_o16, w_116, w_216 = (w.astype(jnp.bfloat16) for w in (w_o, w_1, w_2))
    lam_vecs = jnp.stack([lam_q1, lam_k1, lam_q2, lam_k2], axis=1)
    bias_a, bias_c = _make_bias_tiles(rel_bias)

    x2 = x.reshape(bsz * seq, D_MODEL)
    for l in range(depth):
        proj_tm, proj_fm, ft, proj_c = _in_proj(x2, norm1_g[:, None, :], w_tm, w_fm, w_f, w_c, l)
        proj3 = proj_tm.reshape(bsz, seq, N_TM)
        ckp = _decay(ft, b_f[:, :, None], l, bsz, seq)
        lam_init = jnp.full((1,), 0.8 - 0.6 * math.exp(-0.3 * l), jnp.float32)
        oa = _attn_a(proj3, proj_fm, bias_a, lam_vecs, lam_init, diff_norm_g[:, :, None], l)
        ob = _attn_b(proj3, proj_fm, ckp)
        oc = _attn_c(proj_c, bias_c, bsz, seq)
        x2 = _mix_mlp(x2, oa.reshape(bsz * seq, A_W), ob.reshape(bsz * seq, B_W),
                      oc.reshape(bsz * seq, C_W), w_o16, norm2_g[:, None, :], w_116, w_216,
                      final_g[None, :], l, l == depth - 1)
    return x2.reshape(bsz, seq, D_MODEL)
```

```python
import functools
import math

import jax
import jax.numpy as jnp
from jax import lax
from jax.experimental import pallas as pl
from jax.experimental.pallas import tpu as pltpu

D_MODEL = 1024
HEAD_DIM = 64
A_HEADS = 4
A_QK_DIM = 32
B_HEADS = 8
C_HEADS = 4
C_PAIRS = ((128, 1), (512, 4), (2048, 16))
C_GROUPS = 3
A_W = 256
B_W = 512
C_W = 256
D_FF = 4096
N_BUCKETS = 32
MAX_DISTANCE = 2048
NORM_EPS = 1e-6
NEG_INF = -1e30
LOG2E = 1.4426950408889634

LANES = 128
VMEM_LIMIT = 56 * 1024 * 1024

KB, KA = 0, 512
N_TM = 768
QB_T, VB_T, QA_T, VA_T = 0, 512, 1024, 1280
N_FM = 1536
N_C = 2304
N_C_SLABS = N_C // 128
F_ROWS = 16
ONES_ROWS = 16
ACC_ROWS = HEAD_DIM + ONES_ROWS

A_STEP_HEADS = 4
B_STEP_HEADS = 8
C_TILE = 128
C_UNROLL = 8
FLASH_TILE = 256
A_GROUP = 2
B_GROUP = 1


def _bucket_thresholds():
    max_exact = N_BUCKETS // 2
    out = []
    for k in range(1, N_BUCKETS - max_exact):
        t = max_exact * (MAX_DISTANCE / max_exact) ** (k / (N_BUCKETS - max_exact))
        out.append(int(math.ceil(t)))
    return tuple(out)


_THRESHOLDS = _bucket_thresholds()


def _n_near_tiles(tile):
    return -(-(_THRESHOLDS[-1] + tile - 1) // tile)


def _t5_bucket(d):
    big = jnp.full(d.shape, N_BUCKETS // 2, jnp.int32)
    for t in _THRESHOLDS:
        big = big + (d >= t).astype(jnp.int32)
    return jnp.where(d < N_BUCKETS // 2, d, big)


def _bias_lookup(rb_ref, bucket, col):
    val = jnp.zeros(bucket.shape, jnp.float32)
    for b in range(N_BUCKETS):
        val = jnp.where(bucket == b, rb_ref[b, col], val)
    return val


def _bias_a_kernel(rb_ref, out_ref, *, tile):
    delta = pl.program_id(0)
    j = lax.broadcasted_iota(jnp.int32, (tile, tile), 0)
    i = lax.broadcasted_iota(jnp.int32, (tile, tile), 1)
    d = delta * tile + i - j
    bucket = _t5_bucket(d)
    for h in range(A_HEADS):
        val = _bias_lookup(rb_ref, bucket, h) * LOG2E
        out_ref[h, 0] = jnp.where(d >= 0, val, NEG_INF)


def _bias_c_kernel(rb_ref, out_ref):
    g = pl.program_id(0)
    dil = jnp.where(g == 0, C_PAIRS[0][1], jnp.where(g == 1, C_PAIRS[1][1], C_PAIRS[2][1]))
    c = lax.broadcasted_iota(jnp.int32, (2 * C_TILE, C_TILE), 0)
    i = lax.broadcasted_iota(jnp.int32, (2 * C_TILE, C_TILE), 1)
    steps = i + C_TILE - c
    valid = (steps >= 0) & (steps <= C_TILE)
    bucket = _t5_bucket(steps * dil)
    for h in range(C_HEADS):
        val = _bias_lookup(rb_ref, bucket, A_HEADS + g * C_HEADS + h) * LOG2E
        out_ref[h, 1] = jnp.where(valid, val, NEG_INF)
        out_ref[h, 0] = jnp.where(valid & (c >= C_TILE), val, NEG_INF)


def _make_bias_tiles(rel_bias):
    tile = FLASH_TILE
    n_bias = _n_near_tiles(tile) + 1
    smem = pl.BlockSpec(memory_space=pltpu.SMEM)
    bias_a = pl.pallas_call(
        functools.partial(_bias_a_kernel, tile=tile),
        out_shape=jax.ShapeDtypeStruct((A_HEADS, n_bias, tile, tile), jnp.float32),
        grid=(n_bias,),
        in_specs=[smem],
        out_specs=pl.BlockSpec((A_HEADS, 1, tile, tile), lambda d: (0, d, 0, 0)),
        name="bias_a_tiles",
    )(rel_bias)
    bias_c = pl.pallas_call(
        _bias_c_kernel,
        out_shape=jax.ShapeDtypeStruct((C_GROUPS * C_HEADS, 2, 2 * C_TILE, C_TILE), jnp.float32),
        grid=(C_GROUPS,),
        in_specs=[smem],
        out_specs=pl.BlockSpec((C_HEADS, 2, 2 * C_TILE, C_TILE), lambda g: (g, 0, 0, 0)),
        name="bias_c_tiles",
    )(rel_bias)
    return bias_a, bias_c


def _rms(x, g):
    return x * lax.rsqrt(jnp.mean(x * x, axis=-1, keepdims=True) + NORM_EPS) * g


def _dot_nt(a, b):
    return lax.dot_general(a, b, (((1,), (1,)), ((), ())), preferred_element_type=jnp.float32)


def _in_proj_kernel(x_ref, g_ref, wtm_ref, wfm_ref, wf_ref, wc_ref, tm_ref, fm_ref, ft_ref, c_ref):
    h = _rms(x_ref[...], g_ref[0]).astype(jnp.bfloat16)
    tm_ref[...] = jnp.dot(h, wtm_ref[0], preferred_element_type=jnp.float32).astype(jnp.bfloat16)
    for c in range(N_FM // 512):
        rows = slice(c * 512, (c + 1) * 512)
        fm_ref[rows, :] = _dot_nt(wfm_ref[0, rows, :], h).astype(jnp.bfloat16)
    ft_ref[...] = _dot_nt(wf_ref[0], h)
    for c in range(N_C // 256):
        pc = jnp.dot(h, wc_ref[0, :, c * 256:(c + 1) * 256], preferred_element_type=jnp.float32)
        c_ref[2 * c] = pc[:, :LANES]
        c_ref[2 * c + 1] = pc[:, LANES:]


def _in_proj(x2, norm_g, w_tm, w_fm, w_f, w_c, layer, *, tm=512):
    n_tok = x2.shape[0]
    return pl.pallas_call(
        _in_proj_kernel,
        out_shape=(jax.ShapeDtypeStruct((n_tok, N_TM), jnp.bfloat16),
                   jax.ShapeDtypeStruct((N_FM, n_tok), jnp.bfloat16),
                   jax.ShapeDtypeStruct((F_ROWS, n_tok), jnp.float32),
                   jax.ShapeDtypeStruct((N_C_SLABS, n_tok, LANES), jnp.float32)),
        grid=(n_tok // tm,),
        in_specs=[
            pl.BlockSpec((tm, D_MODEL), lambda i: (i, 0)),
            pl.BlockSpec((1, 1, D_MODEL), lambda i: (layer, 0, 0)),
            pl.BlockSpec((1, D_MODEL, N_TM), lambda i: (layer, 0, 0)),
            pl.BlockSpec((1, N_FM, D_MODEL), lambda i: (layer, 0, 0)),
            pl.BlockSpec((1, F_ROWS, D_MODEL), lambda i: (layer, 0, 0)),
            pl.BlockSpec((1, D_MODEL, N_C), lambda i: (layer, 0, 0)),
        ],
        out_specs=(pl.BlockSpec((tm, N_TM), lambda i: (i, 0)),
                   pl.BlockSpec((N_FM, tm), lambda i: (0, i)),
                   pl.BlockSpec((F_ROWS, tm), lambda i: (0, i)),
                   pl.BlockSpec((N_C_SLABS, tm, LANES), lambda i: (0, i, 0))),
        compiler_params=pltpu.CompilerParams(dimension_semantics=("parallel",),
                                             vmem_limit_bytes=VMEM_LIMIT),
        name="in_proj",
    )(x2, norm_g, w_tm, w_fm, w_f, w_c)


def _decay_kernel(ft_ref, bf_ref, ckp_ref):
    z = ft_ref[0:B_HEADS, :] + bf_ref[0]
    x = jnp.minimum(z, 0.0) - jnp.log1p(jnp.exp(-jnp.abs(z)))
    seq = x.shape[1]
    pos = lax.broadcasted_iota(jnp.int32, x.shape, 1)
    shift = 1
    while shift < seq:
        x = x + jnp.where(pos >= shift, pltpu.roll(x, shift, 1), 0.0)
        shift *= 2
    c = x * (-LOG2E)
    hi = c.astype(jnp.bfloat16).astype(jnp.float32)
    mid = (c - hi).astype(jnp.bfloat16).astype(jnp.float32)
    lo = c - hi - mid
    stacked = jnp.concatenate(
        [hi, mid, lo, jnp.zeros((LANES - 3 * B_HEADS, seq), jnp.float32)], axis=0)
    ckp_ref[0] = jnp.transpose(stacked).astype(jnp.bfloat16)


def _decay(ft, b_f, layer, bsz, seq):
    return pl.pallas_call(
        _decay_kernel,
        out_shape=jax.ShapeDtypeStruct((bsz, seq, LANES), jnp.bfloat16),
        grid=(bsz,),
        in_specs=[pl.BlockSpec((F_ROWS, seq), lambda b: (0, b)),
                  pl.BlockSpec((1, B_HEADS, 1), lambda b: (layer, 0, 0))],
        out_specs=pl.BlockSpec((1, seq, LANES), lambda b: (b, 0, 0)),
        compiler_params=pltpu.CompilerParams(dimension_semantics=("parallel",)),
        name="decay_scan",
    )(ft, b_f)


def _colmax(s):
    rows = s.shape[0]
    while rows > 8:
        rows //= 2
        s = jnp.maximum(s[:rows], s[rows:2 * rows])
    return jnp.max(s, axis=0, keepdims=True)


def _online_step(s, smax, v_aug, m_ref, acc_ref, idx):
    m_old = m_ref[idx]
    m_new = jnp.maximum(m_old, smax)
    alpha = jnp.exp2(m_old - m_new)
    p = jnp.exp2(s - m_new).astype(jnp.bfloat16)
    acc_ref[idx] = alpha * acc_ref[idx] + jnp.dot(v_aug, p, preferred_element_type=jnp.float32)
    m_ref[idx] = m_new


def _v_aug(vt_ref, head, k0, tile):
    v = vt_ref[head * HEAD_DIM:(head + 1) * HEAD_DIM, pl.ds(k0, tile)]
    return jnp.concatenate([v, jnp.ones((ONES_ROWS, tile), jnp.bfloat16)], axis=0)


def _flash_pipeline(qi, n_sub, group, produce, consume):
    every = range(n_sub)

    def overlapped(pk, pslot, ck, cslot):
        for g0 in range(0, n_sub, group):
            subs = range(g0, g0 + group)
            produce(pk, pslot, subs)
            consume(ck, cslot, False, subs)

    produce(0, 0, every)

    def pair(j, carry):
        k = 2 * j
        overlapped(k + 1, 1, k, 0)
        overlapped(k + 2, 0, k + 1, 1)
        return carry

    lax.fori_loop(0, qi // 2, pair, 0)

    @pl.when(qi % 2 == 0)
    def _():
        consume(qi, 0, True, every)

    @pl.when(qi % 2 == 1)
    def _():
        overlapped(qi, 1, qi - 1, 0)
        consume(qi, 1, True, every)


def _attn_a_kernel(qt_ref, k_ref, vt_ref, bias_ref, lamv_ref, lami_ref, subg_ref, o_ref,
                   m_ref, acc_ref, s_ref, smax_ref, *, tile, n_near):
    qi = pl.program_id(2)
    n_sub = 2 * A_STEP_HEADS
    qt = qt_ref[...].astype(jnp.float32)
    row = lax.broadcasted_iota(jnp.int32, (LANES, tile), 0)
    qs = []
    for s in range(n_sub):
        blk = qt[(s // 4) * LANES:(s // 4 + 1) * LANES]
        lo = (s % 4) * A_QK_DIM
        qs.append(jnp.where((row >= lo) & (row < lo + A_QK_DIM), blk, 0.0).astype(jnp.bfloat16))
    m_ref[...] = jnp.full(m_ref.shape, NEG_INF, jnp.float32)
    acc_ref[...] = jnp.zeros(acc_ref.shape, jnp.float32)

    def produce(ki, slot, subs):
        k0 = pl.multiple_of(ki * tile, tile)
        delta = jnp.minimum(qi - ki, n_near)
        for s in subs:
            k = k_ref[0, pl.ds(k0, tile), (s // 4) * LANES:(s // 4 + 1) * LANES]
            x = jnp.dot(k, qs[s], preferred_element_type=jnp.float32) + bias_ref[s // 2, delta]
            s_ref[slot, s] = x
            smax_ref[slot, s] = _colmax(x)

    def consume(ki, slot, last, subs):
        k0 = pl.multiple_of(ki * tile, tile)
        for s in subs:
            _online_step(s_ref[slot, s], smax_ref[slot, s], _v_aug(vt_ref, s // 2, k0, tile),
                         m_ref, acc_ref, s)

    _flash_pipeline(qi, n_sub, A_GROUP, produce, consume)

    lam_init = lami_ref[0]
    lv = lamv_ref[0]
    e1 = jnp.exp(jnp.sum(lv[0:1] * lv[1:2], axis=1, keepdims=True))
    e2 = jnp.exp(jnp.sum(lv[2:3] * lv[3:4], axis=1, keepdims=True))
    lam = e1 - e2 + lam_init
    heads = []
    for h in range(A_STEP_HEADS):
        a0, a1 = acc_ref[2 * h], acc_ref[2 * h + 1]
        o = (a0[0:HEAD_DIM] / a0[HEAD_DIM:HEAD_DIM + 1]
             - lam * (a1[0:HEAD_DIM] / a1[HEAD_DIM:HEAD_DIM + 1]))
        ms = jnp.mean(o * o, axis=0, keepdims=True)
        heads.append(o * lax.rsqrt(ms + NORM_EPS) * subg_ref[0] * (1.0 - lam_init))
    o_ref[0] = jnp.transpose(jnp.concatenate(heads, axis=0)).astype(o_ref.dtype)


def _attn_a(proj_tm3, proj_fm, bias_a, lam_vecs, lam_init, sub_g, layer):
    bsz, seq, _ = proj_tm3.shape
    tile = FLASH_TILE
    n_near = _n_near_tiles(tile)
    n_q = seq // tile
    n_h = A_STEP_HEADS
    width = n_h * HEAD_DIM
    return pl.pallas_call(
        functools.partial(_attn_a_kernel, tile=tile, n_near=n_near),
        out_shape=jax.ShapeDtypeStruct((bsz, seq, A_W), jnp.bfloat16),
        grid=(A_HEADS // n_h, bsz, n_q),
        in_specs=[
            pl.BlockSpec((width, tile), lambda hg, b, qi: (QA_T // width + hg, b * n_q + qi)),
            pl.BlockSpec((1, seq, width), lambda hg, b, qi: (b, 0, KA // width + hg)),
            pl.BlockSpec((width, seq), lambda hg, b, qi: (VA_T // width + hg, b)),
            pl.BlockSpec((n_h, n_near + 1, tile, tile), lambda hg, b, qi: (hg, 0, 0, 0)),
            pl.BlockSpec((1, 4, A_QK_DIM), lambda hg, b, qi: (layer, 0, 0)),
            pl.BlockSpec(memory_space=pltpu.SMEM),
            pl.BlockSpec((1, HEAD_DIM, 1), lambda hg, b, qi: (layer, 0, 0)),
        ],
        out_specs=pl.BlockSpec((1, tile, width), lambda hg, b, qi: (b, qi, hg)),
        scratch_shapes=[pltpu.VMEM((2 * n_h, 1, tile), jnp.float32),
                        pltpu.VMEM((2 * n_h, ACC_ROWS, tile), jnp.float32),
                        pltpu.VMEM((2, 2 * n_h, tile, tile), jnp.float32),
                        pltpu.VMEM((2, 2 * n_h, 1, tile), jnp.float32)],
        compiler_params=pltpu.CompilerParams(
            dimension_semantics=("parallel", "parallel", "parallel"),
            vmem_limit_bytes=VMEM_LIMIT),
        name="attn_a",
    )(proj_fm, proj_tm3, proj_fm, bias_a, lam_vecs, lam_init, sub_g)


def _attn_b_kernel(qt_ref, k_ref, ckp_ref, vt_ref, o_ref, m_ref, acc_ref, s_ref, smax_ref, *,
                   tile):
    hg = pl.program_id(0)
    qi = pl.program_id(2)
    n_h = B_STEP_HEADS
    qt = qt_ref[...].astype(jnp.float32)
    row = lax.broadcasted_iota(jnp.int32, (LANES, tile), 0)
    q_aug = []
    for s in range(n_h):
        blk = qt[(s // 2) * LANES:(s // 2 + 1) * LANES]
        lo = (s % 2) * HEAD_DIM
        q_s = jnp.where((row >= lo) & (row < lo + HEAD_DIM), blk, 0.0)
        pick = (row < 3 * B_HEADS) & ((row % B_HEADS) == n_h * hg + s)
        ones = jnp.where(pick, 1.0, 0.0)
        q_aug.append(jnp.concatenate([q_s, ones], axis=0).astype(jnp.bfloat16))
    m_ref[...] = jnp.full(m_ref.shape, NEG_INF, jnp.float32)
    acc_ref[...] = jnp.zeros(acc_ref.shape, jnp.float32)

    def produce(ki, slot, subs):
        k0 = pl.multiple_of(ki * tile, tile)
        ck = ckp_ref[0, pl.ds(k0, tile), :]
        for s in subs:
            k = k_ref[0, pl.ds(k0, tile), (s // 2) * LANES:(s // 2 + 1) * LANES]
            x = jnp.dot(jnp.concatenate([k, ck], axis=1), q_aug[s],
                        preferred_element_type=jnp.float32)
            s_ref[slot, s] = x
            smax_ref[slot, s] = _colmax(x)

    def consume(ki, slot, last, subs):
        k0 = pl.multiple_of(ki * tile, tile)
        if last:
            kk = lax.broadcasted_iota(jnp.int32, (tile, tile), 0)
            qq = lax.broadcasted_iota(jnp.int32, (tile, tile), 1)
            keep = kk <= qq
        for s in subs:
            x, xmax = s_ref[slot, s], smax_ref[slot, s]
            if last:
                x = jnp.where(keep, x, NEG_INF)
                xmax = _colmax(x)
            _online_step(x, xmax, _v_aug(vt_ref, s, k0, tile), m_ref, acc_ref, s)

    _flash_pipeline(qi, n_h, B_GROUP, produce, consume)
    heads = [acc_ref[s][0:HEAD_DIM] / acc_ref[s][HEAD_DIM:HEAD_DIM + 1] for s in range(n_h)]
    o_ref[0] = jnp.transpose(jnp.concatenate(heads, axis=0)).astype(o_ref.dtype)


def _attn_b(proj_tm3, proj_fm, ckp):
    bsz, seq, _ = proj_tm3.shape
    tile = FLASH_TILE
    n_q = seq // tile
    n_h = B_STEP_HEADS
    width = n_h * HEAD_DIM
    return pl.pallas_call(
        functools.partial(_attn_b_kernel, tile=tile),
        out_shape=jax.ShapeDtypeStruct((bsz, seq, B_W), jnp.bfloat16),
        grid=(B_HEADS // n_h, bsz, n_q),
        in_specs=[
            pl.BlockSpec((width, tile), lambda hg, b, qi: (QB_T // width + hg, b * n_q + qi)),
            pl.BlockSpec((1, seq, width), lambda hg, b, qi: (b, 0, KB // width + hg)),
            pl.BlockSpec((1, seq, LANES), lambda hg, b, qi: (b, 0, 0)),
            pl.BlockSpec((width, seq), lambda hg, b, qi: (VB_T // width + hg, b)),
        ],
        out_specs=pl.BlockSpec((1, tile, width), lambda hg, b, qi: (b, qi, hg)),
        scratch_shapes=[pltpu.VMEM((n_h, 1, tile), jnp.float32),
                        pltpu.VMEM((n_h, ACC_ROWS, tile), jnp.float32),
                        pltpu.VMEM((2, n_h, tile, tile), jnp.float32),
                        pltpu.VMEM((2, n_h, 1, tile), jnp.float32)],
        compiler_params=pltpu.CompilerParams(
            dimension_semantics=("parallel", "parallel", "parallel"),
            vmem_limit_bytes=VMEM_LIMIT),
        name="attn_b",
    )(proj_fm, proj_tm3, ckp, proj_fm)


def _attn_c_kernel(q_ref, k_ref, v_ref, bias_ref, o_ref, qd_ref, kd_ref, vtd_ref, onat_ref,
                   lnat_ref, *, seq):
    g = pl.program_id(2)
    n_chunks = seq // C_TILE
    lane = lax.broadcasted_iota(jnp.int32, (1, LANES), 1)
    first = lane < HEAD_DIM

    def run_group(group, dil):
        n_tiles = seq // dil // C_TILE

        def src_rows(i):
            r, t = i // n_tiles, i % n_tiles
            return pl.ds(r + t * (C_TILE * dil), C_TILE, stride=dil), t

        kd_ref[0:C_TILE, :] = jnp.zeros((C_TILE, LANES), jnp.bfloat16)
        vtd_ref[:, 0:C_TILE] = jnp.zeros((LANES, C_TILE), jnp.bfloat16)

        def gather(j, carry):
            ids = [j * C_UNROLL + u for u in range(C_UNROLL)]
            vts = [jnp.transpose(v_ref[src_rows(i)[0], :]) for i in ids]
            for i, vt in zip(ids, vts):
                rows, _ = src_rows(i)
                dst = pl.multiple_of(i * C_TILE, C_TILE)
                qd_ref[pl.ds(dst, C_TILE), :] = q_ref[rows, :].astype(jnp.bfloat16)
                kd_ref[pl.ds(dst + C_TILE, C_TILE), :] = k_ref[rows, :].astype(jnp.bfloat16)
                vtd_ref[:, pl.ds(dst + C_TILE, C_TILE)] = vt.astype(jnp.bfloat16)
            return carry

        lax.fori_loop(0, n_chunks // C_UNROLL, gather, 0)

        def tiles(j, carry):
            ids = [j * C_UNROLL + u for u in range(C_UNROLL)]
            row0s = [pl.multiple_of(i * C_TILE, C_TILE) for i in ids]
            scores = []
            for row0 in row0s:
                q = qd_ref[pl.ds(row0, C_TILE), :]
                q2 = jnp.concatenate([jnp.where(first, q, jnp.zeros_like(q)),
                                      jnp.where(first, jnp.zeros_like(q), q)], axis=0)
                scores.append(_dot_nt(kd_ref[pl.ds(row0, 2 * C_TILE), :], q2))
            probs = []
            for i, st in zip(ids, scores):
                ver = jnp.minimum(i % n_tiles, 1)
                for h in range(2):
                    x = st[:, h * C_TILE:(h + 1) * C_TILE] + bias_ref[h, ver]
                    m = _colmax(x)
                    probs.append((m, jnp.exp2(x - m).astype(jnp.bfloat16)))
            pvs = []
            for u, row0 in enumerate(row0s):
                for h in range(2):
                    v_aug = jnp.concatenate(
                        [vtd_ref[h * HEAD_DIM:(h + 1) * HEAD_DIM, pl.ds(row0, 2 * C_TILE)],
                         jnp.ones((ONES_ROWS, 2 * C_TILE), jnp.bfloat16)], axis=0)
                    pvs.append(jnp.dot(v_aug, probs[2 * u + h][1],
                                       preferred_element_type=jnp.float32))
            for u, i in enumerate(ids):
                outs, lses = [], []
                for h in range(2):
                    pv, m = pvs[2 * u + h], probs[2 * u + h][0]
                    l = pv[HEAD_DIM:HEAD_DIM + 1]
                    outs.append(pv[0:HEAD_DIM] / l)
                    lses.append(jnp.broadcast_to(m + jnp.log2(l), (HEAD_DIM, C_TILE)))
                rows, _ = src_rows(i)
                onat_ref[group, rows, :] = jnp.transpose(jnp.concatenate(outs, axis=0))
                lnat_ref[group, rows, :] = jnp.transpose(jnp.concatenate(lses, axis=0))
            return carry

        lax.fori_loop(0, n_chunks // C_UNROLL, tiles, 0)

    for group, (_, dil) in enumerate(C_PAIRS):
        pl.when(g == group)(functools.partial(run_group, group, dil))

    @pl.when(g == C_GROUPS - 1)
    def _():
        rows_per = 512

        def combine(c, carry):
            rows = pl.ds(pl.multiple_of(c * rows_per, rows_per), rows_per)
            l0, l1, l2 = lnat_ref[0, rows, :], lnat_ref[1, rows, :], lnat_ref[2, rows, :]
            m = jnp.maximum(jnp.maximum(l0, l1), l2)
            e0, e1, e2 = jnp.exp2(l0 - m), jnp.exp2(l1 - m), jnp.exp2(l2 - m)
            num = e0 * onat_ref[0, rows, :] + e1 * onat_ref[1, rows, :] + e2 * onat_ref[2, rows, :]
            o_ref[0, rows, :] = (num / (e0 + e1 + e2)).astype(o_ref.dtype)
            return carry

        lax.fori_loop(0, seq // rows_per, combine, 0)


def _attn_c(proj_c, bias_c, bsz, seq):
    slab = lambda role: pl.BlockSpec(
        (None, seq, LANES), lambda b, hp, g: (role * 2 * C_GROUPS + g * 2 + hp, b, 0))
    return pl.pallas_call(
        functools.partial(_attn_c_kernel, seq=seq),
        out_shape=jax.ShapeDtypeStruct((bsz, seq, C_W), jnp.bfloat16),
        grid=(bsz, C_HEADS // 2, C_GROUPS),
        in_specs=[slab(0), slab(1), slab(2),
                  pl.BlockSpec((2, 2, 2 * C_TILE, C_TILE), lambda b, hp, g: (g * 2 + hp, 0, 0, 0))],
        out_specs=pl.BlockSpec((1, seq, LANES), lambda b, hp, g: (b, 0, hp)),
        scratch_shapes=[pltpu.VMEM((seq, LANES), jnp.bfloat16),
                        pltpu.VMEM((seq + C_TILE, LANES), jnp.bfloat16),
                        pltpu.VMEM((LANES, seq + C_TILE), jnp.bfloat16),
                        pltpu.VMEM((C_GROUPS, seq, LANES), jnp.float32),
                        pltpu.VMEM((C_GROUPS, seq, LANES), jnp.float32)],
        compiler_params=pltpu.CompilerParams(
            dimension_semantics=("parallel", "parallel", "arbitrary"),
            vmem_limit_bytes=VMEM_LIMIT),
        name="attn_c",
    )(proj_c, proj_c, proj_c, bias_c)


def _mix_mlp_kernel(x_ref, oa_ref, ob_ref, oc_ref, wo_ref, g_ref, w1_ref, w2_ref, fg_ref, out_ref,
                    acc_ref, h_ref, *, final):
    f = pl.program_id(1)

    @pl.when(f == 0)
    def _():
        wo = wo_ref.at[0]
        y = jnp.dot(oa_ref[...], wo[0:A_W, :], preferred_element_type=jnp.float32)
        y += jnp.dot(ob_ref[...], wo[A_W:A_W + B_W, :], preferred_element_type=jnp.float32)
        y += jnp.dot(oc_ref[...], wo[A_W + B_W:, :], preferred_element_type=jnp.float32)
        x1 = x_ref[...] + y
        acc_ref[...] = x1
        h_ref[...] = _rms(x1, g_ref[0]).astype(jnp.bfloat16)

    u = jnp.dot(h_ref[...], w1_ref[0], preferred_element_type=jnp.float32)
    u = jnp.square(jnp.maximum(u, 0.0)).astype(jnp.bfloat16)
    acc_ref[...] += jnp.dot(u, w2_ref[0], preferred_element_type=jnp.float32)

    @pl.when(f == pl.num_programs(1) - 1)
    def _():
        out_ref[...] = _rms(acc_ref[...], fg_ref[...]) if final else acc_ref[...]


def _mix_mlp(x2, oa, ob, oc, w_o, norm_g, w_1, w_2, final_g, layer, final, *, tm=1024, tf=1024):
    n_tok = x2.shape[0]
    row = lambda w: pl.BlockSpec((tm, w), lambda i, f: (i, 0))
    return pl.pallas_call(
        functools.partial(_mix_mlp_kernel, final=final),
        out_shape=jax.ShapeDtypeStruct((n_tok, D_MODEL), jnp.float32),
        grid=(n_tok // tm, D_FF // tf),
        in_specs=[row(D_MODEL), row(A_W), row(B_W), row(C_W),
                  pl.BlockSpec((1, D_MODEL, D_MODEL), lambda i, f: (layer, 0, 0)),
                  pl.BlockSpec((1, 1, D_MODEL), lambda i, f: (layer, 0, 0)),
                  pl.BlockSpec((1, D_MODEL, tf), lambda i, f: (layer, 0, f)),
                  pl.BlockSpec((1, tf, D_MODEL), lambda i, f: (layer, f, 0)),
                  pl.BlockSpec((1, D_MODEL), lambda i, f: (0, 0))],
        out_specs=row(D_MODEL),
        scratch_shapes=[pltpu.VMEM((tm, D_MODEL), jnp.float32),
                        pltpu.VMEM((tm, D_MODEL), jnp.bfloat16)],
        compiler_params=pltpu.CompilerParams(dimension_semantics=("parallel", "arbitrary"),
                                             vmem_limit_bytes=VMEM_LIMIT),
        name="mix_mlp",
    )(x2, oa, ob, oc, w_o, norm_g, w_1, w_2, final_g)


def _prep_w_in(w_in):
    a0, b0, f0, c0 = 0, 3 * A_W, 3 * A_W + 3 * B_W, 3 * A_W + 3 * B_W + B_HEADS
    sa, sb = A_QK_DIM ** -0.5 * LOG2E, HEAD_DIM ** -0.5 * LOG2E
    cw = C_GROUPS * C_W
    w_tm = jnp.concatenate([w_in[:, :, b0 + B_W:b0 + 2 * B_W],
                            w_in[:, :, a0 + A_W:a0 + 2 * A_W]], axis=-1).astype(jnp.bfloat16)
    w_fm = jnp.concatenate([
        w_in[:, :, b0:b0 + B_W] * sb, w_in[:, :, b0 + 2 * B_W:f0],
        w_in[:, :, a0:a0 + A_W] * sa, w_in[:, :, a0 + 2 * A_W:b0],
    ], axis=-1)
    w_fm = jnp.transpose(w_fm, (0, 2, 1)).astype(jnp.bfloat16)
    w_f = jnp.transpose(w_in[:, :, f0:c0], (0, 2, 1))
    w_f = jnp.pad(w_f, ((0, 0), (0, F_ROWS - B_HEADS), (0, 0))).astype(jnp.bfloat16)
    w_c = jnp.concatenate([w_in[:, :, c0:c0 + cw] * sb, w_in[:, :, c0 + cw:]],
                          axis=-1).astype(jnp.bfloat16)
    return w_tm, w_fm, w_f, w_c


def kernel(x, norm1_g, w_in, b_f, lam_q1, lam_k1, lam_q2, lam_k2, diff_norm_g, w_o, norm2_g,
           w_1, w_2, rel_bias, final_g):
    bsz, seq, _ = x.shape
    depth = w_in.shape[0]
    w_tm, w_fm, w_f, w_c = _prep_w_in(w_in)
    w_o16, w_116, w_216 = (w.astype(jnp.bfloat16) for w in (w_o, w_1, w_2))
    lam_vecs = jnp.stack([lam_q1, lam_k1, lam_q2, lam_k2], axis=1)
    bias_a, bias_c = _make_bias_tiles(rel_bias)

    x2 = x.reshape(bsz * seq, D_MODEL)
    for l in range(depth):
        proj_tm, proj_fm, ft, proj_c = _in_proj(x2, norm1_g[:, None, :], w_tm, w_fm, w_f, w_c, l)
        proj3 = proj_tm.reshape(bsz, seq, N_TM)
        ckp = _decay(ft, b_f[:, :, None], l, bsz, seq)
        lam_init = jnp.full((1,), 0.8 - 0.6 * math.exp(-0.3 * l), jnp.float32)
        oa = _attn_a(proj3, proj_fm, bias_a, lam_vecs, lam_init, diff_norm_g[:, :, None], l)
        ob = _attn_b(proj3, proj_fm, ckp)
        oc = _attn_c(proj_c, bias_c, bsz, seq)
        x2 = _mix_mlp(x2, oa.reshape(bsz * seq, A_W), ob.reshape(bsz * seq, B_W),
                      oc.reshape(bsz * seq, C_W), w_o16, norm2_g[:, None, :], w_116, w_216,
                      final_g[None, :], l, l == depth - 1)
    return x2.reshape(bsz, seq, D_MODEL)
```

```python
import functools
import math

import jax
import jax.numpy as jnp
from jax import lax
from jax.experimental import pallas as pl
from jax.experimental.pallas import tpu as pltpu

D_MODEL = 1024
HEAD_DIM = 64
A_HEADS = 4
A_QK_DIM = 32
B_HEADS = 8
C_HEADS = 4
C_PAIRS = ((128, 1), (512, 4), (2048, 16))
C_GROUPS = 3
A_W = 256
B_W = 512
C_W = 256
D_FF = 4096
N_BUCKETS = 32
MAX_DISTANCE = 2048
NORM_EPS = 1e-6
NEG_INF = -1e30
LOG2E = 1.4426950408889634

LANES = 128
VMEM_LIMIT = 56 * 1024 * 1024

KB, KA = 0, 512
N_TM = 768
QB_T, VB_T, QA_T, VA_T = 0, 512, 1024, 1280
N_FM = 1536
N_C = 2304
N_C_SLABS = N_C // 128
F_ROWS = 16
ONES_ROWS = 16
ACC_ROWS = HEAD_DIM + ONES_ROWS

A_STEP_HEADS = 4
B_STEP_HEADS = 8
C_TILE = 128
C_UNROLL = 16
C_AHEAD = 4
FLASH_TILE = 256
A_GROUP = 2
B_GROUP = 1


def _bucket_thresholds():
    max_exact = N_BUCKETS // 2
    out = []
    for k in range(1, N_BUCKETS - max_exact):
        t = max_exact * (MAX_DISTANCE / max_exact) ** (k / (N_BUCKETS - max_exact))
        out.append(int(math.ceil(t)))
    return tuple(out)


_THRESHOLDS = _bucket_thresholds()


def _n_near_tiles(tile):
    return -(-(_THRESHOLDS[-1] + tile - 1) // tile)


def _t5_bucket(d):
    big = jnp.full(d.shape, N_BUCKETS // 2, jnp.int32)
    for t in _THRESHOLDS:
        big = big + (d >= t).astype(jnp.int32)
    return jnp.where(d < N_BUCKETS // 2, d, big)


def _bias_lookup(rb_ref, bucket, col):
    val = jnp.zeros(bucket.shape, jnp.float32)
    for b in range(N_BUCKETS):
        val = jnp.where(bucket == b, rb_ref[b, col], val)
    return val


def _bias_a_kernel(rb_ref, out_ref, *, tile):
    delta = pl.program_id(0)
    j = lax.broadcasted_iota(jnp.int32, (tile, tile), 0)
    i = lax.broadcasted_iota(jnp.int32, (tile, tile), 1)
    d = delta * tile + i - j
    bucket = _t5_bucket(d)
    for h in range(A_HEADS):
        val = _bias_lookup(rb_ref, bucket, h) * LOG2E
        out_ref[h, 0] = jnp.where(d >= 0, val, NEG_INF)


def _bias_c_kernel(rb_ref, out_ref):
    g = pl.program_id(0)
    dil = jnp.where(g == 0, C_PAIRS[0][1], jnp.where(g == 1, C_PAIRS[1][1], C_PAIRS[2][1]))
    c = lax.broadcasted_iota(jnp.int32, (2 * C_TILE, C_TILE), 0)
    i = lax.broadcasted_iota(jnp.int32, (2 * C_TILE, C_TILE), 1)
    steps = i + C_TILE - c
    valid = (steps >= 0) & (steps <= C_TILE)
    bucket = _t5_bucket(steps * dil)
    for h in range(C_HEADS):
        val = _bias_lookup(rb_ref, bucket, A_HEADS + g * C_HEADS + h) * LOG2E
        out_ref[h, 1] = jnp.where(valid, val, NEG_INF)
        out_ref[h, 0] = jnp.where(valid & (c >= C_TILE), val, NEG_INF)


def _make_bias_tiles(rel_bias):
    tile = FLASH_TILE
    n_bias = _n_near_tiles(tile) + 1
    smem = pl.BlockSpec(memory_space=pltpu.SMEM)
    bias_a = pl.pallas_call(
        functools.partial(_bias_a_kernel, tile=tile),
        out_shape=jax.ShapeDtypeStruct((A_HEADS, n_bias, tile, tile), jnp.float32),
        grid=(n_bias,),
        in_specs=[smem],
        out_specs=pl.BlockSpec((A_HEADS, 1, tile, tile), lambda d: (0, d, 0, 0)),
        name="bias_a_tiles",
    )(rel_bias)
    bias_c = pl.pallas_call(
        _bias_c_kernel,
        out_shape=jax.ShapeDtypeStruct((C_GROUPS * C_HEADS, 2, 2 * C_TILE, C_TILE), jnp.float32),
        grid=(C_GROUPS,),
        in_specs=[smem],
        out_specs=pl.BlockSpec((C_HEADS, 2, 2 * C_TILE, C_TILE), lambda g: (g, 0, 0, 0)),
        name="bias_c_tiles",
    )(rel_bias)
    return bias_a, bias_c


def _rms(x, g):
    return x * lax.rsqrt(jnp.mean(x * x, axis=-1, keepdims=True) + NORM_EPS) * g


def _dot_nt(a, b):
    return lax.dot_general(a, b, (((1,), (1,)), ((), ())), preferred_element_type=jnp.float32)


def _in_proj_kernel(x_ref, g_ref, wtm_ref, wfm_ref, wf_ref, wc_ref, tm_ref, fm_ref, ft_ref, c_ref):
    h = _rms(x_ref[...], g_ref[0]).astype(jnp.bfloat16)
    tm_ref[...] = jnp.dot(h, wtm_ref[0], preferred_element_type=jnp.float32).astype(jnp.bfloat16)
    for c in range(N_FM // 512):
        rows = slice(c * 512, (c + 1) * 512)
        fm_ref[rows, :] = _dot_nt(wfm_ref[0, rows, :], h).astype(jnp.bfloat16)
    ft_ref[...] = _dot_nt(wf_ref[0], h)
    for c in range(N_C // 256):
        pc = jnp.dot(h, wc_ref[0, :, c * 256:(c + 1) * 256], preferred_element_type=jnp.float32)
        c_ref[2 * c] = pc[:, :LANES]
        c_ref[2 * c + 1] = pc[:, LANES:]


def _in_proj(x2, norm_g, w_tm, w_fm, w_f, w_c, layer, *, tm=512):
    n_tok = x2.shape[0]
    return pl.pallas_call(
        _in_proj_kernel,
        out_shape=(jax.ShapeDtypeStruct((n_tok, N_TM), jnp.bfloat16),
                   jax.ShapeDtypeStruct((N_FM, n_tok), jnp.bfloat16),
                   jax.ShapeDtypeStruct((F_ROWS, n_tok), jnp.float32),
                   jax.ShapeDtypeStruct((N_C_SLABS, n_tok, LANES), jnp.float32)),
        grid=(n_tok // tm,),
        in_specs=[
            pl.BlockSpec((tm, D_MODEL), lambda i: (i, 0)),
            pl.BlockSpec((1, 1, D_MODEL), lambda i: (layer, 0, 0)),
            pl.BlockSpec((1, D_MODEL, N_TM), lambda i: (layer, 0, 0)),
            pl.BlockSpec((1, N_FM, D_MODEL), lambda i: (layer, 0, 0)),
            pl.BlockSpec((1, F_ROWS, D_MODEL), lambda i: (layer, 0, 0)),
            pl.BlockSpec((1, D_MODEL, N_C), lambda i: (layer, 0, 0)),
        ],
        out_specs=(pl.BlockSpec((tm, N_TM), lambda i: (i, 0)),
                   pl.BlockSpec((N_FM, tm), lambda i: (0, i)),
                   pl.BlockSpec((F_ROWS, tm), lambda i: (0, i)),
                   pl.BlockSpec((N_C_SLABS, tm, LANES), lambda i: (0, i, 0))),
        compiler_params=pltpu.CompilerParams(dimension_semantics=("parallel",),
                                             vmem_limit_bytes=VMEM_LIMIT),
        name="in_proj",
    )(x2, norm_g, w_tm, w_fm, w_f, w_c)


def _decay_kernel(ft_ref, bf_ref, ckp_ref):
    z = ft_ref[0:B_HEADS, :] + bf_ref[0]
    x = jnp.minimum(z, 0.0) - jnp.log1p(jnp.exp(-jnp.abs(z)))
    seq = x.shape[1]
    pos = lax.broadcasted_iota(jnp.int32, x.shape, 1)
    shift = 1
    while shift < seq:
        x = x + jnp.where(pos >= shift, pltpu.roll(x, shift, 1), 0.0)
        shift *= 2
    c = x * (-LOG2E)
    hi = c.astype(jnp.bfloat16).astype(jnp.float32)
    mid = (c - hi).astype(jnp.bfloat16).astype(jnp.float32)
    lo = c - hi - mid
    stacked = jnp.concatenate(
        [hi, mid, lo, jnp.zeros((LANES - 3 * B_HEADS, seq), jnp.float32)], axis=0)
    ckp_ref[0] = jnp.transpose(stacked).astype(jnp.bfloat16)


def _decay(ft, b_f, layer, bsz, seq):
    return pl.pallas_call(
        _decay_kernel,
        out_shape=jax.ShapeDtypeStruct((bsz, seq, LANES), jnp.bfloat16),
        grid=(bsz,),
        in_specs=[pl.BlockSpec((F_ROWS, seq), lambda b: (0, b)),
                  pl.BlockSpec((1, B_HEADS, 1), lambda b: (layer, 0, 0))],
        out_specs=pl.BlockSpec((1, seq, LANES), lambda b: (b, 0, 0)),
        compiler_params=pltpu.CompilerParams(dimension_semantics=("parallel",)),
        name="decay_scan",
    )(ft, b_f)


def _colmax(s):
    rows = s.shape[0]
    while rows > 8:
        rows //= 2
        s = jnp.maximum(s[:rows], s[rows:2 * rows])
    return jnp.max(s, axis=0, keepdims=True)


def _online_step(s, smax, v_aug, m_ref, acc_ref, idx):
    m_old = m_ref[idx]
    m_new = jnp.maximum(m_old, smax)
    alpha = jnp.exp2(m_old - m_new)
    p = jnp.exp2(s - m_new).astype(jnp.bfloat16)
    acc_ref[idx] = alpha * acc_ref[idx] + jnp.dot(v_aug, p, preferred_element_type=jnp.float32)
    m_ref[idx] = m_new


def _v_aug(vt_ref, head, k0, tile):
    v = vt_ref[head * HEAD_DIM:(head + 1) * HEAD_DIM, pl.ds(k0, tile)]
    return jnp.concatenate([v, jnp.ones((ONES_ROWS, tile), jnp.bfloat16)], axis=0)


def _flash_pipeline(qi, n_sub, group, produce, consume):
    every = range(n_sub)

    def overlapped(pk, pslot, ck, cslot):
        for g0 in range(0, n_sub, group):
            subs = range(g0, g0 + group)
            produce(pk, pslot, subs)
            consume(ck, cslot, False, subs)

    produce(0, 0, every)

    def pair(j, carry):
        k = 2 * j
        overlapped(k + 1, 1, k, 0)
        overlapped(k + 2, 0, k + 1, 1)
        return carry

    lax.fori_loop(0, qi // 2, pair, 0)

    @pl.when(qi % 2 == 0)
    def _():
        consume(qi, 0, True, every)

    @pl.when(qi % 2 == 1)
    def _():
        overlapped(qi, 1, qi - 1, 0)
        consume(qi, 1, True, every)


def _attn_a_kernel(qt_ref, k_ref, vt_ref, bias_ref, lamv_ref, lami_ref, subg_ref, o_ref,
                   m_ref, acc_ref, s_ref, smax_ref, *, tile, n_near):
    qi = pl.program_id(2)
    n_sub = 2 * A_STEP_HEADS
    qt = qt_ref[...].astype(jnp.float32)
    row = lax.broadcasted_iota(jnp.int32, (LANES, tile), 0)
    qs = []
    for s in range(n_sub):
        blk = qt[(s // 4) * LANES:(s // 4 + 1) * LANES]
        lo = (s % 4) * A_QK_DIM
        qs.append(jnp.where((row >= lo) & (row < lo + A_QK_DIM), blk, 0.0).astype(jnp.bfloat16))
    m_ref[...] = jnp.full(m_ref.shape, NEG_INF, jnp.float32)
    acc_ref[...] = jnp.zeros(acc_ref.shape, jnp.float32)

    def produce(ki, slot, subs):
        k0 = pl.multiple_of(ki * tile, tile)
        delta = jnp.minimum(qi - ki, n_near)
        for s in subs:
            k = k_ref[0, pl.ds(k0, tile), (s // 4) * LANES:(s // 4 + 1) * LANES]
            x = jnp.dot(k, qs[s], preferred_element_type=jnp.float32) + bias_ref[s // 2, delta]
            s_ref[slot, s] = x
            smax_ref[slot, s] = _colmax(x)

    def consume(ki, slot, last, subs):
        k0 = pl.multiple_of(ki * tile, tile)
        for s in subs:
            _online_step(s_ref[slot, s], smax_ref[slot, s], _v_aug(vt_ref, s // 2, k0, tile),
                         m_ref, acc_ref, s)

    _flash_pipeline(qi, n_sub, A_GROUP, produce, consume)

    lam_init = lami_ref[0]
    lv = lamv_ref[0]
    e1 = jnp.exp(jnp.sum(lv[0:1] * lv[1:2], axis=1, keepdims=True))
    e2 = jnp.exp(jnp.sum(lv[2:3] * lv[3:4], axis=1, keepdims=True))
    lam = e1 - e2 + lam_init
    heads = []
    for h in range(A_STEP_HEADS):
        a0, a1 = acc_ref[2 * h], acc_ref[2 * h + 1]
        o = (a0[0:HEAD_DIM] / a0[HEAD_DIM:HEAD_DIM + 1]
             - lam * (a1[0:HEAD_DIM] / a1[HEAD_DIM:HEAD_DIM + 1]))
        ms = jnp.mean(o * o, axis=0, keepdims=True)
        heads.append(o * lax.rsqrt(ms + NORM_EPS) * subg_ref[0] * (1.0 - lam_init))
    o_ref[0] = jnp.transpose(jnp.concatenate(heads, axis=0)).astype(o_ref.dtype)


def _attn_a(proj_tm3, proj_fm, bias_a, lam_vecs, lam_init, sub_g, layer):
    bsz, seq, _ = proj_tm3.shape
    tile = FLASH_TILE
    n_near = _n_near_tiles(tile)
    n_q = seq // tile
    n_h = A_STEP_HEADS
    width = n_h * HEAD_DIM
    return pl.pallas_call(
        functools.partial(_attn_a_kernel, tile=tile, n_near=n_near),
        out_shape=jax.ShapeDtypeStruct((bsz, seq, A_W), jnp.bfloat16),
        grid=(A_HEADS // n_h, bsz, n_q),
        in_specs=[
            pl.BlockSpec((width, tile), lambda hg, b, qi: (QA_T // width + hg, b * n_q + qi)),
            pl.BlockSpec((1, seq, width), lambda hg, b, qi: (b, 0, KA // width + hg)),
            pl.BlockSpec((width, seq), lambda hg, b, qi: (VA_T // width + hg, b)),
            pl.BlockSpec((n_h, n_near + 1, tile, tile), lambda hg, b, qi: (hg, 0, 0, 0)),
            pl.BlockSpec((1, 4, A_QK_DIM), lambda hg, b, qi: (layer, 0, 0)),
            pl.BlockSpec(memory_space=pltpu.SMEM),
            pl.BlockSpec((1, HEAD_DIM, 1), lambda hg, b, qi: (layer, 0, 0)),
        ],
        out_specs=pl.BlockSpec((1, tile, width), lambda hg, b, qi: (b, qi, hg)),
        scratch_shapes=[pltpu.VMEM((2 * n_h, 1, tile), jnp.float32),
                        pltpu.VMEM((2 * n_h, ACC_ROWS, tile), jnp.float32),
                        pltpu.VMEM((2, 2 * n_h, tile, tile), jnp.float32),
                        pltpu.VMEM((2, 2 * n_h, 1, tile), jnp.float32)],
        compiler_params=pltpu.CompilerParams(
            dimension_semantics=("parallel", "parallel", "parallel"),
            vmem_limit_bytes=VMEM_LIMIT),
        name="attn_a",
    )(proj_fm, proj_tm3, proj_fm, bias_a, lam_vecs, lam_init, sub_g)


def _attn_b_kernel(qt_ref, k_ref, ckp_ref, vt_ref, o_ref, m_ref, acc_ref, s_ref, smax_ref, *,
                   tile):
    hg = pl.program_id(0)
    qi = pl.program_id(2)
    n_h = B_STEP_HEADS
    qt = qt_ref[...].astype(jnp.float32)
    row = lax.broadcasted_iota(jnp.int32, (LANES, tile), 0)
    q_aug = []
    for s in range(n_h):
        blk = qt[(s // 2) * LANES:(s // 2 + 1) * LANES]
        lo = (s % 2) * HEAD_DIM
        q_s = jnp.where((row >= lo) & (row < lo + HEAD_DIM), blk, 0.0)
        pick = (row < 3 * B_HEADS) & ((row % B_HEADS) == n_h * hg + s)
        ones = jnp.where(pick, 1.0, 0.0)
        q_aug.append(jnp.concatenate([q_s, ones], axis=0).astype(jnp.bfloat16))
    m_ref[...] = jnp.full(m_ref.shape, NEG_INF, jnp.float32)
    acc_ref[...] = jnp.zeros(acc_ref.shape, jnp.float32)

    def produce(ki, slot, subs):
        k0 = pl.multiple_of(ki * tile, tile)
        ck = ckp_ref[0, pl.ds(k0, tile), :]
        for s in subs:
            k = k_ref[0, pl.ds(k0, tile), (s // 2) * LANES:(s // 2 + 1) * LANES]
            x = jnp.dot(jnp.concatenate([k, ck], axis=1), q_aug[s],
                        preferred_element_type=jnp.float32)
            s_ref[slot, s] = x
            smax_ref[slot, s] = _colmax(x)

    def consume(ki, slot, last, subs):
        k0 = pl.multiple_of(ki * tile, tile)
        if last:
            kk = lax.broadcasted_iota(jnp.int32, (tile, tile), 0)
            qq = lax.broadcasted_iota(jnp.int32, (tile, tile), 1)
            keep = kk <= qq
        for s in subs:
            x, xmax = s_ref[slot, s], smax_ref[slot, s]
            if last:
                x = jnp.where(keep, x, NEG_INF)
                xmax = _colmax(x)
            _online_step(x, xmax, _v_aug(vt_ref, s, k0, tile), m_ref, acc_ref, s)

    _flash_pipeline(qi, n_h, B_GROUP, produce, consume)
    heads = [acc_ref[s][0:HEAD_DIM] / acc_ref[s][HEAD_DIM:HEAD_DIM + 1] for s in range(n_h)]
    o_ref[0] = jnp.transpose(jnp.concatenate(heads, axis=0)).astype(o_ref.dtype)


def _attn_b(proj_tm3, proj_fm, ckp):
    bsz, seq, _ = proj_tm3.shape
    tile = FLASH_TILE
    n_q = seq // tile
    n_h = B_STEP_HEADS
    width = n_h * HEAD_DIM
    return pl.pallas_call(
        functools.partial(_attn_b_kernel, tile=tile),
        out_shape=jax.ShapeDtypeStruct((bsz, seq, B_W), jnp.bfloat16),
        grid=(B_HEADS // n_h, bsz, n_q),
        in_specs=[
            pl.BlockSpec((width, tile), lambda hg, b, qi: (QB_T // width + hg, b * n_q + qi)),
            pl.BlockSpec((1, seq, width), lambda hg, b, qi: (b, 0, KB // width + hg)),
            pl.BlockSpec((1, seq, LANES), lambda hg, b, qi: (b, 0, 0)),
            pl.BlockSpec((width, seq), lambda hg, b, qi: (VB_T // width + hg, b)),
        ],
        out_specs=pl.BlockSpec((1, tile, width), lambda hg, b, qi: (b, qi, hg)),
        scratch_shapes=[pltpu.VMEM((n_h, 1, tile), jnp.float32),
                        pltpu.VMEM((n_h, ACC_ROWS, tile), jnp.float32),
                        pltpu.VMEM((2, n_h, tile, tile), jnp.float32),
                        pltpu.VMEM((2, n_h, 1, tile), jnp.float32)],
        compiler_params=pltpu.CompilerParams(
            dimension_semantics=("parallel", "parallel", "parallel"),
            vmem_limit_bytes=VMEM_LIMIT),
        name="attn_b",
    )(proj_fm, proj_tm3, ckp, proj_fm)


def _attn_c_kernel(q_ref, k_ref, v_ref, bias_ref, o_ref, qd_ref, kd_ref, vtd_ref, onat_ref,
                   lnat_ref, *, seq):
    g = pl.program_id(2)
    n_chunks = seq // C_TILE
    lane = lax.broadcasted_iota(jnp.int32, (1, LANES), 1)
    first = lane < HEAD_DIM

    def run_group(group, dil):
        n_tiles = seq // dil // C_TILE

        def src_rows(i):
            r, t = i // n_tiles, i % n_tiles
            return pl.ds(r + t * (C_TILE * dil), C_TILE, stride=dil), t

        kd_ref[0:C_TILE, :] = jnp.zeros((C_TILE, LANES), jnp.bfloat16)
        vtd_ref[:, 0:C_TILE] = jnp.zeros((LANES, C_TILE), jnp.bfloat16)

        def gather(j, carry):
            ids = [j * C_UNROLL + u for u in range(C_UNROLL)]
            vts = [jnp.transpose(v_ref[src_rows(i)[0], :]) for i in ids]
            for i, vt in zip(ids, vts):
                rows, _ = src_rows(i)
                dst = pl.multiple_of(i * C_TILE, C_TILE)
                qd_ref[pl.ds(dst, C_TILE), :] = q_ref[rows, :].astype(jnp.bfloat16)
                kd_ref[pl.ds(dst + C_TILE, C_TILE), :] = k_ref[rows, :].astype(jnp.bfloat16)
                vtd_ref[:, pl.ds(dst + C_TILE, C_TILE)] = vt.astype(jnp.bfloat16)
            return carry

        lax.fori_loop(0, n_chunks // C_UNROLL, gather, 0)

        def tiles(j, carry):
            ids = [j * C_UNROLL + u for u in range(C_UNROLL)]
            row0s = [pl.multiple_of(i * C_TILE, C_TILE) for i in ids]

            def scores(u):
                q = qd_ref[pl.ds(row0s[u], C_TILE), :]
                q2 = jnp.concatenate([jnp.where(first, q, jnp.zeros_like(q)),
                                      jnp.where(first, jnp.zeros_like(q), q)], axis=0)
                return _dot_nt(kd_ref[pl.ds(row0s[u], 2 * C_TILE), :], q2)

            def softmax_pv(u, st):
                ver = jnp.minimum(ids[u] % n_tiles, 1)
                res = []
                for h in range(2):
                    x = st[:, h * C_TILE:(h + 1) * C_TILE] + bias_ref[h, ver]
                    m = _colmax(x)
                    p = jnp.exp2(x - m).astype(jnp.bfloat16)
                    v_aug = jnp.concatenate(
                        [vtd_ref[h * HEAD_DIM:(h + 1) * HEAD_DIM, pl.ds(row0s[u], 2 * C_TILE)],
                         jnp.ones((ONES_ROWS, 2 * C_TILE), jnp.bfloat16)], axis=0)
                    res.append((m, jnp.dot(v_aug, p, preferred_element_type=jnp.float32)))
                return res

            def write_back(u, res):
                outs, lses = [], []
                for m, pv in res:
                    l = pv[HEAD_DIM:HEAD_DIM + 1]
                    outs.append(pv[0:HEAD_DIM] / l)
                    lses.append(jnp.broadcast_to(m + jnp.log2(l), (HEAD_DIM, C_TILE)))
                rows, _ = src_rows(ids[u])
                onat_ref[group, rows, :] = jnp.transpose(jnp.concatenate(outs, axis=0))
                lnat_ref[group, rows, :] = jnp.transpose(jnp.concatenate(lses, axis=0))

            sts = {u: scores(u) for u in range(C_AHEAD)}
            pending = None
            for u in range(C_UNROLL):
                if u + C_AHEAD < C_UNROLL:
                    sts[u + C_AHEAD] = scores(u + C_AHEAD)
                res = softmax_pv(u, sts.pop(u))
                if pending is not None:
                    write_back(*pending)
                pending = (u, res)
            write_back(*pending)
            return carry

        lax.fori_loop(0, n_chunks // C_UNROLL, tiles, 0)

    for group, (_, dil) in enumerate(C_PAIRS):
        pl.when(g == group)(functools.partial(run_group, group, dil))

    @pl.when(g == C_GROUPS - 1)
    def _():
        rows_per = 512

        def combine(c, carry):
            rows = pl.ds(pl.multiple_of(c * rows_per, rows_per), rows_per)
            l0, l1, l2 = lnat_ref[0, rows, :], lnat_ref[1, rows, :], lnat_ref[2, rows, :]
            m = jnp.maximum(jnp.maximum(l0, l1), l2)
            e0, e1, e2 = jnp.exp2(l0 - m), jnp.exp2(l1 - m), jnp.exp2(l2 - m)
            num = e0 * onat_ref[0, rows, :] + e1 * onat_ref[1, rows, :] + e2 * onat_ref[2, rows, :]
            o_ref[0, rows, :] = (num / (e0 + e1 + e2)).astype(o_ref.dtype)
            return carry

        lax.fori_loop(0, seq // rows_per, combine, 0)


def _attn_c(proj_c, bias_c, bsz, seq):
    slab = lambda role: pl.BlockSpec(
        (None, seq, LANES), lambda b, hp, g: (role * 2 * C_GROUPS + g * 2 + hp, b, 0))
    return pl.pallas_call(
        functools.partial(_attn_c_kernel, seq=seq),
        out_shape=jax.ShapeDtypeStruct((bsz, seq, C_W), jnp.bfloat16),
        grid=(bsz, C_HEADS // 2, C_GROUPS),
        in_specs=[slab(0), slab(1), slab(2),
                  pl.BlockSpec((2, 2, 2 * C_TILE, C_TILE), lambda b, hp, g: (g * 2 + hp, 0, 0, 0))],
        out_specs=pl.BlockSpec((1, seq, LANES), lambda b, hp, g: (b, 0, hp)),
        scratch_shapes=[pltpu.VMEM((seq, LANES), jnp.bfloat16),
                        pltpu.VMEM((seq + C_TILE, LANES), jnp.bfloat16),
                        pltpu.VMEM((LANES, seq + C_TILE), jnp.bfloat16),
                        pltpu.VMEM((C_GROUPS, seq, LANES), jnp.float32),
                        pltpu.VMEM((C_GROUPS, seq, LANES), jnp.float32)],
        compiler_params=pltpu.CompilerParams(
            dimension_semantics=("parallel", "parallel", "arbitrary"),
            vmem_limit_bytes=VMEM_LIMIT),
        name="attn_c",
    )(proj_c, proj_c, proj_c, bias_c)


def _mix_mlp_kernel(x_ref, oa_ref, ob_ref, oc_ref, wo_ref, g_ref, w1_ref, w2_ref, fg_ref, out_ref,
                    acc_ref, h_ref, *, final):
    f = pl.program_id(1)

    @pl.when(f == 0)
    def _():
        wo = wo_ref.at[0]
        y = jnp.dot(oa_ref[...], wo[0:A_W, :], preferred_element_type=jnp.float32)
        y += jnp.dot(ob_ref[...], wo[A_W:A_W + B_W, :], preferred_element_type=jnp.float32)
        y += jnp.dot(oc_ref[...], wo[A_W + B_W:, :], preferred_element_type=jnp.float32)
        x1 = x_ref[...] + y
        acc_ref[...] = x1
        h_ref[...] = _rms(x1, g_ref[0]).astype(jnp.bfloat16)

    u = jnp.dot(h_ref[...], w1_ref[0], preferred_element_type=jnp.float32)
    u = jnp.square(jnp.maximum(u, 0.0)).astype(jnp.bfloat16)
    acc_ref[...] += jnp.dot(u, w2_ref[0], preferred_element_type=jnp.float32)

    @pl.when(f == pl.num_programs(1) - 1)
    def _():
        out_ref[...] = _rms(acc_ref[...], fg_ref[...]) if final else acc_ref[...]


def _mix_mlp(x2, oa, ob, oc, w_o, norm_g, w_1, w_2, final_g, layer, final, *, tm=1024, tf=1024):
    n_tok = x2.shape[0]
    row = lambda w: pl.BlockSpec((tm, w), lambda i, f: (i, 0))
    return pl.pallas_call(
        functools.partial(_mix_mlp_kernel, final=final),
        out_shape=jax.ShapeDtypeStruct((n_tok, D_MODEL), jnp.float32),
        grid=(n_tok // tm, D_FF // tf),
        in_specs=[row(D_MODEL), row(A_W), row(B_W), row(C_W),
                  pl.BlockSpec((1, D_MODEL, D_MODEL), lambda i, f: (layer, 0, 0)),
                  pl.BlockSpec((1, 1, D_MODEL), lambda i, f: (layer, 0, 0)),
                  pl.BlockSpec((1, D_MODEL, tf), lambda i, f: (layer, 0, f)),
                  pl.BlockSpec((1, tf, D_MODEL), lambda i, f: (layer, f, 0)),
                  pl.BlockSpec((1, D_MODEL), lambda i, f: (0, 0))],
        out_specs=row(D_MODEL),
        scratch_shapes=[pltpu.VMEM((tm, D_MODEL), jnp.float32),
                        pltpu.VMEM((tm, D_MODEL), jnp.bfloat16)],
        compiler_params=pltpu.CompilerParams(dimension_semantics=("parallel", "arbitrary"),
                                             vmem_limit_bytes=VMEM_LIMIT),
        name="mix_mlp",
    )(x2, oa, ob, oc, w_o, norm_g, w_1, w_2, final_g)


def _prep_w_in(w_in):
    a0, b0, f0, c0 = 0, 3 * A_W, 3 * A_W + 3 * B_W, 3 * A_W + 3 * B_W + B_HEADS
    sa, sb = A_QK_DIM ** -0.5 * LOG2E, HEAD_DIM ** -0.5 * LOG2E
    cw = C_GROUPS * C_W
    w_tm = jnp.concatenate([w_in[:, :, b0 + B_W:b0 + 2 * B_W],
                            w_in[:, :, a0 + A_W:a0 + 2 * A_W]], axis=-1).astype(jnp.bfloat16)
    w_fm = jnp.concatenate([
        w_in[:, :, b0:b0 + B_W] * sb, w_in[:, :, b0 + 2 * B_W:f0],
        w_in[:, :, a0:a0 + A_W] * sa, w_in[:, :, a0 + 2 * A_W:b0],
    ], axis=-1)
    w_fm = jnp.transpose(w_fm, (0, 2, 1)).astype(jnp.bfloat16)
    w_f = jnp.transpose(w_in[:, :, f0:c0], (0, 2, 1))
    w_f = jnp.pad(w_f, ((0, 0), (0, F_ROWS - B_HEADS), (0, 0))).astype(jnp.bfloat16)
    w_c = jnp.concatenate([w_in[:, :, c0:c0 + cw] * sb, w_in[:, :, c0 + cw:]],
                          axis=-1).astype(jnp.bfloat16)
    return w_tm, w_fm, w_f, w_c


def kernel(x, norm1_g, w_in, b_f, lam_q1, lam_k1, lam_q2, lam_k2, diff_norm_g, w_o, norm2_g,
           w_1, w_2, rel_bias, final_g):
    bsz, seq, _ = x.shape
    depth = w_in.shape[0]
    w_tm, w_fm, w_f, w_c = _prep_w_in(w_in)
    w_o16, w_116, w_216 = (w.astype(jnp.bfloat16) for w in (w_o, w_1, w_2))
    lam_vecs = jnp.stack([lam_q1, lam_k1, lam_q2, lam_k2], axis=1)
    bias_a, bias_c = _make_bias_tiles(rel_bias)

    x2 = x.reshape(bsz * seq, D_MODEL)
    for l in range(depth):
        proj_tm, proj_fm, ft, proj_c = _in_proj(x2, norm1_g[:, None, :], w_tm, w_fm, w_f, w_c, l)
        proj3 = proj_tm.reshape(bsz, seq, N_TM)
        ckp = _decay(ft, b_f[:, :, None], l, bsz, seq)
        lam_init = jnp.full((1,), 0.8 - 0.6 * math.exp(-0.3 * l), jnp.float32)
        oa = _attn_a(proj3, proj_fm, bias_a, lam_vecs, lam_init, diff_norm_g[:, :, None], l)
        ob = _attn_b(proj3, proj_fm, ckp)
        oc = _attn_c(proj_c, bias_c, bsz, seq)
        x2 = _mix_mlp(x2, oa.reshape(bsz * seq, A_W), ob.reshape(bsz * seq, B_W),
                      oc.reshape(bsz * seq, C_W), w_o16, norm2_g[:, None, :], w_116, w_216,
                      final_g[None, :], l, l == depth - 1)
    return x2.reshape(bsz, seq, D_MODEL)
```

```python
import functools
import math

import jax
import jax.numpy as jnp
from jax import lax
from jax.experimental import pallas as pl
from jax.experimental.pallas import tpu as pltpu

D_MODEL = 1024
HEAD_DIM = 64
A_HEADS = 4
A_QK_DIM = 32
B_HEADS = 8
C_HEADS = 4
C_PAIRS = ((128, 1), (512, 4), (2048, 16))
C_GROUPS = 3
A_W = 256
B_W = 512
C_W = 256
D_FF = 4096
N_BUCKETS = 32
MAX_DISTANCE = 2048
NORM_EPS = 1e-6
NEG_INF = -1e30
LOG2E = 1.4426950408889634

LANES = 128
VMEM_LIMIT = 56 * 1024 * 1024

KB, KA = 0, 512
N_TM = 768
QB_T, VB_T, QA_T, VA_T = 0, 512, 1024, 1280
N_FM = 1536
N_C = 2304
N_C_SLABS = N_C // 128
F_ROWS = 16
ONES_ROWS = 16
ACC_ROWS = HEAD_DIM + ONES_ROWS

A_STEP_HEADS = 4
B_STEP_HEADS = 8
C_TILE = 128
C_UNROLL = 16
C_AHEAD = 4
FLASH_TILE = 256
A_GROUP = 2
B_GROUP = 1


def _bucket_thresholds():
    max_exact = N_BUCKETS // 2
    out = []
    for k in range(1, N_BUCKETS - max_exact):
        t = max_exact * (MAX_DISTANCE / max_exact) ** (k / (N_BUCKETS - max_exact))
        out.append(int(math.ceil(t)))
    return tuple(out)


_THRESHOLDS = _bucket_thresholds()


def _n_near_tiles(tile):
    return -(-(_THRESHOLDS[-1] + tile - 1) // tile)


def _t5_bucket(d):
    big = jnp.full(d.shape, N_BUCKETS // 2, jnp.int32)
    for t in _THRESHOLDS:
        big = big + (d >= t).astype(jnp.int32)
    return jnp.where(d < N_BUCKETS // 2, d, big)


def _bias_lookup(rb_ref, bucket, col):
    val = jnp.zeros(bucket.shape, jnp.float32)
    for b in range(N_BUCKETS):
        val = jnp.where(bucket == b, rb_ref[b, col], val)
    return val


def _bias_a_kernel(rb_ref, out_ref, *, tile):
    delta = pl.program_id(0)
    j = lax.broadcasted_iota(jnp.int32, (tile, tile), 0)
    i = lax.broadcasted_iota(jnp.int32, (tile, tile), 1)
    d = delta * tile + i - j
    bucket = _t5_bucket(d)
    for h in range(A_HEADS):
        val = _bias_lookup(rb_ref, bucket, h) * LOG2E
        out_ref[h, 0] = jnp.where(d >= 0, val, NEG_INF)


def _bias_c_kernel(rb_ref, out_ref):
    g = pl.program_id(0)
    dil = jnp.where(g == 0, C_PAIRS[0][1], jnp.where(g == 1, C_PAIRS[1][1], C_PAIRS[2][1]))
    c = lax.broadcasted_iota(jnp.int32, (2 * C_TILE, C_TILE), 0)
    i = lax.broadcasted_iota(jnp.int32, (2 * C_TILE, C_TILE), 1)
    steps = i + C_TILE - c
    valid = (steps >= 0) & (steps <= C_TILE)
    bucket = _t5_bucket(steps * dil)
    for h in range(C_HEADS):
        val = _bias_lookup(rb_ref, bucket, A_HEADS + g * C_HEADS + h) * LOG2E
        out_ref[h, 1] = jnp.where(valid, val, NEG_INF)
        out_ref[h, 0] = jnp.where(valid & (c >= C_TILE), val, NEG_INF)


def _make_bias_tiles(rel_bias):
    tile = FLASH_TILE
    n_bias = _n_near_tiles(tile) + 1
    smem = pl.BlockSpec(memory_space=pltpu.SMEM)
    bias_a = pl.pallas_call(
        functools.partial(_bias_a_kernel, tile=tile),
        out_shape=jax.ShapeDtypeStruct((A_HEADS, n_bias, tile, tile), jnp.float32),
        grid=(n_bias,),
        in_specs=[smem],
        out_specs=pl.BlockSpec((A_HEADS, 1, tile, tile), lambda d: (0, d, 0, 0)),
        name="bias_a_tiles",
    )(rel_bias)
    bias_c = pl.pallas_call(
        _bias_c_kernel,
        out_shape=jax.ShapeDtypeStruct((C_GROUPS * C_HEADS, 2, 2 * C_TILE, C_TILE), jnp.float32),
        grid=(C_GROUPS,),
        in_specs=[smem],
        out_specs=pl.BlockSpec((C_HEADS, 2, 2 * C_TILE, C_TILE), lambda g: (g, 0, 0, 0)),
        name="bias_c_tiles",
    )(rel_bias)
    return bias_a, bias_c


def _rms(x, g):
    return x * lax.rsqrt(jnp.mean(x * x, axis=-1, keepdims=True) + NORM_EPS) * g


def _dot_nt(a, b):
    return lax.dot_general(a, b, (((1,), (1,)), ((), ())), preferred_element_type=jnp.float32)


def _in_proj_kernel(x_ref, g_ref, wtm_ref, wfm_ref, wf_ref, wc_ref, tm_ref, fm_ref, ft_ref, c_ref):
    h = _rms(x_ref[...], g_ref[0]).astype(jnp.bfloat16)
    tm_ref[...] = jnp.dot(h, wtm_ref[0], preferred_element_type=jnp.float32).astype(jnp.bfloat16)
    for c in range(N_FM // 512):
        rows = slice(c * 512, (c + 1) * 512)
        fm_ref[rows, :] = _dot_nt(wfm_ref[0, rows, :], h).astype(jnp.bfloat16)
    ft_ref[...] = _dot_nt(wf_ref[0], h)
    for c in range(N_C // 256):
        pc = jnp.dot(h, wc_ref[0, :, c * 256:(c + 1) * 256], preferred_element_type=jnp.float32)
        c_ref[2 * c] = pc[:, :LANES]
        c_ref[2 * c + 1] = pc[:, LANES:]


def _in_proj(x2, norm_g, w_tm, w_fm, w_f, w_c, layer, *, tm=512):
    n_tok = x2.shape[0]
    return pl.pallas_call(
        _in_proj_kernel,
        out_shape=(jax.ShapeDtypeStruct((n_tok, N_TM), jnp.bfloat16),
                   jax.ShapeDtypeStruct((N_FM, n_tok), jnp.bfloat16),
                   jax.ShapeDtypeStruct((F_ROWS, n_tok), jnp.float32),
                   jax.ShapeDtypeStruct((N_C_SLABS, n_tok, LANES), jnp.float32)),
        grid=(n_tok // tm,),
        in_specs=[
            pl.BlockSpec((tm, D_MODEL), lambda i: (i, 0)),
            pl.BlockSpec((1, 1, D_MODEL), lambda i: (layer, 0, 0)),
            pl.BlockSpec((1, D_MODEL, N_TM), lambda i: (layer, 0, 0)),
            pl.BlockSpec((1, N_FM, D_MODEL), lambda i: (layer, 0, 0)),
            pl.BlockSpec((1, F_ROWS, D_MODEL), lambda i: (layer, 0, 0)),
            pl.BlockSpec((1, D_MODEL, N_C), lambda i: (layer, 0, 0)),
        ],
        out_specs=(pl.BlockSpec((tm, N_TM), lambda i: (i, 0)),
                   pl.BlockSpec((N_FM, tm), lambda i: (0, i)),
                   pl.BlockSpec((F_ROWS, tm), lambda i: (0, i)),
                   pl.BlockSpec((N_C_SLABS, tm, LANES), lambda i: (0, i, 0))),
        compiler_params=pltpu.CompilerParams(dimension_semantics=("parallel",),
                                             vmem_limit_bytes=VMEM_LIMIT),
        name="in_proj",
    )(x2, norm_g, w_tm, w_fm, w_f, w_c)


def _decay_kernel(ft_ref, bf_ref, ckp_ref):
    z = ft_ref[0:B_HEADS, :] + bf_ref[0]
    x = jnp.minimum(z, 0.0) - jnp.log1p(jnp.exp(-jnp.abs(z)))
    seq = x.shape[1]
    pos = lax.broadcasted_iota(jnp.int32, x.shape, 1)
    shift = 1
    while shift < seq:
        x = x + jnp.where(pos >= shift, pltpu.roll(x, shift, 1), 0.0)
        shift *= 2
    c = x * (-LOG2E)
    hi = c.astype(jnp.bfloat16).astype(jnp.float32)
    mid = (c - hi).astype(jnp.bfloat16).astype(jnp.float32)
    lo = c - hi - mid
    stacked = jnp.concatenate(
        [hi, mid, lo, jnp.zeros((LANES - 3 * B_HEADS, seq), jnp.float32)], axis=0)
    ckp_ref[0] = jnp.transpose(stacked).astype(jnp.bfloat16)


def _decay(ft, b_f, layer, bsz, seq):
    return pl.pallas_call(
        _decay_kernel,
        out_shape=jax.ShapeDtypeStruct((bsz, seq, LANES), jnp.bfloat16),
        grid=(bsz,),
        in_specs=[pl.BlockSpec((F_ROWS, seq), lambda b: (0, b)),
                  pl.BlockSpec((1, B_HEADS, 1), lambda b: (layer, 0, 0))],
        out_specs=pl.BlockSpec((1, seq, LANES), lambda b: (b, 0, 0)),
        compiler_params=pltpu.CompilerParams(dimension_semantics=("parallel",)),
        name="decay_scan",
    )(ft, b_f)


def _colmax(s):
    rows = s.shape[0]
    while rows > 8:
        rows //= 2
        s = jnp.maximum(s[:rows], s[rows:2 * rows])
    return jnp.max(s, axis=0, keepdims=True)


def _online_step(s, smax, v_aug, m_ref, acc_ref, idx):
    m_old = m_ref[idx]
    m_new = jnp.maximum(m_old, smax)
    alpha = jnp.exp2(m_old - m_new)
    p = jnp.exp2(s - m_new).astype(jnp.bfloat16)
    acc_ref[idx] = alpha * acc_ref[idx] + jnp.dot(v_aug, p, preferred_element_type=jnp.float32)
    m_ref[idx] = m_new


def _v_aug(vt_ref, head, k0, tile):
    v = vt_ref[head * HEAD_DIM:(head + 1) * HEAD_DIM, pl.ds(k0, tile)]
    return jnp.concatenate([v, jnp.ones((ONES_ROWS, tile), jnp.bfloat16)], axis=0)


def _flash_pipeline(qi, n_sub, group, produce, consume):
    every = range(n_sub)

    def overlapped(pk, pslot, ck, cslot):
        for g0 in range(0, n_sub, group):
            subs = range(g0, g0 + group)
            produce(pk, pslot, subs)
            consume(ck, cslot, False, subs)

    produce(0, 0, every)

    def pair(j, carry):
        k = 2 * j
        overlapped(k + 1, 1, k, 0)
        overlapped(k + 2, 0, k + 1, 1)
        return carry

    lax.fori_loop(0, qi // 2, pair, 0)

    @pl.when(qi % 2 == 0)
    def _():
        consume(qi, 0, True, every)

    @pl.when(qi % 2 == 1)
    def _():
        overlapped(qi, 1, qi - 1, 0)
        consume(qi, 1, True, every)


def _attn_a_kernel(qt_ref, k_ref, vt_ref, bias_ref, lamv_ref, lami_ref, subg_ref, o_ref,
                   m_ref, acc_ref, s_ref, smax_ref, *, tile, n_near):
    n_sub = 2 * A_STEP_HEADS
    lax.fori_loop(0, qt_ref.shape[1] // tile,
                  lambda qi, carry: _attn_a_tile(qi, qt_ref, k_ref, vt_ref, bias_ref, lamv_ref,
                                                 lami_ref, subg_ref, o_ref, m_ref, acc_ref, s_ref,
                                                 smax_ref, tile, n_near, n_sub), 0)


def _attn_a_tile(qi, qt_ref, k_ref, vt_ref, bias_ref, lamv_ref, lami_ref, subg_ref, o_ref,
                 m_ref, acc_ref, s_ref, smax_ref, tile, n_near, n_sub):
    q0 = pl.multiple_of(qi * tile, tile)
    qt = qt_ref[:, pl.ds(q0, tile)].astype(jnp.float32)
    row = lax.broadcasted_iota(jnp.int32, (LANES, tile), 0)
    qs = []
    for s in range(n_sub):
        blk = qt[(s // 4) * LANES:(s // 4 + 1) * LANES]
        lo = (s % 4) * A_QK_DIM
        qs.append(jnp.where((row >= lo) & (row < lo + A_QK_DIM), blk, 0.0).astype(jnp.bfloat16))
    m_ref[...] = jnp.full(m_ref.shape, NEG_INF, jnp.float32)
    acc_ref[...] = jnp.zeros(acc_ref.shape, jnp.float32)

    def produce(ki, slot, subs):
        k0 = pl.multiple_of(ki * tile, tile)
        delta = jnp.minimum(qi - ki, n_near)
        for s in subs:
            k = k_ref[0, pl.ds(k0, tile), (s // 4) * LANES:(s // 4 + 1) * LANES]
            x = jnp.dot(k, qs[s], preferred_element_type=jnp.float32) + bias_ref[s // 2, delta]
            s_ref[slot, s] = x
            smax_ref[slot, s] = _colmax(x)

    def consume(ki, slot, last, subs):
        k0 = pl.multiple_of(ki * tile, tile)
        for s in subs:
            _online_step(s_ref[slot, s], smax_ref[slot, s], _v_aug(vt_ref, s // 2, k0, tile),
                         m_ref, acc_ref, s)

    _flash_pipeline(qi, n_sub, A_GROUP, produce, consume)

    lam_init = lami_ref[0]
    lv = lamv_ref[0]
    e1 = jnp.exp(jnp.sum(lv[0:1] * lv[1:2], axis=1, keepdims=True))
    e2 = jnp.exp(jnp.sum(lv[2:3] * lv[3:4], axis=1, keepdims=True))
    lam = e1 - e2 + lam_init
    heads = []
    for h in range(A_STEP_HEADS):
        a0, a1 = acc_ref[2 * h], acc_ref[2 * h + 1]
        o = (a0[0:HEAD_DIM] / a0[HEAD_DIM:HEAD_DIM + 1]
             - lam * (a1[0:HEAD_DIM] / a1[HEAD_DIM:HEAD_DIM + 1]))
        ms = jnp.mean(o * o, axis=0, keepdims=True)
        heads.append(o * lax.rsqrt(ms + NORM_EPS) * subg_ref[0] * (1.0 - lam_init))
    o_ref[0, pl.ds(q0, tile), :] = jnp.transpose(
        jnp.concatenate(heads, axis=0)).astype(o_ref.dtype)
    return 0


def _attn_a(proj_tm3, proj_fm, bias_a, lam_vecs, lam_init, sub_g, layer):
    bsz, seq, _ = proj_tm3.shape
    tile = FLASH_TILE
    n_near = _n_near_tiles(tile)
    n_h = A_STEP_HEADS
    width = n_h * HEAD_DIM
    return pl.pallas_call(
        functools.partial(_attn_a_kernel, tile=tile, n_near=n_near),
        out_shape=jax.ShapeDtypeStruct((bsz, seq, A_W), jnp.bfloat16),
        grid=(A_HEADS // n_h, bsz),
        in_specs=[
            pl.BlockSpec((width, seq), lambda hg, b: (QA_T // width + hg, b)),
            pl.BlockSpec((1, seq, width), lambda hg, b: (b, 0, KA // width + hg)),
            pl.BlockSpec((width, seq), lambda hg, b: (VA_T // width + hg, b)),
            pl.BlockSpec((n_h, n_near + 1, tile, tile), lambda hg, b: (hg, 0, 0, 0)),
            pl.BlockSpec((1, 4, A_QK_DIM), lambda hg, b: (layer, 0, 0)),
            pl.BlockSpec(memory_space=pltpu.SMEM),
            pl.BlockSpec((1, HEAD_DIM, 1), lambda hg, b: (layer, 0, 0)),
        ],
        out_specs=pl.BlockSpec((1, seq, width), lambda hg, b: (b, 0, hg)),
        scratch_shapes=[pltpu.VMEM((2 * n_h, 1, tile), jnp.float32),
                        pltpu.VMEM((2 * n_h, ACC_ROWS, tile), jnp.float32),
                        pltpu.VMEM((2, 2 * n_h, tile, tile), jnp.float32),
                        pltpu.VMEM((2, 2 * n_h, 1, tile), jnp.float32)],
        compiler_params=pltpu.CompilerParams(
            dimension_semantics=("parallel", "parallel"),
            vmem_limit_bytes=VMEM_LIMIT),
        name="attn_a",
    )(proj_fm, proj_tm3, proj_fm, bias_a, lam_vecs, lam_init, sub_g)


def _attn_b_kernel(qt_ref, k_ref, ckp_ref, vt_ref, o_ref, m_ref, acc_ref, s_ref, smax_ref, *,
                   tile):
    lax.fori_loop(0, qt_ref.shape[1] // tile,
                  lambda qi, carry: _attn_b_tile(qi, qt_ref, k_ref, ckp_ref, vt_ref, o_ref, m_ref,
                                                 acc_ref, s_ref, smax_ref, tile), 0)


def _attn_b_tile(qi, qt_ref, k_ref, ckp_ref, vt_ref, o_ref, m_ref, acc_ref, s_ref, smax_ref, tile):
    hg = pl.program_id(0)
    n_h = B_STEP_HEADS
    q0 = pl.multiple_of(qi * tile, tile)
    qt = qt_ref[:, pl.ds(q0, tile)].astype(jnp.float32)
    row = lax.broadcasted_iota(jnp.int32, (LANES, tile), 0)
    q_aug = []
    for s in range(n_h):
        blk = qt[(s // 2) * LANES:(s // 2 + 1) * LANES]
        lo = (s % 2) * HEAD_DIM
        q_s = jnp.where((row >= lo) & (row < lo + HEAD_DIM), blk, 0.0)
        pick = (row < 3 * B_HEADS) & ((row % B_HEADS) == n_h * hg + s)
        ones = jnp.where(pick, 1.0, 0.0)
        q_aug.append(jnp.concatenate([q_s, ones], axis=0).astype(jnp.bfloat16))
    m_ref[...] = jnp.full(m_ref.shape, NEG_INF, jnp.float32)
    acc_ref[...] = jnp.zeros(acc_ref.shape, jnp.float32)

    def produce(ki, slot, subs):
        k0 = pl.multiple_of(ki * tile, tile)
        ck = ckp_ref[0, pl.ds(k0, tile), :]
        for s in subs:
            k = k_ref[0, pl.ds(k0, tile), (s // 2) * LANES:(s // 2 + 1) * LANES]
            x = jnp.dot(jnp.concatenate([k, ck], axis=1), q_aug[s],
                        preferred_element_type=jnp.float32)
            s_ref[slot, s] = x
            smax_ref[slot, s] = _colmax(x)

    def consume(ki, slot, last, subs):
        k0 = pl.multiple_of(ki * tile, tile)
        if last:
            kk = lax.broadcasted_iota(jnp.int32, (tile, tile), 0)
            qq = lax.broadcasted_iota(jnp.int32, (tile, tile), 1)
            keep = kk <= qq
        for s in subs:
            x, xmax = s_ref[slot, s], smax_ref[slot, s]
            if last:
                x = jnp.where(keep, x, NEG_INF)
                xmax = _colmax(x)
            _online_step(x, xmax, _v_aug(vt_ref, s, k0, tile), m_ref, acc_ref, s)

    _flash_pipeline(qi, n_h, B_GROUP, produce, consume)
    heads = [acc_ref[s][0:HEAD_DIM] / acc_ref[s][HEAD_DIM:HEAD_DIM + 1] for s in range(n_h)]
    o_ref[0, pl.ds(q0, tile), :] = jnp.transpose(
        jnp.concatenate(heads, axis=0)).astype(o_ref.dtype)
    return 0


def _attn_b(proj_tm3, proj_fm, ckp):
    bsz, seq, _ = proj_tm3.shape
    tile = FLASH_TILE
    n_h = B_STEP_HEADS
    width = n_h * HEAD_DIM
    return pl.pallas_call(
        functools.partial(_attn_b_kernel, tile=tile),
        out_shape=jax.ShapeDtypeStruct((bsz, seq, B_W), jnp.bfloat16),
        grid=(B_HEADS // n_h, bsz),
        in_specs=[
            pl.BlockSpec((width, seq), lambda hg, b: (QB_T // width + hg, b)),
            pl.BlockSpec((1, seq, width), lambda hg, b: (b, 0, KB // width + hg)),
            pl.BlockSpec((1, seq, LANES), lambda hg, b: (b, 0, 0)),
            pl.BlockSpec((width, seq), lambda hg, b: (VB_T // width + hg, b)),
        ],
        out_specs=pl.BlockSpec((1, seq, width), lambda hg, b: (b, 0, hg)),
        scratch_shapes=[pltpu.VMEM((n_h, 1, tile), jnp.float32),
                        pltpu.VMEM((n_h, ACC_ROWS, tile), jnp.float32),
                        pltpu.VMEM((2, n_h, tile, tile), jnp.float32),
                        pltpu.VMEM((2, n_h, 1, tile), jnp.float32)],
        compiler_params=pltpu.CompilerParams(
            dimension_semantics=("parallel", "parallel"),
            vmem_limit_bytes=VMEM_LIMIT),
        name="attn_b",
    )(proj_fm, proj_tm3, ckp, proj_fm)


def _attn_c_kernel(q_ref, k_ref, v_ref, bias_ref, o_ref, qd_ref, kd_ref, vtd_ref, onat_ref,
                   lnat_ref, *, seq):
    g = pl.program_id(2)
    n_chunks = seq // C_TILE
    lane = lax.broadcasted_iota(jnp.int32, (1, LANES), 1)
    first = lane < HEAD_DIM

    def run_group(group, dil):
        n_tiles = seq // dil // C_TILE

        def src_rows(i):
            r, t = i // n_tiles, i % n_tiles
            return pl.ds(r + t * (C_TILE * dil), C_TILE, stride=dil), t

        kd_ref[0:C_TILE, :] = jnp.zeros((C_TILE, LANES), jnp.bfloat16)
        vtd_ref[:, 0:C_TILE] = jnp.zeros((LANES, C_TILE), jnp.bfloat16)

        def gather(j, carry):
            ids = [j * C_UNROLL + u for u in range(C_UNROLL)]
            vts = [jnp.transpose(v_ref[src_rows(i)[0], :]) for i in ids]
            for i, vt in zip(ids, vts):
                rows, _ = src_rows(i)
                dst = pl.multiple_of(i * C_TILE, C_TILE)
                qd_ref[pl.ds(dst, C_TILE), :] = q_ref[rows, :].astype(jnp.bfloat16)
                kd_ref[pl.ds(dst + C_TILE, C_TILE), :] = k_ref[rows, :].astype(jnp.bfloat16)
                vtd_ref[:, pl.ds(dst + C_TILE, C_TILE)] = vt.astype(jnp.bfloat16)
            return carry

        lax.fori_loop(0, n_chunks // C_UNROLL, gather, 0)

        def tiles(j, carry):
            ids = [j * C_UNROLL + u for u in range(C_UNROLL)]
            row0s = [pl.multiple_of(i * C_TILE, C_TILE) for i in ids]

            def scores(u):
                q = qd_ref[pl.ds(row0s[u], C_TILE), :]
                q2 = jnp.concatenate([jnp.where(first, q, jnp.zeros_like(q)),
                                      jnp.where(first, jnp.zeros_like(q), q)], axis=0)
                return _dot_nt(kd_ref[pl.ds(row0s[u], 2 * C_TILE), :], q2)

            def softmax_pv(u, st):
                ver = jnp.minimum(ids[u] % n_tiles, 1)
                res = []
                for h in range(2):
                    x = st[:, h * C_TILE:(h + 1) * C_TILE] + bias_ref[h, ver]
                    m = _colmax(x)
                    p = jnp.exp2(x - m).astype(jnp.bfloat16)
                    v_aug = jnp.concatenate(
                        [vtd_ref[h * HEAD_DIM:(h + 1) * HEAD_DIM, pl.ds(row0s[u], 2 * C_TILE)],
                         jnp.ones((ONES_ROWS, 2 * C_TILE), jnp.bfloat16)], axis=0)
                    res.append((m, jnp.dot(v_aug, p, preferred_element_type=jnp.float32)))
                return res

            def write_back(u, res):
                outs, lses = [], []
                for m, pv in res:
                    l = pv[HEAD_DIM:HEAD_DIM + 1]
                    outs.append(pv[0:HEAD_DIM] / l)
                    lses.append(jnp.broadcast_to(m + jnp.log2(l), (HEAD_DIM, C_TILE)))
                rows, _ = src_rows(ids[u])
                onat_ref[group, rows, :] = jnp.transpose(jnp.concatenate(outs, axis=0))
                lnat_ref[group, rows, :] = jnp.transpose(jnp.concatenate(lses, axis=0))

            sts = {u: scores(u) for u in range(C_AHEAD)}
            pending = None
            for u in range(C_UNROLL):
                if u + C_AHEAD < C_UNROLL:
                    sts[u + C_AHEAD] = scores(u + C_AHEAD)
                res = softmax_pv(u, sts.pop(u))
                if pending is not None:
                    write_back(*pending)
                pending = (u, res)
            write_back(*pending)
            return carry

        lax.fori_loop(0, n_chunks // C_UNROLL, tiles, 0)

    for group, (_, dil) in enumerate(C_PAIRS):
        pl.when(g == group)(functools.partial(run_group, group, dil))

    @pl.when(g == C_GROUPS - 1)
    def _():
        rows_per = 512

        def combine(c, carry):
            rows = pl.ds(pl.multiple_of(c * rows_per, rows_per), rows_per)
            l0, l1, l2 = lnat_ref[0, rows, :], lnat_ref[1, rows, :], lnat_ref[2, rows, :]
            m = jnp.maximum(jnp.maximum(l0, l1), l2)
            e0, e1, e2 = jnp.exp2(l0 - m), jnp.exp2(l1 - m), jnp.exp2(l2 - m)
            num = e0 * onat_ref[0, rows, :] + e1 * onat_ref[1, rows, :] + e2 * onat_ref[2, rows, :]
            o_ref[0, rows, :] = (num / (e0 + e1 + e2)).astype(o_ref.dtype)
            return carry

        lax.fori_loop(0, seq // rows_per, combine, 0)


def _attn_c(proj_c, bias_c, bsz, seq):
    slab = lambda role: pl.BlockSpec(
        (None, seq, LANES), lambda b, hp, g: (role * 2 * C_GROUPS + g * 2 + hp, b, 0))
    return pl.pallas_call(
        functools.partial(_attn_c_kernel, seq=seq),
        out_shape=jax.ShapeDtypeStruct((bsz, seq, C_W), jnp.bfloat16),
        grid=(bsz, C_HEADS // 2, C_GROUPS),
        in_specs=[slab(0), slab(1), slab(2),
                  pl.BlockSpec((2, 2, 2 * C_TILE, C_TILE), lambda b, hp, g: (g * 2 + hp, 0, 0, 0))],
        out_specs=pl.BlockSpec((1, seq, LANES), lambda b, hp, g: (b, 0, hp)),
        scratch_shapes=[pltpu.VMEM((seq, LANES), jnp.bfloat16),
                        pltpu.VMEM((seq + C_TILE, LANES), jnp.bfloat16),
                        pltpu.VMEM((LANES, seq + C_TILE), jnp.bfloat16),
                        pltpu.VMEM((C_GROUPS, seq, LANES), jnp.float32),
                        pltpu.VMEM((C_GROUPS, seq, LANES), jnp.float32)],
        compiler_params=pltpu.CompilerParams(
            dimension_semantics=("parallel", "parallel", "arbitrary"),
            vmem_limit_bytes=VMEM_LIMIT),
        name="attn_c",
    )(proj_c, proj_c, proj_c, bias_c)


def _mix_mlp_kernel(x_ref, oa_ref, ob_ref, oc_ref, wo_ref, g_ref, w1_ref, w2_ref, fg_ref, out_ref,
                    acc_ref, h_ref, *, final):
    f = pl.program_id(1)

    @pl.when(f == 0)
    def _():
        wo = wo_ref.at[0]
        y = jnp.dot(oa_ref[...], wo[0:A_W, :], preferred_element_type=jnp.float32)
        y += jnp.dot(ob_ref[...], wo[A_W:A_W + B_W, :], preferred_element_type=jnp.float32)
        y += jnp.dot(oc_ref[...], wo[A_W + B_W:, :], preferred_element_type=jnp.float32)
        x1 = x_ref[...] + y
        acc_ref[...] = x1
        h_ref[...] = _rms(x1, g_ref[0]).astype(jnp.bfloat16)

    u = jnp.dot(h_ref[...], w1_ref[0], preferred_element_type=jnp.float32)
    u = jnp.square(jnp.maximum(u, 0.0)).astype(jnp.bfloat16)
    acc_ref[...] += jnp.dot(u, w2_ref[0], preferred_element_type=jnp.float32)

    @pl.when(f == pl.num_programs(1) - 1)
    def _():
        out_ref[...] = _rms(acc_ref[...], fg_ref[...]) if final else acc_ref[...]


def _mix_mlp(x2, oa, ob, oc, w_o, norm_g, w_1, w_2, final_g, layer, final, *, tm=1024, tf=1024):
    n_tok = x2.shape[0]
    row = lambda w: pl.BlockSpec((tm, w), lambda i, f: (i, 0))
    return pl.pallas_call(
        functools.partial(_mix_mlp_kernel, final=final),
        out_shape=jax.ShapeDtypeStruct((n_tok, D_MODEL), jnp.float32),
        grid=(n_tok // tm, D_FF // tf),
        in_specs=[row(D_MODEL), row(A_W), row(B_W), row(C_W),
                  pl.BlockSpec((1, D_MODEL, D_MODEL), lambda i, f: (layer, 0, 0)),
                  pl.BlockSpec((1, 1, D_MODEL), lambda i, f: (layer, 0, 0)),
                  pl.BlockSpec((1, D_MODEL, tf), lambda i, f: (layer, 0, f)),
                  pl.BlockSpec((1, tf, D_MODEL), lambda i, f: (layer, f, 0)),
                  pl.BlockSpec((1, D_MODEL), lambda i, f: (0, 0))],
        out_specs=row(D_MODEL),
        scratch_shapes=[pltpu.VMEM((tm, D_MODEL), jnp.float32),
                        pltpu.VMEM((tm, D_MODEL), jnp.bfloat16)],
        compiler_params=pltpu.CompilerParams(dimension_semantics=("parallel", "arbitrary"),
                                             vmem_limit_bytes=VMEM_LIMIT),
        name="mix_mlp",
    )(x2, oa, ob, oc, w_o, norm_g, w_1, w_2, final_g)


def _prep_w_in(w_in):
    a0, b0, f0, c0 = 0, 3 * A_W, 3 * A_W + 3 * B_W, 3 * A_W + 3 * B_W + B_HEADS
    sa, sb = A_QK_DIM ** -0.5 * LOG2E, HEAD_DIM ** -0.5 * LOG2E
    cw = C_GROUPS * C_W
    w_tm = jnp.concatenate([w_in[:, :, b0 + B_W:b0 + 2 * B_W],
                            w_in[:, :, a0 + A_W:a0 + 2 * A_W]], axis=-1).astype(jnp.bfloat16)
    w_fm = jnp.concatenate([
        w_in[:, :, b0:b0 + B_W] * sb, w_in[:, :, b0 + 2 * B_W:f0],
        w_in[:, :, a0:a0 + A_W] * sa, w_in[:, :, a0 + 2 * A_W:b0],
    ], axis=-1)
    w_fm = jnp.transpose(w_fm, (0, 2, 1)).astype(jnp.bfloat16)
    w_f = jnp.transpose(w_in[:, :, f0:c0], (0, 2, 1))
    w_f = jnp.pad(w_f, ((0, 0), (0, F_ROWS - B_HEADS), (0, 0))).astype(jnp.bfloat16)
    w_c = jnp.concatenate([w_in[:, :, c0:c0 + cw] * sb, w_in[:, :, c0 + cw:]],
                          axis=-1).astype(jnp.bfloat16)
    return w_tm, w_fm, w_f, w_c


def kernel(x, norm1_g, w_in, b_f, lam_q1, lam_k1, lam_q2, lam_k2, diff_norm_g, w_o, norm2_g,
           w_1, w_2, rel_bias, final_g):
    bsz, seq, _ = x.shape
    depth = w_in.shape[0]
    w_tm, w_fm, w_f, w_c = _prep_w_in(w_in)
    w_o16, w_116, w_216 = (w.astype(jnp.bfloat16) for w in (w_o, w_1, w_2))
    lam_vecs = jnp.stack([lam_q1, lam_k1, lam_q2, lam_k2], axis=1)
    bias_a, bias_c = _make_bias_tiles(rel_bias)

    x2 = x.reshape(bsz * seq, D_MODEL)
    for l in range(depth):
        proj_tm, proj_fm, ft, proj_c = _in_proj(x2, norm1_g[:, None, :], w_tm, w_fm, w_f, w_c, l)
        proj3 = proj_tm.reshape(bsz, seq, N_TM)
        ckp = _decay(ft, b_f[:, :, None], l, bsz, seq)
        lam_init = jnp.full((1,), 0.8 - 0.6 * math.exp(-0.3 * l), jnp.float32)
        oa = _attn_a(proj3, proj_fm, bias_a, lam_vecs, lam_init, diff_norm_g[:, :, None], l)
        ob = _attn_b(proj3, proj_fm, ckp)
        oc = _attn_c(proj_c, bias_c, bsz, seq)
        x2 = _mix_mlp(x2, oa.reshape(bsz * seq, A_W), ob.reshape(bsz * seq, B_W),
                      oc.reshape(bsz * seq, C_W), w_o16, norm2_g[:, None, :], w_116, w_216,
                      final_g[None, :], l, l == depth - 1)
    return x2.reshape(bsz, seq, D_MODEL)
```

```python
import functools
import math

import jax
import jax.numpy as jnp
from jax import lax
from jax.experimental import pallas as pl
from jax.experimental.pallas import tpu as pltpu

D_MODEL = 1024
HEAD_DIM = 64
A_HEADS = 4
A_QK_DIM = 32
B_HEADS = 8
C_HEADS = 4
C_PAIRS = ((128, 1), (512, 4), (2048, 16))
C_GROUPS = 3
A_W = 256
B_W = 512
C_W = 256
D_FF = 4096
N_BUCKETS = 32
MAX_DISTANCE = 2048
NORM_EPS = 1e-6
NEG_INF = -1e30
LOG2E = 1.4426950408889634

LANES = 128
VMEM_LIMIT = 56 * 1024 * 1024

KB, KA = 0, 512
N_TM = 768
QB_T, VB_T, QA_T, VA_T = 0, 512, 1024, 1280
N_FM = 1536
N_C = 2304
N_C_SLABS = N_C // 128
F_ROWS = 16
ONES_ROWS = 16
ACC_ROWS = HEAD_DIM + ONES_ROWS

A_STEP_HEADS = 4
B_STEP_HEADS = 8
C_TILE = 128
C_UNROLL = 16
C_AHEAD = 4
C_SPLIT = 4
FLASH_TILE = 256
A_GROUP = 2
B_GROUP = 1


def _bucket_thresholds():
    max_exact = N_BUCKETS // 2
    out = []
    for k in range(1, N_BUCKETS - max_exact):
        t = max_exact * (MAX_DISTANCE / max_exact) ** (k / (N_BUCKETS - max_exact))
        out.append(int(math.ceil(t)))
    return tuple(out)


_THRESHOLDS = _bucket_thresholds()


def _n_near_tiles(tile):
    return -(-(_THRESHOLDS[-1] + tile - 1) // tile)


def _t5_bucket(d):
    big = jnp.full(d.shape, N_BUCKETS // 2, jnp.int32)
    for t in _THRESHOLDS:
        big = big + (d >= t).astype(jnp.int32)
    return jnp.where(d < N_BUCKETS // 2, d, big)


def _bias_lookup(rb_ref, bucket, col):
    val = jnp.zeros(bucket.shape, jnp.float32)
    for b in range(N_BUCKETS):
        val = jnp.where(bucket == b, rb_ref[b, col], val)
    return val


def _bias_a_kernel(rb_ref, out_ref, *, tile):
    delta = pl.program_id(0)
    j = lax.broadcasted_iota(jnp.int32, (tile, tile), 0)
    i = lax.broadcasted_iota(jnp.int32, (tile, tile), 1)
    d = delta * tile + i - j
    bucket = _t5_bucket(d)
    for h in range(A_HEADS):
        val = _bias_lookup(rb_ref, bucket, h) * LOG2E
        out_ref[h, 0] = jnp.where(d >= 0, val, NEG_INF)


def _bias_c_kernel(rb_ref, out_ref):
    g = pl.program_id(0)
    dil = jnp.where(g == 0, C_PAIRS[0][1], jnp.where(g == 1, C_PAIRS[1][1], C_PAIRS[2][1]))
    c = lax.broadcasted_iota(jnp.int32, (2 * C_TILE, C_TILE), 0)
    i = lax.broadcasted_iota(jnp.int32, (2 * C_TILE, C_TILE), 1)
    steps = i + C_TILE - c
    valid = (steps >= 0) & (steps <= C_TILE)
    bucket = _t5_bucket(steps * dil)
    for h in range(C_HEADS):
        val = _bias_lookup(rb_ref, bucket, A_HEADS + g * C_HEADS + h) * LOG2E
        out_ref[h, 1] = jnp.where(valid, val, NEG_INF)
        out_ref[h, 0] = jnp.where(valid & (c >= C_TILE), val, NEG_INF)


def _make_bias_tiles(rel_bias):
    tile = FLASH_TILE
    n_bias = _n_near_tiles(tile) + 1
    smem = pl.BlockSpec(memory_space=pltpu.SMEM)
    bias_a = pl.pallas_call(
        functools.partial(_bias_a_kernel, tile=tile),
        out_shape=jax.ShapeDtypeStruct((A_HEADS, n_bias, tile, tile), jnp.float32),
        grid=(n_bias,),
        in_specs=[smem],
        out_specs=pl.BlockSpec((A_HEADS, 1, tile, tile), lambda d: (0, d, 0, 0)),
        name="bias_a_tiles",
    )(rel_bias)
    bias_c = pl.pallas_call(
        _bias_c_kernel,
        out_shape=jax.ShapeDtypeStruct((C_GROUPS * C_HEADS, 2, 2 * C_TILE, C_TILE), jnp.float32),
        grid=(C_GROUPS,),
        in_specs=[smem],
        out_specs=pl.BlockSpec((C_HEADS, 2, 2 * C_TILE, C_TILE), lambda g: (g, 0, 0, 0)),
        name="bias_c_tiles",
    )(rel_bias)
    return bias_a, bias_c


def _rms(x, g):
    return x * lax.rsqrt(jnp.mean(x * x, axis=-1, keepdims=True) + NORM_EPS) * g


def _dot_nt(a, b):
    return lax.dot_general(a, b, (((1,), (1,)), ((), ())), preferred_element_type=jnp.float32)


def _in_proj_kernel(x_ref, g_ref, wtm_ref, wfm_ref, wf_ref, wc_ref, tm_ref, fm_ref, ft_ref, c_ref):
    h = _rms(x_ref[...], g_ref[0]).astype(jnp.bfloat16)
    tm_ref[...] = jnp.dot(h, wtm_ref[0], preferred_element_type=jnp.float32).astype(jnp.bfloat16)
    for c in range(N_FM // 512):
        rows = slice(c * 512, (c + 1) * 512)
        fm_ref[rows, :] = _dot_nt(wfm_ref[0, rows, :], h).astype(jnp.bfloat16)
    ft_ref[...] = _dot_nt(wf_ref[0], h)
    for c in range(N_C // 256):
        pc = jnp.dot(h, wc_ref[0, :, c * 256:(c + 1) * 256], preferred_element_type=jnp.float32)
        c_ref[2 * c] = pc[:, :LANES]
        c_ref[2 * c + 1] = pc[:, LANES:]


def _in_proj(x2, norm_g, w_tm, w_fm, w_f, w_c, layer, *, tm=512):
    n_tok = x2.shape[0]
    return pl.pallas_call(
        _in_proj_kernel,
        out_shape=(jax.ShapeDtypeStruct((n_tok, N_TM), jnp.bfloat16),
                   jax.ShapeDtypeStruct((N_FM, n_tok), jnp.bfloat16),
                   jax.ShapeDtypeStruct((F_ROWS, n_tok), jnp.float32),
                   jax.ShapeDtypeStruct((N_C_SLABS, n_tok, LANES), jnp.float32)),
        grid=(n_tok // tm,),
        in_specs=[
            pl.BlockSpec((tm, D_MODEL), lambda i: (i, 0)),
            pl.BlockSpec((1, 1, D_MODEL), lambda i: (layer, 0, 0)),
            pl.BlockSpec((1, D_MODEL, N_TM), lambda i: (layer, 0, 0)),
            pl.BlockSpec((1, N_FM, D_MODEL), lambda i: (layer, 0, 0)),
            pl.BlockSpec((1, F_ROWS, D_MODEL), lambda i: (layer, 0, 0)),
            pl.BlockSpec((1, D_MODEL, N_C), lambda i: (layer, 0, 0)),
        ],
        out_specs=(pl.BlockSpec((tm, N_TM), lambda i: (i, 0)),
                   pl.BlockSpec((N_FM, tm), lambda i: (0, i)),
                   pl.BlockSpec((F_ROWS, tm), lambda i: (0, i)),
                   pl.BlockSpec((N_C_SLABS, tm, LANES), lambda i: (0, i, 0))),
        compiler_params=pltpu.CompilerParams(dimension_semantics=("parallel",),
                                             vmem_limit_bytes=VMEM_LIMIT),
        name="in_proj",
    )(x2, norm_g, w_tm, w_fm, w_f, w_c)


def _decay_kernel(ft_ref, bf_ref, ckp_ref):
    z = ft_ref[0:B_HEADS, :] + bf_ref[0]
    x = jnp.minimum(z, 0.0) - jnp.log1p(jnp.exp(-jnp.abs(z)))
    seq = x.shape[1]
    pos = lax.broadcasted_iota(jnp.int32, x.shape, 1)
    shift = 1
    while shift < seq:
        x = x + jnp.where(pos >= shift, pltpu.roll(x, shift, 1), 0.0)
        shift *= 2
    c = x * (-LOG2E)
    hi = c.astype(jnp.bfloat16).astype(jnp.float32)
    mid = (c - hi).astype(jnp.bfloat16).astype(jnp.float32)
    lo = c - hi - mid
    stacked = jnp.concatenate(
        [hi, mid, lo, jnp.zeros((LANES - 3 * B_HEADS, seq), jnp.float32)], axis=0)
    ckp_ref[0] = jnp.transpose(stacked).astype(jnp.bfloat16)


def _decay(ft, b_f, layer, bsz, seq):
    return pl.pallas_call(
        _decay_kernel,
        out_shape=jax.ShapeDtypeStruct((bsz, seq, LANES), jnp.bfloat16),
        grid=(bsz,),
        in_specs=[pl.BlockSpec((F_ROWS, seq), lambda b: (0, b)),
                  pl.BlockSpec((1, B_HEADS, 1), lambda b: (layer, 0, 0))],
        out_specs=pl.BlockSpec((1, seq, LANES), lambda b: (b, 0, 0)),
        compiler_params=pltpu.CompilerParams(dimension_semantics=("parallel",)),
        name="decay_scan",
    )(ft, b_f)


def _colmax(s):
    rows = s.shape[0]
    while rows > 8:
        rows //= 2
        s = jnp.maximum(s[:rows], s[rows:2 * rows])
    return jnp.max(s, axis=0, keepdims=True)


def _online_step(s, smax, v_aug, m_ref, acc_ref, idx):
    m_old = m_ref[idx]
    m_new = jnp.maximum(m_old, smax)
    alpha = jnp.exp2(m_old - m_new)
    p = jnp.exp2(s - m_new).astype(jnp.bfloat16)
    acc_ref[idx] = alpha * acc_ref[idx] + jnp.dot(v_aug, p, preferred_element_type=jnp.float32)
    m_ref[idx] = m_new


def _v_aug(vt_ref, head, k0, tile):
    v = vt_ref[head * HEAD_DIM:(head + 1) * HEAD_DIM, pl.ds(k0, tile)]
    return jnp.concatenate([v, jnp.ones((ONES_ROWS, tile), jnp.bfloat16)], axis=0)


def _flash_pipeline(qi, n_sub, group, produce, consume):
    every = range(n_sub)

    def overlapped(pk, pslot, ck, cslot):
        for g0 in range(0, n_sub, group):
            subs = range(g0, g0 + group)
            produce(pk, pslot, subs)
            consume(ck, cslot, False, subs)

    produce(0, 0, every)

    def pair(j, carry):
        k = 2 * j
        overlapped(k + 1, 1, k, 0)
        overlapped(k + 2, 0, k + 1, 1)
        return carry

    lax.fori_loop(0, qi // 2, pair, 0)

    @pl.when(qi % 2 == 0)
    def _():
        consume(qi, 0, True, every)

    @pl.when(qi % 2 == 1)
    def _():
        overlapped(qi, 1, qi - 1, 0)
        consume(qi, 1, True, every)


def _attn_a_kernel(qt_ref, k_ref, vt_ref, bias_ref, lamv_ref, lami_ref, subg_ref, o_ref,
                   m_ref, acc_ref, s_ref, smax_ref, *, tile, n_near):
    qi = pl.program_id(2)
    n_sub = 2 * A_STEP_HEADS
    qt = qt_ref[...].astype(jnp.float32)
    row = lax.broadcasted_iota(jnp.int32, (LANES, tile), 0)
    qs = []
    for s in range(n_sub):
        blk = qt[(s // 4) * LANES:(s // 4 + 1) * LANES]
        lo = (s % 4) * A_QK_DIM
        qs.append(jnp.where((row >= lo) & (row < lo + A_QK_DIM), blk, 0.0).astype(jnp.bfloat16))
    m_ref[...] = jnp.full(m_ref.shape, NEG_INF, jnp.float32)
    acc_ref[...] = jnp.zeros(acc_ref.shape, jnp.float32)

    def produce(ki, slot, subs):
        k0 = pl.multiple_of(ki * tile, tile)
        delta = jnp.minimum(qi - ki, n_near)
        for s in subs:
            k = k_ref[0, pl.ds(k0, tile), (s // 4) * LANES:(s // 4 + 1) * LANES]
            x = jnp.dot(k, qs[s], preferred_element_type=jnp.float32) + bias_ref[s // 2, delta]
            s_ref[slot, s] = x
            smax_ref[slot, s] = _colmax(x)

    def consume(ki, slot, last, subs):
        k0 = pl.multiple_of(ki * tile, tile)
        for s in subs:
            _online_step(s_ref[slot, s], smax_ref[slot, s], _v_aug(vt_ref, s // 2, k0, tile),
                         m_ref, acc_ref, s)

    _flash_pipeline(qi, n_sub, A_GROUP, produce, consume)

    lam_init = lami_ref[0]
    lv = lamv_ref[0]
    e1 = jnp.exp(jnp.sum(lv[0:1] * lv[1:2], axis=1, keepdims=True))
    e2 = jnp.exp(jnp.sum(lv[2:3] * lv[3:4], axis=1, keepdims=True))
    lam = e1 - e2 + lam_init
    heads = []
    for h in range(A_STEP_HEADS):
        a0, a1 = acc_ref[2 * h], acc_ref[2 * h + 1]
        o = (a0[0:HEAD_DIM] / a0[HEAD_DIM:HEAD_DIM + 1]
             - lam * (a1[0:HEAD_DIM] / a1[HEAD_DIM:HEAD_DIM + 1]))
        ms = jnp.mean(o * o, axis=0, keepdims=True)
        heads.append(o * lax.rsqrt(ms + NORM_EPS) * subg_ref[0] * (1.0 - lam_init))
    o_ref[0] = jnp.transpose(jnp.concatenate(heads, axis=0)).astype(o_ref.dtype)


def _attn_a(proj_tm3, proj_fm, bias_a, lam_vecs, lam_init, sub_g, layer):
    bsz, seq, _ = proj_tm3.shape
    tile = FLASH_TILE
    n_near = _n_near_tiles(tile)
    n_q = seq // tile
    n_h = A_STEP_HEADS
    width = n_h * HEAD_DIM
    return pl.pallas_call(
        functools.partial(_attn_a_kernel, tile=tile, n_near=n_near),
        out_shape=jax.ShapeDtypeStruct((bsz, seq, A_W), jnp.bfloat16),
        grid=(A_HEADS // n_h, bsz, n_q),
        in_specs=[
            pl.BlockSpec((width, tile), lambda hg, b, qi: (QA_T // width + hg, b * n_q + qi)),
            pl.BlockSpec((1, seq, width), lambda hg, b, qi: (b, 0, KA // width + hg)),
            pl.BlockSpec((width, seq), lambda hg, b, qi: (VA_T // width + hg, b)),
            pl.BlockSpec((n_h, n_near + 1, tile, tile), lambda hg, b, qi: (hg, 0, 0, 0)),
            pl.BlockSpec((1, 4, A_QK_DIM), lambda hg, b, qi: (layer, 0, 0)),
            pl.BlockSpec(memory_space=pltpu.SMEM),
            pl.BlockSpec((1, HEAD_DIM, 1), lambda hg, b, qi: (layer, 0, 0)),
        ],
        out_specs=pl.BlockSpec((1, tile, width), lambda hg, b, qi: (b, qi, hg)),
        scratch_shapes=[pltpu.VMEM((2 * n_h, 1, tile), jnp.float32),
                        pltpu.VMEM((2 * n_h, ACC_ROWS, tile), jnp.float32),
                        pltpu.VMEM((2, 2 * n_h, tile, tile), jnp.float32),
                        pltpu.VMEM((2, 2 * n_h, 1, tile), jnp.float32)],
        compiler_params=pltpu.CompilerParams(
            dimension_semantics=("parallel", "parallel", "parallel"),
            vmem_limit_bytes=VMEM_LIMIT),
        name="attn_a",
    )(proj_fm, proj_tm3, proj_fm, bias_a, lam_vecs, lam_init, sub_g)


def _attn_b_kernel(qt_ref, k_ref, ckp_ref, vt_ref, o_ref, m_ref, acc_ref, s_ref, smax_ref, *,
                   tile):
    hg = pl.program_id(0)
    qi = pl.program_id(2)
    n_h = B_STEP_HEADS
    qt = qt_ref[...].astype(jnp.float32)
    row = lax.broadcasted_iota(jnp.int32, (LANES, tile), 0)
    q_aug = []
    for s in range(n_h):
        blk = qt[(s // 2) * LANES:(s // 2 + 1) * LANES]
        lo = (s % 2) * HEAD_DIM
        q_s = jnp.where((row >= lo) & (row < lo + HEAD_DIM), blk, 0.0)
        pick = (row < 3 * B_HEADS) & ((row % B_HEADS) == n_h * hg + s)
        ones = jnp.where(pick, 1.0, 0.0)
        q_aug.append(jnp.concatenate([q_s, ones], axis=0).astype(jnp.bfloat16))
    m_ref[...] = jnp.full(m_ref.shape, NEG_INF, jnp.float32)
    acc_ref[...] = jnp.zeros(acc_ref.shape, jnp.float32)

    def produce(ki, slot, subs):
        k0 = pl.multiple_of(ki * tile, tile)
        ck = ckp_ref[0, pl.ds(k0, tile), :]
        for s in subs:
            k = k_ref[0, pl.ds(k0, tile), (s // 2) * LANES:(s // 2 + 1) * LANES]
            x = jnp.dot(jnp.concatenate([k, ck], axis=1), q_aug[s],
                        preferred_element_type=jnp.float32)
            s_ref[slot, s] = x
            smax_ref[slot, s] = _colmax(x)

    def consume(ki, slot, last, subs):
        k0 = pl.multiple_of(ki * tile, tile)
        if last:
            kk = lax.broadcasted_iota(jnp.int32, (tile, tile), 0)
            qq = lax.broadcasted_iota(jnp.int32, (tile, tile), 1)
            keep = kk <= qq
        for s in subs:
            x, xmax = s_ref[slot, s], smax_ref[slot, s]
            if last:
                x = jnp.where(keep, x, NEG_INF)
                xmax = _colmax(x)
            _online_step(x, xmax, _v_aug(vt_ref, s, k0, tile), m_ref, acc_ref, s)

    _flash_pipeline(qi, n_h, B_GROUP, produce, consume)
    heads = [acc_ref[s][0:HEAD_DIM] / acc_ref[s][HEAD_DIM:HEAD_DIM + 1] for s in range(n_h)]
    o_ref[0] = jnp.transpose(jnp.concatenate(heads, axis=0)).astype(o_ref.dtype)


def _attn_b(proj_tm3, proj_fm, ckp):
    bsz, seq, _ = proj_tm3.shape
    tile = FLASH_TILE
    n_q = seq // tile
    n_h = B_STEP_HEADS
    width = n_h * HEAD_DIM
    return pl.pallas_call(
        functools.partial(_attn_b_kernel, tile=tile),
        out_shape=jax.ShapeDtypeStruct((bsz, seq, B_W), jnp.bfloat16),
        grid=(B_HEADS // n_h, bsz, n_q),
        in_specs=[
            pl.BlockSpec((width, tile), lambda hg, b, qi: (QB_T // width + hg, b * n_q + qi)),
            pl.BlockSpec((1, seq, width), lambda hg, b, qi: (b, 0, KB // width + hg)),
            pl.BlockSpec((1, seq, LANES), lambda hg, b, qi: (b, 0, 0)),
            pl.BlockSpec((width, seq), lambda hg, b, qi: (VB_T // width + hg, b)),
        ],
        out_specs=pl.BlockSpec((1, tile, width), lambda hg, b, qi: (b, qi, hg)),
        scratch_shapes=[pltpu.VMEM((n_h, 1, tile), jnp.float32),
                        pltpu.VMEM((n_h, ACC_ROWS, tile), jnp.float32),
                        pltpu.VMEM((2, n_h, tile, tile), jnp.float32),
                        pltpu.VMEM((2, n_h, 1, tile), jnp.float32)],
        compiler_params=pltpu.CompilerParams(
            dimension_semantics=("parallel", "parallel", "parallel"),
            vmem_limit_bytes=VMEM_LIMIT),
        name="attn_b",
    )(proj_fm, proj_tm3, ckp, proj_fm)


def _attn_c_kernel(q_ref, k_ref, v_ref, bias_ref, o_ref, qd_ref, kd_ref, vtd_ref, onat_ref,
                   lnat_ref, tq_ref, tk_ref, tv_ref, *, seq):
    g = pl.program_id(2)
    n_chunks = seq // C_TILE
    lane = lax.broadcasted_iota(jnp.int32, (1, LANES), 1)
    first = lane < HEAD_DIM

    def run_group(group, dil):
        n_tiles = seq // dil // C_TILE

        def out_rows(i):
            r, t = i // n_tiles, i % n_tiles
            return pl.ds(r + t * (C_TILE * dil), C_TILE, stride=dil)

        if dil > C_SPLIT:
            per = seq // C_SPLIT // C_TILE

            def presplit(j, carry):
                for u in range(C_UNROLL):
                    i = j * C_UNROLL + u
                    rows = pl.ds(i // per + (i % per) * (C_TILE * C_SPLIT), C_TILE, stride=C_SPLIT)
                    dst = pl.ds(pl.multiple_of(i * C_TILE, C_TILE), C_TILE)
                    for src, tmp in ((q_ref, tq_ref), (k_ref, tk_ref), (v_ref, tv_ref)):
                        tmp[dst, :] = src[rows, :]
                return carry

            lax.fori_loop(0, n_chunks // C_UNROLL, presplit, 0)
            srcs, step = (tq_ref, tk_ref, tv_ref), dil // C_SPLIT

            def src_rows(i):
                r, t = i // n_tiles, i % n_tiles
                base = (r % C_SPLIT) * (seq // C_SPLIT) + r // C_SPLIT + t * (C_TILE * step)
                return pl.ds(base, C_TILE, stride=step)
        else:
            srcs, src_rows = (q_ref, k_ref, v_ref), out_rows

        kd_ref[0:C_TILE, :] = jnp.zeros((C_TILE, LANES), jnp.bfloat16)
        vtd_ref[:, 0:C_TILE] = jnp.zeros((LANES, C_TILE), jnp.bfloat16)

        def gather(j, carry):
            ids = [j * C_UNROLL + u for u in range(C_UNROLL)]
            vts = [jnp.transpose(srcs[2][src_rows(i), :]) for i in ids]
            for i, vt in zip(ids, vts):
                rows = src_rows(i)
                dst = pl.multiple_of(i * C_TILE, C_TILE)
                qd_ref[pl.ds(dst, C_TILE), :] = srcs[0][rows, :].astype(jnp.bfloat16)
                kd_ref[pl.ds(dst + C_TILE, C_TILE), :] = srcs[1][rows, :].astype(jnp.bfloat16)
                vtd_ref[:, pl.ds(dst + C_TILE, C_TILE)] = vt.astype(jnp.bfloat16)
            return carry

        lax.fori_loop(0, n_chunks // C_UNROLL, gather, 0)

        def tiles(j, carry):
            ids = [j * C_UNROLL + u for u in range(C_UNROLL)]
            row0s = [pl.multiple_of(i * C_TILE, C_TILE) for i in ids]

            def scores(u):
                q = qd_ref[pl.ds(row0s[u], C_TILE), :]
                q2 = jnp.concatenate([jnp.where(first, q, jnp.zeros_like(q)),
                                      jnp.where(first, jnp.zeros_like(q), q)], axis=0)
                return _dot_nt(kd_ref[pl.ds(row0s[u], 2 * C_TILE), :], q2)

            def softmax_pv(u, st):
                ver = jnp.minimum(ids[u] % n_tiles, 1)
                res = []
                for h in range(2):
                    x = st[:, h * C_TILE:(h + 1) * C_TILE] + bias_ref[h, ver]
                    m = _colmax(x)
                    p = jnp.exp2(x - m).astype(jnp.bfloat16)
                    v_aug = jnp.concatenate(
                        [vtd_ref[h * HEAD_DIM:(h + 1) * HEAD_DIM, pl.ds(row0s[u], 2 * C_TILE)],
                         jnp.ones((ONES_ROWS, 2 * C_TILE), jnp.bfloat16)], axis=0)
                    res.append((m, jnp.dot(v_aug, p, preferred_element_type=jnp.float32)))
                return res

            def write_back(u, res):
                outs, lses = [], []
                for m, pv in res:
                    l = pv[HEAD_DIM:HEAD_DIM + 1]
                    outs.append(pv[0:HEAD_DIM] / l)
                    lses.append(jnp.broadcast_to(m + jnp.log2(l), (HEAD_DIM, C_TILE)))
                rows = out_rows(ids[u])
                onat_ref[group, rows, :] = jnp.transpose(jnp.concatenate(outs, axis=0))
                lnat_ref[group, rows, :] = jnp.transpose(jnp.concatenate(lses, axis=0))

            sts = {u: scores(u) for u in range(C_AHEAD)}
            pending = None
            for u in range(C_UNROLL):
                if u + C_AHEAD < C_UNROLL:
                    sts[u + C_AHEAD] = scores(u + C_AHEAD)
                res = softmax_pv(u, sts.pop(u))
                if pending is not None:
                    write_back(*pending)
                pending = (u, res)
            write_back(*pending)
            return carry

        lax.fori_loop(0, n_chunks // C_UNROLL, tiles, 0)

    for group, (_, dil) in enumerate(C_PAIRS):
        pl.when(g == group)(functools.partial(run_group, group, dil))

    @pl.when(g == C_GROUPS - 1)
    def _():
        rows_per = 512

        def combine(c, carry):
            rows = pl.ds(pl.multiple_of(c * rows_per, rows_per), rows_per)
            l0, l1, l2 = lnat_ref[0, rows, :], lnat_ref[1, rows, :], lnat_ref[2, rows, :]
            m = jnp.maximum(jnp.maximum(l0, l1), l2)
            e0, e1, e2 = jnp.exp2(l0 - m), jnp.exp2(l1 - m), jnp.exp2(l2 - m)
            num = e0 * onat_ref[0, rows, :] + e1 * onat_ref[1, rows, :] + e2 * onat_ref[2, rows, :]
            o_ref[0, rows, :] = (num / (e0 + e1 + e2)).astype(o_ref.dtype)
            return carry

        lax.fori_loop(0, seq // rows_per, combine, 0)


def _attn_c(proj_c, bias_c, bsz, seq):
    slab = lambda role: pl.BlockSpec(
        (None, seq, LANES), lambda b, hp, g: (role * 2 * C_GROUPS + g * 2 + hp, b, 0))
    return pl.pallas_call(
        functools.partial(_attn_c_kernel, seq=seq),
        out_shape=jax.ShapeDtypeStruct((bsz, seq, C_W), jnp.bfloat16),
        grid=(bsz, C_HEADS // 2, C_GROUPS),
        in_specs=[slab(0), slab(1), slab(2),
                  pl.BlockSpec((2, 2, 2 * C_TILE, C_TILE), lambda b, hp, g: (g * 2 + hp, 0, 0, 0))],
        out_specs=pl.BlockSpec((1, seq, LANES), lambda b, hp, g: (b, 0, hp)),
        scratch_shapes=[pltpu.VMEM((seq, LANES), jnp.bfloat16),
                        pltpu.VMEM((seq + C_TILE, LANES), jnp.bfloat16),
                        pltpu.VMEM((LANES, seq + C_TILE), jnp.bfloat16),
                        pltpu.VMEM((C_GROUPS, seq, LANES), jnp.float32),
                        pltpu.VMEM((C_GROUPS, seq, LANES), jnp.float32)]
        + [pltpu.VMEM((seq, LANES), jnp.float32)] * 3,
        compiler_params=pltpu.CompilerParams(
            dimension_semantics=("parallel", "parallel", "arbitrary"),
            vmem_limit_bytes=VMEM_LIMIT),
        name="attn_c",
    )(proj_c, proj_c, proj_c, bias_c)


def _mix_mlp_kernel(x_ref, oa_ref, ob_ref, oc_ref, wo_ref, g_ref, w1_ref, w2_ref, fg_ref, out_ref,
                    acc_ref, h_ref, *, final):
    f = pl.program_id(1)

    @pl.when(f == 0)
    def _():
        mixed = jnp.concatenate([oa_ref[...], ob_ref[...], oc_ref[...]], axis=1)
        x1 = x_ref[...] + jnp.dot(mixed, wo_ref[0], preferred_element_type=jnp.float32)
        acc_ref[...] = x1
        h_ref[...] = _rms(x1, g_ref[0]).astype(jnp.bfloat16)

    u = jnp.dot(h_ref[...], w1_ref[0], preferred_element_type=jnp.float32)
    u = jnp.square(jnp.maximum(u, 0.0)).astype(jnp.bfloat16)
    acc_ref[...] += jnp.dot(u, w2_ref[0], preferred_element_type=jnp.float32)

    @pl.when(f == pl.num_programs(1) - 1)
    def _():
        out_ref[...] = _rms(acc_ref[...], fg_ref[...]) if final else acc_ref[...]


def _mix_mlp(x2, oa, ob, oc, w_o, norm_g, w_1, w_2, final_g, layer, final, *, tm=1024, tf=1024):
    n_tok = x2.shape[0]
    row = lambda w: pl.BlockSpec((tm, w), lambda i, f: (i, 0))
    return pl.pallas_call(
        functools.partial(_mix_mlp_kernel, final=final),
        out_shape=jax.ShapeDtypeStruct((n_tok, D_MODEL), jnp.float32),
        grid=(n_tok // tm, D_FF // tf),
        in_specs=[row(D_MODEL), row(A_W), row(B_W), row(C_W),
                  pl.BlockSpec((1, D_MODEL, D_MODEL), lambda i, f: (layer, 0, 0)),
                  pl.BlockSpec((1, 1, D_MODEL), lambda i, f: (layer, 0, 0)),
                  pl.BlockSpec((1, D_MODEL, tf), lambda i, f: (layer, 0, f)),
                  pl.BlockSpec((1, tf, D_MODEL), lambda i, f: (layer, f, 0)),
                  pl.BlockSpec((1, D_MODEL), lambda i, f: (0, 0))],
        out_specs=row(D_MODEL),
        scratch_shapes=[pltpu.VMEM((tm, D_MODEL), jnp.float32),
                        pltpu.VMEM((tm, D_MODEL), jnp.bfloat16)],
        compiler_params=pltpu.CompilerParams(dimension_semantics=("parallel", "arbitrary"),
                                             vmem_limit_bytes=VMEM_LIMIT),
        name="mix_mlp",
    )(x2, oa, ob, oc, w_o, norm_g, w_1, w_2, final_g)


def _prep_w_in(w_in):
    a0, b0, f0, c0 = 0, 3 * A_W, 3 * A_W + 3 * B_W, 3 * A_W + 3 * B_W + B_HEADS
    sa, sb = A_QK_DIM ** -0.5 * LOG2E, HEAD_DIM ** -0.5 * LOG2E
    cw = C_GROUPS * C_W
    w_tm = jnp.concatenate([w_in[:, :, b0 + B_W:b0 + 2 * B_W],
                            w_in[:, :, a0 + A_W:a0 + 2 * A_W]], axis=-1).astype(jnp.bfloat16)
    w_fm = jnp.concatenate([
        w_in[:, :, b0:b0 + B_W] * sb, w_in[:, :, b0 + 2 * B_W:f0],
        w_in[:, :, a0:a0 + A_W] * sa, w_in[:, :, a0 + 2 * A_W:b0],
    ], axis=-1)
    w_fm = jnp.transpose(w_fm, (0, 2, 1)).astype(jnp.bfloat16)
    w_f = jnp.transpose(w_in[:, :, f0:c0], (0, 2, 1))
    w_f = jnp.pad(w_f, ((0, 0), (0, F_ROWS - B_HEADS), (0, 0))).astype(jnp.bfloat16)
    w_c = jnp.concatenate([w_in[:, :, c0:c0 + cw] * sb, w_in[:, :, c0 + cw:]],
                          axis=-1).astype(jnp.bfloat16)
    return w_tm, w_fm, w_f, w_c


def kernel(x, norm1_g, w_in, b_f, lam_q1, lam_k1, lam_q2, lam_k2, diff_norm_g, w_o, norm2_g,
           w_1, w_2, rel_bias, final_g):
    bsz, seq, _ = x.shape
    depth = w_in.shape[0]
    w_tm, w_fm, w_f, w_c = _prep_w_in(w_in)
    w_o16, w_116, w_216 = (w.astype(jnp.bfloat16) for w in (w_o, w_1, w_2))
    lam_vecs = jnp.stack([lam_q1, lam_k1, lam_q2, lam_k2], axis=1)
    bias_a, bias_c = _make_bias_tiles(rel_bias)

    x2 = x.reshape(bsz * seq, D_MODEL)
    for l in range(depth):
        proj_tm, proj_fm, ft, proj_c = _in_proj(x2, norm1_g[:, None, :], w_tm, w_fm, w_f, w_c, l)
        proj3 = proj_tm.reshape(bsz, seq, N_TM)
        ckp = _decay(ft, b_f[:, :, None], l, bsz, seq)
        lam_init = jnp.full((1,), 0.8 - 0.6 * math.exp(-0.3 * l), jnp.float32)
        oa = _attn_a(proj3, proj_fm, bias_a, lam_vecs, lam_init, diff_norm_g[:, :, None], l)
        ob = _attn_b(proj3, proj_fm, ckp)
        oc = _attn_c(proj_c, bias_c, bsz, seq)
        x2 = _mix_mlp(x2, oa.reshape(bsz * seq, A_W), ob.reshape(bsz * seq, B_W),
                      oc.reshape(bsz * seq, C_W), w_o16, norm2_g[:, None, :], w_116, w_216,
                      final_g[None, :], l, l == depth - 1)
    return x2.reshape(bsz, seq, D_MODEL)
```

```python
import functools
import math

import jax
import jax.numpy as jnp
from jax import lax
from jax.experimental import pallas as pl
from jax.experimental.pallas import tpu as pltpu

D_MODEL = 1024
HEAD_DIM = 64
A_HEADS = 4
A_QK_DIM = 32
B_HEADS = 8
C_HEADS = 4
C_PAIRS = ((128, 1), (512, 4), (2048, 16))
C_GROUPS = 3
A_W = 256
B_W = 512
C_W = 256
D_FF = 4096
N_BUCKETS = 32
MAX_DISTANCE = 2048
NORM_EPS = 1e-6
NEG_INF = -1e30
LOG2E = 1.4426950408889634

LANES = 128
VMEM_LIMIT = 56 * 1024 * 1024

KB, KA = 0, 512
N_TM = 768
QB_T, VB_T, QA_T, VA_T = 0, 512, 1024, 1280
N_FM = 1536
N_C = 2304
N_C_SLABS = N_C // 128
F_ROWS = 16
ONES_ROWS = 16
ACC_ROWS = HEAD_DIM + ONES_ROWS

A_STEP_HEADS = 4
B_STEP_HEADS = 8
C_TILE = 128
C_UNROLL = 16
C_AHEAD = 4
C_SPLIT = 4
FLASH_TILE = 256
A_GROUP = 2
B_GROUP = 1
FLASH_PAIRS = 2


def _bucket_thresholds():
    max_exact = N_BUCKETS // 2
    out = []
    for k in range(1, N_BUCKETS - max_exact):
        t = max_exact * (MAX_DISTANCE / max_exact) ** (k / (N_BUCKETS - max_exact))
        out.append(int(math.ceil(t)))
    return tuple(out)


_THRESHOLDS = _bucket_thresholds()


def _n_near_tiles(tile):
    return -(-(_THRESHOLDS[-1] + tile - 1) // tile)


def _t5_bucket(d):
    big = jnp.full(d.shape, N_BUCKETS // 2, jnp.int32)
    for t in _THRESHOLDS:
        big = big + (d >= t).astype(jnp.int32)
    return jnp.where(d < N_BUCKETS // 2, d, big)


def _bias_lookup(rb_ref, bucket, col):
    val = jnp.zeros(bucket.shape, jnp.float32)
    for b in range(N_BUCKETS):
        val = jnp.where(bucket == b, rb_ref[b, col], val)
    return val


def _bias_a_kernel(rb_ref, out_ref, *, tile):
    delta = pl.program_id(0)
    j = lax.broadcasted_iota(jnp.int32, (tile, tile), 0)
    i = lax.broadcasted_iota(jnp.int32, (tile, tile), 1)
    d = delta * tile + i - j
    bucket = _t5_bucket(d)
    for h in range(A_HEADS):
        val = _bias_lookup(rb_ref, bucket, h) * LOG2E
        out_ref[h, 0] = jnp.where(d >= 0, val, NEG_INF)


def _bias_c_kernel(rb_ref, out_ref):
    g = pl.program_id(0)
    dil = jnp.where(g == 0, C_PAIRS[0][1], jnp.where(g == 1, C_PAIRS[1][1], C_PAIRS[2][1]))
    c = lax.broadcasted_iota(jnp.int32, (2 * C_TILE, C_TILE), 0)
    i = lax.broadcasted_iota(jnp.int32, (2 * C_TILE, C_TILE), 1)
    steps = i + C_TILE - c
    valid = (steps >= 0) & (steps <= C_TILE)
    bucket = _t5_bucket(steps * dil)
    for h in range(C_HEADS):
        val = _bias_lookup(rb_ref, bucket, A_HEADS + g * C_HEADS + h) * LOG2E
        out_ref[h, 1] = jnp.where(valid, val, NEG_INF)
        out_ref[h, 0] = jnp.where(valid & (c >= C_TILE), val, NEG_INF)


def _make_bias_tiles(rel_bias):
    tile = FLASH_TILE
    n_bias = _n_near_tiles(tile) + 1
    smem = pl.BlockSpec(memory_space=pltpu.SMEM)
    bias_a = pl.pallas_call(
        functools.partial(_bias_a_kernel, tile=tile),
        out_shape=jax.ShapeDtypeStruct((A_HEADS, n_bias, tile, tile), jnp.float32),
        grid=(n_bias,),
        in_specs=[smem],
        out_specs=pl.BlockSpec((A_HEADS, 1, tile, tile), lambda d: (0, d, 0, 0)),
        name="bias_a_tiles",
    )(rel_bias)
    bias_c = pl.pallas_call(
        _bias_c_kernel,
        out_shape=jax.ShapeDtypeStruct((C_GROUPS * C_HEADS, 2, 2 * C_TILE, C_TILE), jnp.float32),
        grid=(C_GROUPS,),
        in_specs=[smem],
        out_specs=pl.BlockSpec((C_HEADS, 2, 2 * C_TILE, C_TILE), lambda g: (g, 0, 0, 0)),
        name="bias_c_tiles",
    )(rel_bias)
    return bias_a, bias_c


def _rms(x, g):
    return x * lax.rsqrt(jnp.mean(x * x, axis=-1, keepdims=True) + NORM_EPS) * g


def _dot_nt(a, b):
    return lax.dot_general(a, b, (((1,), (1,)), ((), ())), preferred_element_type=jnp.float32)


def _in_proj_kernel(x_ref, g_ref, wtm_ref, wfm_ref, wf_ref, wc_ref, tm_ref, fm_ref, ft_ref, c_ref):
    h = _rms(x_ref[...], g_ref[0]).astype(jnp.bfloat16)
    tm_ref[...] = jnp.dot(h, wtm_ref[0], preferred_element_type=jnp.float32).astype(jnp.bfloat16)
    for c in range(N_FM // 512):
        rows = slice(c * 512, (c + 1) * 512)
        fm_ref[rows, :] = _dot_nt(wfm_ref[0, rows, :], h).astype(jnp.bfloat16)
    ft_ref[...] = _dot_nt(wf_ref[0], h)
    for c in range(N_C // 256):
        pc = jnp.dot(h, wc_ref[0, :, c * 256:(c + 1) * 256], preferred_element_type=jnp.float32)
        c_ref[2 * c] = pc[:, :LANES]
        c_ref[2 * c + 1] = pc[:, LANES:]


def _in_proj(x2, norm_g, w_tm, w_fm, w_f, w_c, layer, *, tm=512):
    n_tok = x2.shape[0]
    return pl.pallas_call(
        _in_proj_kernel,
        out_shape=(jax.ShapeDtypeStruct((n_tok, N_TM), jnp.bfloat16),
                   jax.ShapeDtypeStruct((N_FM, n_tok), jnp.bfloat16),
                   jax.ShapeDtypeStruct((F_ROWS, n_tok), jnp.float32),
                   jax.ShapeDtypeStruct((N_C_SLABS, n_tok, LANES), jnp.float32)),
        grid=(n_tok // tm,),
        in_specs=[
            pl.BlockSpec((tm, D_MODEL), lambda i: (i, 0)),
            pl.BlockSpec((1, 1, D_MODEL), lambda i: (layer, 0, 0)),
            pl.BlockSpec((1, D_MODEL, N_TM), lambda i: (layer, 0, 0)),
            pl.BlockSpec((1, N_FM, D_MODEL), lambda i: (layer, 0, 0)),
            pl.BlockSpec((1, F_ROWS, D_MODEL), lambda i: (layer, 0, 0)),
            pl.BlockSpec((1, D_MODEL, N_C), lambda i: (layer, 0, 0)),
        ],
        out_specs=(pl.BlockSpec((tm, N_TM), lambda i: (i, 0)),
                   pl.BlockSpec((N_FM, tm), lambda i: (0, i)),
                   pl.BlockSpec((F_ROWS, tm), lambda i: (0, i)),
                   pl.BlockSpec((N_C_SLABS, tm, LANES), lambda i: (0, i, 0))),
        compiler_params=pltpu.CompilerParams(dimension_semantics=("parallel",),
                                             vmem_limit_bytes=VMEM_LIMIT),
        name="in_proj",
    )(x2, norm_g, w_tm, w_fm, w_f, w_c)


def _decay_kernel(ft_ref, bf_ref, ckp_ref):
    z = ft_ref[0:B_HEADS, :] + bf_ref[0]
    x = jnp.minimum(z, 0.0) - jnp.log1p(jnp.exp(-jnp.abs(z)))
    seq = x.shape[1]
    pos = lax.broadcasted_iota(jnp.int32, x.shape, 1)
    shift = 1
    while shift < seq:
        x = x + jnp.where(pos >= shift, pltpu.roll(x, shift, 1), 0.0)
        shift *= 2
    c = x * (-LOG2E)
    hi = c.astype(jnp.bfloat16).astype(jnp.float32)
    mid = (c - hi).astype(jnp.bfloat16).astype(jnp.float32)
    lo = c - hi - mid
    stacked = jnp.concatenate(
        [hi, mid, lo, jnp.zeros((LANES - 3 * B_HEADS, seq), jnp.float32)], axis=0)
    ckp_ref[0] = jnp.transpose(stacked).astype(jnp.bfloat16)


def _decay(ft, b_f, layer, bsz, seq):
    return pl.pallas_call(
        _decay_kernel,
        out_shape=jax.ShapeDtypeStruct((bsz, seq, LANES), jnp.bfloat16),
        grid=(bsz,),
        in_specs=[pl.BlockSpec((F_ROWS, seq), lambda b: (0, b)),
                  pl.BlockSpec((1, B_HEADS, 1), lambda b: (layer, 0, 0))],
        out_specs=pl.BlockSpec((1, seq, LANES), lambda b: (b, 0, 0)),
        compiler_params=pltpu.CompilerParams(dimension_semantics=("parallel",)),
        name="decay_scan",
    )(ft, b_f)


def _colmax(s):
    rows = s.shape[0]
    while rows > 8:
        rows //= 2
        s = jnp.maximum(s[:rows], s[rows:2 * rows])
    return jnp.max(s, axis=0, keepdims=True)


def _online_step(s, smax, v_aug, m_ref, acc_ref, idx):
    m_old = m_ref[idx]
    m_new = jnp.maximum(m_old, smax)
    alpha = jnp.exp2(m_old - m_new)
    p = jnp.exp2(s - m_new).astype(jnp.bfloat16)
    acc_ref[idx] = alpha * acc_ref[idx] + jnp.dot(v_aug, p, preferred_element_type=jnp.float32)
    m_ref[idx] = m_new


def _v_aug(vt_ref, head, k0, tile):
    v = vt_ref[head * HEAD_DIM:(head + 1) * HEAD_DIM, pl.ds(k0, tile)]
    return jnp.concatenate([v, jnp.ones((ONES_ROWS, tile), jnp.bfloat16)], axis=0)


def _flash_pipeline(qi, n_sub, group, produce, consume):
    every = range(n_sub)

    def overlapped(pk, pslot, ck, cslot):
        for g0 in range(0, n_sub, group):
            subs = range(g0, g0 + group)
            produce(pk, pslot, subs)
            consume(ck, cslot, False, subs)

    produce(0, 0, every)

    def pair(k):
        overlapped(k + 1, 1, k, 0)
        overlapped(k + 2, 0, k + 1, 1)

    def pairs(j, carry):
        for u in range(FLASH_PAIRS):
            pair(2 * (FLASH_PAIRS * j + u))
        return carry

    n_pairs = qi // 2
    lax.fori_loop(0, n_pairs // FLASH_PAIRS, pairs, 0)
    for u in range(FLASH_PAIRS - 1):
        done = (n_pairs // FLASH_PAIRS) * FLASH_PAIRS + u
        pl.when(done < n_pairs)(functools.partial(pair, 2 * done))

    @pl.when(qi % 2 == 0)
    def _():
        consume(qi, 0, True, every)

    @pl.when(qi % 2 == 1)
    def _():
        overlapped(qi, 1, qi - 1, 0)
        consume(qi, 1, True, every)


def _attn_a_kernel(qt_ref, k_ref, vt_ref, bias_ref, lamv_ref, lami_ref, subg_ref, o_ref,
                   m_ref, acc_ref, s_ref, smax_ref, *, tile, n_near):
    qi = pl.program_id(2)
    n_sub = 2 * A_STEP_HEADS
    qt = qt_ref[...].astype(jnp.float32)
    row = lax.broadcasted_iota(jnp.int32, (LANES, tile), 0)
    qs = []
    for s in range(n_sub):
        blk = qt[(s // 4) * LANES:(s // 4 + 1) * LANES]
        lo = (s % 4) * A_QK_DIM
        qs.append(jnp.where((row >= lo) & (row < lo + A_QK_DIM), blk, 0.0).astype(jnp.bfloat16))
    m_ref[...] = jnp.full(m_ref.shape, NEG_INF, jnp.float32)
    acc_ref[...] = jnp.zeros(acc_ref.shape, jnp.float32)

    def produce(ki, slot, subs):
        k0 = pl.multiple_of(ki * tile, tile)
        delta = jnp.minimum(qi - ki, n_near)
        for s in subs:
            k = k_ref[0, pl.ds(k0, tile), (s // 4) * LANES:(s // 4 + 1) * LANES]
            x = jnp.dot(k, qs[s], preferred_element_type=jnp.float32) + bias_ref[s // 2, delta]
            s_ref[slot, s] = x
            smax_ref[slot, s] = _colmax(x)

    def consume(ki, slot, last, subs):
        k0 = pl.multiple_of(ki * tile, tile)
        for s in subs:
            _online_step(s_ref[slot, s], smax_ref[slot, s], _v_aug(vt_ref, s // 2, k0, tile),
                         m_ref, acc_ref, s)

    _flash_pipeline(qi, n_sub, A_GROUP, produce, consume)

    lam_init = lami_ref[0]
    lv = lamv_ref[0]
    e1 = jnp.exp(jnp.sum(lv[0:1] * lv[1:2], axis=1, keepdims=True))
    e2 = jnp.exp(jnp.sum(lv[2:3] * lv[3:4], axis=1, keepdims=True))
    lam = e1 - e2 + lam_init
    heads = []
    for h in range(A_STEP_HEADS):
        a0, a1 = acc_ref[2 * h], acc_ref[2 * h + 1]
        o = (a0[0:HEAD_DIM] / a0[HEAD_DIM:HEAD_DIM + 1]
             - lam * (a1[0:HEAD_DIM] / a1[HEAD_DIM:HEAD_DIM + 1]))
        ms = jnp.mean(o * o, axis=0, keepdims=True)
        heads.append(o * lax.rsqrt(ms + NORM_EPS) * subg_ref[0] * (1.0 - lam_init))
    o_ref[0] = jnp.transpose(jnp.concatenate(heads, axis=0)).astype(o_ref.dtype)


def _attn_a(proj_tm3, proj_fm, bias_a, lam_vecs, lam_init, sub_g, layer):
    bsz, seq, _ = proj_tm3.shape
    tile = FLASH_TILE
    n_near = _n_near_tiles(tile)
    n_q = seq // tile
    n_h = A_STEP_HEADS
    width = n_h * HEAD_DIM
    return pl.pallas_call(
        functools.partial(_attn_a_kernel, tile=tile, n_near=n_near),
        out_shape=jax.ShapeDtypeStruct((bsz, seq, A_W), jnp.bfloat16),
        grid=(A_HEADS // n_h, bsz, n_q),
        in_specs=[
            pl.BlockSpec((width, tile), lambda hg, b, qi: (QA_T // width + hg, b * n_q + qi)),
            pl.BlockSpec((1, seq, width), lambda hg, b, qi: (b, 0, KA // width + hg)),
            pl.BlockSpec((width, seq), lambda hg, b, qi: (VA_T // width + hg, b)),
            pl.BlockSpec((n_h, n_near + 1, tile, tile), lambda hg, b, qi: (hg, 0, 0, 0)),
            pl.BlockSpec((1, 4, A_QK_DIM), lambda hg, b, qi: (layer, 0, 0)),
            pl.BlockSpec(memory_space=pltpu.SMEM),
            pl.BlockSpec((1, HEAD_DIM, 1), lambda hg, b, qi: (layer, 0, 0)),
        ],
        out_specs=pl.BlockSpec((1, tile, width), lambda hg, b, qi: (b, qi, hg)),
        scratch_shapes=[pltpu.VMEM((2 * n_h, 1, tile), jnp.float32),
                        pltpu.VMEM((2 * n_h, ACC_ROWS, tile), jnp.float32),
                        pltpu.VMEM((2, 2 * n_h, tile, tile), jnp.float32),
                        pltpu.VMEM((2, 2 * n_h, 1, tile), jnp.float32)],
        compiler_params=pltpu.CompilerParams(
            dimension_semantics=("parallel", "parallel", "parallel"),
            vmem_limit_bytes=VMEM_LIMIT),
        name="attn_a",
    )(proj_fm, proj_tm3, proj_fm, bias_a, lam_vecs, lam_init, sub_g)


def _attn_b_kernel(qt_ref, k_ref, ckp_ref, vt_ref, o_ref, m_ref, acc_ref, s_ref, smax_ref, *,
                   tile):
    hg = pl.program_id(0)
    qi = pl.program_id(2)
    n_h = B_STEP_HEADS
    qt = qt_ref[...].astype(jnp.float32)
    row = lax.broadcasted_iota(jnp.int32, (LANES, tile), 0)
    q_aug = []
    for s in range(n_h):
        blk = qt[(s // 2) * LANES:(s // 2 + 1) * LANES]
        lo = (s % 2) * HEAD_DIM
        q_s = jnp.where((row >= lo) & (row < lo + HEAD_DIM), blk, 0.0)
        pick = (row < 3 * B_HEADS) & ((row % B_HEADS) == n_h * hg + s)
        ones = jnp.where(pick, 1.0, 0.0)
        q_aug.append(jnp.concatenate([q_s, ones], axis=0).astype(jnp.bfloat16))
    m_ref[...] = jnp.full(m_ref.shape, NEG_INF, jnp.float32)
    acc_ref[...] = jnp.zeros(acc_ref.shape, jnp.float32)

    def produce(ki, slot, subs):
        k0 = pl.multiple_of(ki * tile, tile)
        ck = ckp_ref[0, pl.ds(k0, tile), :]
        for s in subs:
            k = k_ref[0, pl.ds(k0, tile), (s // 2) * LANES:(s // 2 + 1) * LANES]
            x = jnp.dot(jnp.concatenate([k, ck], axis=1), q_aug[s],
                        preferred_element_type=jnp.float32)
            s_ref[slot, s] = x
            smax_ref[slot, s] = _colmax(x)

    def consume(ki, slot, last, subs):
        k0 = pl.multiple_of(ki * tile, tile)
        if last:
            kk = lax.broadcasted_iota(jnp.int32, (tile, tile), 0)
            qq = lax.broadcasted_iota(jnp.int32, (tile, tile), 1)
            keep = kk <= qq
        for s in subs:
            x, xmax = s_ref[slot, s], smax_ref[slot, s]
            if last:
                x = jnp.where(keep, x, NEG_INF)
                xmax = _colmax(x)
            _online_step(x, xmax, _v_aug(vt_ref, s, k0, tile), m_ref, acc_ref, s)

    _flash_pipeline(qi, n_h, B_GROUP, produce, consume)
    heads = [acc_ref[s][0:HEAD_DIM] / acc_ref[s][HEAD_DIM:HEAD_DIM + 1] for s in range(n_h)]
    o_ref[0] = jnp.transpose(jnp.concatenate(heads, axis=0)).astype(o_ref.dtype)


def _attn_b(proj_tm3, proj_fm, ckp):
    bsz, seq, _ = proj_tm3.shape
    tile = FLASH_TILE
    n_q = seq // tile
    n_h = B_STEP_HEADS
    width = n_h * HEAD_DIM
    return pl.pallas_call(
        functools.partial(_attn_b_kernel, tile=tile),
        out_shape=jax.ShapeDtypeStruct((bsz, seq, B_W), jnp.bfloat16),
        grid=(B_HEADS // n_h, bsz, n_q),
        in_specs=[
            pl.BlockSpec((width, tile), lambda hg, b, qi: (QB_T // width + hg, b * n_q + qi)),
            pl.BlockSpec((1, seq, width), lambda hg, b, qi: (b, 0, KB // width + hg)),
            pl.BlockSpec((1, seq, LANES), lambda hg, b, qi: (b, 0, 0)),
            pl.BlockSpec((width, seq), lambda hg, b, qi: (VB_T // width + hg, b)),
        ],
        out_specs=pl.BlockSpec((1, tile, width), lambda hg, b, qi: (b, qi, hg)),
        scratch_shapes=[pltpu.VMEM((n_h, 1, tile), jnp.float32),
                        pltpu.VMEM((n_h, ACC_ROWS, tile), jnp.float32),
                        pltpu.VMEM((2, n_h, tile, tile), jnp.float32),
                        pltpu.VMEM((2, n_h, 1, tile), jnp.float32)],
        compiler_params=pltpu.CompilerParams(
            dimension_semantics=("parallel", "parallel", "parallel"),
            vmem_limit_bytes=VMEM_LIMIT),
        name="attn_b",
    )(proj_fm, proj_tm3, ckp, proj_fm)


def _attn_c_kernel(q_ref, k_ref, v_ref, bias_ref, o_ref, qd_ref, kd_ref, vtd_ref, onat_ref,
                   lnat_ref, tq_ref, tk_ref, tv_ref, *, seq):
    g = pl.program_id(2)
    n_chunks = seq // C_TILE
    lane = lax.broadcasted_iota(jnp.int32, (1, LANES), 1)
    first = lane < HEAD_DIM

    def run_group(group, dil):
        n_tiles = seq // dil // C_TILE

        def out_rows(i):
            r, t = i // n_tiles, i % n_tiles
            return pl.ds(r + t * (C_TILE * dil), C_TILE, stride=dil)

        if dil > C_SPLIT:
            per = seq // C_SPLIT // C_TILE

            def presplit(j, carry):
                for u in range(C_UNROLL):
                    i = j * C_UNROLL + u
                    rows = pl.ds(i // per + (i % per) * (C_TILE * C_SPLIT), C_TILE, stride=C_SPLIT)
                    dst = pl.ds(pl.multiple_of(i * C_TILE, C_TILE), C_TILE)
                    for src, tmp in ((q_ref, tq_ref), (k_ref, tk_ref), (v_ref, tv_ref)):
                        tmp[dst, :] = src[rows, :]
                return carry

            lax.fori_loop(0, n_chunks // C_UNROLL, presplit, 0)
            srcs, step = (tq_ref, tk_ref, tv_ref), dil // C_SPLIT

            def src_rows(i):
                r, t = i // n_tiles, i % n_tiles
                base = (r % C_SPLIT) * (seq // C_SPLIT) + r // C_SPLIT + t * (C_TILE * step)
                return pl.ds(base, C_TILE, stride=step)
        else:
            srcs, src_rows = (q_ref, k_ref, v_ref), out_rows

        kd_ref[0:C_TILE, :] = jnp.zeros((C_TILE, LANES), jnp.bfloat16)
        vtd_ref[:, 0:C_TILE] = jnp.zeros((LANES, C_TILE), jnp.bfloat16)

        def gather(j, carry):
            ids = [j * C_UNROLL + u for u in range(C_UNROLL)]
            vts = [jnp.transpose(srcs[2][src_rows(i), :]) for i in ids]
            for i, vt in zip(ids, vts):
                rows = src_rows(i)
                dst = pl.multiple_of(i * C_TILE, C_TILE)
                qd_ref[pl.ds(dst, C_TILE), :] = srcs[0][rows, :].astype(jnp.bfloat16)
                kd_ref[pl.ds(dst + C_TILE, C_TILE), :] = srcs[1][rows, :].astype(jnp.bfloat16)
                vtd_ref[:, pl.ds(dst + C_TILE, C_TILE)] = vt.astype(jnp.bfloat16)
            return carry

        lax.fori_loop(0, n_chunks // C_UNROLL, gather, 0)

        def tiles(j, carry):
            ids = [j * C_UNROLL + u for u in range(C_UNROLL)]
            row0s = [pl.multiple_of(i * C_TILE, C_TILE) for i in ids]

            def scores(u):
                q = qd_ref[pl.ds(row0s[u], C_TILE), :]
                q2 = jnp.concatenate([jnp.where(first, q, jnp.zeros_like(q)),
                                      jnp.where(first, jnp.zeros_like(q), q)], axis=0)
                return _dot_nt(kd_ref[pl.ds(row0s[u], 2 * C_TILE), :], q2)

            def softmax_pv(u, st):
                ver = jnp.minimum(ids[u] % n_tiles, 1)
                res = []
                for h in range(2):
                    x = st[:, h * C_TILE:(h + 1) * C_TILE] + bias_ref[h, ver]
                    m = _colmax(x)
                    p = jnp.exp2(x - m).astype(jnp.bfloat16)
                    v_aug = jnp.concatenate(
                        [vtd_ref[h * HEAD_DIM:(h + 1) * HEAD_DIM, pl.ds(row0s[u], 2 * C_TILE)],
                         jnp.ones((ONES_ROWS, 2 * C_TILE), jnp.bfloat16)], axis=0)
                    res.append((m, jnp.dot(v_aug, p, preferred_element_type=jnp.float32)))
                return res

            def write_back(u, res):
                outs, lses = [], []
                for m, pv in res:
                    l = pv[HEAD_DIM:HEAD_DIM + 1]
                    outs.append(pv[0:HEAD_DIM] / l)
                    lses.append(jnp.broadcast_to(m + jnp.log2(l), (HEAD_DIM, C_TILE)))
                rows = out_rows(ids[u])
                onat_ref[group, rows, :] = jnp.transpose(jnp.concatenate(outs, axis=0))
                lnat_ref[group, rows, :] = jnp.transpose(jnp.concatenate(lses, axis=0))

            sts = {u: scores(u) for u in range(C_AHEAD)}
            pending = None
            for u in range(C_UNROLL):
                if u + C_AHEAD < C_UNROLL:
                    sts[u + C_AHEAD] = scores(u + C_AHEAD)
                res = softmax_pv(u, sts.pop(u))
                if pending is not None:
                    write_back(*pending)
                pending = (u, res)
            write_back(*pending)
            return carry

        lax.fori_loop(0, n_chunks // C_UNROLL, tiles, 0)

    for group, (_, dil) in enumerate(C_PAIRS):
        pl.when(g == group)(functools.partial(run_group, group, dil))

    @pl.when(g == C_GROUPS - 1)
    def _():
        rows_per = 512

        def combine(c, carry):
            rows = pl.ds(pl.multiple_of(c * rows_per, rows_per), rows_per)
            l0, l1, l2 = lnat_ref[0, rows, :], lnat_ref[1, rows, :], lnat_ref[2, rows, :]
            m = jnp.maximum(jnp.maximum(l0, l1), l2)
            e0, e1, e2 = jnp.exp2(l0 - m), jnp.exp2(l1 - m), jnp.exp2(l2 - m)
            num = e0 * onat_ref[0, rows, :] + e1 * onat_ref[1, rows, :] + e2 * onat_ref[2, rows, :]
            o_ref[0, rows, :] = (num / (e0 + e1 + e2)).astype(o_ref.dtype)
            return carry

        lax.fori_loop(0, seq // rows_per, combine, 0)


def _attn_c(proj_c, bias_c, bsz, seq):
    slab = lambda role: pl.BlockSpec(
        (None, seq, LANES), lambda b, hp, g: (role * 2 * C_GROUPS + g * 2 + hp, b, 0))
    return pl.pallas_call(
        functools.partial(_attn_c_kernel, seq=seq),
        out_shape=jax.ShapeDtypeStruct((bsz, seq, C_W), jnp.bfloat16),
        grid=(bsz, C_HEADS // 2, C_GROUPS),
        in_specs=[slab(0), slab(1), slab(2),
                  pl.BlockSpec((2, 2, 2 * C_TILE, C_TILE), lambda b, hp, g: (g * 2 + hp, 0, 0, 0))],
        out_specs=pl.BlockSpec((1, seq, LANES), lambda b, hp, g: (b, 0, hp)),
        scratch_shapes=[pltpu.VMEM((seq, LANES), jnp.bfloat16),
                        pltpu.VMEM((seq + C_TILE, LANES), jnp.bfloat16),
                        pltpu.VMEM((LANES, seq + C_TILE), jnp.bfloat16),
                        pltpu.VMEM((C_GROUPS, seq, LANES), jnp.float32),
                        pltpu.VMEM((C_GROUPS, seq, LANES), jnp.float32)]
        + [pltpu.VMEM((seq, LANES), jnp.float32)] * 3,
        compiler_params=pltpu.CompilerParams(
            dimension_semantics=("parallel", "parallel", "arbitrary"),
            vmem_limit_bytes=VMEM_LIMIT),
        name="attn_c",
    )(proj_c, proj_c, proj_c, bias_c)


def _mix_mlp_kernel(x_ref, oa_ref, ob_ref, oc_ref, wo_ref, g_ref, w1_ref, w2_ref, fg_ref, out_ref,
                    acc_ref, h_ref, *, final):
    f = pl.program_id(1)

    @pl.when(f == 0)
    def _():
        mixed = jnp.concatenate([oa_ref[...], ob_ref[...], oc_ref[...]], axis=1)
        x1 = x_ref[...] + jnp.dot(mixed, wo_ref[0], preferred_element_type=jnp.float32)
        acc_ref[...] = x1
        h_ref[...] = _rms(x1, g_ref[0]).astype(jnp.bfloat16)

    u = jnp.dot(h_ref[...], w1_ref[0], preferred_element_type=jnp.float32)
    u = jnp.square(jnp.maximum(u, 0.0)).astype(jnp.bfloat16)
    acc_ref[...] += jnp.dot(u, w2_ref[0], preferred_element_type=jnp.float32)

    @pl.when(f == pl.num_programs(1) - 1)
    def _():
        out_ref[...] = _rms(acc_ref[...], fg_ref[...]) if final else acc_ref[...]


def _mix_mlp(x2, oa, ob, oc, w_o, norm_g, w_1, w_2, final_g, layer, final, *, tm=1024, tf=1024):
    n_tok = x2.shape[0]
    row = lambda w: pl.BlockSpec((tm, w), lambda i, f: (i, 0))
    return pl.pallas_call(
        functools.partial(_mix_mlp_kernel, final=final),
        out_shape=jax.ShapeDtypeStruct((n_tok, D_MODEL), jnp.float32),
        grid=(n_tok // tm, D_FF // tf),
        in_specs=[row(D_MODEL), row(A_W), row(B_W), row(C_W),
                  pl.BlockSpec((1, D_MODEL, D_MODEL), lambda i, f: (layer, 0, 0)),
                  pl.BlockSpec((1, 1, D_MODEL), lambda i, f: (layer, 0, 0)),
                  pl.BlockSpec((1, D_MODEL, tf), lambda i, f: (layer, 0, f)),
                  pl.BlockSpec((1, tf, D_MODEL), lambda i, f: (layer, f, 0)),
                  pl.BlockSpec((1, D_MODEL), lambda i, f: (0, 0))],
        out_specs=row(D_MODEL),
        scratch_shapes=[pltpu.VMEM((tm, D_MODEL), jnp.float32),
                        pltpu.VMEM((tm, D_MODEL), jnp.bfloat16)],
        compiler_params=pltpu.CompilerParams(dimension_semantics=("parallel", "arbitrary"),
                                             vmem_limit_bytes=VMEM_LIMIT),
        name="mix_mlp",
    )(x2, oa, ob, oc, w_o, norm_g, w_1, w_2, final_g)


def _prep_w_in(w_in):
    a0, b0, f0, c0 = 0, 3 * A_W, 3 * A_W + 3 * B_W, 3 * A_W + 3 * B_W + B_HEADS
    sa, sb = A_QK_DIM ** -0.5 * LOG2E, HEAD_DIM ** -0.5 * LOG2E
    cw = C_GROUPS * C_W
    w_tm = jnp.concatenate([w_in[:, :, b0 + B_W:b0 + 2 * B_W],
                            w_in[:, :, a0 + A_W:a0 + 2 * A_W]], axis=-1).astype(jnp.bfloat16)
    w_fm = jnp.concatenate([
        w_in[:, :, b0:b0 + B_W] * sb, w_in[:, :, b0 + 2 * B_W:f0],
        w_in[:, :, a0:a0 + A_W] * sa, w_in[:, :, a0 + 2 * A_W:b0],
    ], axis=-1)
    w_fm = jnp.transpose(w_fm, (0, 2, 1)).astype(jnp.bfloat16)
    w_f = jnp.transpose(w_in[:, :, f0:c0], (0, 2, 1))
    w_f = jnp.pad(w_f, ((0, 0), (0, F_ROWS - B_HEADS), (0, 0))).astype(jnp.bfloat16)
    w_c = jnp.concatenate([w_in[:, :, c0:c0 + cw] * sb, w_in[:, :, c0 + cw:]],
                          axis=-1).astype(jnp.bfloat16)
    return w_tm, w_fm, w_f, w_c


def kernel(x, norm1_g, w_in, b_f, lam_q1, lam_k1, lam_q2, lam_k2, diff_norm_g, w_o, norm2_g,
           w_1, w_2, rel_bias, final_g):
    bsz, seq, _ = x.shape
    depth = w_in.shape[0]
    w_tm, w_fm, w_f, w_c = _prep_w_in(w_in)
    w_o16, w_116, w_216 = (w.astype(jnp.bfloat16) for w in (w_o, w_1, w_2))
    lam_vecs = jnp.stack([lam_q1, lam_k1, lam_q2, lam_k2], axis=1)
    bias_a, bias_c = _make_bias_tiles(rel_bias)

    x2 = x.reshape(bsz * seq, D_MODEL)
    for l in range(depth):
        proj_tm, proj_fm, ft, proj_c = _in_proj(x2, norm1_g[:, None, :], w_tm, w_fm, w_f, w_c, l)
        proj3 = proj_tm.reshape(bsz, seq, N_TM)
        ckp = _decay(ft, b_f[:, :, None], l, bsz, seq)
        lam_init = jnp.full((1,), 0.8 - 0.6 * math.exp(-0.3 * l), jnp.float32)
        oa = _attn_a(proj3, proj_fm, bias_a, lam_vecs, lam_init, diff_norm_g[:, :, None], l)
        ob = _attn_b(proj3, proj_fm, ckp)
        oc = _attn_c(proj_c, bias_c, bsz, seq)
        x2 = _mix_mlp(x2, oa.reshape(bsz * seq, A_W), ob.reshape(bsz * seq, B_W),
                      oc.reshape(bsz * seq, C_W), w_o16, norm2_g[:, None, :], w_116, w_216,
                      final_g[None, :], l, l == depth - 1)
    return x2.reshape(bsz, seq, D_MODEL)
```

```python
import functools
import math

import jax
import jax.numpy as jnp
from jax import lax
from jax.experimental import pallas as pl
from jax.experimental.pallas import tpu as pltpu

D_MODEL = 1024
HEAD_DIM = 64
A_HEADS = 4
A_QK_DIM = 32
B_HEADS = 8
C_HEADS = 4
C_PAIRS = ((128, 1), (512, 4), (2048, 16))
C_GROUPS = 3
A_W = 256
B_W = 512
C_W = 256
D_FF = 4096
N_BUCKETS = 32
MAX_DISTANCE = 2048
NORM_EPS = 1e-6
NEG_INF = -1e30
LOG2E = 1.4426950408889634

LANES = 128
VMEM_LIMIT = 56 * 1024 * 1024

KB, KA = 0, 512
N_TM = 768
QB_T, VB_T, QA_T, VA_T = 0, 512, 1024, 1280
N_FM = 1536
N_C = 2304
N_C_SLABS = N_C // 128
F_ROWS = 16
ONES_ROWS = 16
ACC_ROWS = HEAD_DIM + ONES_ROWS

A_STEP_HEADS = 4
B_STEP_HEADS = 8
C_TILE = 128
C_UNROLL = 32
C_AHEAD = 4
C_SPLIT = 4
FLASH_TILE = 256
A_GROUP = 2
B_GROUP = 1
FLASH_PAIRS = 2


def _bucket_thresholds():
    max_exact = N_BUCKETS // 2
    out = []
    for k in range(1, N_BUCKETS - max_exact):
        t = max_exact * (MAX_DISTANCE / max_exact) ** (k / (N_BUCKETS - max_exact))
        out.append(int(math.ceil(t)))
    return tuple(out)


_THRESHOLDS = _bucket_thresholds()


def _n_near_tiles(tile):
    return -(-(_THRESHOLDS[-1] + tile - 1) // tile)


def _t5_bucket(d):
    big = jnp.full(d.shape, N_BUCKETS // 2, jnp.int32)
    for t in _THRESHOLDS:
        big = big + (d >= t).astype(jnp.int32)
    return jnp.where(d < N_BUCKETS // 2, d, big)


def _bias_lookup(rb_ref, bucket, col):
    val = jnp.zeros(bucket.shape, jnp.float32)
    for b in range(N_BUCKETS):
        val = jnp.where(bucket == b, rb_ref[b, col], val)
    return val


def _bias_a_kernel(rb_ref, out_ref, *, tile):
    delta = pl.program_id(0)
    j = lax.broadcasted_iota(jnp.int32, (tile, tile), 0)
    i = lax.broadcasted_iota(jnp.int32, (tile, tile), 1)
    d = delta * tile + i - j
    bucket = _t5_bucket(d)
    for h in range(A_HEADS):
        val = _bias_lookup(rb_ref, bucket, h) * LOG2E
        out_ref[h, 0] = jnp.where(d >= 0, val, NEG_INF)


def _bias_c_kernel(rb_ref, out_ref):
    g = pl.program_id(0)
    dil = jnp.where(g == 0, C_PAIRS[0][1], jnp.where(g == 1, C_PAIRS[1][1], C_PAIRS[2][1]))
    c = lax.broadcasted_iota(jnp.int32, (2 * C_TILE, C_TILE), 0)
    i = lax.broadcasted_iota(jnp.int32, (2 * C_TILE, C_TILE), 1)
    steps = i + C_TILE - c
    valid = (steps >= 0) & (steps <= C_TILE)
    bucket = _t5_bucket(steps * dil)
    for h in range(C_HEADS):
        val = _bias_lookup(rb_ref, bucket, A_HEADS + g * C_HEADS + h) * LOG2E
        out_ref[h, 1] = jnp.where(valid, val, NEG_INF)
        out_ref[h, 0] = jnp.where(valid & (c >= C_TILE), val, NEG_INF)


def _make_bias_tiles(rel_bias):
    tile = FLASH_TILE
    n_bias = _n_near_tiles(tile) + 1
    smem = pl.BlockSpec(memory_space=pltpu.SMEM)
    bias_a = pl.pallas_call(
        functools.partial(_bias_a_kernel, tile=tile),
        out_shape=jax.ShapeDtypeStruct((A_HEADS, n_bias, tile, tile), jnp.float32),
        grid=(n_bias,),
        in_specs=[smem],
        out_specs=pl.BlockSpec((A_HEADS, 1, tile, tile), lambda d: (0, d, 0, 0)),
        name="bias_a_tiles",
    )(rel_bias)
    bias_c = pl.pallas_call(
        _bias_c_kernel,
        out_shape=jax.ShapeDtypeStruct((C_GROUPS * C_HEADS, 2, 2 * C_TILE, C_TILE), jnp.float32),
        grid=(C_GROUPS,),
        in_specs=[smem],
        out_specs=pl.BlockSpec((C_HEADS, 2, 2 * C_TILE, C_TILE), lambda g: (g, 0, 0, 0)),
        name="bias_c_tiles",
    )(rel_bias)
    return bias_a, bias_c


def _rms(x, g):
    return x * lax.rsqrt(jnp.mean(x * x, axis=-1, keepdims=True) + NORM_EPS) * g


def _dot_nt(a, b):
    return lax.dot_general(a, b, (((1,), (1,)), ((), ())), preferred_element_type=jnp.float32)


def _in_proj_kernel(x_ref, g_ref, wtm_ref, wfm_ref, wf_ref, wc_ref, tm_ref, fm_ref, ft_ref, c_ref):
    h = _rms(x_ref[...], g_ref[0]).astype(jnp.bfloat16)
    tm_ref[...] = jnp.dot(h, wtm_ref[0], preferred_element_type=jnp.float32).astype(jnp.bfloat16)
    for c in range(N_FM // 512):
        rows = slice(c * 512, (c + 1) * 512)
        fm_ref[rows, :] = _dot_nt(wfm_ref[0, rows, :], h).astype(jnp.bfloat16)
    ft_ref[...] = _dot_nt(wf_ref[0], h)
    for c in range(N_C // 256):
        pc = jnp.dot(h, wc_ref[0, :, c * 256:(c + 1) * 256], preferred_element_type=jnp.float32)
        c_ref[2 * c] = pc[:, :LANES]
        c_ref[2 * c + 1] = pc[:, LANES:]


def _in_proj(x2, norm_g, w_tm, w_fm, w_f, w_c, layer, *, tm=512):
    n_tok = x2.shape[0]
    return pl.pallas_call(
        _in_proj_kernel,
        out_shape=(jax.ShapeDtypeStruct((n_tok, N_TM), jnp.bfloat16),
                   jax.ShapeDtypeStruct((N_FM, n_tok), jnp.bfloat16),
                   jax.ShapeDtypeStruct((F_ROWS, n_tok), jnp.float32),
                   jax.ShapeDtypeStruct((N_C_SLABS, n_tok, LANES), jnp.float32)),
        grid=(n_tok // tm,),
        in_specs=[
            pl.BlockSpec((tm, D_MODEL), lambda i: (i, 0)),
            pl.BlockSpec((1, 1, D_MODEL), lambda i: (layer, 0, 0)),
            pl.BlockSpec((1, D_MODEL, N_TM), lambda i: (layer, 0, 0)),
            pl.BlockSpec((1, N_FM, D_MODEL), lambda i: (layer, 0, 0)),
            pl.BlockSpec((1, F_ROWS, D_MODEL), lambda i: (layer, 0, 0)),
            pl.BlockSpec((1, D_MODEL, N_C), lambda i: (layer, 0, 0)),
        ],
        out_specs=(pl.BlockSpec((tm, N_TM), lambda i: (i, 0)),
                   pl.BlockSpec((N_FM, tm), lambda i: (0, i)),
                   pl.BlockSpec((F_ROWS, tm), lambda i: (0, i)),
                   pl.BlockSpec((N_C_SLABS, tm, LANES), lambda i: (0, i, 0))),
        compiler_params=pltpu.CompilerParams(dimension_semantics=("parallel",),
                                             vmem_limit_bytes=VMEM_LIMIT),
        name="in_proj",
    )(x2, norm_g, w_tm, w_fm, w_f, w_c)


def _decay_kernel(ft_ref, bf_ref, ckp_ref):
    z = ft_ref[0:B_HEADS, :] + bf_ref[0]
    x = jnp.minimum(z, 0.0) - jnp.log1p(jnp.exp(-jnp.abs(z)))
    seq = x.shape[1]
    pos = lax.broadcasted_iota(jnp.int32, x.shape, 1)
    shift = 1
    while shift < seq:
        x = x + jnp.where(pos >= shift, pltpu.roll(x, shift, 1), 0.0)
        shift *= 2
    c = x * (-LOG2E)
    hi = c.astype(jnp.bfloat16).astype(jnp.float32)
    mid = (c - hi).astype(jnp.bfloat16).astype(jnp.float32)
    lo = c - hi - mid
    stacked = jnp.concatenate(
        [hi, mid, lo, jnp.zeros((LANES - 3 * B_HEADS, seq), jnp.float32)], axis=0)
    ckp_ref[0] = jnp.transpose(stacked).astype(jnp.bfloat16)


def _decay(ft, b_f, layer, bsz, seq):
    return pl.pallas_call(
        _decay_kernel,
        out_shape=jax.ShapeDtypeStruct((bsz, seq, LANES), jnp.bfloat16),
        grid=(bsz,),
        in_specs=[pl.BlockSpec((F_ROWS, seq), lambda b: (0, b)),
                  pl.BlockSpec((1, B_HEADS, 1), lambda b: (layer, 0, 0))],
        out_specs=pl.BlockSpec((1, seq, LANES), lambda b: (b, 0, 0)),
        compiler_params=pltpu.CompilerParams(dimension_semantics=("parallel",)),
        name="decay_scan",
    )(ft, b_f)


def _colmax(s):
    rows = s.shape[0]
    while rows > 8:
        rows //= 2
        s = jnp.maximum(s[:rows], s[rows:2 * rows])
    return jnp.max(s, axis=0, keepdims=True)


def _online_step(s, smax, v_aug, m_ref, acc_ref, idx):
    m_old = m_ref[idx]
    m_new = jnp.maximum(m_old, smax)
    alpha = jnp.exp2(m_old - m_new)
    p = jnp.exp2(s - m_new).astype(jnp.bfloat16)
    acc_ref[idx] = alpha * acc_ref[idx] + jnp.dot(v_aug, p, preferred_element_type=jnp.float32)
    m_ref[idx] = m_new


def _v_aug(vt_ref, head, k0, tile):
    v = vt_ref[head * HEAD_DIM:(head + 1) * HEAD_DIM, pl.ds(k0, tile)]
    return jnp.concatenate([v, jnp.ones((ONES_ROWS, tile), jnp.bfloat16)], axis=0)


def _flash_pipeline(qi, n_sub, group, produce, consume, finish):
    every = range(n_sub)

    def overlapped(pk, pslot, ck, cslot):
        for g0 in range(0, n_sub, group):
            subs = range(g0, g0 + group)
            produce(pk, pslot, subs)
            consume(ck, cslot, False, subs)

    produce(0, 0, every)

    def pair(k):
        overlapped(k + 1, 1, k, 0)
        overlapped(k + 2, 0, k + 1, 1)

    def pairs(j, carry):
        for u in range(FLASH_PAIRS):
            pair(2 * (FLASH_PAIRS * j + u))
        return carry

    n_pairs = qi // 2
    lax.fori_loop(0, n_pairs // FLASH_PAIRS, pairs, 0)
    for u in range(FLASH_PAIRS - 1):
        done = (n_pairs // FLASH_PAIRS) * FLASH_PAIRS + u
        pl.when(done < n_pairs)(functools.partial(pair, 2 * done))

    @pl.when(qi % 2 == 0)
    def _():
        consume(qi, 0, True, every)
        finish()

    @pl.when(qi % 2 == 1)
    def _():
        overlapped(qi, 1, qi - 1, 0)
        consume(qi, 1, True, every)
        finish()


def _attn_a_kernel(qt_ref, k_ref, vt_ref, bias_ref, lamv_ref, lami_ref, subg_ref, o_ref,
                   m_ref, acc_ref, s_ref, smax_ref, *, tile, n_near):
    qi = pl.program_id(2)
    n_sub = 2 * A_STEP_HEADS
    qt = qt_ref[...].astype(jnp.float32)
    row = lax.broadcasted_iota(jnp.int32, (LANES, tile), 0)
    qs = []
    for s in range(n_sub):
        blk = qt[(s // 4) * LANES:(s // 4 + 1) * LANES]
        lo = (s % 4) * A_QK_DIM
        qs.append(jnp.where((row >= lo) & (row < lo + A_QK_DIM), blk, 0.0).astype(jnp.bfloat16))
    m_ref[...] = jnp.full(m_ref.shape, NEG_INF, jnp.float32)
    acc_ref[...] = jnp.zeros(acc_ref.shape, jnp.float32)

    def produce(ki, slot, subs):
        k0 = pl.multiple_of(ki * tile, tile)
        delta = jnp.minimum(qi - ki, n_near)
        for s in subs:
            k = k_ref[0, pl.ds(k0, tile), (s // 4) * LANES:(s // 4 + 1) * LANES]
            x = jnp.dot(k, qs[s], preferred_element_type=jnp.float32) + bias_ref[s // 2, delta]
            s_ref[slot, s] = x
            smax_ref[slot, s] = _colmax(x)

    def consume(ki, slot, last, subs):
        k0 = pl.multiple_of(ki * tile, tile)
        for s in subs:
            _online_step(s_ref[slot, s], smax_ref[slot, s], _v_aug(vt_ref, s // 2, k0, tile),
                         m_ref, acc_ref, s)

    def finish():
        lam_init = lami_ref[0]
        lv = lamv_ref[0]
        e1 = jnp.exp(jnp.sum(lv[0:1] * lv[1:2], axis=1, keepdims=True))
        e2 = jnp.exp(jnp.sum(lv[2:3] * lv[3:4], axis=1, keepdims=True))
        lam = e1 - e2 + lam_init
        heads = []
        for h in range(A_STEP_HEADS):
            a0, a1 = acc_ref[2 * h], acc_ref[2 * h + 1]
            o = (a0[0:HEAD_DIM] / a0[HEAD_DIM:HEAD_DIM + 1]
                 - lam * (a1[0:HEAD_DIM] / a1[HEAD_DIM:HEAD_DIM + 1]))
            ms = jnp.mean(o * o, axis=0, keepdims=True)
            heads.append(o * lax.rsqrt(ms + NORM_EPS) * subg_ref[0] * (1.0 - lam_init))
        o_ref[0] = jnp.transpose(jnp.concatenate(heads, axis=0)).astype(o_ref.dtype)

    _flash_pipeline(qi, n_sub, A_GROUP, produce, consume, finish)


def _attn_a(proj_tm3, proj_fm, bias_a, lam_vecs, lam_init, sub_g, layer):
    bsz, seq, _ = proj_tm3.shape
    tile = FLASH_TILE
    n_near = _n_near_tiles(tile)
    n_q = seq // tile
    n_h = A_STEP_HEADS
    width = n_h * HEAD_DIM
    return pl.pallas_call(
        functools.partial(_attn_a_kernel, tile=tile, n_near=n_near),
        out_shape=jax.ShapeDtypeStruct((bsz, seq, A_W), jnp.bfloat16),
        grid=(A_HEADS // n_h, bsz, n_q),
        in_specs=[
            pl.BlockSpec((width, tile), lambda hg, b, qi: (QA_T // width + hg, b * n_q + qi)),
            pl.BlockSpec((1, seq, width), lambda hg, b, qi: (b, 0, KA // width + hg)),
            pl.BlockSpec((width, seq), lambda hg, b, qi: (VA_T // width + hg, b)),
            pl.BlockSpec((n_h, n_near + 1, tile, tile), lambda hg, b, qi: (hg, 0, 0, 0)),
            pl.BlockSpec((1, 4, A_QK_DIM), lambda hg, b, qi: (layer, 0, 0)),
            pl.BlockSpec(memory_space=pltpu.SMEM),
            pl.BlockSpec((1, HEAD_DIM, 1), lambda hg, b, qi: (layer, 0, 0)),
        ],
        out_specs=pl.BlockSpec((1, tile, width), lambda hg, b, qi: (b, qi, hg)),
        scratch_shapes=[pltpu.VMEM((2 * n_h, 1, tile), jnp.float32),
                        pltpu.VMEM((2 * n_h, ACC_ROWS, tile), jnp.float32),
                        pltpu.VMEM((2, 2 * n_h, tile, tile), jnp.float32),
                        pltpu.VMEM((2, 2 * n_h, 1, tile), jnp.float32)],
        compiler_params=pltpu.CompilerParams(
            dimension_semantics=("parallel", "parallel", "parallel"),
            vmem_limit_bytes=VMEM_LIMIT),
        name="attn_a",
    )(proj_fm, proj_tm3, proj_fm, bias_a, lam_vecs, lam_init, sub_g)


def _attn_b_kernel(qt_ref, k_ref, ckp_ref, vt_ref, o_ref, m_ref, acc_ref, s_ref, smax_ref, *,
                   tile):
    hg = pl.program_id(0)
    qi = pl.program_id(2)
    n_h = B_STEP_HEADS
    qt = qt_ref[...].astype(jnp.float32)
    row = lax.broadcasted_iota(jnp.int32, (LANES, tile), 0)
    q_aug = []
    for s in range(n_h):
        blk = qt[(s // 2) * LANES:(s // 2 + 1) * LANES]
        lo = (s % 2) * HEAD_DIM
        q_s = jnp.where((row >= lo) & (row < lo + HEAD_DIM), blk, 0.0)
        pick = (row < 3 * B_HEADS) & ((row % B_HEADS) == n_h * hg + s)
        ones = jnp.where(pick, 1.0, 0.0)
        q_aug.append(jnp.concatenate([q_s, ones], axis=0).astype(jnp.bfloat16))
    m_ref[...] = jnp.full(m_ref.shape, NEG_INF, jnp.float32)
    acc_ref[...] = jnp.zeros(acc_ref.shape, jnp.float32)

    def produce(ki, slot, subs):
        k0 = pl.multiple_of(ki * tile, tile)
        ck = ckp_ref[0, pl.ds(k0, tile), :]
        for s in subs:
            k = k_ref[0, pl.ds(k0, tile), (s // 2) * LANES:(s // 2 + 1) * LANES]
            x = jnp.dot(jnp.concatenate([k, ck], axis=1), q_aug[s],
                        preferred_element_type=jnp.float32)
            s_ref[slot, s] = x
            smax_ref[slot, s] = _colmax(x)

    def consume(ki, slot, last, subs):
        k0 = pl.multiple_of(ki * tile, tile)
        if last:
            kk = lax.broadcasted_iota(jnp.int32, (tile, tile), 0)
            qq = lax.broadcasted_iota(jnp.int32, (tile, tile), 1)
            keep = kk <= qq
        for s in subs:
            x, xmax = s_ref[slot, s], smax_ref[slot, s]
            if last:
                x = jnp.where(keep, x, NEG_INF)
                xmax = _colmax(x)
            _online_step(x, xmax, _v_aug(vt_ref, s, k0, tile), m_ref, acc_ref, s)

    def finish():
        heads = [acc_ref[s][0:HEAD_DIM] / acc_ref[s][HEAD_DIM:HEAD_DIM + 1] for s in range(n_h)]
        o_ref[0] = jnp.transpose(jnp.concatenate(heads, axis=0)).astype(o_ref.dtype)

    _flash_pipeline(qi, n_h, B_GROUP, produce, consume, finish)


def _attn_b(proj_tm3, proj_fm, ckp):
    bsz, seq, _ = proj_tm3.shape
    tile = FLASH_TILE
    n_q = seq // tile
    n_h = B_STEP_HEADS
    width = n_h * HEAD_DIM
    return pl.pallas_call(
        functools.partial(_attn_b_kernel, tile=tile),
        out_shape=jax.ShapeDtypeStruct((bsz, seq, B_W), jnp.bfloat16),
        grid=(B_HEADS // n_h, bsz, n_q),
        in_specs=[
            pl.BlockSpec((width, tile), lambda hg, b, qi: (QB_T // width + hg, b * n_q + qi)),
            pl.BlockSpec((1, seq, width), lambda hg, b, qi: (b, 0, KB // width + hg)),
            pl.BlockSpec((1, seq, LANES), lambda hg, b, qi: (b, 0, 0)),
            pl.BlockSpec((width, seq), lambda hg, b, qi: (VB_T // width + hg, b)),
        ],
        out_specs=pl.BlockSpec((1, tile, width), lambda hg, b, qi: (b, qi, hg)),
        scratch_shapes=[pltpu.VMEM((n_h, 1, tile), jnp.float32),
                        pltpu.VMEM((n_h, ACC_ROWS, tile), jnp.float32),
                        pltpu.VMEM((2, n_h, tile, tile), jnp.float32),
                        pltpu.VMEM((2, n_h, 1, tile), jnp.float32)],
        compiler_params=pltpu.CompilerParams(
            dimension_semantics=("parallel", "parallel", "parallel"),
            vmem_limit_bytes=VMEM_LIMIT),
        name="attn_b",
    )(proj_fm, proj_tm3, ckp, proj_fm)


def _attn_c_kernel(q_ref, k_ref, v_ref, bias_ref, o_ref, qd_ref, kd_ref, vtd_ref, onat_ref,
                   lnat_ref, tq_ref, tk_ref, tv_ref, *, seq):
    g = pl.program_id(2)
    n_chunks = seq // C_TILE
    lane = lax.broadcasted_iota(jnp.int32, (1, LANES), 1)
    first = lane < HEAD_DIM

    def run_group(group, dil):
        n_tiles = seq // dil // C_TILE

        def out_rows(i):
            r, t = i // n_tiles, i % n_tiles
            return pl.ds(r + t * (C_TILE * dil), C_TILE, stride=dil)

        if dil > C_SPLIT:
            per = seq // C_SPLIT // C_TILE

            def presplit(j, carry):
                for u in range(C_UNROLL):
                    i = j * C_UNROLL + u
                    rows = pl.ds(i // per + (i % per) * (C_TILE * C_SPLIT), C_TILE, stride=C_SPLIT)
                    dst = pl.ds(pl.multiple_of(i * C_TILE, C_TILE), C_TILE)
                    for src, tmp in ((q_ref, tq_ref), (k_ref, tk_ref), (v_ref, tv_ref)):
                        tmp[dst, :] = src[rows, :]
                return carry

            lax.fori_loop(0, n_chunks // C_UNROLL, presplit, 0)
            srcs, step = (tq_ref, tk_ref, tv_ref), dil // C_SPLIT

            def src_rows(i):
                r, t = i // n_tiles, i % n_tiles
                base = (r % C_SPLIT) * (seq // C_SPLIT) + r // C_SPLIT + t * (C_TILE * step)
                return pl.ds(base, C_TILE, stride=step)
        else:
            srcs, src_rows = (q_ref, k_ref, v_ref), out_rows

        kd_ref[0:C_TILE, :] = jnp.zeros((C_TILE, LANES), jnp.bfloat16)
        vtd_ref[:, 0:C_TILE] = jnp.zeros((LANES, C_TILE), jnp.bfloat16)

        def gather(j, carry):
            ids = [j * C_UNROLL + u for u in range(C_UNROLL)]
            vts = [jnp.transpose(srcs[2][src_rows(i), :]) for i in ids]
            for i, vt in zip(ids, vts):
                rows = src_rows(i)
                dst = pl.multiple_of(i * C_TILE, C_TILE)
                qd_ref[pl.ds(dst, C_TILE), :] = srcs[0][rows, :].astype(jnp.bfloat16)
                kd_ref[pl.ds(dst + C_TILE, C_TILE), :] = srcs[1][rows, :].astype(jnp.bfloat16)
                vtd_ref[:, pl.ds(dst + C_TILE, C_TILE)] = vt.astype(jnp.bfloat16)
            return carry

        lax.fori_loop(0, n_chunks // C_UNROLL, gather, 0)

        def tiles(j, carry):
            ids = [j * C_UNROLL + u for u in range(C_UNROLL)]
            row0s = [pl.multiple_of(i * C_TILE, C_TILE) for i in ids]

            def scores(u):
                q = qd_ref[pl.ds(row0s[u], C_TILE), :]
                q2 = jnp.concatenate([jnp.where(first, q, jnp.zeros_like(q)),
                                      jnp.where(first, jnp.zeros_like(q), q)], axis=0)
                return _dot_nt(kd_ref[pl.ds(row0s[u], 2 * C_TILE), :], q2)

            def softmax_pv(u, st):
                ver = jnp.minimum(ids[u] % n_tiles, 1)
                res = []
                for h in range(2):
                    x = st[:, h * C_TILE:(h + 1) * C_TILE] + bias_ref[h, ver]
                    m = _colmax(x)
                    p = jnp.exp2(x - m).astype(jnp.bfloat16)
                    v_aug = jnp.concatenate(
                        [vtd_ref[h * HEAD_DIM:(h + 1) * HEAD_DIM, pl.ds(row0s[u], 2 * C_TILE)],
                         jnp.ones((ONES_ROWS, 2 * C_TILE), jnp.bfloat16)], axis=0)
                    res.append((m, jnp.dot(v_aug, p, preferred_element_type=jnp.float32)))
                return res

            def write_back(u, res):
                outs, lses = [], []
                for m, pv in res:
                    l = pv[HEAD_DIM:HEAD_DIM + 1]
                    outs.append(pv[0:HEAD_DIM] / l)
                    lses.append(jnp.broadcast_to(m + jnp.log2(l), (HEAD_DIM, C_TILE)))
                rows = out_rows(ids[u])
                onat_ref[group, rows, :] = jnp.transpose(jnp.concatenate(outs, axis=0))
                lnat_ref[group, rows, :] = jnp.transpose(jnp.concatenate(lses, axis=0))

            sts = {u: scores(u) for u in range(C_AHEAD)}
            pending = None
            for u in range(C_UNROLL):
                if u + C_AHEAD < C_UNROLL:
                    sts[u + C_AHEAD] = scores(u + C_AHEAD)
                res = softmax_pv(u, sts.pop(u))
                if pending is not None:
                    write_back(*pending)
                pending = (u, res)
            write_back(*pending)
            return carry

        lax.fori_loop(0, n_chunks // C_UNROLL, tiles, 0)

    for group, (_, dil) in enumerate(C_PAIRS):
        pl.when(g == group)(functools.partial(run_group, group, dil))

    @pl.when(g == C_GROUPS - 1)
    def _():
        rows_per = 512

        def combine(c, carry):
            rows = pl.ds(pl.multiple_of(c * rows_per, rows_per), rows_per)
            l0, l1, l2 = lnat_ref[0, rows, :], lnat_ref[1, rows, :], lnat_ref[2, rows, :]
            m = jnp.maximum(jnp.maximum(l0, l1), l2)
            e0, e1, e2 = jnp.exp2(l0 - m), jnp.exp2(l1 - m), jnp.exp2(l2 - m)
            num = e0 * onat_ref[0, rows, :] + e1 * onat_ref[1, rows, :] + e2 * onat_ref[2, rows, :]
            o_ref[0, rows, :] = (num / (e0 + e1 + e2)).astype(o_ref.dtype)
            return carry

        lax.fori_loop(0, seq // rows_per, combine, 0)


def _attn_c(proj_c, bias_c, bsz, seq):
    slab = lambda role: pl.BlockSpec(
        (None, seq, LANES), lambda b, hp, g: (role * 2 * C_GROUPS + g * 2 + hp, b, 0))
    return pl.pallas_call(
        functools.partial(_attn_c_kernel, seq=seq),
        out_shape=jax.ShapeDtypeStruct((bsz, seq, C_W), jnp.bfloat16),
        grid=(bsz, C_HEADS // 2, C_GROUPS),
        in_specs=[slab(0), slab(1), slab(2),
                  pl.BlockSpec((2, 2, 2 * C_TILE, C_TILE), lambda b, hp, g: (g * 2 + hp, 0, 0, 0))],
        out_specs=pl.BlockSpec((1, seq, LANES), lambda b, hp, g: (b, 0, hp)),
        scratch_shapes=[pltpu.VMEM((seq, LANES), jnp.bfloat16),
                        pltpu.VMEM((seq + C_TILE, LANES), jnp.bfloat16),
                        pltpu.VMEM((LANES, seq + C_TILE), jnp.bfloat16),
                        pltpu.VMEM((C_GROUPS, seq, LANES), jnp.float32),
                        pltpu.VMEM((C_GROUPS, seq, LANES), jnp.float32)]
        + [pltpu.VMEM((seq, LANES), jnp.float32)] * 3,
        compiler_params=pltpu.CompilerParams(
            dimension_semantics=("parallel", "parallel", "arbitrary"),
            vmem_limit_bytes=VMEM_LIMIT),
        name="attn_c",
    )(proj_c, proj_c, proj_c, bias_c)


def _mix_mlp_kernel(x_ref, oa_ref, ob_ref, oc_ref, wo_ref, g_ref, w1_ref, w2_ref, fg_ref, out_ref,
                    acc_ref, h_ref, *, final):
    f = pl.program_id(1)

    @pl.when(f == 0)
    def _():
        mixed = jnp.concatenate([oa_ref[...], ob_ref[...], oc_ref[...]], axis=1)
        x1 = x_ref[...] + jnp.dot(mixed, wo_ref[0], preferred_element_type=jnp.float32)
        acc_ref[...] = x1
        h_ref[...] = _rms(x1, g_ref[0]).astype(jnp.bfloat16)

    u = jnp.dot(h_ref[...], w1_ref[0], preferred_element_type=jnp.float32)
    u = jnp.square(jnp.maximum(u, 0.0)).astype(jnp.bfloat16)
    acc_ref[...] += jnp.dot(u, w2_ref[0], preferred_element_type=jnp.float32)

    @pl.when(f == pl.num_programs(1) - 1)
    def _():
        out_ref[...] = _rms(acc_ref[...], fg_ref[...]) if final else acc_ref[...]


def _mix_mlp(x2, oa, ob, oc, w_o, norm_g, w_1, w_2, final_g, layer, final, *, tm=1024, tf=1024):
    n_tok = x2.shape[0]
    row = lambda w: pl.BlockSpec((tm, w), lambda i, f: (i, 0))
    return pl.pallas_call(
        functools.partial(_mix_mlp_kernel, final=final),
        out_shape=jax.ShapeDtypeStruct((n_tok, D_MODEL), jnp.float32),
        grid=(n_tok // tm, D_FF // tf),
        in_specs=[row(D_MODEL), row(A_W), row(B_W), row(C_W),
                  pl.BlockSpec((1, D_MODEL, D_MODEL), lambda i, f: (layer, 0, 0)),
                  pl.BlockSpec((1, 1, D_MODEL), lambda i, f: (layer, 0, 0)),
                  pl.BlockSpec((1, D_MODEL, tf), lambda i, f: (layer, 0, f)),
                  pl.BlockSpec((1, tf, D_MODEL), lambda i, f: (layer, f, 0)),
                  pl.BlockSpec((1, D_MODEL), lambda i, f: (0, 0))],
        out_specs=row(D_MODEL),
        scratch_shapes=[pltpu.VMEM((tm, D_MODEL), jnp.float32),
                        pltpu.VMEM((tm, D_MODEL), jnp.bfloat16)],
        compiler_params=pltpu.CompilerParams(dimension_semantics=("parallel", "arbitrary"),
                                             vmem_limit_bytes=VMEM_LIMIT),
        name="mix_mlp",
    )(x2, oa, ob, oc, w_o, norm_g, w_1, w_2, final_g)


def _prep_w_in(w_in):
    a0, b0, f0, c0 = 0, 3 * A_W, 3 * A_W + 3 * B_W, 3 * A_W + 3 * B_W + B_HEADS
    sa, sb = A_QK_DIM ** -0.5 * LOG2E, HEAD_DIM ** -0.5 * LOG2E
    cw = C_GROUPS * C_W
    w_tm = jnp.concatenate([w_in[:, :, b0 + B_W:b0 + 2 * B_W],
                            w_in[:, :, a0 + A_W:a0 + 2 * A_W]], axis=-1).astype(jnp.bfloat16)
    w_fm = jnp.concatenate([
        w_in[:, :, b0:b0 + B_W] * sb, w_in[:, :, b0 + 2 * B_W:f0],
        w_in[:, :, a0:a0 + A_W] * sa, w_in[:, :, a0 + 2 * A_W:b0],
    ], axis=-1)
    w_fm = jnp.transpose(w_fm, (0, 2, 1)).astype(jnp.bfloat16)
    w_f = jnp.transpose(w_in[:, :, f0:c0], (0, 2, 1))
    w_f = jnp.pad(w_f, ((0, 0), (0, F_ROWS - B_HEADS), (0, 0))).astype(jnp.bfloat16)
    w_c = jnp.concatenate([w_in[:, :, c0:c0 + cw] * sb, w_in[:, :, c0 + cw:]],
                          axis=-1).astype(jnp.bfloat16)
    return w_tm, w_fm, w_f, w_c


def kernel(x, norm1_g, w_in, b_f, lam_q1, lam_k1, lam_q2, lam_k2, diff_norm_g, w_o, norm2_g,
           w_1, w_2, rel_bias, final_g):
    bsz, seq, _ = x.shape
    depth = w_in.shape[0]
    w_tm, w_fm, w_f, w_c = _prep_w_in(w_in)
    w_o16, w_116, w_216 = (w.astype(jnp.bfloat16) for w in (w_o, w_1, w_2))
    lam_vecs = jnp.stack([lam_q1, lam_k1, lam_q2, lam_k2], axis=1)
    bias_a, bias_c = _make_bias_tiles(rel_bias)

    x2 = x.reshape(bsz * seq, D_MODEL)
    for l in range(depth):
        proj_tm, proj_fm, ft, proj_c = _in_proj(x2, norm1_g[:, None, :], w_tm, w_fm, w_f, w_c, l)
        proj3 = proj_tm.reshape(bsz, seq, N_TM)
        ckp = _decay(ft, b_f[:, :, None], l, bsz, seq)
        lam_init = jnp.full((1,), 0.8 - 0.6 * math.exp(-0.3 * l), jnp.float32)
        oa = _attn_a(proj3, proj_fm, bias_a, lam_vecs, lam_init, diff_norm_g[:, :, None], l)
        ob = _attn_b(proj3, proj_fm, ckp)
        oc = _attn_c(proj_c, bias_c, bsz, seq)
        x2 = _mix_mlp(x2, oa.reshape(bsz * seq, A_W), ob.reshape(bsz * seq, B_W),
                      oc.reshape(bsz * seq, C_W), w_o16, norm2_g[:, None, :], w_116, w_216,
                      final_g[None, :], l, l == depth - 1)
    return x2.reshape(bsz, seq, D_MODEL)
```

```python
import functools
import math

import jax
import jax.numpy as jnp
from jax import lax
from jax.experimental import pallas as pl
from jax.experimental.pallas import tpu as pltpu

D_MODEL = 1024
HEAD_DIM = 64
A_HEADS = 4
A_QK_DIM = 32
B_HEADS = 8
C_HEADS = 4
C_PAIRS = ((128, 1), (512, 4), (2048, 16))
C_GROUPS = 3
A_W = 256
B_W = 512
C_W = 256
D_FF = 4096
N_BUCKETS = 32
MAX_DISTANCE = 2048
NORM_EPS = 1e-6
NEG_INF = -1e30
LOG2E = 1.4426950408889634

LANES = 128
VMEM_LIMIT = 56 * 1024 * 1024

KB, KA = 0, 512
N_TM = 768
QB_T, VB_T, QA_T, VA_T = 0, 512, 1024, 1280
N_FM = 1536
N_C = 2304
N_C_SLABS = N_C // 128
F_ROWS = 16
ONES_ROWS = 16
ACC_ROWS = HEAD_DIM + ONES_ROWS

A_STEP_HEADS = 4
B_STEP_HEADS = 8
C_TILE = 128
C_UNROLL = 32
C_AHEAD = 4
C_SPLIT = 4
FLASH_TILE = 256
MLP_TM = 512
MLP_TF = 1024
A_GROUP = 2
B_GROUP = 1
FLASH_PAIRS = 2


def _bucket_thresholds():
    max_exact = N_BUCKETS // 2
    out = []
    for k in range(1, N_BUCKETS - max_exact):
        t = max_exact * (MAX_DISTANCE / max_exact) ** (k / (N_BUCKETS - max_exact))
        out.append(int(math.ceil(t)))
    return tuple(out)


_THRESHOLDS = _bucket_thresholds()


def _n_near_tiles(tile):
    return -(-(_THRESHOLDS[-1] + tile - 1) // tile)


def _t5_bucket(d):
    big = jnp.full(d.shape, N_BUCKETS // 2, jnp.int32)
    for t in _THRESHOLDS:
        big = big + (d >= t).astype(jnp.int32)
    return jnp.where(d < N_BUCKETS // 2, d, big)


def _bias_lookup(rb_ref, bucket, col):
    val = jnp.zeros(bucket.shape, jnp.float32)
    for b in range(N_BUCKETS):
        val = jnp.where(bucket == b, rb_ref[b, col], val)
    return val


def _bias_a_kernel(rb_ref, out_ref, *, tile):
    delta = pl.program_id(0)
    j = lax.broadcasted_iota(jnp.int32, (tile, tile), 0)
    i = lax.broadcasted_iota(jnp.int32, (tile, tile), 1)
    d = delta * tile + i - j
    bucket = _t5_bucket(d)
    for h in range(A_HEADS):
        val = _bias_lookup(rb_ref, bucket, h) * LOG2E
        out_ref[h, 0] = jnp.where(d >= 0, val, NEG_INF)


def _bias_c_kernel(rb_ref, out_ref):
    g = pl.program_id(0)
    dil = jnp.where(g == 0, C_PAIRS[0][1], jnp.where(g == 1, C_PAIRS[1][1], C_PAIRS[2][1]))
    c = lax.broadcasted_iota(jnp.int32, (2 * C_TILE, C_TILE), 0)
    i = lax.broadcasted_iota(jnp.int32, (2 * C_TILE, C_TILE), 1)
    steps = i + C_TILE - c
    valid = (steps >= 0) & (steps <= C_TILE)
    bucket = _t5_bucket(steps * dil)
    for h in range(C_HEADS):
        val = _bias_lookup(rb_ref, bucket, A_HEADS + g * C_HEADS + h) * LOG2E
        out_ref[h, 1] = jnp.where(valid, val, NEG_INF)
        out_ref[h, 0] = jnp.where(valid & (c >= C_TILE), val, NEG_INF)


def _make_bias_tiles(rel_bias):
    tile = FLASH_TILE
    n_bias = _n_near_tiles(tile) + 1
    smem = pl.BlockSpec(memory_space=pltpu.SMEM)
    bias_a = pl.pallas_call(
        functools.partial(_bias_a_kernel, tile=tile),
        out_shape=jax.ShapeDtypeStruct((A_HEADS, n_bias, tile, tile), jnp.float32),
        grid=(n_bias,),
        in_specs=[smem],
        out_specs=pl.BlockSpec((A_HEADS, 1, tile, tile), lambda d: (0, d, 0, 0)),
        name="bias_a_tiles",
    )(rel_bias)
    bias_c = pl.pallas_call(
        _bias_c_kernel,
        out_shape=jax.ShapeDtypeStruct((C_GROUPS * C_HEADS, 2, 2 * C_TILE, C_TILE), jnp.float32),
        grid=(C_GROUPS,),
        in_specs=[smem],
        out_specs=pl.BlockSpec((C_HEADS, 2, 2 * C_TILE, C_TILE), lambda g: (g, 0, 0, 0)),
        name="bias_c_tiles",
    )(rel_bias)
    return bias_a, bias_c


def _rms(x, g):
    return x * lax.rsqrt(jnp.mean(x * x, axis=-1, keepdims=True) + NORM_EPS) * g


def _dot_nt(a, b):
    return lax.dot_general(a, b, (((1,), (1,)), ((), ())), preferred_element_type=jnp.float32)


def _in_proj_kernel(x_ref, g_ref, wtm_ref, wfm_ref, wf_ref, wc_ref, tm_ref, fm_ref, ft_ref, c_ref):
    h = _rms(x_ref[...], g_ref[0]).astype(jnp.bfloat16)
    tm_ref[...] = jnp.dot(h, wtm_ref[0], preferred_element_type=jnp.float32).astype(jnp.bfloat16)
    for c in range(N_FM // 512):
        rows = slice(c * 512, (c + 1) * 512)
        fm_ref[rows, :] = _dot_nt(wfm_ref[0, rows, :], h).astype(jnp.bfloat16)
    ft_ref[...] = _dot_nt(wf_ref[0], h)
    for c in range(N_C // 256):
        pc = jnp.dot(h, wc_ref[0, :, c * 256:(c + 1) * 256], preferred_element_type=jnp.float32)
        c_ref[2 * c] = pc[:, :LANES]
        c_ref[2 * c + 1] = pc[:, LANES:]


def _in_proj(x2, norm_g, w_tm, w_fm, w_f, w_c, layer, *, tm=512):
    n_tok = x2.shape[0]
    return pl.pallas_call(
        _in_proj_kernel,
        out_shape=(jax.ShapeDtypeStruct((n_tok, N_TM), jnp.bfloat16),
                   jax.ShapeDtypeStruct((N_FM, n_tok), jnp.bfloat16),
                   jax.ShapeDtypeStruct((F_ROWS, n_tok), jnp.float32),
                   jax.ShapeDtypeStruct((N_C_SLABS, n_tok, LANES), jnp.float32)),
        grid=(n_tok // tm,),
        in_specs=[
            pl.BlockSpec((tm, D_MODEL), lambda i: (i, 0)),
            pl.BlockSpec((1, 1, D_MODEL), lambda i: (layer, 0, 0)),
            pl.BlockSpec((1, D_MODEL, N_TM), lambda i: (layer, 0, 0)),
            pl.BlockSpec((1, N_FM, D_MODEL), lambda i: (layer, 0, 0)),
            pl.BlockSpec((1, F_ROWS, D_MODEL), lambda i: (layer, 0, 0)),
            pl.BlockSpec((1, D_MODEL, N_C), lambda i: (layer, 0, 0)),
        ],
        out_specs=(pl.BlockSpec((tm, N_TM), lambda i: (i, 0)),
                   pl.BlockSpec((N_FM, tm), lambda i: (0, i)),
                   pl.BlockSpec((F_ROWS, tm), lambda i: (0, i)),
                   pl.BlockSpec((N_C_SLABS, tm, LANES), lambda i: (0, i, 0))),
        compiler_params=pltpu.CompilerParams(dimension_semantics=("parallel",),
                                             vmem_limit_bytes=VMEM_LIMIT),
        name="in_proj",
    )(x2, norm_g, w_tm, w_fm, w_f, w_c)


def _decay_kernel(ft_ref, bf_ref, ckp_ref):
    z = ft_ref[0:B_HEADS, :] + bf_ref[0]
    x = jnp.minimum(z, 0.0) - jnp.log1p(jnp.exp(-jnp.abs(z)))
    seq = x.shape[1]
    pos = lax.broadcasted_iota(jnp.int32, x.shape, 1)
    shift = 1
    while shift < seq:
        x = x + jnp.where(pos >= shift, pltpu.roll(x, shift, 1), 0.0)
        shift *= 2
    c = x * (-LOG2E)
    hi = c.astype(jnp.bfloat16).astype(jnp.float32)
    mid = (c - hi).astype(jnp.bfloat16).astype(jnp.float32)
    lo = c - hi - mid
    stacked = jnp.concatenate(
        [hi, mid, lo, jnp.zeros((LANES - 3 * B_HEADS, seq), jnp.float32)], axis=0)
    ckp_ref[0] = jnp.transpose(stacked).astype(jnp.bfloat16)


def _decay(ft, b_f, layer, bsz, seq):
    return pl.pallas_call(
        _decay_kernel,
        out_shape=jax.ShapeDtypeStruct((bsz, seq, LANES), jnp.bfloat16),
        grid=(bsz,),
        in_specs=[pl.BlockSpec((F_ROWS, seq), lambda b: (0, b)),
                  pl.BlockSpec((1, B_HEADS, 1), lambda b: (layer, 0, 0))],
        out_specs=pl.BlockSpec((1, seq, LANES), lambda b: (b, 0, 0)),
        compiler_params=pltpu.CompilerParams(dimension_semantics=("parallel",)),
        name="decay_scan",
    )(ft, b_f)


def _colmax(s):
    rows = s.shape[0]
    while rows > 8:
        rows //= 2
        s = jnp.maximum(s[:rows], s[rows:2 * rows])
    return jnp.max(s, axis=0, keepdims=True)


def _online_step(s, smax, v_aug, m_ref, acc_ref, idx):
    m_old = m_ref[idx]
    m_new = jnp.maximum(m_old, smax)
    alpha = jnp.exp2(m_old - m_new)
    p = jnp.exp2(s - m_new).astype(jnp.bfloat16)
    acc_ref[idx] = alpha * acc_ref[idx] + jnp.dot(v_aug, p, preferred_element_type=jnp.float32)
    m_ref[idx] = m_new


def _v_aug(vt_ref, head, k0, tile):
    v = vt_ref[head * HEAD_DIM:(head + 1) * HEAD_DIM, pl.ds(k0, tile)]
    return jnp.concatenate([v, jnp.ones((ONES_ROWS, tile), jnp.bfloat16)], axis=0)


def _flash_pipeline(qi, n_sub, group, produce, consume, finish):
    every = range(n_sub)

    def overlapped(pk, pslot, ck, cslot):
        for g0 in range(0, n_sub, group):
            subs = range(g0, g0 + group)
            produce(pk, pslot, subs)
            consume(ck, cslot, False, subs)

    produce(0, 0, every)

    def pair(k):
        overlapped(k + 1, 1, k, 0)
        overlapped(k + 2, 0, k + 1, 1)

    def pairs(j, carry):
        for u in range(FLASH_PAIRS):
            pair(2 * (FLASH_PAIRS * j + u))
        return carry

    n_pairs = qi // 2
    lax.fori_loop(0, n_pairs // FLASH_PAIRS, pairs, 0)
    for u in range(FLASH_PAIRS - 1):
        done = (n_pairs // FLASH_PAIRS) * FLASH_PAIRS + u
        pl.when(done < n_pairs)(functools.partial(pair, 2 * done))

    @pl.when(qi % 2 == 0)
    def _():
        consume(qi, 0, True, every)
        finish()

    @pl.when(qi % 2 == 1)
    def _():
        overlapped(qi, 1, qi - 1, 0)
        consume(qi, 1, True, every)
        finish()


def _attn_a_kernel(qt_ref, k_ref, vt_ref, bias_ref, lamv_ref, lami_ref, subg_ref, o_ref,
                   m_ref, acc_ref, s_ref, smax_ref, *, tile, n_near):
    qi = pl.program_id(2)
    n_sub = 2 * A_STEP_HEADS
    qt = qt_ref[...].astype(jnp.float32)
    row = lax.broadcasted_iota(jnp.int32, (LANES, tile), 0)
    qs = []
    for s in range(n_sub):
        blk = qt[(s // 4) * LANES:(s // 4 + 1) * LANES]
        lo = (s % 4) * A_QK_DIM
        qs.append(jnp.where((row >= lo) & (row < lo + A_QK_DIM), blk, 0.0).astype(jnp.bfloat16))
    m_ref[...] = jnp.full(m_ref.shape, NEG_INF, jnp.float32)
    acc_ref[...] = jnp.zeros(acc_ref.shape, jnp.float32)

    def produce(ki, slot, subs):
        k0 = pl.multiple_of(ki * tile, tile)
        delta = jnp.minimum(qi - ki, n_near)
        for s in subs:
            k = k_ref[0, pl.ds(k0, tile), (s // 4) * LANES:(s // 4 + 1) * LANES]
            x = jnp.dot(k, qs[s], preferred_element_type=jnp.float32) + bias_ref[s // 2, delta]
            s_ref[slot, s] = x
            smax_ref[slot, s] = _colmax(x)

    def consume(ki, slot, last, subs):
        k0 = pl.multiple_of(ki * tile, tile)
        for s in subs:
            _online_step(s_ref[slot, s], smax_ref[slot, s], _v_aug(vt_ref, s // 2, k0, tile),
                         m_ref, acc_ref, s)

    def finish():
        lam_init = lami_ref[0]
        lv = lamv_ref[0]
        e1 = jnp.exp(jnp.sum(lv[0:1] * lv[1:2], axis=1, keepdims=True))
        e2 = jnp.exp(jnp.sum(lv[2:3] * lv[3:4], axis=1, keepdims=True))
        lam = e1 - e2 + lam_init
        heads = []
        for h in range(A_STEP_HEADS):
            a0, a1 = acc_ref[2 * h], acc_ref[2 * h + 1]
            o = (a0[0:HEAD_DIM] / a0[HEAD_DIM:HEAD_DIM + 1]
                 - lam * (a1[0:HEAD_DIM] / a1[HEAD_DIM:HEAD_DIM + 1]))
            ms = jnp.mean(o * o, axis=0, keepdims=True)
            heads.append(o * lax.rsqrt(ms + NORM_EPS) * subg_ref[0] * (1.0 - lam_init))
        o_ref[0] = jnp.transpose(jnp.concatenate(heads, axis=0)).astype(o_ref.dtype)

    _flash_pipeline(qi, n_sub, A_GROUP, produce, consume, finish)


def _attn_a(proj_tm3, proj_fm, bias_a, lam_vecs, lam_init, sub_g, layer):
    bsz, seq, _ = proj_tm3.shape
    tile = FLASH_TILE
    n_near = _n_near_tiles(tile)
    n_q = seq // tile
    n_h = A_STEP_HEADS
    width = n_h * HEAD_DIM
    return pl.pallas_call(
        functools.partial(_attn_a_kernel, tile=tile, n_near=n_near),
        out_shape=jax.ShapeDtypeStruct((bsz, seq, A_W), jnp.bfloat16),
        grid=(A_HEADS // n_h, bsz, n_q),
        in_specs=[
            pl.BlockSpec((width, tile), lambda hg, b, qi: (QA_T // width + hg, b * n_q + qi)),
            pl.BlockSpec((1, seq, width), lambda hg, b, qi: (b, 0, KA // width + hg)),
            pl.BlockSpec((width, seq), lambda hg, b, qi: (VA_T // width + hg, b)),
            pl.BlockSpec((n_h, n_near + 1, tile, tile), lambda hg, b, qi: (hg, 0, 0, 0)),
            pl.BlockSpec((1, 4, A_QK_DIM), lambda hg, b, qi: (layer, 0, 0)),
            pl.BlockSpec(memory_space=pltpu.SMEM),
            pl.BlockSpec((1, HEAD_DIM, 1), lambda hg, b, qi: (layer, 0, 0)),
        ],
        out_specs=pl.BlockSpec((1, tile, width), lambda hg, b, qi: (b, qi, hg)),
        scratch_shapes=[pltpu.VMEM((2 * n_h, 1, tile), jnp.float32),
                        pltpu.VMEM((2 * n_h, ACC_ROWS, tile), jnp.float32),
                        pltpu.VMEM((2, 2 * n_h, tile, tile), jnp.float32),
                        pltpu.VMEM((2, 2 * n_h, 1, tile), jnp.float32)],
        compiler_params=pltpu.CompilerParams(
            dimension_semantics=("parallel", "parallel", "parallel"),
            vmem_limit_bytes=VMEM_LIMIT),
        name="attn_a",
    )(proj_fm, proj_tm3, proj_fm, bias_a, lam_vecs, lam_init, sub_g)


def _attn_b_kernel(qt_ref, k_ref, ckp_ref, vt_ref, o_ref, m_ref, acc_ref, s_ref, smax_ref, *,
                   tile):
    hg = pl.program_id(0)
    qi = pl.program_id(2)
    n_h = B_STEP_HEADS
    qt = qt_ref[...].astype(jnp.float32)
    row = lax.broadcasted_iota(jnp.int32, (LANES, tile), 0)
    q_aug = []
    for s in range(n_h):
        blk = qt[(s // 2) * LANES:(s // 2 + 1) * LANES]
        lo = (s % 2) * HEAD_DIM
        q_s = jnp.where((row >= lo) & (row < lo + HEAD_DIM), blk, 0.0)
        pick = (row < 3 * B_HEADS) & ((row % B_HEADS) == n_h * hg + s)
        ones = jnp.where(pick, 1.0, 0.0)
        q_aug.append(jnp.concatenate([q_s, ones], axis=0).astype(jnp.bfloat16))
    m_ref[...] = jnp.full(m_ref.shape, NEG_INF, jnp.float32)
    acc_ref[...] = jnp.zeros(acc_ref.shape, jnp.float32)

    def produce(ki, slot, subs):
        k0 = pl.multiple_of(ki * tile, tile)
        ck = ckp_ref[0, pl.ds(k0, tile), :]
        for s in subs:
            k = k_ref[0, pl.ds(k0, tile), (s // 2) * LANES:(s // 2 + 1) * LANES]
            x = jnp.dot(jnp.concatenate([k, ck], axis=1), q_aug[s],
                        preferred_element_type=jnp.float32)
            s_ref[slot, s] = x
            smax_ref[slot, s] = _colmax(x)

    def consume(ki, slot, last, subs):
        k0 = pl.multiple_of(ki * tile, tile)
        if last:
            kk = lax.broadcasted_iota(jnp.int32, (tile, tile), 0)
            qq = lax.broadcasted_iota(jnp.int32, (tile, tile), 1)
            keep = kk <= qq
        for s in subs:
            x, xmax = s_ref[slot, s], smax_ref[slot, s]
            if last:
                x = jnp.where(keep, x, NEG_INF)
                xmax = _colmax(x)
            _online_step(x, xmax, _v_aug(vt_ref, s, k0, tile), m_ref, acc_ref, s)

    def finish():
        heads = [acc_ref[s][0:HEAD_DIM] / acc_ref[s][HEAD_DIM:HEAD_DIM + 1] for s in range(n_h)]
        o_ref[0] = jnp.transpose(jnp.concatenate(heads, axis=0)).astype(o_ref.dtype)

    _flash_pipeline(qi, n_h, B_GROUP, produce, consume, finish)


def _attn_b(proj_tm3, proj_fm, ckp):
    bsz, seq, _ = proj_tm3.shape
    tile = FLASH_TILE
    n_q = seq // tile
    n_h = B_STEP_HEADS
    width = n_h * HEAD_DIM
    return pl.pallas_call(
        functools.partial(_attn_b_kernel, tile=tile),
        out_shape=jax.ShapeDtypeStruct((bsz, seq, B_W), jnp.bfloat16),
        grid=(B_HEADS // n_h, bsz, n_q),
        in_specs=[
            pl.BlockSpec((width, tile), lambda hg, b, qi: (QB_T // width + hg, b * n_q + qi)),
            pl.BlockSpec((1, seq, width), lambda hg, b, qi: (b, 0, KB // width + hg)),
            pl.BlockSpec((1, seq, LANES), lambda hg, b, qi: (b, 0, 0)),
            pl.BlockSpec((width, seq), lambda hg, b, qi: (VB_T // width + hg, b)),
        ],
        out_specs=pl.BlockSpec((1, tile, width), lambda hg, b, qi: (b, qi, hg)),
        scratch_shapes=[pltpu.VMEM((n_h, 1, tile), jnp.float32),
                        pltpu.VMEM((n_h, ACC_ROWS, tile), jnp.float32),
                        pltpu.VMEM((2, n_h, tile, tile), jnp.float32),
                        pltpu.VMEM((2, n_h, 1, tile), jnp.float32)],
        compiler_params=pltpu.CompilerParams(
            dimension_semantics=("parallel", "parallel", "parallel"),
            vmem_limit_bytes=VMEM_LIMIT),
        name="attn_b",
    )(proj_fm, proj_tm3, ckp, proj_fm)


def _attn_c_kernel(q_ref, k_ref, v_ref, bias_ref, o_ref, qd_ref, kd_ref, vtd_ref, onat_ref,
                   lnat_ref, tq_ref, tk_ref, tv_ref, *, seq):
    g = pl.program_id(2)
    n_chunks = seq // C_TILE
    lane = lax.broadcasted_iota(jnp.int32, (1, LANES), 1)
    first = lane < HEAD_DIM

    def run_group(group, dil):
        n_tiles = seq // dil // C_TILE

        def out_rows(i):
            r, t = i // n_tiles, i % n_tiles
            return pl.ds(r + t * (C_TILE * dil), C_TILE, stride=dil)

        if dil > C_SPLIT:
            per = seq // C_SPLIT // C_TILE

            def presplit(j, carry):
                for u in range(C_UNROLL):
                    i = j * C_UNROLL + u
                    rows = pl.ds(i // per + (i % per) * (C_TILE * C_SPLIT), C_TILE, stride=C_SPLIT)
                    dst = pl.ds(pl.multiple_of(i * C_TILE, C_TILE), C_TILE)
                    for src, tmp in ((q_ref, tq_ref), (k_ref, tk_ref), (v_ref, tv_ref)):
                        tmp[dst, :] = src[rows, :]
                return carry

            lax.fori_loop(0, n_chunks // C_UNROLL, presplit, 0)
            srcs, step = (tq_ref, tk_ref, tv_ref), dil // C_SPLIT

            def src_rows(i):
                r, t = i // n_tiles, i % n_tiles
                base = (r % C_SPLIT) * (seq // C_SPLIT) + r // C_SPLIT + t * (C_TILE * step)
                return pl.ds(base, C_TILE, stride=step)
        else:
            srcs, src_rows = (q_ref, k_ref, v_ref), out_rows

        kd_ref[0:C_TILE, :] = jnp.zeros((C_TILE, LANES), jnp.bfloat16)
        vtd_ref[:, 0:C_TILE] = jnp.zeros((LANES, C_TILE), jnp.bfloat16)

        def gather(j, carry):
            ids = [j * C_UNROLL + u for u in range(C_UNROLL)]
            vts = [jnp.transpose(srcs[2][src_rows(i), :]) for i in ids]
            for i, vt in zip(ids, vts):
                rows = src_rows(i)
                dst = pl.multiple_of(i * C_TILE, C_TILE)
                qd_ref[pl.ds(dst, C_TILE), :] = srcs[0][rows, :].astype(jnp.bfloat16)
                kd_ref[pl.ds(dst + C_TILE, C_TILE), :] = srcs[1][rows, :].astype(jnp.bfloat16)
                vtd_ref[:, pl.ds(dst + C_TILE, C_TILE)] = vt.astype(jnp.bfloat16)
            return carry

        lax.fori_loop(0, n_chunks // C_UNROLL, gather, 0)

        def tiles(j, carry):
            ids = [j * C_UNROLL + u for u in range(C_UNROLL)]
            row0s = [pl.multiple_of(i * C_TILE, C_TILE) for i in ids]

            def scores(u):
                q = qd_ref[pl.ds(row0s[u], C_TILE), :]
                q2 = jnp.concatenate([jnp.where(first, q, jnp.zeros_like(q)),
                                      jnp.where(first, jnp.zeros_like(q), q)], axis=0)
                return _dot_nt(kd_ref[pl.ds(row0s[u], 2 * C_TILE), :], q2)

            def softmax_pv(u, st):
                ver = jnp.minimum(ids[u] % n_tiles, 1)
                res = []
                for h in range(2):
                    x = st[:, h * C_TILE:(h + 1) * C_TILE] + bias_ref[h, ver]
                    m = _colmax(x)
                    p = jnp.exp2(x - m).astype(jnp.bfloat16)
                    v_aug = jnp.concatenate(
                        [vtd_ref[h * HEAD_DIM:(h + 1) * HEAD_DIM, pl.ds(row0s[u], 2 * C_TILE)],
                         jnp.ones((ONES_ROWS, 2 * C_TILE), jnp.bfloat16)], axis=0)
                    res.append((m, jnp.dot(v_aug, p, preferred_element_type=jnp.float32)))
                return res

            def write_back(u, res):
                outs, lses = [], []
                for m, pv in res:
                    l = pv[HEAD_DIM:HEAD_DIM + 1]
                    outs.append(pv[0:HEAD_DIM] / l)
                    lses.append(jnp.broadcast_to(m + jnp.log2(l), (HEAD_DIM, C_TILE)))
                rows = out_rows(ids[u])
                onat_ref[group, rows, :] = jnp.transpose(jnp.concatenate(outs, axis=0))
                lnat_ref[group, rows, :] = jnp.transpose(jnp.concatenate(lses, axis=0))

            sts = {u: scores(u) for u in range(C_AHEAD)}
            pending = None
            for u in range(C_UNROLL):
                if u + C_AHEAD < C_UNROLL:
                    sts[u + C_AHEAD] = scores(u + C_AHEAD)
                res = softmax_pv(u, sts.pop(u))
                if pending is not None:
                    write_back(*pending)
                pending = (u, res)
            write_back(*pending)
            return carry

        lax.fori_loop(0, n_chunks // C_UNROLL, tiles, 0)

    for group, (_, dil) in enumerate(C_PAIRS):
        pl.when(g == group)(functools.partial(run_group, group, dil))

    @pl.when(g == C_GROUPS - 1)
    def _():
        rows_per = 512

        def combine(c, carry):
            rows = pl.ds(pl.multiple_of(c * rows_per, rows_per), rows_per)
            l0, l1, l2 = lnat_ref[0, rows, :], lnat_ref[1, rows, :], lnat_ref[2, rows, :]
            m = jnp.maximum(jnp.maximum(l0, l1), l2)
            e0, e1, e2 = jnp.exp2(l0 - m), jnp.exp2(l1 - m), jnp.exp2(l2 - m)
            num = e0 * onat_ref[0, rows, :] + e1 * onat_ref[1, rows, :] + e2 * onat_ref[2, rows, :]
            o_ref[0, rows, :] = (num / (e0 + e1 + e2)).astype(o_ref.dtype)
            return carry

        lax.fori_loop(0, seq // rows_per, combine, 0)


def _attn_c(proj_c, bias_c, bsz, seq):
    assert seq % (C_TILE * C_UNROLL) == 0 and seq % (C_TILE * max(d for _, d in C_PAIRS)) == 0
    slab = lambda role: pl.BlockSpec(
        (None, seq, LANES), lambda b, hp, g: (role * 2 * C_GROUPS + g * 2 + hp, b, 0))
    return pl.pallas_call(
        functools.partial(_attn_c_kernel, seq=seq),
        out_shape=jax.ShapeDtypeStruct((bsz, seq, C_W), jnp.bfloat16),
        grid=(bsz, C_HEADS // 2, C_GROUPS),
        in_specs=[slab(0), slab(1), slab(2),
                  pl.BlockSpec((2, 2, 2 * C_TILE, C_TILE), lambda b, hp, g: (g * 2 + hp, 0, 0, 0))],
        out_specs=pl.BlockSpec((1, seq, LANES), lambda b, hp, g: (b, 0, hp)),
        scratch_shapes=[pltpu.VMEM((seq, LANES), jnp.bfloat16),
                        pltpu.VMEM((seq + C_TILE, LANES), jnp.bfloat16),
                        pltpu.VMEM((LANES, seq + C_TILE), jnp.bfloat16),
                        pltpu.VMEM((C_GROUPS, seq, LANES), jnp.float32),
                        pltpu.VMEM((C_GROUPS, seq, LANES), jnp.float32)]
        + [pltpu.VMEM((seq, LANES), jnp.float32)] * 3,
        compiler_params=pltpu.CompilerParams(
            dimension_semantics=("parallel", "parallel", "arbitrary"),
            vmem_limit_bytes=VMEM_LIMIT),
        name="attn_c",
    )(proj_c, proj_c, proj_c, bias_c)


def _mix_mlp_kernel(x_ref, oa_ref, ob_ref, oc_ref, wo_ref, g_ref, w1_ref, w2_ref, fg_ref, out_ref, *,
                    final, tf):
    mixed = jnp.concatenate([oa_ref[...], ob_ref[...], oc_ref[...]], axis=1)
    acc = x_ref[...] + jnp.dot(mixed, wo_ref[0], preferred_element_type=jnp.float32)
    h = _rms(acc, g_ref[0]).astype(jnp.bfloat16)
    for c in range(D_FF // tf):
        u = jnp.dot(h, w1_ref[0, :, c * tf:(c + 1) * tf], preferred_element_type=jnp.float32)
        u = jnp.square(jnp.maximum(u, 0.0)).astype(jnp.bfloat16)
        acc = acc + jnp.dot(u, w2_ref[0, c * tf:(c + 1) * tf, :],
                            preferred_element_type=jnp.float32)
    out_ref[...] = _rms(acc, fg_ref[...]) if final else acc


def _mix_mlp(x2, oa, ob, oc, w_o, norm_g, w_1, w_2, final_g, layer, final, *, tm=MLP_TM, tf=MLP_TF):
    n_tok = x2.shape[0]
    row = lambda w: pl.BlockSpec((tm, w), lambda i: (i, 0))
    resident = lambda shape: pl.BlockSpec(shape, lambda i: (layer, 0, 0),
                                          pipeline_mode=pl.Buffered(1))
    return pl.pallas_call(
        functools.partial(_mix_mlp_kernel, final=final, tf=tf),
        out_shape=jax.ShapeDtypeStruct((n_tok, D_MODEL), jnp.float32),
        grid=(n_tok // tm,),
        in_specs=[row(D_MODEL), row(A_W), row(B_W), row(C_W),
                  resident((1, D_MODEL, D_MODEL)),
                  pl.BlockSpec((1, 1, D_MODEL), lambda i: (layer, 0, 0)),
                  resident((1, D_MODEL, D_FF)),
                  resident((1, D_FF, D_MODEL)),
                  pl.BlockSpec((1, D_MODEL), lambda i: (0, 0))],
        out_specs=row(D_MODEL),
        compiler_params=pltpu.CompilerParams(dimension_semantics=("parallel",),
                                             vmem_limit_bytes=VMEM_LIMIT),
        name="mix_mlp",
    )(x2, oa, ob, oc, w_o, norm_g, w_1, w_2, final_g)


def _prep_w_in(w_in):
    a0, b0, f0, c0 = 0, 3 * A_W, 3 * A_W + 3 * B_W, 3 * A_W + 3 * B_W + B_HEADS
    sa, sb = A_QK_DIM ** -0.5 * LOG2E, HEAD_DIM ** -0.5 * LOG2E
    cw = C_GROUPS * C_W
    w_tm = jnp.concatenate([w_in[:, :, b0 + B_W:b0 + 2 * B_W],
                            w_in[:, :, a0 + A_W:a0 + 2 * A_W]], axis=-1).astype(jnp.bfloat16)
    w_fm = jnp.concatenate([
        w_in[:, :, b0:b0 + B_W] * sb, w_in[:, :, b0 + 2 * B_W:f0],
        w_in[:, :, a0:a0 + A_W] * sa, w_in[:, :, a0 + 2 * A_W:b0],
    ], axis=-1)
    w_fm = jnp.transpose(w_fm, (0, 2, 1)).astype(jnp.bfloat16)
    w_f = jnp.transpose(w_in[:, :, f0:c0], (0, 2, 1))
    w_f = jnp.pad(w_f, ((0, 0), (0, F_ROWS - B_HEADS), (0, 0))).astype(jnp.bfloat16)
    w_c = jnp.concatenate([w_in[:, :, c0:c0 + cw] * sb, w_in[:, :, c0 + cw:]],
                          axis=-1).astype(jnp.bfloat16)
    return w_tm, w_fm, w_f, w_c


def kernel(x, norm1_g, w_in, b_f, lam_q1, lam_k1, lam_q2, lam_k2, diff_norm_g, w_o, norm2_g,
           w_1, w_2, rel_bias, final_g):
    bsz, seq, _ = x.shape
    depth = w_in.shape[0]
    w_tm, w_fm, w_f, w_c = _prep_w_in(w_in)
    w_o16, w_116, w_216 = (w.astype(jnp.bfloat16) for w in (w_o, w_1, w_2))
    lam_vecs = jnp.stack([lam_q1, lam_k1, lam_q2, lam_k2], axis=1)
    bias_a, bias_c = _make_bias_tiles(rel_bias)

    x2 = x.reshape(bsz * seq, D_MODEL)
    for l in range(depth):
        proj_tm, proj_fm, ft, proj_c = _in_proj(x2, norm1_g[:, None, :], w_tm, w_fm, w_f, w_c, l)
        proj3 = proj_tm.reshape(bsz, seq, N_TM)
        ckp = _decay(ft, b_f[:, :, None], l, bsz, seq)
        lam_init = jnp.full((1,), 0.8 - 0.6 * math.exp(-0.3 * l), jnp.float32)
        oa = _attn_a(proj3, proj_fm, bias_a, lam_vecs, lam_init, diff_norm_g[:, :, None], l)
        ob = _attn_b(proj3, proj_fm, ckp)
        oc = _attn_c(proj_c, bias_c, bsz, seq)
        x2 = _mix_mlp(x2, oa.reshape(bsz * seq, A_W), ob.reshape(bsz * seq, B_W),
                      oc.reshape(bsz * seq, C_W), w_o16, norm2_g[:, None, :], w_116, w_216,
                      final_g[None, :], l, l == depth - 1)
    return x2.reshape(bsz, seq, D_MODEL)
```

```python
import functools
import math

import jax
import jax.numpy as jnp
from jax import lax
from jax.experimental import pallas as pl
from jax.experimental.pallas import tpu as pltpu

D_MODEL = 1024
HEAD_DIM = 64
A_HEADS = 4
A_QK_DIM = 32
B_HEADS = 8
C_HEADS = 4
C_PAIRS = ((128, 1), (512, 4), (2048, 16))
C_GROUPS = 3
A_W = 256
B_W = 512
C_W = 256
D_FF = 4096
N_BUCKETS = 32
MAX_DISTANCE = 2048
NORM_EPS = 1e-6
NEG_INF = -1e30
LOG2E = 1.4426950408889634

LANES = 128
VMEM_LIMIT = 56 * 1024 * 1024

KB, KA = 0, 512
N_TM = 768
QB_T, VB_T, QA_T, VA_T = 0, 512, 1024, 1280
N_FM = 1536
N_C = 2304
N_C_SLABS = N_C // 128
F_ROWS = 16
ONES_ROWS = 16
ACC_ROWS = HEAD_DIM + ONES_ROWS

A_STEP_HEADS = 4
B_STEP_HEADS = 8
C_TILE = 128
C_UNROLL = 32
C_AHEAD = 4
C_SPLIT = 4
FLASH_TILE = 256
MLP_TM = 512
MLP_TF = 1024
A_GROUP = 2
B_GROUP = 1
FLASH_PAIRS = 2


def _bucket_thresholds():
    max_exact = N_BUCKETS // 2
    out = []
    for k in range(1, N_BUCKETS - max_exact):
        t = max_exact * (MAX_DISTANCE / max_exact) ** (k / (N_BUCKETS - max_exact))
        out.append(int(math.ceil(t)))
    return tuple(out)


_THRESHOLDS = _bucket_thresholds()


def _n_near_tiles(tile):
    return -(-(_THRESHOLDS[-1] + tile - 1) // tile)


def _t5_bucket(d):
    big = jnp.full(d.shape, N_BUCKETS // 2, jnp.int32)
    for t in _THRESHOLDS:
        big = big + (d >= t).astype(jnp.int32)
    return jnp.where(d < N_BUCKETS // 2, d, big)


def _bias_lookup(rb_ref, bucket, col):
    val = jnp.zeros(bucket.shape, jnp.float32)
    for b in range(N_BUCKETS):
        val = jnp.where(bucket == b, rb_ref[b, col], val)
    return val


def _bias_a_kernel(rb_ref, out_ref, *, tile):
    delta = pl.program_id(0)
    j = lax.broadcasted_iota(jnp.int32, (tile, tile), 0)
    i = lax.broadcasted_iota(jnp.int32, (tile, tile), 1)
    d = delta * tile + i - j
    bucket = _t5_bucket(d)
    for h in range(A_HEADS):
        val = _bias_lookup(rb_ref, bucket, h) * LOG2E
        out_ref[h, 0] = jnp.where(d >= 0, val, NEG_INF)


def _bias_c_kernel(rb_ref, out_ref):
    g = pl.program_id(0)
    dil = jnp.where(g == 0, C_PAIRS[0][1], jnp.where(g == 1, C_PAIRS[1][1], C_PAIRS[2][1]))
    c = lax.broadcasted_iota(jnp.int32, (2 * C_TILE, C_TILE), 0)
    i = lax.broadcasted_iota(jnp.int32, (2 * C_TILE, C_TILE), 1)
    steps = i + C_TILE - c
    valid = (steps >= 0) & (steps <= C_TILE)
    bucket = _t5_bucket(steps * dil)
    for h in range(C_HEADS):
        val = _bias_lookup(rb_ref, bucket, A_HEADS + g * C_HEADS + h) * LOG2E
        out_ref[h, 1] = jnp.where(valid, val, NEG_INF)
        out_ref[h, 0] = jnp.where(valid & (c >= C_TILE), val, NEG_INF)


def _make_bias_tiles(rel_bias):
    tile = FLASH_TILE
    n_bias = _n_near_tiles(tile) + 1
    smem = pl.BlockSpec(memory_space=pltpu.SMEM)
    bias_a = pl.pallas_call(
        functools.partial(_bias_a_kernel, tile=tile),
        out_shape=jax.ShapeDtypeStruct((A_HEADS, n_bias, tile, tile), jnp.float32),
        grid=(n_bias,),
        in_specs=[smem],
        out_specs=pl.BlockSpec((A_HEADS, 1, tile, tile), lambda d: (0, d, 0, 0)),
        name="bias_a_tiles",
    )(rel_bias)
    bias_c = pl.pallas_call(
        _bias_c_kernel,
        out_shape=jax.ShapeDtypeStruct((C_GROUPS * C_HEADS, 2, 2 * C_TILE, C_TILE), jnp.float32),
        grid=(C_GROUPS,),
        in_specs=[smem],
        out_specs=pl.BlockSpec((C_HEADS, 2, 2 * C_TILE, C_TILE), lambda g: (g, 0, 0, 0)),
        name="bias_c_tiles",
    )(rel_bias)
    return bias_a, bias_c


def _rms(x, g):
    return x * lax.rsqrt(jnp.mean(x * x, axis=-1, keepdims=True) + NORM_EPS) * g


def _dot_nt(a, b):
    return lax.dot_general(a, b, (((1,), (1,)), ((), ())), preferred_element_type=jnp.float32)


def _in_proj_kernel(x_ref, g_ref, wtm_ref, wfm_ref, wc_ref, tm_ref, fm_ref, ft_ref, c_ref):
    h = _rms(x_ref[...], g_ref[0]).astype(jnp.bfloat16)
    tm_ref[...] = jnp.dot(h, wtm_ref[0], preferred_element_type=jnp.float32).astype(jnp.bfloat16)
    fm = _dot_nt(wfm_ref[0], h)
    fm_ref[...] = fm[:N_FM].astype(jnp.bfloat16)
    ft_ref[...] = fm[N_FM:]
    for c in range(N_C // 256):
        pc = jnp.dot(h, wc_ref[0, :, c * 256:(c + 1) * 256], preferred_element_type=jnp.float32)
        c_ref[2 * c] = pc[:, :LANES]
        c_ref[2 * c + 1] = pc[:, LANES:]


def _in_proj(x2, norm_g, w_tm, w_fm, w_c, layer, *, tm=512):
    n_tok = x2.shape[0]
    return pl.pallas_call(
        _in_proj_kernel,
        out_shape=(jax.ShapeDtypeStruct((n_tok, N_TM), jnp.bfloat16),
                   jax.ShapeDtypeStruct((N_FM, n_tok), jnp.bfloat16),
                   jax.ShapeDtypeStruct((F_ROWS, n_tok), jnp.float32),
                   jax.ShapeDtypeStruct((N_C_SLABS, n_tok, LANES), jnp.float32)),
        grid=(n_tok // tm,),
        in_specs=[
            pl.BlockSpec((tm, D_MODEL), lambda i: (i, 0)),
            pl.BlockSpec((1, 1, D_MODEL), lambda i: (layer, 0, 0)),
            pl.BlockSpec((1, D_MODEL, N_TM), lambda i: (layer, 0, 0)),
            pl.BlockSpec((1, N_FM + F_ROWS, D_MODEL), lambda i: (layer, 0, 0)),
            pl.BlockSpec((1, D_MODEL, N_C), lambda i: (layer, 0, 0)),
        ],
        out_specs=(pl.BlockSpec((tm, N_TM), lambda i: (i, 0)),
                   pl.BlockSpec((N_FM, tm), lambda i: (0, i)),
                   pl.BlockSpec((F_ROWS, tm), lambda i: (0, i)),
                   pl.BlockSpec((N_C_SLABS, tm, LANES), lambda i: (0, i, 0))),
        compiler_params=pltpu.CompilerParams(dimension_semantics=("parallel",),
                                             vmem_limit_bytes=VMEM_LIMIT),
        name="in_proj",
    )(x2, norm_g, w_tm, w_fm, w_c)


def _decay_kernel(ft_ref, bf_ref, ckp_ref):
    z = ft_ref[0:B_HEADS, :] + bf_ref[0]
    x = jnp.minimum(z, 0.0) - jnp.log1p(jnp.exp(-jnp.abs(z)))
    seq = x.shape[1]
    pos = lax.broadcasted_iota(jnp.int32, x.shape, 1)
    shift = 1
    while shift < seq:
        x = x + jnp.where(pos >= shift, pltpu.roll(x, shift, 1), 0.0)
        shift *= 2
    c = x * (-LOG2E)
    hi = c.astype(jnp.bfloat16).astype(jnp.float32)
    mid = (c - hi).astype(jnp.bfloat16).astype(jnp.float32)
    lo = c - hi - mid
    stacked = jnp.concatenate(
        [hi, mid, lo, jnp.zeros((LANES - 3 * B_HEADS, seq), jnp.float32)], axis=0)
    ckp_ref[0] = jnp.transpose(stacked).astype(jnp.bfloat16)


def _decay(ft, b_f, layer, bsz, seq):
    return pl.pallas_call(
        _decay_kernel,
        out_shape=jax.ShapeDtypeStruct((bsz, seq, LANES), jnp.bfloat16),
        grid=(bsz,),
        in_specs=[pl.BlockSpec((F_ROWS, seq), lambda b: (0, b)),
                  pl.BlockSpec((1, B_HEADS, 1), lambda b: (layer, 0, 0))],
        out_specs=pl.BlockSpec((1, seq, LANES), lambda b: (b, 0, 0)),
        compiler_params=pltpu.CompilerParams(dimension_semantics=("parallel",)),
        name="decay_scan",
    )(ft, b_f)


def _colmax(s):
    rows = s.shape[0]
    while rows > 8:
        rows //= 2
        s = jnp.maximum(s[:rows], s[rows:2 * rows])
    return jnp.max(s, axis=0, keepdims=True)


def _online_step(s, smax, v_aug, m_ref, acc_ref, idx):
    m_old = m_ref[idx]
    m_new = jnp.maximum(m_old, smax)
    alpha = jnp.exp2(m_old - m_new)
    p = jnp.exp2(s - m_new).astype(jnp.bfloat16)
    acc_ref[idx] = alpha * acc_ref[idx] + jnp.dot(v_aug, p, preferred_element_type=jnp.float32)
    m_ref[idx] = m_new


def _v_aug(vt_ref, head, k0, tile):
    v = vt_ref[head * HEAD_DIM:(head + 1) * HEAD_DIM, pl.ds(k0, tile)]
    return jnp.concatenate([v, jnp.ones((ONES_ROWS, tile), jnp.bfloat16)], axis=0)


def _flash_pipeline(qi, n_sub, group, produce, consume, finish):
    every = range(n_sub)

    def overlapped(pk, pslot, ck, cslot):
        for g0 in range(0, n_sub, group):
            subs = range(g0, g0 + group)
            produce(pk, pslot, subs)
            consume(ck, cslot, False, subs)

    produce(0, 0, every)

    def pair(k):
        overlapped(k + 1, 1, k, 0)
        overlapped(k + 2, 0, k + 1, 1)

    def pairs(j, carry):
        for u in range(FLASH_PAIRS):
            pair(2 * (FLASH_PAIRS * j + u))
        return carry

    n_pairs = qi // 2
    lax.fori_loop(0, n_pairs // FLASH_PAIRS, pairs, 0)
    for u in range(FLASH_PAIRS - 1):
        done = (n_pairs // FLASH_PAIRS) * FLASH_PAIRS + u
        pl.when(done < n_pairs)(functools.partial(pair, 2 * done))

    @pl.when(qi % 2 == 0)
    def _():
        consume(qi, 0, True, every)
        finish()

    @pl.when(qi % 2 == 1)
    def _():
        overlapped(qi, 1, qi - 1, 0)
        consume(qi, 1, True, every)
        finish()


def _attn_a_kernel(qt_ref, k_ref, vt_ref, bias_ref, lamv_ref, lami_ref, subg_ref, o_ref,
                   m_ref, acc_ref, s_ref, smax_ref, *, tile, n_near):
    qi = pl.program_id(2)
    n_sub = 2 * A_STEP_HEADS
    qt = qt_ref[...].astype(jnp.float32)
    row = lax.broadcasted_iota(jnp.int32, (LANES, tile), 0)
    qs = []
    for s in range(n_sub):
        blk = qt[(s // 4) * LANES:(s // 4 + 1) * LANES]
        lo = (s % 4) * A_QK_DIM
        qs.append(jnp.where((row >= lo) & (row < lo + A_QK_DIM), blk, 0.0).astype(jnp.bfloat16))
    m_ref[...] = jnp.full(m_ref.shape, NEG_INF, jnp.float32)
    acc_ref[...] = jnp.zeros(acc_ref.shape, jnp.float32)

    def produce(ki, slot, subs):
        k0 = pl.multiple_of(ki * tile, tile)
        delta = jnp.minimum(qi - ki, n_near)
        for s in subs:
            k = k_ref[0, pl.ds(k0, tile), (s // 4) * LANES:(s // 4 + 1) * LANES]
            x = jnp.dot(k, qs[s], preferred_element_type=jnp.float32) + bias_ref[s // 2, delta]
            s_ref[slot, s] = x
            smax_ref[slot, s] = _colmax(x)

    def consume(ki, slot, last, subs):
        k0 = pl.multiple_of(ki * tile, tile)
        for s in subs:
            _online_step(s_ref[slot, s], smax_ref[slot, s], _v_aug(vt_ref, s // 2, k0, tile),
                         m_ref, acc_ref, s)

    def finish():
        lam_init = lami_ref[0]
        lv = lamv_ref[0]
        e1 = jnp.exp(jnp.sum(lv[0:1] * lv[1:2], axis=1, keepdims=True))
        e2 = jnp.exp(jnp.sum(lv[2:3] * lv[3:4], axis=1, keepdims=True))
        lam = e1 - e2 + lam_init
        heads = []
        for h in range(A_STEP_HEADS):
            a0, a1 = acc_ref[2 * h], acc_ref[2 * h + 1]
            o = (a0[0:HEAD_DIM] / a0[HEAD_DIM:HEAD_DIM + 1]
                 - lam * (a1[0:HEAD_DIM] / a1[HEAD_DIM:HEAD_DIM + 1]))
            ms = jnp.mean(o * o, axis=0, keepdims=True)
            heads.append(o * lax.rsqrt(ms + NORM_EPS) * subg_ref[0] * (1.0 - lam_init))
        o_ref[0] = jnp.transpose(jnp.concatenate(heads, axis=0)).astype(o_ref.dtype)

    _flash_pipeline(qi, n_sub, A_GROUP, produce, consume, finish)


def _attn_a(proj_tm3, proj_fm, bias_a, lam_vecs, lam_init, sub_g, layer):
    bsz, seq, _ = proj_tm3.shape
    tile = FLASH_TILE
    n_near = _n_near_tiles(tile)
    n_q = seq // tile
    n_h = A_STEP_HEADS
    width = n_h * HEAD_DIM
    return pl.pallas_call(
        functools.partial(_attn_a_kernel, tile=tile, n_near=n_near),
        out_shape=jax.ShapeDtypeStruct((bsz, seq, A_W), jnp.bfloat16),
        grid=(A_HEADS // n_h, bsz, n_q),
        in_specs=[
            pl.BlockSpec((width, tile), lambda hg, b, qi: (QA_T // width + hg, b * n_q + qi)),
            pl.BlockSpec((1, seq, width), lambda hg, b, qi: (b, 0, KA // width + hg)),
            pl.BlockSpec((width, seq), lambda hg, b, qi: (VA_T // width + hg, b)),
            pl.BlockSpec((n_h, n_near + 1, tile, tile), lambda hg, b, qi: (hg, 0, 0, 0)),
            pl.BlockSpec((1, 4, A_QK_DIM), lambda hg, b, qi: (layer, 0, 0)),
            pl.BlockSpec(memory_space=pltpu.SMEM),
            pl.BlockSpec((1, HEAD_DIM, 1), lambda hg, b, qi: (layer, 0, 0)),
        ],
        out_specs=pl.BlockSpec((1, tile, width), lambda hg, b, qi: (b, qi, hg)),
        scratch_shapes=[pltpu.VMEM((2 * n_h, 1, tile), jnp.float32),
                        pltpu.VMEM((2 * n_h, ACC_ROWS, tile), jnp.float32),
                        pltpu.VMEM((2, 2 * n_h, tile, tile), jnp.float32),
                        pltpu.VMEM((2, 2 * n_h, 1, tile), jnp.float32)],
        compiler_params=pltpu.CompilerParams(
            dimension_semantics=("parallel", "parallel", "parallel"),
            vmem_limit_bytes=VMEM_LIMIT),
        name="attn_a",
    )(proj_fm, proj_tm3, proj_fm, bias_a, lam_vecs, lam_init, sub_g)


def _attn_b_kernel(qt_ref, k_ref, ckp_ref, vt_ref, o_ref, m_ref, acc_ref, s_ref, smax_ref, *,
                   tile):
    hg = pl.program_id(0)
    qi = pl.program_id(2)
    n_h = B_STEP_HEADS
    qt = qt_ref[...].astype(jnp.float32)
    row = lax.broadcasted_iota(jnp.int32, (LANES, tile), 0)
    q_aug = []
    for s in range(n_h):
        blk = qt[(s // 2) * LANES:(s // 2 + 1) * LANES]
        lo = (s % 2) * HEAD_DIM
        q_s = jnp.where((row >= lo) & (row < lo + HEAD_DIM), blk, 0.0)
        pick = (row < 3 * B_HEADS) & ((row % B_HEADS) == n_h * hg + s)
        ones = jnp.where(pick, 1.0, 0.0)
        q_aug.append(jnp.concatenate([q_s, ones], axis=0).astype(jnp.bfloat16))
    m_ref[...] = jnp.full(m_ref.shape, NEG_INF, jnp.float32)
    acc_ref[...] = jnp.zeros(acc_ref.shape, jnp.float32)

    def produce(ki, slot, subs):
        k0 = pl.multiple_of(ki * tile, tile)
        ck = ckp_ref[0, pl.ds(k0, tile), :]
        for s in subs:
            k = k_ref[0, pl.ds(k0, tile), (s // 2) * LANES:(s // 2 + 1) * LANES]
            x = jnp.dot(jnp.concatenate([k, ck], axis=1), q_aug[s],
                        preferred_element_type=jnp.float32)
            s_ref[slot, s] = x
            smax_ref[slot, s] = _colmax(x)

    def consume(ki, slot, last, subs):
        k0 = pl.multiple_of(ki * tile, tile)
        if last:
            kk = lax.broadcasted_iota(jnp.int32, (tile, tile), 0)
            qq = lax.broadcasted_iota(jnp.int32, (tile, tile), 1)
            keep = kk <= qq
        for s in subs:
            x, xmax = s_ref[slot, s], smax_ref[slot, s]
            if last:
                x = jnp.where(keep, x, NEG_INF)
                xmax = _colmax(x)
            _online_step(x, xmax, _v_aug(vt_ref, s, k0, tile), m_ref, acc_ref, s)

    def finish():
        heads = [acc_ref[s][0:HEAD_DIM] / acc_ref[s][HEAD_DIM:HEAD_DIM + 1] for s in range(n_h)]
        o_ref[0] = jnp.transpose(jnp.concatenate(heads, axis=0)).astype(o_ref.dtype)

    _flash_pipeline(qi, n_h, B_GROUP, produce, consume, finish)


def _attn_b(proj_tm3, proj_fm, ckp):
    bsz, seq, _ = proj_tm3.shape
    tile = FLASH_TILE
    n_q = seq // tile
    n_h = B_STEP_HEADS
    width = n_h * HEAD_DIM
    return pl.pallas_call(
        functools.partial(_attn_b_kernel, tile=tile),
        out_shape=jax.ShapeDtypeStruct((bsz, seq, B_W), jnp.bfloat16),
        grid=(B_HEADS // n_h, bsz, n_q),
        in_specs=[
            pl.BlockSpec((width, tile), lambda hg, b, qi: (QB_T // width + hg, b * n_q + qi)),
            pl.BlockSpec((1, seq, width), lambda hg, b, qi: (b, 0, KB // width + hg)),
            pl.BlockSpec((1, seq, LANES), lambda hg, b, qi: (b, 0, 0)),
            pl.BlockSpec((width, seq), lambda hg, b, qi: (VB_T // width + hg, b)),
        ],
        out_specs=pl.BlockSpec((1, tile, width), lambda hg, b, qi: (b, qi, hg)),
        scratch_shapes=[pltpu.VMEM((n_h, 1, tile), jnp.float32),
                        pltpu.VMEM((n_h, ACC_ROWS, tile), jnp.float32),
                        pltpu.VMEM((2, n_h, tile, tile), jnp.float32),
                        pltpu.VMEM((2, n_h, 1, tile), jnp.float32)],
        compiler_params=pltpu.CompilerParams(
            dimension_semantics=("parallel", "parallel", "parallel"),
            vmem_limit_bytes=VMEM_LIMIT),
        name="attn_b",
    )(proj_fm, proj_tm3, ckp, proj_fm)


def _attn_c_kernel(q_ref, k_ref, v_ref, bias_ref, o_ref, qd_ref, kd_ref, vtd_ref, onat_ref,
                   lnat_ref, tq_ref, tk_ref, tv_ref, *, seq):
    g = pl.program_id(2)
    n_chunks = seq // C_TILE
    lane = lax.broadcasted_iota(jnp.int32, (1, LANES), 1)
    first = lane < HEAD_DIM

    def run_group(group, dil):
        n_tiles = seq // dil // C_TILE

        def out_rows(i):
            r, t = i // n_tiles, i % n_tiles
            return pl.ds(r + t * (C_TILE * dil), C_TILE, stride=dil)

        if dil > C_SPLIT:
            per = seq // C_SPLIT // C_TILE

            def presplit(j, carry):
                for u in range(C_UNROLL):
                    i = j * C_UNROLL + u
                    rows = pl.ds(i // per + (i % per) * (C_TILE * C_SPLIT), C_TILE, stride=C_SPLIT)
                    dst = pl.ds(pl.multiple_of(i * C_TILE, C_TILE), C_TILE)
                    for src, tmp in ((q_ref, tq_ref), (k_ref, tk_ref), (v_ref, tv_ref)):
                        tmp[dst, :] = src[rows, :]
                return carry

            lax.fori_loop(0, n_chunks // C_UNROLL, presplit, 0)
            srcs, step = (tq_ref, tk_ref, tv_ref), dil // C_SPLIT

            def src_rows(i):
                r, t = i // n_tiles, i % n_tiles
                base = (r % C_SPLIT) * (seq // C_SPLIT) + r // C_SPLIT + t * (C_TILE * step)
                return pl.ds(base, C_TILE, stride=step)
        else:
            srcs, src_rows = (q_ref, k_ref, v_ref), out_rows

        kd_ref[0:C_TILE, :] = jnp.zeros((C_TILE, LANES), jnp.bfloat16)
        vtd_ref[:, 0:C_TILE] = jnp.zeros((LANES, C_TILE), jnp.bfloat16)

        def gather(j, carry):
            ids = [j * C_UNROLL + u for u in range(C_UNROLL)]
            vts = [jnp.transpose(srcs[2][src_rows(i), :]) for i in ids]
            for i, vt in zip(ids, vts):
                rows = src_rows(i)
                dst = pl.multiple_of(i * C_TILE, C_TILE)
                qd_ref[pl.ds(dst, C_TILE), :] = srcs[0][rows, :].astype(jnp.bfloat16)
                kd_ref[pl.ds(dst + C_TILE, C_TILE), :] = srcs[1][rows, :].astype(jnp.bfloat16)
                vtd_ref[:, pl.ds(dst + C_TILE, C_TILE)] = vt.astype(jnp.bfloat16)
            return carry

        lax.fori_loop(0, n_chunks // C_UNROLL, gather, 0)

        def tiles(j, carry):
            ids = [j * C_UNROLL + u for u in range(C_UNROLL)]
            row0s = [pl.multiple_of(i * C_TILE, C_TILE) for i in ids]

            def scores(u):
                q = qd_ref[pl.ds(row0s[u], C_TILE), :]
                q2 = jnp.concatenate([jnp.where(first, q, jnp.zeros_like(q)),
                                      jnp.where(first, jnp.zeros_like(q), q)], axis=0)
                return _dot_nt(kd_ref[pl.ds(row0s[u], 2 * C_TILE), :], q2)

            def softmax_pv(u, st):
                ver = jnp.minimum(ids[u] % n_tiles, 1)
                res = []
                for h in range(2):
                    x = st[:, h * C_TILE:(h + 1) * C_TILE] + bias_ref[h, ver]
                    m = _colmax(x)
                    p = jnp.exp2(x - m).astype(jnp.bfloat16)
                    v_aug = jnp.concatenate(
                        [vtd_ref[h * HEAD_DIM:(h + 1) * HEAD_DIM, pl.ds(row0s[u], 2 * C_TILE)],
                         jnp.ones((ONES_ROWS, 2 * C_TILE), jnp.bfloat16)], axis=0)
                    res.append((m, jnp.dot(v_aug, p, preferred_element_type=jnp.float32)))
                return res

            def write_back(u, res):
                outs, lses = [], []
                for m, pv in res:
                    l = pv[HEAD_DIM:HEAD_DIM + 1]
                    outs.append(pv[0:HEAD_DIM] / l)
                    lses.append(jnp.broadcast_to(m + jnp.log2(l), (HEAD_DIM, C_TILE)))
                rows = out_rows(ids[u])
                onat_ref[group, rows, :] = jnp.transpose(jnp.concatenate(outs, axis=0))
                lnat_ref[group, rows, :] = jnp.transpose(jnp.concatenate(lses, axis=0))

            sts = {u: scores(u) for u in range(C_AHEAD)}
            pending = None
            for u in range(C_UNROLL):
                if u + C_AHEAD < C_UNROLL:
                    sts[u + C_AHEAD] = scores(u + C_AHEAD)
                res = softmax_pv(u, sts.pop(u))
                if pending is not None:
                    write_back(*pending)
                pending = (u, res)
            write_back(*pending)
            return carry

        lax.fori_loop(0, n_chunks // C_UNROLL, tiles, 0)

    for group, (_, dil) in enumerate(C_PAIRS):
        pl.when(g == group)(functools.partial(run_group, group, dil))

    @pl.when(g == C_GROUPS - 1)
    def _():
        rows_per = 512

        def combine(c, carry):
            rows = pl.ds(pl.multiple_of(c * rows_per, rows_per), rows_per)
            l0, l1, l2 = lnat_ref[0, rows, :], lnat_ref[1, rows, :], lnat_ref[2, rows, :]
            m = jnp.maximum(jnp.maximum(l0, l1), l2)
            e0, e1, e2 = jnp.exp2(l0 - m), jnp.exp2(l1 - m), jnp.exp2(l2 - m)
            num = e0 * onat_ref[0, rows, :] + e1 * onat_ref[1, rows, :] + e2 * onat_ref[2, rows, :]
            o_ref[0, rows, :] = (num / (e0 + e1 + e2)).astype(o_ref.dtype)
            return carry

        lax.fori_loop(0, seq // rows_per, combine, 0)


def _attn_c(proj_c, bias_c, bsz, seq):
    assert seq % (C_TILE * C_UNROLL) == 0 and seq % (C_TILE * max(d for _, d in C_PAIRS)) == 0
    slab = lambda role: pl.BlockSpec(
        (None, seq, LANES), lambda b, hp, g: (role * 2 * C_GROUPS + g * 2 + hp, b, 0))
    return pl.pallas_call(
        functools.partial(_attn_c_kernel, seq=seq),
        out_shape=jax.ShapeDtypeStruct((bsz, seq, C_W), jnp.bfloat16),
        grid=(bsz, C_HEADS // 2, C_GROUPS),
        in_specs=[slab(0), slab(1), slab(2),
                  pl.BlockSpec((2, 2, 2 * C_TILE, C_TILE), lambda b, hp, g: (g * 2 + hp, 0, 0, 0))],
        out_specs=pl.BlockSpec((1, seq, LANES), lambda b, hp, g: (b, 0, hp)),
        scratch_shapes=[pltpu.VMEM((seq, LANES), jnp.bfloat16),
                        pltpu.VMEM((seq + C_TILE, LANES), jnp.bfloat16),
                        pltpu.VMEM((LANES, seq + C_TILE), jnp.bfloat16),
                        pltpu.VMEM((C_GROUPS, seq, LANES), jnp.float32),
                        pltpu.VMEM((C_GROUPS, seq, LANES), jnp.float32)]
        + [pltpu.VMEM((seq, LANES), jnp.float32)] * 3,
        compiler_params=pltpu.CompilerParams(
            dimension_semantics=("parallel", "parallel", "arbitrary"),
            vmem_limit_bytes=VMEM_LIMIT),
        name="attn_c",
    )(proj_c, proj_c, proj_c, bias_c)


def _mix_mlp_kernel(x_ref, oa_ref, ob_ref, oc_ref, wo_ref, g_ref, w1_ref, w2_ref, fg_ref, out_ref, *,
                    final, tf):
    mixed = jnp.concatenate([oa_ref[...], ob_ref[...], oc_ref[...]], axis=1)
    acc = x_ref[...] + jnp.dot(mixed, wo_ref[0], preferred_element_type=jnp.float32)
    h = _rms(acc, g_ref[0]).astype(jnp.bfloat16)
    for c in range(D_FF // tf):
        u = jnp.dot(h, w1_ref[0, :, c * tf:(c + 1) * tf], preferred_element_type=jnp.float32)
        u = jnp.square(jnp.maximum(u, 0.0)).astype(jnp.bfloat16)
        acc = acc + jnp.dot(u, w2_ref[0, c * tf:(c + 1) * tf, :],
                            preferred_element_type=jnp.float32)
    out_ref[...] = _rms(acc, fg_ref[...]) if final else acc


def _mix_mlp(x2, oa, ob, oc, w_o, norm_g, w_1, w_2, final_g, layer, final, *, tm=MLP_TM, tf=MLP_TF):
    n_tok = x2.shape[0]
    row = lambda w: pl.BlockSpec((tm, w), lambda i: (i, 0))
    resident = lambda shape: pl.BlockSpec(shape, lambda i: (layer, 0, 0),
                                          pipeline_mode=pl.Buffered(1))
    return pl.pallas_call(
        functools.partial(_mix_mlp_kernel, final=final, tf=tf),
        out_shape=jax.ShapeDtypeStruct((n_tok, D_MODEL), jnp.float32),
        grid=(n_tok // tm,),
        in_specs=[row(D_MODEL), row(A_W), row(B_W), row(C_W),
                  resident((1, D_MODEL, D_MODEL)),
                  pl.BlockSpec((1, 1, D_MODEL), lambda i: (layer, 0, 0)),
                  resident((1, D_MODEL, D_FF)),
                  resident((1, D_FF, D_MODEL)),
                  pl.BlockSpec((1, D_MODEL), lambda i: (0, 0))],
        out_specs=row(D_MODEL),
        compiler_params=pltpu.CompilerParams(dimension_semantics=("parallel",),
                                             vmem_limit_bytes=VMEM_LIMIT),
        name="mix_mlp",
    )(x2, oa, ob, oc, w_o, norm_g, w_1, w_2, final_g)


def _prep_w_in(w_in):
    a0, b0, f0, c0 = 0, 3 * A_W, 3 * A_W + 3 * B_W, 3 * A_W + 3 * B_W + B_HEADS
    sa, sb = A_QK_DIM ** -0.5 * LOG2E, HEAD_DIM ** -0.5 * LOG2E
    cw = C_GROUPS * C_W
    w_tm = jnp.concatenate([w_in[:, :, b0 + B_W:b0 + 2 * B_W],
                            w_in[:, :, a0 + A_W:a0 + 2 * A_W]], axis=-1).astype(jnp.bfloat16)
    w_fm = jnp.concatenate([
        w_in[:, :, b0:b0 + B_W] * sb, w_in[:, :, b0 + 2 * B_W:f0],
        w_in[:, :, a0:a0 + A_W] * sa, w_in[:, :, a0 + 2 * A_W:b0],
    ], axis=-1)
    w_fm = jnp.pad(jnp.concatenate([w_fm, w_in[:, :, f0:c0]], axis=-1),
                   ((0, 0), (0, 0), (0, F_ROWS - B_HEADS)))
    w_fm = jnp.transpose(w_fm, (0, 2, 1)).astype(jnp.bfloat16)
    w_c = jnp.concatenate([w_in[:, :, c0:c0 + cw] * sb, w_in[:, :, c0 + cw:]],
                          axis=-1).astype(jnp.bfloat16)
    return w_tm, w_fm, w_c


def kernel(x, norm1_g, w_in, b_f, lam_q1, lam_k1, lam_q2, lam_k2, diff_norm_g, w_o, norm2_g,
           w_1, w_2, rel_bias, final_g):
    bsz, seq, _ = x.shape
    depth = w_in.shape[0]
    w_tm, w_fm, w_c = _prep_w_in(w_in)
    w_o16, w_116, w_216 = (w.astype(jnp.bfloat16) for w in (w_o, w_1, w_2))
    lam_vecs = jnp.stack([lam_q1, lam_k1, lam_q2, lam_k2], axis=1)
    bias_a, bias_c = _make_bias_tiles(rel_bias)

    x2 = x.reshape(bsz * seq, D_MODEL)
    for l in range(depth):
        proj_tm, proj_fm, ft, proj_c = _in_proj(x2, norm1_g[:, None, :], w_tm, w_fm, w_c, l)
        proj3 = proj_tm.reshape(bsz, seq, N_TM)
        ckp = _decay(ft, b_f[:, :, None], l, bsz, seq)
        lam_init = jnp.full((1,), 0.8 - 0.6 * math.exp(-0.3 * l), jnp.float32)
        oa = _attn_a(proj3, proj_fm, bias_a, lam_vecs, lam_init, diff_norm_g[:, :, None], l)
        ob = _attn_b(proj3, proj_fm, ckp)
        oc = _attn_c(proj_c, bias_c, bsz, seq)
        x2 = _mix_mlp(x2, oa.reshape(bsz * seq, A_W), ob.reshape(bsz * seq, B_W),
                      oc.reshape(bsz * seq, C_W), w_o16, norm2_g[:, None, :], w_116, w_216,
                      final_g[None, :], l, l == depth - 1)
    return x2.reshape(bsz, seq, D_MODEL)
```

```python
import functools
import math

import jax
import jax.numpy as jnp
from jax import lax
from jax.experimental import pallas as pl
from jax.experimental.pallas import tpu as pltpu

D_MODEL = 1024
HEAD_DIM = 64
A_HEADS = 4
A_QK_DIM = 32
B_HEADS = 8
C_HEADS = 4
C_PAIRS = ((128, 1), (512, 4), (2048, 16))
C_GROUPS = 3
A_W = 256
B_W = 512
C_W = 256
D_FF = 4096
N_BUCKETS = 32
MAX_DISTANCE = 2048
NORM_EPS = 1e-6
NEG_INF = -1e30
LOG2E = 1.4426950408889634

LANES = 128
VMEM_LIMIT = 56 * 1024 * 1024

KB, KA = 0, 512
N_TM = 768
QB_T, VB_T, QA_T, VA_T = 0, 512, 1024, 1280
N_FM = 1536
N_C = 2304
N_C_SLABS = N_C // 128
F_ROWS = 16
ONES_ROWS = 16
ACC_ROWS = HEAD_DIM + ONES_ROWS

C_TILE = 128
C_UNROLL = 32
C_AHEAD = 4
C_SPLIT = 4
FLASH_TILE = 256
MLP_TM = 512
MLP_TF = 1024
A_GROUP = 2
B_GROUP = 1
FLASH_PAIRS = 2


def _bucket_thresholds():
    max_exact = N_BUCKETS // 2
    out = []
    for k in range(1, N_BUCKETS - max_exact):
        t = max_exact * (MAX_DISTANCE / max_exact) ** (k / (N_BUCKETS - max_exact))
        out.append(int(math.ceil(t)))
    return tuple(out)


_THRESHOLDS = _bucket_thresholds()


def _n_near_tiles(tile):
    return -(-(_THRESHOLDS[-1] + tile - 1) // tile)


def _t5_bucket(d):
    big = jnp.full(d.shape, N_BUCKETS // 2, jnp.int32)
    for t in _THRESHOLDS:
        big = big + (d >= t).astype(jnp.int32)
    return jnp.where(d < N_BUCKETS // 2, d, big)


def _bias_lookup(rb_ref, bucket, col):
    val = jnp.zeros(bucket.shape, jnp.float32)
    for b in range(N_BUCKETS):
        val = jnp.where(bucket == b, rb_ref[b, col], val)
    return val


def _bias_a_kernel(rb_ref, out_ref, *, tile):
    delta = pl.program_id(0)
    j = lax.broadcasted_iota(jnp.int32, (tile, tile), 0)
    i = lax.broadcasted_iota(jnp.int32, (tile, tile), 1)
    d = delta * tile + i - j
    bucket = _t5_bucket(d)
    for h in range(A_HEADS):
        val = _bias_lookup(rb_ref, bucket, h) * LOG2E
        out_ref[h, 0] = jnp.where(d >= 0, val, NEG_INF)


def _bias_c_kernel(rb_ref, out_ref):
    g = pl.program_id(0)
    dil = jnp.where(g == 0, C_PAIRS[0][1], jnp.where(g == 1, C_PAIRS[1][1], C_PAIRS[2][1]))
    c = lax.broadcasted_iota(jnp.int32, (2 * C_TILE, C_TILE), 0)
    i = lax.broadcasted_iota(jnp.int32, (2 * C_TILE, C_TILE), 1)
    steps = i + C_TILE - c
    valid = (steps >= 0) & (steps <= C_TILE)
    bucket = _t5_bucket(steps * dil)
    for h in range(C_HEADS):
        val = _bias_lookup(rb_ref, bucket, A_HEADS + g * C_HEADS + h) * LOG2E
        out_ref[h, 1] = jnp.where(valid, val, NEG_INF)
        out_ref[h, 0] = jnp.where(valid & (c >= C_TILE), val, NEG_INF)


def _make_bias_tiles(rel_bias):
    tile = FLASH_TILE
    n_bias = _n_near_tiles(tile) + 1
    smem = pl.BlockSpec(memory_space=pltpu.SMEM)
    bias_a = pl.pallas_call(
        functools.partial(_bias_a_kernel, tile=tile),
        out_shape=jax.ShapeDtypeStruct((A_HEADS, n_bias, tile, tile), jnp.float32),
        grid=(n_bias,),
        in_specs=[smem],
        out_specs=pl.BlockSpec((A_HEADS, 1, tile, tile), lambda d: (0, d, 0, 0)),
        name="bias_a_tiles",
    )(rel_bias)
    bias_c = pl.pallas_call(
        _bias_c_kernel,
        out_shape=jax.ShapeDtypeStruct((C_GROUPS * C_HEADS, 2, 2 * C_TILE, C_TILE), jnp.float32),
        grid=(C_GROUPS,),
        in_specs=[smem],
        out_specs=pl.BlockSpec((C_HEADS, 2, 2 * C_TILE, C_TILE), lambda g: (g, 0, 0, 0)),
        name="bias_c_tiles",
    )(rel_bias)
    return bias_a, bias_c


def _rms(x, g):
    return x * lax.rsqrt(jnp.mean(x * x, axis=-1, keepdims=True) + NORM_EPS) * g


def _dot_nt(a, b):
    return lax.dot_general(a, b, (((1,), (1,)), ((), ())), preferred_element_type=jnp.float32)


def _in_proj_kernel(x_ref, g_ref, wtm_ref, wfm_ref, wc_ref, tm_ref, fm_ref, ft_ref, c_ref):
    h = _rms(x_ref[...], g_ref[0]).astype(jnp.bfloat16)
    tm_ref[...] = jnp.dot(h, wtm_ref[0], preferred_element_type=jnp.float32).astype(jnp.bfloat16)
    fm = lax.dot_general(wfm_ref[0], h, (((0,), (1,)), ((), ())),
                         preferred_element_type=jnp.float32)
    fm_ref[...] = fm[:N_FM].astype(jnp.bfloat16)
    ft_ref[...] = fm[N_FM:]
    for c in range(N_C // 256):
        pc = jnp.dot(h, wc_ref[0, :, c * 256:(c + 1) * 256], preferred_element_type=jnp.float32)
        c_ref[2 * c] = pc[:, :LANES]
        c_ref[2 * c + 1] = pc[:, LANES:]


def _in_proj(x2, norm_g, w_tm, w_fm, w_c, layer, *, tm=512):
    n_tok = x2.shape[0]
    return pl.pallas_call(
        _in_proj_kernel,
        out_shape=(jax.ShapeDtypeStruct((n_tok, N_TM), jnp.bfloat16),
                   jax.ShapeDtypeStruct((N_FM, n_tok), jnp.bfloat16),
                   jax.ShapeDtypeStruct((F_ROWS, n_tok), jnp.float32),
                   jax.ShapeDtypeStruct((N_C_SLABS, n_tok, LANES), jnp.float32)),
        grid=(n_tok // tm,),
        in_specs=[
            pl.BlockSpec((tm, D_MODEL), lambda i: (i, 0)),
            pl.BlockSpec((1, 1, D_MODEL), lambda i: (layer, 0, 0)),
            pl.BlockSpec((1, D_MODEL, N_TM), lambda i: (layer, 0, 0)),
            pl.BlockSpec((1, D_MODEL, N_FM + F_ROWS), lambda i: (layer, 0, 0)),
            pl.BlockSpec((1, D_MODEL, N_C), lambda i: (layer, 0, 0)),
        ],
        out_specs=(pl.BlockSpec((tm, N_TM), lambda i: (i, 0)),
                   pl.BlockSpec((N_FM, tm), lambda i: (0, i)),
                   pl.BlockSpec((F_ROWS, tm), lambda i: (0, i)),
                   pl.BlockSpec((N_C_SLABS, tm, LANES), lambda i: (0, i, 0))),
        compiler_params=pltpu.CompilerParams(dimension_semantics=("parallel",),
                                             vmem_limit_bytes=VMEM_LIMIT),
        name="in_proj",
    )(x2, norm_g, w_tm, w_fm, w_c)


def _decay_kernel(ft_ref, bf_ref, ckp_ref):
    z = ft_ref[0:B_HEADS, :] + bf_ref[0]
    x = jnp.minimum(z, 0.0) - jnp.log1p(jnp.exp(-jnp.abs(z)))
    seq = x.shape[1]
    pos = lax.broadcasted_iota(jnp.int32, x.shape, 1)
    shift = 1
    while shift < seq:
        x = x + jnp.where(pos >= shift, pltpu.roll(x, shift, 1), 0.0)
        shift *= 2
    c = x * (-LOG2E)
    hi = c.astype(jnp.bfloat16).astype(jnp.float32)
    mid = (c - hi).astype(jnp.bfloat16).astype(jnp.float32)
    lo = c - hi - mid
    stacked = jnp.concatenate(
        [hi, mid, lo, jnp.zeros((LANES - 3 * B_HEADS, seq), jnp.float32)], axis=0)
    ckp_ref[0] = jnp.transpose(stacked).astype(jnp.bfloat16)


def _decay(ft, b_f, layer, bsz, seq):
    return pl.pallas_call(
        _decay_kernel,
        out_shape=jax.ShapeDtypeStruct((bsz, seq, LANES), jnp.bfloat16),
        grid=(bsz,),
        in_specs=[pl.BlockSpec((F_ROWS, seq), lambda b: (0, b)),
                  pl.BlockSpec((1, B_HEADS, 1), lambda b: (layer, 0, 0))],
        out_specs=pl.BlockSpec((1, seq, LANES), lambda b: (b, 0, 0)),
        compiler_params=pltpu.CompilerParams(dimension_semantics=("parallel",)),
        name="decay_scan",
    )(ft, b_f)


def _colmax(s):
    rows = s.shape[0]
    while rows > 8:
        rows //= 2
        s = jnp.maximum(s[:rows], s[rows:2 * rows])
    return jnp.max(s, axis=0, keepdims=True)


def _online_step(s, smax, v_aug, m_ref, acc_ref, idx):
    m_old = m_ref[idx]
    m_new = jnp.maximum(m_old, smax)
    alpha = jnp.exp2(m_old - m_new)
    p = jnp.exp2(s - m_new).astype(jnp.bfloat16)
    acc_ref[idx] = alpha * acc_ref[idx] + jnp.dot(v_aug, p, preferred_element_type=jnp.float32)
    m_ref[idx] = m_new


def _v_aug(vt_ref, head, k0, tile):
    v = vt_ref[head * HEAD_DIM:(head + 1) * HEAD_DIM, pl.ds(k0, tile)]
    return jnp.concatenate([v, jnp.ones((ONES_ROWS, tile), jnp.bfloat16)], axis=0)


def _flash_pipeline(qi, groups, produce, consume, finish):
    def overlapped(pk, pslot, ck, cslot):
        for grp in groups:
            produce(pk, pslot, grp)
            consume(ck, cslot, False, grp)

    def consume_all(k, slot):
        for grp in groups:
            consume(k, slot, True, grp)

    for grp in groups:
        produce(0, 0, grp)

    def pair(k):
        overlapped(k + 1, 1, k, 0)
        overlapped(k + 2, 0, k + 1, 1)

    def pairs(j, carry):
        for u in range(FLASH_PAIRS):
            pair(2 * (FLASH_PAIRS * j + u))
        return carry

    n_pairs = qi // 2
    lax.fori_loop(0, n_pairs // FLASH_PAIRS, pairs, 0)
    for u in range(FLASH_PAIRS - 1):
        done = (n_pairs // FLASH_PAIRS) * FLASH_PAIRS + u
        pl.when(done < n_pairs)(functools.partial(pair, 2 * done))

    @pl.when(qi % 2 == 0)
    def _():
        consume_all(qi, 0)
        finish()

    @pl.when(qi % 2 == 1)
    def _():
        overlapped(qi, 1, qi - 1, 0)
        consume_all(qi, 1)
        finish()


def _mixer_a_stages(qi, qt_ref, k_ref, vt_ref, bias_ref, lamv_ref, lami_ref, subg_ref, o_ref,
                    m_ref, acc_ref, s_ref, smax_ref, tile, n_near):
    n_sub = 2 * A_HEADS
    qt = qt_ref[...].astype(jnp.float32)
    row = lax.broadcasted_iota(jnp.int32, (LANES, tile), 0)
    qs = []
    for s in range(n_sub):
        blk = qt[(s // 4) * LANES:(s // 4 + 1) * LANES]
        lo = (s % 4) * A_QK_DIM
        qs.append(jnp.where((row >= lo) & (row < lo + A_QK_DIM), blk, 0.0).astype(jnp.bfloat16))
    m_ref[...] = jnp.full(m_ref.shape, NEG_INF, jnp.float32)
    acc_ref[...] = jnp.zeros(acc_ref.shape, jnp.float32)

    def produce(ki, slot, subs):
        k0 = pl.multiple_of(ki * tile, tile)
        delta = jnp.minimum(qi - ki, n_near)
        for s in subs:
            k = k_ref[0, pl.ds(k0, tile), (s // 4) * LANES:(s // 4 + 1) * LANES]
            x = jnp.dot(k, qs[s], preferred_element_type=jnp.float32) + bias_ref[s // 2, delta]
            s_ref[slot, s] = x
            smax_ref[slot, s] = _colmax(x)

    def consume(ki, slot, last, subs):
        k0 = pl.multiple_of(ki * tile, tile)
        for s in subs:
            _online_step(s_ref[slot, s], smax_ref[slot, s], _v_aug(vt_ref, s // 2, k0, tile),
                         m_ref, acc_ref, s)

    def finish():
        lam_init = lami_ref[0]
        lv = lamv_ref[0]
        e1 = jnp.exp(jnp.sum(lv[0:1] * lv[1:2], axis=1, keepdims=True))
        e2 = jnp.exp(jnp.sum(lv[2:3] * lv[3:4], axis=1, keepdims=True))
        lam = e1 - e2 + lam_init
        heads = []
        for h in range(A_HEADS):
            a0, a1 = acc_ref[2 * h], acc_ref[2 * h + 1]
            o = (a0[0:HEAD_DIM] / a0[HEAD_DIM:HEAD_DIM + 1]
                 - lam * (a1[0:HEAD_DIM] / a1[HEAD_DIM:HEAD_DIM + 1]))
            ms = jnp.mean(o * o, axis=0, keepdims=True)
            heads.append(o * lax.rsqrt(ms + NORM_EPS) * subg_ref[0] * (1.0 - lam_init))
        o_ref[0] = jnp.transpose(jnp.concatenate(heads, axis=0)).astype(o_ref.dtype)

    return produce, consume, finish


def _mixer_b_stages(qi, qt_ref, k_ref, ckp_ref, vt_ref, o_ref, m_ref, acc_ref, s_ref, smax_ref,
                    tile):
    n_h = B_HEADS
    qt = qt_ref[...].astype(jnp.float32)
    row = lax.broadcasted_iota(jnp.int32, (LANES, tile), 0)
    q_aug = []
    for s in range(n_h):
        blk = qt[(s // 2) * LANES:(s // 2 + 1) * LANES]
        lo = (s % 2) * HEAD_DIM
        q_s = jnp.where((row >= lo) & (row < lo + HEAD_DIM), blk, 0.0)
        pick = (row < 3 * B_HEADS) & ((row % B_HEADS) == s)
        ones = jnp.where(pick, 1.0, 0.0)
        q_aug.append(jnp.concatenate([q_s, ones], axis=0).astype(jnp.bfloat16))
    m_ref[...] = jnp.full(m_ref.shape, NEG_INF, jnp.float32)
    acc_ref[...] = jnp.zeros(acc_ref.shape, jnp.float32)

    def produce(ki, slot, subs):
        k0 = pl.multiple_of(ki * tile, tile)
        ck = ckp_ref[0, pl.ds(k0, tile), :]
        for s in subs:
            k = k_ref[0, pl.ds(k0, tile), (s // 2) * LANES:(s // 2 + 1) * LANES]
            x = jnp.dot(jnp.concatenate([k, ck], axis=1), q_aug[s],
                        preferred_element_type=jnp.float32)
            s_ref[slot, s] = x
            smax_ref[slot, s] = _colmax(x)

    def consume(ki, slot, last, subs):
        k0 = pl.multiple_of(ki * tile, tile)
        if last:
            kk = lax.broadcasted_iota(jnp.int32, (tile, tile), 0)
            qq = lax.broadcasted_iota(jnp.int32, (tile, tile), 1)
            keep = kk <= qq
        for s in subs:
            x, xmax = s_ref[slot, s], smax_ref[slot, s]
            if last:
                x = jnp.where(keep, x, NEG_INF)
                xmax = _colmax(x)
            _online_step(x, xmax, _v_aug(vt_ref, s, k0, tile), m_ref, acc_ref, s)

    def finish():
        heads = [acc_ref[s][0:HEAD_DIM] / acc_ref[s][HEAD_DIM:HEAD_DIM + 1] for s in range(n_h)]
        o_ref[0] = jnp.transpose(jnp.concatenate(heads, axis=0)).astype(o_ref.dtype)

    return produce, consume, finish


def _attn_ab_kernel(qa_ref, ka_ref, va_ref, bias_ref, lamv_ref, lami_ref, subg_ref,
                    qb_ref, kb_ref, ckp_ref, vb_ref, oa_ref, ob_ref,
                    ma_ref, acca_ref, sa_ref, smaxa_ref, mb_ref, accb_ref, sb_ref, smaxb_ref, *,
                    tile, n_near):
    qi = pl.program_id(1)
    stages = (_mixer_a_stages(qi, qa_ref, ka_ref, va_ref, bias_ref, lamv_ref, lami_ref, subg_ref,
                              oa_ref, ma_ref, acca_ref, sa_ref, smaxa_ref, tile, n_near),
              _mixer_b_stages(qi, qb_ref, kb_ref, ckp_ref, vb_ref, ob_ref, mb_ref, accb_ref,
                              sb_ref, smaxb_ref, tile))
    groups = []
    for g in range(2 * A_HEADS // A_GROUP):
        groups.append((0, range(g * A_GROUP, (g + 1) * A_GROUP)))
        b_per = B_HEADS * A_GROUP // (2 * A_HEADS)
        for b0 in range(g * b_per, (g + 1) * b_per, B_GROUP):
            groups.append((1, range(b0, b0 + B_GROUP)))

    def finish():
        stages[0][2]()
        stages[1][2]()

    _flash_pipeline(qi, groups,
                    lambda k, slot, grp: stages[grp[0]][0](k, slot, grp[1]),
                    lambda k, slot, last, grp: stages[grp[0]][1](k, slot, last, grp[1]),
                    finish)


def _attn_ab(proj_tm3, proj_fm, bias_a, lam_vecs, lam_init, sub_g, ckp, layer):
    bsz, seq, _ = proj_tm3.shape
    tile = FLASH_TILE
    n_near = _n_near_tiles(tile)
    n_q = seq // tile
    n_a, n_b = 2 * A_HEADS, B_HEADS
    fm_blk = lambda w, off, whole: pl.BlockSpec(
        (w, seq if whole else tile),
        (lambda b, qi: (off // w, b)) if whole else (lambda b, qi: (off // w, b * n_q + qi)))
    tm_blk = lambda w, off: pl.BlockSpec((1, seq, w), lambda b, qi: (b, 0, off // w))
    out_blk = lambda w: pl.BlockSpec((1, tile, w), lambda b, qi: (b, qi, 0))
    state = lambda n: [pltpu.VMEM((n, 1, tile), jnp.float32),
                       pltpu.VMEM((n, ACC_ROWS, tile), jnp.float32),
                       pltpu.VMEM((2, n, tile, tile), jnp.float32),
                       pltpu.VMEM((2, n, 1, tile), jnp.float32)]
    return pl.pallas_call(
        functools.partial(_attn_ab_kernel, tile=tile, n_near=n_near),
        out_shape=(jax.ShapeDtypeStruct((bsz, seq, A_W), jnp.bfloat16),
                   jax.ShapeDtypeStruct((bsz, seq, B_W), jnp.bfloat16)),
        grid=(bsz, n_q),
        in_specs=[
            fm_blk(A_W, QA_T, False), tm_blk(A_W, KA), fm_blk(A_W, VA_T, True),
            pl.BlockSpec((A_HEADS, n_near + 1, tile, tile), lambda b, qi: (0, 0, 0, 0),
                         pipeline_mode=pl.Buffered(1)),
            pl.BlockSpec((1, 4, A_QK_DIM), lambda b, qi: (layer, 0, 0)),
            pl.BlockSpec(memory_space=pltpu.SMEM),
            pl.BlockSpec((1, HEAD_DIM, 1), lambda b, qi: (layer, 0, 0)),
            fm_blk(B_W, QB_T, False), tm_blk(B_W, KB),
            pl.BlockSpec((1, seq, LANES), lambda b, qi: (b, 0, 0)),
            fm_blk(B_W, VB_T, True),
        ],
        out_specs=(out_blk(A_W), out_blk(B_W)),
        scratch_shapes=state(n_a) + state(n_b),
        compiler_params=pltpu.CompilerParams(
            dimension_semantics=("parallel", "parallel"),
            vmem_limit_bytes=VMEM_LIMIT),
        name="attn_ab",
    )(proj_fm, proj_tm3, proj_fm, bias_a, lam_vecs, lam_init, sub_g,
      proj_fm, proj_tm3, ckp, proj_fm)


def _attn_c_kernel(q_ref, k_ref, v_ref, bias_ref, o_ref, qd_ref, kd_ref, vtd_ref, onat_ref,
                   lnat_ref, tq_ref, tk_ref, tv_ref, *, seq):
    g = pl.program_id(2)
    n_chunks = seq // C_TILE
    lane = lax.broadcasted_iota(jnp.int32, (1, LANES), 1)
    first = lane < HEAD_DIM

    def run_group(group, dil):
        n_tiles = seq // dil // C_TILE

        def out_rows(i):
            r, t = i // n_tiles, i % n_tiles
            return pl.ds(r + t * (C_TILE * dil), C_TILE, stride=dil)

        if dil > C_SPLIT:
            per = seq // C_SPLIT // C_TILE

            def presplit(j, carry):
                for u in range(C_UNROLL):
                    i = j * C_UNROLL + u
                    rows = pl.ds(i // per + (i % per) * (C_TILE * C_SPLIT), C_TILE, stride=C_SPLIT)
                    dst = pl.ds(pl.multiple_of(i * C_TILE, C_TILE), C_TILE)
                    for src, tmp in ((q_ref, tq_ref), (k_ref, tk_ref), (v_ref, tv_ref)):
                        tmp[dst, :] = src[rows, :]
                return carry

            lax.fori_loop(0, n_chunks // C_UNROLL, presplit, 0)
            srcs, step = (tq_ref, tk_ref, tv_ref), dil // C_SPLIT

            def src_rows(i):
                r, t = i // n_tiles, i % n_tiles
                base = (r % C_SPLIT) * (seq // C_SPLIT) + r // C_SPLIT + t * (C_TILE * step)
                return pl.ds(base, C_TILE, stride=step)
        else:
            srcs, src_rows = (q_ref, k_ref, v_ref), out_rows

        kd_ref[0:C_TILE, :] = jnp.zeros((C_TILE, LANES), jnp.bfloat16)
        vtd_ref[:, 0:C_TILE] = jnp.zeros((LANES, C_TILE), jnp.bfloat16)

        def gather(j, carry):
            ids = [j * C_UNROLL + u for u in range(C_UNROLL)]
            vts = [jnp.transpose(srcs[2][src_rows(i), :]) for i in ids]
            for i, vt in zip(ids, vts):
                rows = src_rows(i)
                dst = pl.multiple_of(i * C_TILE, C_TILE)
                qd_ref[pl.ds(dst, C_TILE), :] = srcs[0][rows, :].astype(jnp.bfloat16)
                kd_ref[pl.ds(dst + C_TILE, C_TILE), :] = srcs[1][rows, :].astype(jnp.bfloat16)
                vtd_ref[:, pl.ds(dst + C_TILE, C_TILE)] = vt.astype(jnp.bfloat16)
            return carry

        lax.fori_loop(0, n_chunks // C_UNROLL, gather, 0)

        def tiles(j, carry):
            ids = [j * C_UNROLL + u for u in range(C_UNROLL)]
            row0s = [pl.multiple_of(i * C_TILE, C_TILE) for i in ids]

            def scores(u):
                q = qd_ref[pl.ds(row0s[u], C_TILE), :]
                q2 = jnp.concatenate([jnp.where(first, q, jnp.zeros_like(q)),
                                      jnp.where(first, jnp.zeros_like(q), q)], axis=0)
                return _dot_nt(kd_ref[pl.ds(row0s[u], 2 * C_TILE), :], q2)

            def softmax_pv(u, st):
                ver = jnp.minimum(ids[u] % n_tiles, 1)
                res = []
                for h in range(2):
                    x = st[:, h * C_TILE:(h + 1) * C_TILE] + bias_ref[h, ver]
                    m = _colmax(x)
                    p = jnp.exp2(x - m).astype(jnp.bfloat16)
                    v_aug = jnp.concatenate(
                        [vtd_ref[h * HEAD_DIM:(h + 1) * HEAD_DIM, pl.ds(row0s[u], 2 * C_TILE)],
                         jnp.ones((ONES_ROWS, 2 * C_TILE), jnp.bfloat16)], axis=0)
                    res.append((m, jnp.dot(v_aug, p, preferred_element_type=jnp.float32)))
                return res

            def write_back(u, res):
                outs, lses = [], []
                for m, pv in res:
                    l = pv[HEAD_DIM:HEAD_DIM + 1]
                    outs.append(pv[0:HEAD_DIM] / l)
                    lses.append(jnp.broadcast_to(m + jnp.log2(l), (HEAD_DIM, C_TILE)))
                rows = out_rows(ids[u])
                onat_ref[group, rows, :] = jnp.transpose(jnp.concatenate(outs, axis=0))
                lnat_ref[group, rows, :] = jnp.transpose(jnp.concatenate(lses, axis=0))

            sts = {u: scores(u) for u in range(C_AHEAD)}
            pending = None
            for u in range(C_UNROLL):
                if u + C_AHEAD < C_UNROLL:
                    sts[u + C_AHEAD] = scores(u + C_AHEAD)
                res = softmax_pv(u, sts.pop(u))
                if pending is not None:
                    write_back(*pending)
                pending = (u, res)
            write_back(*pending)
            return carry

        lax.fori_loop(0, n_chunks // C_UNROLL, tiles, 0)

    for group, (_, dil) in enumerate(C_PAIRS):
        pl.when(g == group)(functools.partial(run_group, group, dil))

    @pl.when(g == C_GROUPS - 1)
    def _():
        rows_per = 512

        def combine(c, carry):
            rows = pl.ds(pl.multiple_of(c * rows_per, rows_per), rows_per)
            l0, l1, l2 = lnat_ref[0, rows, :], lnat_ref[1, rows, :], lnat_ref[2, rows, :]
            m = jnp.maximum(jnp.maximum(l0, l1), l2)
            e0, e1, e2 = jnp.exp2(l0 - m), jnp.exp2(l1 - m), jnp.exp2(l2 - m)
            num = e0 * onat_ref[0, rows, :] + e1 * onat_ref[1, rows, :] + e2 * onat_ref[2, rows, :]
            o_ref[0, rows, :] = (num / (e0 + e1 + e2)).astype(o_ref.dtype)
            return carry

        lax.fori_loop(0, seq // rows_per, combine, 0)


def _attn_c(proj_c, bias_c, bsz, seq):
    assert seq % (C_TILE * C_UNROLL) == 0 and seq % (C_TILE * max(d for _, d in C_PAIRS)) == 0
    slab = lambda role: pl.BlockSpec(
        (None, seq, LANES), lambda b, hp, g: (role * 2 * C_GROUPS + g * 2 + hp, b, 0))
    return pl.pallas_call(
        functools.partial(_attn_c_kernel, seq=seq),
        out_shape=jax.ShapeDtypeStruct((bsz, seq, C_W), jnp.bfloat16),
        grid=(bsz, C_HEADS // 2, C_GROUPS),
        in_specs=[slab(0), slab(1), slab(2),
                  pl.BlockSpec((2, 2, 2 * C_TILE, C_TILE), lambda b, hp, g: (g * 2 + hp, 0, 0, 0))],
        out_specs=pl.BlockSpec((1, seq, LANES), lambda b, hp, g: (b, 0, hp)),
        scratch_shapes=[pltpu.VMEM((seq, LANES), jnp.bfloat16),
                        pltpu.VMEM((seq + C_TILE, LANES), jnp.bfloat16),
                        pltpu.VMEM((LANES, seq + C_TILE), jnp.bfloat16),
                        pltpu.VMEM((C_GROUPS, seq, LANES), jnp.float32),
                        pltpu.VMEM((C_GROUPS, seq, LANES), jnp.float32)]
        + [pltpu.VMEM((seq, LANES), jnp.float32)] * 3,
        compiler_params=pltpu.CompilerParams(
            dimension_semantics=("parallel", "parallel", "arbitrary"),
            vmem_limit_bytes=VMEM_LIMIT),
        name="attn_c",
    )(proj_c, proj_c, proj_c, bias_c)


def _mix_mlp_kernel(x_ref, oa_ref, ob_ref, oc_ref, wo_ref, g_ref, w1_ref, w2_ref, fg_ref, out_ref, *,
                    final, tf):
    mixed = jnp.concatenate([oa_ref[...], ob_ref[...], oc_ref[...]], axis=1)
    acc = x_ref[...] + jnp.dot(mixed, wo_ref[0], preferred_element_type=jnp.float32)
    h = _rms(acc, g_ref[0]).astype(jnp.bfloat16)
    for c in range(D_FF // tf):
        u = jnp.dot(h, w1_ref[0, :, c * tf:(c + 1) * tf], preferred_element_type=jnp.float32)
        u = jnp.square(jnp.maximum(u, 0.0)).astype(jnp.bfloat16)
        acc = acc + jnp.dot(u, w2_ref[0, c * tf:(c + 1) * tf, :],
                            preferred_element_type=jnp.float32)
    out_ref[...] = _rms(acc, fg_ref[...]) if final else acc


def _mix_mlp(x2, oa, ob, oc, w_o, norm_g, w_1, w_2, final_g, layer, final, *, tm=MLP_TM, tf=MLP_TF):
    n_tok = x2.shape[0]
    row = lambda w: pl.BlockSpec((tm, w), lambda i: (i, 0))
    resident = lambda shape: pl.BlockSpec(shape, lambda i: (layer, 0, 0),
                                          pipeline_mode=pl.Buffered(1))
    return pl.pallas_call(
        functools.partial(_mix_mlp_kernel, final=final, tf=tf),
        out_shape=jax.ShapeDtypeStruct((n_tok, D_MODEL), jnp.float32),
        grid=(n_tok // tm,),
        in_specs=[row(D_MODEL), row(A_W), row(B_W), row(C_W),
                  resident((1, D_MODEL, D_MODEL)),
                  pl.BlockSpec((1, 1, D_MODEL), lambda i: (layer, 0, 0)),
                  resident((1, D_MODEL, D_FF)),
                  resident((1, D_FF, D_MODEL)),
                  pl.BlockSpec((1, D_MODEL), lambda i: (0, 0))],
        out_specs=row(D_MODEL),
        compiler_params=pltpu.CompilerParams(dimension_semantics=("parallel",),
                                             vmem_limit_bytes=VMEM_LIMIT),
        name="mix_mlp",
    )(x2, oa, ob, oc, w_o, norm_g, w_1, w_2, final_g)


def _prep_w_in(w_in):
    a0, b0, f0, c0 = 0, 3 * A_W, 3 * A_W + 3 * B_W, 3 * A_W + 3 * B_W + B_HEADS
    sa, sb = A_QK_DIM ** -0.5 * LOG2E, HEAD_DIM ** -0.5 * LOG2E
    cw = C_GROUPS * C_W
    w_tm = jnp.concatenate([w_in[:, :, b0 + B_W:b0 + 2 * B_W],
                            w_in[:, :, a0 + A_W:a0 + 2 * A_W]], axis=-1).astype(jnp.bfloat16)
    w_fm = jnp.concatenate([
        w_in[:, :, b0:b0 + B_W] * sb, w_in[:, :, b0 + 2 * B_W:f0],
        w_in[:, :, a0:a0 + A_W] * sa, w_in[:, :, a0 + 2 * A_W:b0],
    ], axis=-1)
    w_fm = jnp.pad(jnp.concatenate([w_fm, w_in[:, :, f0:c0]], axis=-1),
                   ((0, 0), (0, 0), (0, F_ROWS - B_HEADS)))
    w_fm = w_fm.astype(jnp.bfloat16)
    w_c = jnp.concatenate([w_in[:, :, c0:c0 + cw] * sb, w_in[:, :, c0 + cw:]],
                          axis=-1).astype(jnp.bfloat16)
    return w_tm, w_fm, w_c


def kernel(x, norm1_g, w_in, b_f, lam_q1, lam_k1, lam_q2, lam_k2, diff_norm_g, w_o, norm2_g,
           w_1, w_2, rel_bias, final_g):
    bsz, seq, _ = x.shape
    depth = w_in.shape[0]
    w_tm, w_fm, w_c = _prep_w_in(w_in)
    w_o16, w_116, w_216 = (w.astype(jnp.bfloat16) for w in (w_o, w_1, w_2))
    lam_vecs = jnp.stack([lam_q1, lam_k1, lam_q2, lam_k2], axis=1)
    bias_a, bias_c = _make_bias_tiles(rel_bias)

    x2 = x.reshape(bsz * seq, D_MODEL)
    for l in range(depth):
        proj_tm, proj_fm, ft, proj_c = _in_proj(x2, norm1_g[:, None, :], w_tm, w_fm, w_c, l)
        proj3 = proj_tm.reshape(bsz, seq, N_TM)
        ckp = _decay(ft, b_f[:, :, None], l, bsz, seq)
        lam_init = jnp.full((1,), 0.8 - 0.6 * math.exp(-0.3 * l), jnp.float32)
        oa, ob = _attn_ab(proj3, proj_fm, bias_a, lam_vecs, lam_init, diff_norm_g[:, :, None],
                          ckp, l)
        oc = _attn_c(proj_c, bias_c, bsz, seq)
        x2 = _mix_mlp(x2, oa.reshape(bsz * seq, A_W), ob.reshape(bsz * seq, B_W),
                      oc.reshape(bsz * seq, C_W), w_o16, norm2_g[:, None, :], w_116, w_216,
                      final_g[None, :], l, l == depth - 1)
    return x2.reshape(bsz, seq, D_MODEL)
```

```python
import functools
import math

import jax
import jax.numpy as jnp
from jax import lax
from jax.experimental import pallas as pl
from jax.experimental.pallas import tpu as pltpu

D_MODEL = 1024
HEAD_DIM = 64
A_HEADS = 4
A_QK_DIM = 32
B_HEADS = 8
C_HEADS = 4
C_PAIRS = ((128, 1), (512, 4), (2048, 16))
C_GROUPS = 3
A_W = 256
B_W = 512
C_W = 256
D_FF = 4096
N_BUCKETS = 32
MAX_DISTANCE = 2048
NORM_EPS = 1e-6
NEG_INF = -1e30
LOG2E = 1.4426950408889634

LANES = 128
VMEM_LIMIT = 56 * 1024 * 1024

KB, KA = 0, 512
N_TM = 768
QB_T, VB_T, QA_T, VA_T = 0, 512, 1024, 1280
N_FM = 1536
N_C = 2304
N_C_SLABS = N_C // 128
F_ROWS = 16
ONES_ROWS = 16
ACC_ROWS = HEAD_DIM + ONES_ROWS

C_TILE = 128
C_UNROLL = 32
C_AHEAD = 3
C_SPLIT = 4
FLASH_TILE = 256
MLP_TM = 512
MLP_TF = 1024
A_GROUP = 1
B_GROUP = 1
FLASH_PAIRS = 2


def _bucket_thresholds():
    max_exact = N_BUCKETS // 2
    out = []
    for k in range(1, N_BUCKETS - max_exact):
        t = max_exact * (MAX_DISTANCE / max_exact) ** (k / (N_BUCKETS - max_exact))
        out.append(int(math.ceil(t)))
    return tuple(out)


_THRESHOLDS = _bucket_thresholds()


def _n_near_tiles(tile):
    return -(-(_THRESHOLDS[-1] + tile - 1) // tile)


def _t5_bucket(d):
    big = jnp.full(d.shape, N_BUCKETS // 2, jnp.int32)
    for t in _THRESHOLDS:
        big = big + (d >= t).astype(jnp.int32)
    return jnp.where(d < N_BUCKETS // 2, d, big)


def _bias_lookup(rb_ref, bucket, col):
    val = jnp.zeros(bucket.shape, jnp.float32)
    for b in range(N_BUCKETS):
        val = jnp.where(bucket == b, rb_ref[b, col], val)
    return val


def _bias_a_kernel(rb_ref, out_ref, *, tile):
    delta = pl.program_id(0)
    j = lax.broadcasted_iota(jnp.int32, (tile, tile), 0)
    i = lax.broadcasted_iota(jnp.int32, (tile, tile), 1)
    d = delta * tile + i - j
    bucket = _t5_bucket(d)
    for h in range(A_HEADS):
        val = _bias_lookup(rb_ref, bucket, h) * LOG2E
        out_ref[h, 0] = jnp.where(d >= 0, val, NEG_INF)


def _bias_c_kernel(rb_ref, out_ref):
    g = pl.program_id(0)
    dil = jnp.where(g == 0, C_PAIRS[0][1], jnp.where(g == 1, C_PAIRS[1][1], C_PAIRS[2][1]))
    c = lax.broadcasted_iota(jnp.int32, (2 * C_TILE, C_TILE), 0)
    i = lax.broadcasted_iota(jnp.int32, (2 * C_TILE, C_TILE), 1)
    steps = i + C_TILE - c
    valid = (steps >= 0) & (steps <= C_TILE)
    bucket = _t5_bucket(steps * dil)
    for h in range(C_HEADS):
        val = _bias_lookup(rb_ref, bucket, A_HEADS + g * C_HEADS + h) * LOG2E
        out_ref[h, 1] = jnp.where(valid, val, NEG_INF)
        out_ref[h, 0] = jnp.where(valid & (c >= C_TILE), val, NEG_INF)


def _make_bias_tiles(rel_bias):
    tile = FLASH_TILE
    n_bias = _n_near_tiles(tile) + 1
    smem = pl.BlockSpec(memory_space=pltpu.SMEM)
    bias_a = pl.pallas_call(
        functools.partial(_bias_a_kernel, tile=tile),
        out_shape=jax.ShapeDtypeStruct((A_HEADS, n_bias, tile, tile), jnp.float32),
        grid=(n_bias,),
        in_specs=[smem],
        out_specs=pl.BlockSpec((A_HEADS, 1, tile, tile), lambda d: (0, d, 0, 0)),
        name="bias_a_tiles",
    )(rel_bias)
    bias_c = pl.pallas_call(
        _bias_c_kernel,
        out_shape=jax.ShapeDtypeStruct((C_GROUPS * C_HEADS, 2, 2 * C_TILE, C_TILE), jnp.float32),
        grid=(C_GROUPS,),
        in_specs=[smem],
        out_specs=pl.BlockSpec((C_HEADS, 2, 2 * C_TILE, C_TILE), lambda g: (g, 0, 0, 0)),
        name="bias_c_tiles",
    )(rel_bias)
    return bias_a, bias_c


def _rms(x, g):
    return x * lax.rsqrt(jnp.mean(x * x, axis=-1, keepdims=True) + NORM_EPS) * g


def _dot_nt(a, b):
    return lax.dot_general(a, b, (((1,), (1,)), ((), ())), preferred_element_type=jnp.float32)


def _in_proj_kernel(x_ref, g_ref, wtm_ref, wfm_ref, wc_ref, tm_ref, fm_ref, ft_ref, c_ref):
    h = _rms(x_ref[...], g_ref[0]).astype(jnp.bfloat16)
    tm_ref[...] = jnp.dot(h, wtm_ref[0], preferred_element_type=jnp.float32).astype(jnp.bfloat16)
    fm = lax.dot_general(wfm_ref[0], h, (((0,), (1,)), ((), ())),
                         preferred_element_type=jnp.float32)
    fm_ref[...] = fm[:N_FM].astype(jnp.bfloat16)
    ft_ref[...] = fm[N_FM:]
    for c in range(N_C // 256):
        pc = jnp.dot(h, wc_ref[0, :, c * 256:(c + 1) * 256], preferred_element_type=jnp.float32)
        c_ref[2 * c] = pc[:, :LANES]
        c_ref[2 * c + 1] = pc[:, LANES:]


def _in_proj(x2, norm_g, w_tm, w_fm, w_c, layer, *, tm=512):
    n_tok = x2.shape[0]
    return pl.pallas_call(
        _in_proj_kernel,
        out_shape=(jax.ShapeDtypeStruct((n_tok, N_TM), jnp.bfloat16),
                   jax.ShapeDtypeStruct((N_FM, n_tok), jnp.bfloat16),
                   jax.ShapeDtypeStruct((F_ROWS, n_tok), jnp.float32),
                   jax.ShapeDtypeStruct((N_C_SLABS, n_tok, LANES), jnp.float32)),
        grid=(n_tok // tm,),
        in_specs=[
            pl.BlockSpec((tm, D_MODEL), lambda i: (i, 0)),
            pl.BlockSpec((1, 1, D_MODEL), lambda i: (layer, 0, 0)),
            pl.BlockSpec((1, D_MODEL, N_TM), lambda i: (layer, 0, 0)),
            pl.BlockSpec((1, D_MODEL, N_FM + F_ROWS), lambda i: (layer, 0, 0)),
            pl.BlockSpec((1, D_MODEL, N_C), lambda i: (layer, 0, 0)),
        ],
        out_specs=(pl.BlockSpec((tm, N_TM), lambda i: (i, 0)),
                   pl.BlockSpec((N_FM, tm), lambda i: (0, i)),
                   pl.BlockSpec((F_ROWS, tm), lambda i: (0, i)),
                   pl.BlockSpec((N_C_SLABS, tm, LANES), lambda i: (0, i, 0))),
        compiler_params=pltpu.CompilerParams(dimension_semantics=("parallel",),
                                             vmem_limit_bytes=VMEM_LIMIT),
        name="in_proj",
    )(x2, norm_g, w_tm, w_fm, w_c)


def _decay_kernel(ft_ref, bf_ref, ckp_ref):
    z = ft_ref[0:B_HEADS, :] + bf_ref[0]
    x = jnp.minimum(z, 0.0) - jnp.log1p(jnp.exp(-jnp.abs(z)))
    seq = x.shape[1]
    pos = lax.broadcasted_iota(jnp.int32, x.shape, 1)
    shift = 1
    while shift < seq:
        x = x + jnp.where(pos >= shift, pltpu.roll(x, shift, 1), 0.0)
        shift *= 2
    c = x * (-LOG2E)
    hi = c.astype(jnp.bfloat16).astype(jnp.float32)
    mid = (c - hi).astype(jnp.bfloat16).astype(jnp.float32)
    lo = c - hi - mid
    stacked = jnp.concatenate(
        [hi, mid, lo, jnp.zeros((LANES - 3 * B_HEADS, seq), jnp.float32)], axis=0)
    ckp_ref[0] = jnp.transpose(stacked).astype(jnp.bfloat16)


def _decay(ft, b_f, layer, bsz, seq):
    return pl.pallas_call(
        _decay_kernel,
        out_shape=jax.ShapeDtypeStruct((bsz, seq, LANES), jnp.bfloat16),
        grid=(bsz,),
        in_specs=[pl.BlockSpec((F_ROWS, seq), lambda b: (0, b)),
                  pl.BlockSpec((1, B_HEADS, 1), lambda b: (layer, 0, 0))],
        out_specs=pl.BlockSpec((1, seq, LANES), lambda b: (b, 0, 0)),
        compiler_params=pltpu.CompilerParams(dimension_semantics=("parallel",)),
        name="decay_scan",
    )(ft, b_f)


def _colmax(s):
    rows = s.shape[0]
    while rows > 8:
        rows //= 2
        s = jnp.maximum(s[:rows], s[rows:2 * rows])
    return jnp.max(s, axis=0, keepdims=True)


def _online_step(s, smax, v_aug, m_ref, acc_ref, idx):
    m_old = m_ref[idx]
    m_new = jnp.maximum(m_old, smax)
    alpha = jnp.exp2(m_old - m_new)
    p = jnp.exp2(s - m_new).astype(jnp.bfloat16)
    acc_ref[idx] = alpha * acc_ref[idx] + jnp.dot(v_aug, p, preferred_element_type=jnp.float32)
    m_ref[idx] = m_new


def _v_aug(vt_ref, head, k0, tile):
    v = vt_ref[head * HEAD_DIM:(head + 1) * HEAD_DIM, pl.ds(k0, tile)]
    return jnp.concatenate([v, jnp.ones((ONES_ROWS, tile), jnp.bfloat16)], axis=0)


def _flash_pipeline(qi, groups, produce, consume, finish):
    def overlapped(pk, pslot, ck, cslot):
        for grp in groups:
            produce(pk, pslot, grp)
            consume(ck, cslot, False, grp)

    def consume_all(k, slot):
        for grp in groups:
            consume(k, slot, True, grp)

    for grp in groups:
        produce(0, 0, grp)

    def pair(k):
        overlapped(k + 1, 1, k, 0)
        overlapped(k + 2, 0, k + 1, 1)

    def pairs(j, carry):
        for u in range(FLASH_PAIRS):
            pair(2 * (FLASH_PAIRS * j + u))
        return carry

    n_pairs = qi // 2
    lax.fori_loop(0, n_pairs // FLASH_PAIRS, pairs, 0)
    for u in range(FLASH_PAIRS - 1):
        done = (n_pairs // FLASH_PAIRS) * FLASH_PAIRS + u
        pl.when(done < n_pairs)(functools.partial(pair, 2 * done))

    @pl.when(qi % 2 == 0)
    def _():
        consume_all(qi, 0)
        finish()

    @pl.when(qi % 2 == 1)
    def _():
        overlapped(qi, 1, qi - 1, 0)
        consume_all(qi, 1)
        finish()


def _mixer_a_stages(qi, qt_ref, k_ref, vt_ref, bias_ref, lamv_ref, lami_ref, subg_ref, o_ref,
                    m_ref, acc_ref, s_ref, smax_ref, tile, n_near):
    n_sub = 2 * A_HEADS
    qt = qt_ref[...].astype(jnp.float32)
    row = lax.broadcasted_iota(jnp.int32, (LANES, tile), 0)
    qs = []
    for s in range(n_sub):
        blk = qt[(s // 4) * LANES:(s // 4 + 1) * LANES]
        lo = (s % 4) * A_QK_DIM
        qs.append(jnp.where((row >= lo) & (row < lo + A_QK_DIM), blk, 0.0).astype(jnp.bfloat16))
    m_ref[...] = jnp.full(m_ref.shape, NEG_INF, jnp.float32)
    acc_ref[...] = jnp.zeros(acc_ref.shape, jnp.float32)

    def produce(ki, slot, subs):
        k0 = pl.multiple_of(ki * tile, tile)
        delta = jnp.minimum(qi - ki, n_near)
        for s in subs:
            k = k_ref[0, pl.ds(k0, tile), (s // 4) * LANES:(s // 4 + 1) * LANES]
            x = jnp.dot(k, qs[s], preferred_element_type=jnp.float32) + bias_ref[s // 2, delta]
            s_ref[slot, s] = x
            smax_ref[slot, s] = _colmax(x)

    def consume(ki, slot, last, subs):
        k0 = pl.multiple_of(ki * tile, tile)
        for s in subs:
            _online_step(s_ref[slot, s], smax_ref[slot, s], _v_aug(vt_ref, s // 2, k0, tile),
                         m_ref, acc_ref, s)

    def finish():
        lam_init = lami_ref[0]
        lv = lamv_ref[0]
        e1 = jnp.exp(jnp.sum(lv[0:1] * lv[1:2], axis=1, keepdims=True))
        e2 = jnp.exp(jnp.sum(lv[2:3] * lv[3:4], axis=1, keepdims=True))
        lam = e1 - e2 + lam_init
        heads = []
        for h in range(A_HEADS):
            a0, a1 = acc_ref[2 * h], acc_ref[2 * h + 1]
            o = (a0[0:HEAD_DIM] / a0[HEAD_DIM:HEAD_DIM + 1]
                 - lam * (a1[0:HEAD_DIM] / a1[HEAD_DIM:HEAD_DIM + 1]))
            ms = jnp.mean(o * o, axis=0, keepdims=True)
            heads.append(o * lax.rsqrt(ms + NORM_EPS) * subg_ref[0] * (1.0 - lam_init))
        o_ref[0] = jnp.transpose(jnp.concatenate(heads, axis=0)).astype(o_ref.dtype)

    return produce, consume, finish


def _mixer_b_stages(qi, qt_ref, k_ref, ckp_ref, vt_ref, o_ref, m_ref, acc_ref, s_ref, smax_ref,
                    tile):
    n_h = B_HEADS
    qt = qt_ref[...].astype(jnp.float32)
    row = lax.broadcasted_iota(jnp.int32, (LANES, tile), 0)
    q_aug = []
    for s in range(n_h):
        blk = qt[(s // 2) * LANES:(s // 2 + 1) * LANES]
        lo = (s % 2) * HEAD_DIM
        q_s = jnp.where((row >= lo) & (row < lo + HEAD_DIM), blk, 0.0)
        pick = (row < 3 * B_HEADS) & ((row % B_HEADS) == s)
        ones = jnp.where(pick, 1.0, 0.0)
        q_aug.append(jnp.concatenate([q_s, ones], axis=0).astype(jnp.bfloat16))
    m_ref[...] = jnp.full(m_ref.shape, NEG_INF, jnp.float32)
    acc_ref[...] = jnp.zeros(acc_ref.shape, jnp.float32)

    def produce(ki, slot, subs):
        k0 = pl.multiple_of(ki * tile, tile)
        ck = ckp_ref[0, pl.ds(k0, tile), :]
        for s in subs:
            k = k_ref[0, pl.ds(k0, tile), (s // 2) * LANES:(s // 2 + 1) * LANES]
            x = jnp.dot(jnp.concatenate([k, ck], axis=1), q_aug[s],
                        preferred_element_type=jnp.float32)
            s_ref[slot, s] = x
            smax_ref[slot, s] = _colmax(x)

    def consume(ki, slot, last, subs):
        k0 = pl.multiple_of(ki * tile, tile)
        if last:
            kk = lax.broadcasted_iota(jnp.int32, (tile, tile), 0)
            qq = lax.broadcasted_iota(jnp.int32, (tile, tile), 1)
            keep = kk <= qq
        for s in subs:
            x, xmax = s_ref[slot, s], smax_ref[slot, s]
            if last:
                x = jnp.where(keep, x, NEG_INF)
                xmax = _colmax(x)
            _online_step(x, xmax, _v_aug(vt_ref, s, k0, tile), m_ref, acc_ref, s)

    def finish():
        heads = [acc_ref[s][0:HEAD_DIM] / acc_ref[s][HEAD_DIM:HEAD_DIM + 1] for s in range(n_h)]
        o_ref[0] = jnp.transpose(jnp.concatenate(heads, axis=0)).astype(o_ref.dtype)

    return produce, consume, finish


def _attn_ab_kernel(qa_ref, ka_ref, va_ref, bias_ref, lamv_ref, lami_ref, subg_ref,
                    qb_ref, kb_ref, ckp_ref, vb_ref, oa_ref, ob_ref,
                    ma_ref, acca_ref, sa_ref, smaxa_ref, mb_ref, accb_ref, sb_ref, smaxb_ref, *,
                    tile, n_near):
    qi = pl.program_id(1)
    stages = (_mixer_a_stages(qi, qa_ref, ka_ref, va_ref, bias_ref, lamv_ref, lami_ref, subg_ref,
                              oa_ref, ma_ref, acca_ref, sa_ref, smaxa_ref, tile, n_near),
              _mixer_b_stages(qi, qb_ref, kb_ref, ckp_ref, vb_ref, ob_ref, mb_ref, accb_ref,
                              sb_ref, smaxb_ref, tile))
    groups = []
    for g in range(2 * A_HEADS // A_GROUP):
        groups.append((0, range(g * A_GROUP, (g + 1) * A_GROUP)))
        b_per = B_HEADS * A_GROUP // (2 * A_HEADS)
        for b0 in range(g * b_per, (g + 1) * b_per, B_GROUP):
            groups.append((1, range(b0, b0 + B_GROUP)))

    def finish():
        stages[0][2]()
        stages[1][2]()

    _flash_pipeline(qi, groups,
                    lambda k, slot, grp: stages[grp[0]][0](k, slot, grp[1]),
                    lambda k, slot, last, grp: stages[grp[0]][1](k, slot, last, grp[1]),
                    finish)


def _attn_ab(proj_tm3, proj_fm, bias_a, lam_vecs, lam_init, sub_g, ckp, layer):
    bsz, seq, _ = proj_tm3.shape
    tile = FLASH_TILE
    n_near = _n_near_tiles(tile)
    n_q = seq // tile
    n_a, n_b = 2 * A_HEADS, B_HEADS
    fm_blk = lambda w, off, whole: pl.BlockSpec(
        (w, seq if whole else tile),
        (lambda b, qi: (off // w, b)) if whole else (lambda b, qi: (off // w, b * n_q + qi)))
    tm_blk = lambda w, off: pl.BlockSpec((1, seq, w), lambda b, qi: (b, 0, off // w))
    out_blk = lambda w: pl.BlockSpec((1, tile, w), lambda b, qi: (b, qi, 0))
    state = lambda n: [pltpu.VMEM((n, 1, tile), jnp.float32),
                       pltpu.VMEM((n, ACC_ROWS, tile), jnp.float32),
                       pltpu.VMEM((2, n, tile, tile), jnp.float32),
                       pltpu.VMEM((2, n, 1, tile), jnp.float32)]
    return pl.pallas_call(
        functools.partial(_attn_ab_kernel, tile=tile, n_near=n_near),
        out_shape=(jax.ShapeDtypeStruct((bsz, seq, A_W), jnp.bfloat16),
                   jax.ShapeDtypeStruct((bsz, seq, B_W), jnp.bfloat16)),
        grid=(bsz, n_q),
        in_specs=[
            fm_blk(A_W, QA_T, False), tm_blk(A_W, KA), fm_blk(A_W, VA_T, True),
            pl.BlockSpec((A_HEADS, n_near + 1, tile, tile), lambda b, qi: (0, 0, 0, 0),
                         pipeline_mode=pl.Buffered(1)),
            pl.BlockSpec((1, 4, A_QK_DIM), lambda b, qi: (layer, 0, 0)),
            pl.BlockSpec(memory_space=pltpu.SMEM),
            pl.BlockSpec((1, HEAD_DIM, 1), lambda b, qi: (layer, 0, 0)),
            fm_blk(B_W, QB_T, False), tm_blk(B_W, KB),
            pl.BlockSpec((1, seq, LANES), lambda b, qi: (b, 0, 0)),
            fm_blk(B_W, VB_T, True),
        ],
        out_specs=(out_blk(A_W), out_blk(B_W)),
        scratch_shapes=state(n_a) + state(n_b),
        compiler_params=pltpu.CompilerParams(
            dimension_semantics=("parallel", "parallel"),
            vmem_limit_bytes=VMEM_LIMIT),
        name="attn_ab",
    )(proj_fm, proj_tm3, proj_fm, bias_a, lam_vecs, lam_init, sub_g,
      proj_fm, proj_tm3, ckp, proj_fm)


def _attn_c_kernel(q_ref, k_ref, v_ref, bias_ref, o_ref, qd_ref, kd_ref, vtd_ref, onat_ref,
                   lnat_ref, tq_ref, tk_ref, tv_ref, *, seq):
    g = pl.program_id(2)
    n_chunks = seq // C_TILE
    lane = lax.broadcasted_iota(jnp.int32, (1, LANES), 1)
    first = lane < HEAD_DIM

    def run_group(group, dil):
        n_tiles = seq // dil // C_TILE

        def out_rows(i):
            r, t = i // n_tiles, i % n_tiles
            return pl.ds(r + t * (C_TILE * dil), C_TILE, stride=dil)

        if dil > C_SPLIT:
            per = seq // C_SPLIT // C_TILE

            def presplit(j, carry):
                for u in range(C_UNROLL):
                    i = j * C_UNROLL + u
                    rows = pl.ds(i // per + (i % per) * (C_TILE * C_SPLIT), C_TILE, stride=C_SPLIT)
                    dst = pl.ds(pl.multiple_of(i * C_TILE, C_TILE), C_TILE)
                    for src, tmp in ((q_ref, tq_ref), (k_ref, tk_ref), (v_ref, tv_ref)):
                        tmp[dst, :] = src[rows, :]
                return carry

            lax.fori_loop(0, n_chunks // C_UNROLL, presplit, 0)
            srcs, step = (tq_ref, tk_ref, tv_ref), dil // C_SPLIT

            def src_rows(i):
                r, t = i // n_tiles, i % n_tiles
                base = (r % C_SPLIT) * (seq // C_SPLIT) + r // C_SPLIT + t * (C_TILE * step)
                return pl.ds(base, C_TILE, stride=step)
        else:
            srcs, src_rows = (q_ref, k_ref, v_ref), out_rows

        kd_ref[0:C_TILE, :] = jnp.zeros((C_TILE, LANES), jnp.bfloat16)
        vtd_ref[:, 0:C_TILE] = jnp.zeros((LANES, C_TILE), jnp.bfloat16)

        def gather(j, carry):
            ids = [j * C_UNROLL + u for u in range(C_UNROLL)]
            vts = [jnp.transpose(srcs[2][src_rows(i), :]) for i in ids]
            for i, vt in zip(ids, vts):
                rows = src_rows(i)
                dst = pl.multiple_of(i * C_TILE, C_TILE)
                qd_ref[pl.ds(dst, C_TILE), :] = srcs[0][rows, :].astype(jnp.bfloat16)
                kd_ref[pl.ds(dst + C_TILE, C_TILE), :] = srcs[1][rows, :].astype(jnp.bfloat16)
                vtd_ref[:, pl.ds(dst + C_TILE, C_TILE)] = vt.astype(jnp.bfloat16)
            return carry

        lax.fori_loop(0, n_chunks // C_UNROLL, gather, 0)

        def tiles(j, carry):
            ids = [j * C_UNROLL + u for u in range(C_UNROLL)]
            row0s = [pl.multiple_of(i * C_TILE, C_TILE) for i in ids]

            def scores(u):
                q = qd_ref[pl.ds(row0s[u], C_TILE), :]
                q2 = jnp.concatenate([jnp.where(first, q, jnp.zeros_like(q)),
                                      jnp.where(first, jnp.zeros_like(q), q)], axis=0)
                return _dot_nt(kd_ref[pl.ds(row0s[u], 2 * C_TILE), :], q2)

            def softmax_pv(u, st):
                ver = jnp.minimum(ids[u] % n_tiles, 1)
                res = []
                for h in range(2):
                    x = st[:, h * C_TILE:(h + 1) * C_TILE] + bias_ref[h, ver]
                    m = _colmax(x)
                    p = jnp.exp2(x - m).astype(jnp.bfloat16)
                    v_aug = jnp.concatenate(
                        [vtd_ref[h * HEAD_DIM:(h + 1) * HEAD_DIM, pl.ds(row0s[u], 2 * C_TILE)],
                         jnp.ones((ONES_ROWS, 2 * C_TILE), jnp.bfloat16)], axis=0)
                    res.append((m, jnp.dot(v_aug, p, preferred_element_type=jnp.float32)))
                return res

            def write_back(u, res):
                outs, lses = [], []
                for m, pv in res:
                    l = pv[HEAD_DIM:HEAD_DIM + 1]
                    outs.append(pv[0:HEAD_DIM] / l)
                    lses.append(jnp.broadcast_to(m + jnp.log2(l), (HEAD_DIM, C_TILE)))
                rows = out_rows(ids[u])
                onat_ref[group, rows, :] = jnp.transpose(jnp.concatenate(outs, axis=0))
                lnat_ref[group, rows, :] = jnp.transpose(jnp.concatenate(lses, axis=0))

            sts = {u: scores(u) for u in range(C_AHEAD)}
            pending = None
            for u in range(C_UNROLL):
                if u + C_AHEAD < C_UNROLL:
                    sts[u + C_AHEAD] = scores(u + C_AHEAD)
                res = softmax_pv(u, sts.pop(u))
                if pending is not None:
                    write_back(*pending)
                pending = (u, res)
            write_back(*pending)
            return carry

        lax.fori_loop(0, n_chunks // C_UNROLL, tiles, 0)

    for group, (_, dil) in enumerate(C_PAIRS):
        pl.when(g == group)(functools.partial(run_group, group, dil))

    @pl.when(g == C_GROUPS - 1)
    def _():
        rows_per = 512

        def combine(c, carry):
            rows = pl.ds(pl.multiple_of(c * rows_per, rows_per), rows_per)
            l0, l1, l2 = lnat_ref[0, rows, :], lnat_ref[1, rows, :], lnat_ref[2, rows, :]
            m = jnp.maximum(jnp.maximum(l0, l1), l2)
            e0, e1, e2 = jnp.exp2(l0 - m), jnp.exp2(l1 - m), jnp.exp2(l2 - m)
            num = e0 * onat_ref[0, rows, :] + e1 * onat_ref[1, rows, :] + e2 * onat_ref[2, rows, :]
            o_ref[0, rows, :] = (num / (e0 + e1 + e2)).astype(o_ref.dtype)
            return carry

        lax.fori_loop(0, seq // rows_per, combine, 0)


def _attn_c(proj_c, bias_c, bsz, seq):
    assert seq % (C_TILE * C_UNROLL) == 0 and seq % (C_TILE * max(d for _, d in C_PAIRS)) == 0
    slab = lambda role: pl.BlockSpec(
        (None, seq, LANES), lambda b, hp, g: (role * 2 * C_GROUPS + g * 2 + hp, b, 0))
    return pl.pallas_call(
        functools.partial(_attn_c_kernel, seq=seq),
        out_shape=jax.ShapeDtypeStruct((bsz, seq, C_W), jnp.bfloat16),
        grid=(bsz, C_HEADS // 2, C_GROUPS),
        in_specs=[slab(0), slab(1), slab(2),
                  pl.BlockSpec((2, 2, 2 * C_TILE, C_TILE), lambda b, hp, g: (g * 2 + hp, 0, 0, 0))],
        out_specs=pl.BlockSpec((1, seq, LANES), lambda b, hp, g: (b, 0, hp)),
        scratch_shapes=[pltpu.VMEM((seq, LANES), jnp.bfloat16),
                        pltpu.VMEM((seq + C_TILE, LANES), jnp.bfloat16),
                        pltpu.VMEM((LANES, seq + C_TILE), jnp.bfloat16),
                        pltpu.VMEM((C_GROUPS, seq, LANES), jnp.float32),
                        pltpu.VMEM((C_GROUPS, seq, LANES), jnp.float32)]
        + [pltpu.VMEM((seq, LANES), jnp.float32)] * 3,
        compiler_params=pltpu.CompilerParams(
            dimension_semantics=("parallel", "parallel", "arbitrary"),
            vmem_limit_bytes=VMEM_LIMIT),
        name="attn_c",
    )(proj_c, proj_c, proj_c, bias_c)


def _mix_mlp_kernel(x_ref, oa_ref, ob_ref, oc_ref, wo_ref, g_ref, w1_ref, w2_ref, fg_ref, out_ref, *,
                    final, tf):
    mixed = jnp.concatenate([oa_ref[...], ob_ref[...], oc_ref[...]], axis=1)
    acc = x_ref[...] + jnp.dot(mixed, wo_ref[0], preferred_element_type=jnp.float32)
    h = _rms(acc, g_ref[0]).astype(jnp.bfloat16)
    for c in range(D_FF // tf):
        u = jnp.dot(h, w1_ref[0, :, c * tf:(c + 1) * tf], preferred_element_type=jnp.float32)
        u = jnp.square(jnp.maximum(u, 0.0)).astype(jnp.bfloat16)
        acc = acc + jnp.dot(u, w2_ref[0, c * tf:(c + 1) * tf, :],
                            preferred_element_type=jnp.float32)
    out_ref[...] = _rms(acc, fg_ref[...]) if final else acc


def _mix_mlp(x2, oa, ob, oc, w_o, norm_g, w_1, w_2, final_g, layer, final, *, tm=MLP_TM, tf=MLP_TF):
    n_tok = x2.shape[0]
    row = lambda w: pl.BlockSpec((tm, w), lambda i: (i, 0))
    resident = lambda shape: pl.BlockSpec(shape, lambda i: (layer, 0, 0),
                                          pipeline_mode=pl.Buffered(1))
    return pl.pallas_call(
        functools.partial(_mix_mlp_kernel, final=final, tf=tf),
        out_shape=jax.ShapeDtypeStruct((n_tok, D_MODEL), jnp.float32),
        grid=(n_tok // tm,),
        in_specs=[row(D_MODEL), row(A_W), row(B_W), row(C_W),
                  resident((1, D_MODEL, D_MODEL)),
                  pl.BlockSpec((1, 1, D_MODEL), lambda i: (layer, 0, 0)),
                  resident((1, D_MODEL, D_FF)),
                  resident((1, D_FF, D_MODEL)),
                  pl.BlockSpec((1, D_MODEL), lambda i: (0, 0))],
        out_specs=row(D_MODEL),
        compiler_params=pltpu.CompilerParams(dimension_semantics=("parallel",),
                                             vmem_limit_bytes=VMEM_LIMIT),
        name="mix_mlp",
    )(x2, oa, ob, oc, w_o, norm_g, w_1, w_2, final_g)


def _prep_w_in(w_in):
    a0, b0, f0, c0 = 0, 3 * A_W, 3 * A_W + 3 * B_W, 3 * A_W + 3 * B_W + B_HEADS
    sa, sb = A_QK_DIM ** -0.5 * LOG2E, HEAD_DIM ** -0.5 * LOG2E
    cw = C_GROUPS * C_W
    w_tm = jnp.concatenate([w_in[:, :, b0 + B_W:b0 + 2 * B_W],
                            w_in[:, :, a0 + A_W:a0 + 2 * A_W]], axis=-1).astype(jnp.bfloat16)
    w_fm = jnp.concatenate([
        w_in[:, :, b0:b0 + B_W] * sb, w_in[:, :, b0 + 2 * B_W:f0],
        w_in[:, :, a0:a0 + A_W] * sa, w_in[:, :, a0 + 2 * A_W:b0],
    ], axis=-1)
    w_fm = jnp.pad(jnp.concatenate([w_fm, w_in[:, :, f0:c0]], axis=-1),
                   ((0, 0), (0, 0), (0, F_ROWS - B_HEADS)))
    w_fm = w_fm.astype(jnp.bfloat16)
    w_c = jnp.concatenate([w_in[:, :, c0:c0 + cw] * sb, w_in[:, :, c0 + cw:]],
                          axis=-1).astype(jnp.bfloat16)
    return w_tm, w_fm, w_c


def kernel(x, norm1_g, w_in, b_f, lam_q1, lam_k1, lam_q2, lam_k2, diff_norm_g, w_o, norm2_g,
           w_1, w_2, rel_bias, final_g):
    bsz, seq, _ = x.shape
    depth = w_in.shape[0]
    w_tm, w_fm, w_c = _prep_w_in(w_in)
    w_o16, w_116, w_216 = (w.astype(jnp.bfloat16) for w in (w_o, w_1, w_2))
    lam_vecs = jnp.stack([lam_q1, lam_k1, lam_q2, lam_k2], axis=1)
    bias_a, bias_c = _make_bias_tiles(rel_bias)

    x2 = x.reshape(bsz * seq, D_MODEL)
    for l in range(depth):
        proj_tm, proj_fm, ft, proj_c = _in_proj(x2, norm1_g[:, None, :], w_tm, w_fm, w_c, l)
        proj3 = proj_tm.reshape(bsz, seq, N_TM)
        ckp = _decay(ft, b_f[:, :, None], l, bsz, seq)
        lam_init = jnp.full((1,), 0.8 - 0.6 * math.exp(-0.3 * l), jnp.float32)
        oa, ob = _attn_ab(proj3, proj_fm, bias_a, lam_vecs, lam_init, diff_norm_g[:, :, None],
                          ckp, l)
        oc = _attn_c(proj_c, bias_c, bsz, seq)
        x2 = _mix_mlp(x2, oa.reshape(bsz * seq, A_W), ob.reshape(bsz * seq, B_W),
                      oc.reshape(bsz * seq, C_W), w_o16, norm2_g[:, None, :], w_116, w_216,
                      final_g[None, :], l, l == depth - 1)
    return x2.reshape(bsz, seq, D_MODEL)
```

```python
import functools
import math

import jax
import jax.numpy as jnp
from jax import lax
from jax.experimental import pallas as pl
from jax.experimental.pallas import tpu as pltpu

D_MODEL = 1024
HEAD_DIM = 64
A_HEADS = 4
A_QK_DIM = 32
B_HEADS = 8
C_HEADS = 4
C_PAIRS = ((128, 1), (512, 4), (2048, 16))
C_GROUPS = 3
A_W = 256
B_W = 512
C_W = 256
D_FF = 4096
N_BUCKETS = 32
MAX_DISTANCE = 2048
NORM_EPS = 1e-6
NEG_INF = -1e30
LOG2E = 1.4426950408889634
SCALE_A = A_QK_DIM ** -0.5 * LOG2E
SCALE_BC = HEAD_DIM ** -0.5 * LOG2E

LANES = 128
VMEM_LIMIT = 56 * 1024 * 1024

KB, KA = 0, 512
N_TM = 768
QB_T, VB_T, QA_T, VA_T = 0, 512, 1024, 1280
N_FM = 1536
N_C = 2304
N_C_SLABS = N_C // 128
F_ROWS = 16
ONES_ROWS = 16
ACC_ROWS = HEAD_DIM + ONES_ROWS

C_TILE = 128
C_UNROLL = 32
C_AHEAD = 3
C_SPLIT = 4
FLASH_TILE = 256
MLP_TM = 512
MLP_TF = 1024
A_GROUP = 1
B_GROUP = 1
FLASH_PAIRS = 2


def _bucket_thresholds():
    max_exact = N_BUCKETS // 2
    out = []
    for k in range(1, N_BUCKETS - max_exact):
        t = max_exact * (MAX_DISTANCE / max_exact) ** (k / (N_BUCKETS - max_exact))
        out.append(int(math.ceil(t)))
    return tuple(out)


_THRESHOLDS = _bucket_thresholds()


def _n_near_tiles(tile):
    return -(-(_THRESHOLDS[-1] + tile - 1) // tile)


def _t5_bucket(d):
    big = jnp.full(d.shape, N_BUCKETS // 2, jnp.int32)
    for t in _THRESHOLDS:
        big = big + (d >= t).astype(jnp.int32)
    return jnp.where(d < N_BUCKETS // 2, d, big)


def _bias_lookup(rb_ref, bucket, col):
    val = jnp.zeros(bucket.shape, jnp.float32)
    for b in range(N_BUCKETS):
        val = jnp.where(bucket == b, rb_ref[b, col], val)
    return val


def _bias_a_kernel(rb_ref, out_ref, *, tile):
    delta = pl.program_id(0)
    j = lax.broadcasted_iota(jnp.int32, (tile, tile), 0)
    i = lax.broadcasted_iota(jnp.int32, (tile, tile), 1)
    d = delta * tile + i - j
    bucket = _t5_bucket(d)
    for h in range(A_HEADS):
        val = _bias_lookup(rb_ref, bucket, h) * LOG2E
        out_ref[h, 0] = jnp.where(d >= 0, val, NEG_INF)


def _bias_c_kernel(rb_ref, out_ref):
    g = pl.program_id(0)
    dil = jnp.where(g == 0, C_PAIRS[0][1], jnp.where(g == 1, C_PAIRS[1][1], C_PAIRS[2][1]))
    c = lax.broadcasted_iota(jnp.int32, (2 * C_TILE, C_TILE), 0)
    i = lax.broadcasted_iota(jnp.int32, (2 * C_TILE, C_TILE), 1)
    steps = i + C_TILE - c
    valid = (steps >= 0) & (steps <= C_TILE)
    bucket = _t5_bucket(steps * dil)
    for h in range(C_HEADS):
        val = _bias_lookup(rb_ref, bucket, A_HEADS + g * C_HEADS + h) * LOG2E
        out_ref[h, 1] = jnp.where(valid, val, NEG_INF)
        out_ref[h, 0] = jnp.where(valid & (c >= C_TILE), val, NEG_INF)


def _make_bias_tiles(rel_bias):
    tile = FLASH_TILE
    n_bias = _n_near_tiles(tile) + 1
    smem = pl.BlockSpec(memory_space=pltpu.SMEM)
    bias_a = pl.pallas_call(
        functools.partial(_bias_a_kernel, tile=tile),
        out_shape=jax.ShapeDtypeStruct((A_HEADS, n_bias, tile, tile), jnp.float32),
        grid=(n_bias,),
        in_specs=[smem],
        out_specs=pl.BlockSpec((A_HEADS, 1, tile, tile), lambda d: (0, d, 0, 0)),
        name="bias_a_tiles",
    )(rel_bias)
    bias_c = pl.pallas_call(
        _bias_c_kernel,
        out_shape=jax.ShapeDtypeStruct((C_GROUPS * C_HEADS, 2, 2 * C_TILE, C_TILE), jnp.float32),
        grid=(C_GROUPS,),
        in_specs=[smem],
        out_specs=pl.BlockSpec((C_HEADS, 2, 2 * C_TILE, C_TILE), lambda g: (g, 0, 0, 0)),
        name="bias_c_tiles",
    )(rel_bias)
    return bias_a, bias_c


def _rms(x, g):
    return x * lax.rsqrt(jnp.mean(x * x, axis=-1, keepdims=True) + NORM_EPS) * g


def _dot_nt(a, b):
    return lax.dot_general(a, b, (((1,), (1,)), ((), ())), preferred_element_type=jnp.float32)


def _in_proj_kernel(x_ref, g_ref, wtm_ref, wfm_ref, wc_ref, tm_ref, fm_ref, ft_ref, c_ref):
    h = _rms(x_ref[...], g_ref[0]).astype(jnp.bfloat16)
    tm_ref[...] = jnp.dot(h, wtm_ref[0], preferred_element_type=jnp.float32).astype(jnp.bfloat16)
    fm = lax.dot_general(wfm_ref[0], h, (((0,), (1,)), ((), ())),
                         preferred_element_type=jnp.float32)
    for lo, hi, scale in ((QB_T, VB_T, SCALE_BC), (VB_T, QA_T, 1.0), (QA_T, VA_T, SCALE_A),
                          (VA_T, N_FM, 1.0)):
        blk = fm[lo:hi] if scale == 1.0 else fm[lo:hi] * scale
        fm_ref[lo:hi, :] = blk.astype(jnp.bfloat16)
    ft_ref[...] = fm[N_FM:]
    for c in range(N_C // 256):
        pc = jnp.dot(h, wc_ref[0, :, c * 256:(c + 1) * 256], preferred_element_type=jnp.float32)
        if c * 256 < C_GROUPS * C_W:
            pc = pc * SCALE_BC
        c_ref[2 * c] = pc[:, :LANES]
        c_ref[2 * c + 1] = pc[:, LANES:]


def _in_proj(x2, norm_g, w_tm, w_fm, w_c, layer, *, tm=512):
    n_tok = x2.shape[0]
    return pl.pallas_call(
        _in_proj_kernel,
        out_shape=(jax.ShapeDtypeStruct((n_tok, N_TM), jnp.bfloat16),
                   jax.ShapeDtypeStruct((N_FM, n_tok), jnp.bfloat16),
                   jax.ShapeDtypeStruct((F_ROWS, n_tok), jnp.float32),
                   jax.ShapeDtypeStruct((N_C_SLABS, n_tok, LANES), jnp.float32)),
        grid=(n_tok // tm,),
        in_specs=[
            pl.BlockSpec((tm, D_MODEL), lambda i: (i, 0)),
            pl.BlockSpec((1, 1, D_MODEL), lambda i: (layer, 0, 0)),
            pl.BlockSpec((1, D_MODEL, N_TM), lambda i: (layer, 0, 0)),
            pl.BlockSpec((1, D_MODEL, N_FM + F_ROWS), lambda i: (layer, 0, 0)),
            pl.BlockSpec((1, D_MODEL, N_C), lambda i: (layer, 0, 0)),
        ],
        out_specs=(pl.BlockSpec((tm, N_TM), lambda i: (i, 0)),
                   pl.BlockSpec((N_FM, tm), lambda i: (0, i)),
                   pl.BlockSpec((F_ROWS, tm), lambda i: (0, i)),
                   pl.BlockSpec((N_C_SLABS, tm, LANES), lambda i: (0, i, 0))),
        compiler_params=pltpu.CompilerParams(dimension_semantics=("parallel",),
                                             vmem_limit_bytes=VMEM_LIMIT),
        name="in_proj",
    )(x2, norm_g, w_tm, w_fm, w_c)


def _decay_kernel(ft_ref, bf_ref, ckp_ref):
    z = ft_ref[0:B_HEADS, :] + bf_ref[0]
    x = jnp.minimum(z, 0.0) - jnp.log1p(jnp.exp(-jnp.abs(z)))
    seq = x.shape[1]
    pos = lax.broadcasted_iota(jnp.int32, x.shape, 1)
    shift = 1
    while shift < seq:
        x = x + jnp.where(pos >= shift, pltpu.roll(x, shift, 1), 0.0)
        shift *= 2
    c = x * (-LOG2E)
    hi = c.astype(jnp.bfloat16).astype(jnp.float32)
    mid = (c - hi).astype(jnp.bfloat16).astype(jnp.float32)
    lo = c - hi - mid
    stacked = jnp.concatenate(
        [hi, mid, lo, jnp.zeros((LANES - 3 * B_HEADS, seq), jnp.float32)], axis=0)
    ckp_ref[0] = jnp.transpose(stacked).astype(jnp.bfloat16)


def _decay(ft, b_f, layer, bsz, seq):
    return pl.pallas_call(
        _decay_kernel,
        out_shape=jax.ShapeDtypeStruct((bsz, seq, LANES), jnp.bfloat16),
        grid=(bsz,),
        in_specs=[pl.BlockSpec((F_ROWS, seq), lambda b: (0, b)),
                  pl.BlockSpec((1, B_HEADS, 1), lambda b: (layer, 0, 0))],
        out_specs=pl.BlockSpec((1, seq, LANES), lambda b: (b, 0, 0)),
        compiler_params=pltpu.CompilerParams(dimension_semantics=("parallel",)),
        name="decay_scan",
    )(ft, b_f)


def _colmax(s):
    rows = s.shape[0]
    while rows > 8:
        rows //= 2
        s = jnp.maximum(s[:rows], s[rows:2 * rows])
    return jnp.max(s, axis=0, keepdims=True)


def _online_step(s, smax, v_aug, m_ref, acc_ref, idx):
    m_old = m_ref[idx]
    m_new = jnp.maximum(m_old, smax)
    alpha = jnp.exp2(m_old - m_new)
    p = jnp.exp2(s - m_new).astype(jnp.bfloat16)
    acc_ref[idx] = alpha * acc_ref[idx] + jnp.dot(v_aug, p, preferred_element_type=jnp.float32)
    m_ref[idx] = m_new


def _v_aug(vt_ref, head, k0, tile):
    v = vt_ref[head * HEAD_DIM:(head + 1) * HEAD_DIM, pl.ds(k0, tile)]
    return jnp.concatenate([v, jnp.ones((ONES_ROWS, tile), jnp.bfloat16)], axis=0)


def _flash_pipeline(qi, groups, produce, consume, finish):
    def overlapped(pk, pslot, ck, cslot):
        for grp in groups:
            produce(pk, pslot, grp)
            consume(ck, cslot, False, grp)

    def consume_all(k, slot):
        for grp in groups:
            consume(k, slot, True, grp)

    for grp in groups:
        produce(0, 0, grp)

    def pair(k):
        overlapped(k + 1, 1, k, 0)
        overlapped(k + 2, 0, k + 1, 1)

    def pairs(j, carry):
        for u in range(FLASH_PAIRS):
            pair(2 * (FLASH_PAIRS * j + u))
        return carry

    n_pairs = qi // 2
    lax.fori_loop(0, n_pairs // FLASH_PAIRS, pairs, 0)
    for u in range(FLASH_PAIRS - 1):
        done = (n_pairs // FLASH_PAIRS) * FLASH_PAIRS + u
        pl.when(done < n_pairs)(functools.partial(pair, 2 * done))

    @pl.when(qi % 2 == 0)
    def _():
        consume_all(qi, 0)
        finish()

    @pl.when(qi % 2 == 1)
    def _():
        overlapped(qi, 1, qi - 1, 0)
        consume_all(qi, 1)
        finish()


def _mixer_a_stages(qi, qt_ref, k_ref, vt_ref, bias_ref, lamv_ref, lami_ref, subg_ref, o_ref,
                    m_ref, acc_ref, s_ref, smax_ref, tile, n_near):
    n_sub = 2 * A_HEADS
    qt = qt_ref[...].astype(jnp.float32)
    row = lax.broadcasted_iota(jnp.int32, (LANES, tile), 0)
    qs = []
    for s in range(n_sub):
        blk = qt[(s // 4) * LANES:(s // 4 + 1) * LANES]
        lo = (s % 4) * A_QK_DIM
        qs.append(jnp.where((row >= lo) & (row < lo + A_QK_DIM), blk, 0.0).astype(jnp.bfloat16))
    m_ref[...] = jnp.full(m_ref.shape, NEG_INF, jnp.float32)
    acc_ref[...] = jnp.zeros(acc_ref.shape, jnp.float32)

    def produce(ki, slot, subs):
        k0 = pl.multiple_of(ki * tile, tile)
        delta = jnp.minimum(qi - ki, n_near)
        for s in subs:
            k = k_ref[0, pl.ds(k0, tile), (s // 4) * LANES:(s // 4 + 1) * LANES]
            x = jnp.dot(k, qs[s], preferred_element_type=jnp.float32) + bias_ref[s // 2, delta]
            s_ref[slot, s] = x
            smax_ref[slot, s] = _colmax(x)

    def consume(ki, slot, last, subs):
        k0 = pl.multiple_of(ki * tile, tile)
        for s in subs:
            _online_step(s_ref[slot, s], smax_ref[slot, s], _v_aug(vt_ref, s // 2, k0, tile),
                         m_ref, acc_ref, s)

    def finish():
        lam_init = lami_ref[0]
        lv = lamv_ref[0]
        e1 = jnp.exp(jnp.sum(lv[0:1] * lv[1:2], axis=1, keepdims=True))
        e2 = jnp.exp(jnp.sum(lv[2:3] * lv[3:4], axis=1, keepdims=True))
        lam = e1 - e2 + lam_init
        heads = []
        for h in range(A_HEADS):
            a0, a1 = acc_ref[2 * h], acc_ref[2 * h + 1]
            o = (a0[0:HEAD_DIM] / a0[HEAD_DIM:HEAD_DIM + 1]
                 - lam * (a1[0:HEAD_DIM] / a1[HEAD_DIM:HEAD_DIM + 1]))
            ms = jnp.mean(o * o, axis=0, keepdims=True)
            heads.append(o * lax.rsqrt(ms + NORM_EPS) * subg_ref[0] * (1.0 - lam_init))
        o_ref[0] = jnp.transpose(jnp.concatenate(heads, axis=0)).astype(o_ref.dtype)

    return produce, consume, finish


def _mixer_b_stages(qi, qt_ref, k_ref, ckp_ref, vt_ref, o_ref, m_ref, acc_ref, s_ref, smax_ref,
                    tile):
    n_h = B_HEADS
    qt = qt_ref[...].astype(jnp.float32)
    row = lax.broadcasted_iota(jnp.int32, (LANES, tile), 0)
    q_aug = []
    for s in range(n_h):
        blk = qt[(s // 2) * LANES:(s // 2 + 1) * LANES]
        lo = (s % 2) * HEAD_DIM
        q_s = jnp.where((row >= lo) & (row < lo + HEAD_DIM), blk, 0.0)
        pick = (row < 3 * B_HEADS) & ((row % B_HEADS) == s)
        ones = jnp.where(pick, 1.0, 0.0)
        q_aug.append(jnp.concatenate([q_s, ones], axis=0).astype(jnp.bfloat16))
    m_ref[...] = jnp.full(m_ref.shape, NEG_INF, jnp.float32)
    acc_ref[...] = jnp.zeros(acc_ref.shape, jnp.float32)

    def produce(ki, slot, subs):
        k0 = pl.multiple_of(ki * tile, tile)
        ck = ckp_ref[0, pl.ds(k0, tile), :]
        for s in subs:
            k = k_ref[0, pl.ds(k0, tile), (s // 2) * LANES:(s // 2 + 1) * LANES]
            x = jnp.dot(jnp.concatenate([k, ck], axis=1), q_aug[s],
                        preferred_element_type=jnp.float32)
            s_ref[slot, s] = x
            smax_ref[slot, s] = _colmax(x)

    def consume(ki, slot, last, subs):
        k0 = pl.multiple_of(ki * tile, tile)
        if last:
            kk = lax.broadcasted_iota(jnp.int32, (tile, tile), 0)
            qq = lax.broadcasted_iota(jnp.int32, (tile, tile), 1)
            keep = kk <= qq
        for s in subs:
            x, xmax = s_ref[slot, s], smax_ref[slot, s]
            if last:
                x = jnp.where(keep, x, NEG_INF)
                xmax = _colmax(x)
            _online_step(x, xmax, _v_aug(vt_ref, s, k0, tile), m_ref, acc_ref, s)

    def finish():
        heads = [acc_ref[s][0:HEAD_DIM] / acc_ref[s][HEAD_DIM:HEAD_DIM + 1] for s in range(n_h)]
        o_ref[0] = jnp.transpose(jnp.concatenate(heads, axis=0)).astype(o_ref.dtype)

    return produce, consume, finish


def _attn_ab_kernel(qa_ref, ka_ref, va_ref, bias_ref, lamv_ref, lami_ref, subg_ref,
                    qb_ref, kb_ref, ckp_ref, vb_ref, oa_ref, ob_ref,
                    ma_ref, acca_ref, sa_ref, smaxa_ref, mb_ref, accb_ref, sb_ref, smaxb_ref, *,
                    tile, n_near):
    qi = pl.program_id(1)
    stages = (_mixer_a_stages(qi, qa_ref, ka_ref, va_ref, bias_ref, lamv_ref, lami_ref, subg_ref,
                              oa_ref, ma_ref, acca_ref, sa_ref, smaxa_ref, tile, n_near),
              _mixer_b_stages(qi, qb_ref, kb_ref, ckp_ref, vb_ref, ob_ref, mb_ref, accb_ref,
                              sb_ref, smaxb_ref, tile))
    groups = []
    for g in range(2 * A_HEADS // A_GROUP):
        b_per = B_HEADS * A_GROUP // (2 * A_HEADS)
        for b0 in range(g * b_per, (g + 1) * b_per, B_GROUP):
            groups.append((1, range(b0, b0 + B_GROUP)))
        groups.append((0, range(g * A_GROUP, (g + 1) * A_GROUP)))

    def finish():
        stages[0][2]()
        stages[1][2]()

    _flash_pipeline(qi, groups,
                    lambda k, slot, grp: stages[grp[0]][0](k, slot, grp[1]),
                    lambda k, slot, last, grp: stages[grp[0]][1](k, slot, last, grp[1]),
                    finish)


def _attn_ab(proj_tm3, proj_fm, bias_a, lam_vecs, lam_init, sub_g, ckp, layer):
    bsz, seq, _ = proj_tm3.shape
    tile = FLASH_TILE
    n_near = _n_near_tiles(tile)
    n_q = seq // tile
    n_a, n_b = 2 * A_HEADS, B_HEADS
    fm_blk = lambda w, off, whole: pl.BlockSpec(
        (w, seq if whole else tile),
        (lambda b, qi: (off // w, b)) if whole else (lambda b, qi: (off // w, b * n_q + qi)))
    tm_blk = lambda w, off: pl.BlockSpec((1, seq, w), lambda b, qi: (b, 0, off // w))
    out_blk = lambda w: pl.BlockSpec((1, tile, w), lambda b, qi: (b, qi, 0))
    state = lambda n: [pltpu.VMEM((n, 1, tile), jnp.float32),
                       pltpu.VMEM((n, ACC_ROWS, tile), jnp.float32),
                       pltpu.VMEM((2, n, tile, tile), jnp.float32),
                       pltpu.VMEM((2, n, 1, tile), jnp.float32)]
    return pl.pallas_call(
        functools.partial(_attn_ab_kernel, tile=tile, n_near=n_near),
        out_shape=(jax.ShapeDtypeStruct((bsz, seq, A_W), jnp.bfloat16),
                   jax.ShapeDtypeStruct((bsz, seq, B_W), jnp.bfloat16)),
        grid=(bsz, n_q),
        in_specs=[
            fm_blk(A_W, QA_T, False), tm_blk(A_W, KA), fm_blk(A_W, VA_T, True),
            pl.BlockSpec((A_HEADS, n_near + 1, tile, tile), lambda b, qi: (0, 0, 0, 0),
                         pipeline_mode=pl.Buffered(1)),
            pl.BlockSpec((1, 4, A_QK_DIM), lambda b, qi: (layer, 0, 0)),
            pl.BlockSpec(memory_space=pltpu.SMEM),
            pl.BlockSpec((1, HEAD_DIM, 1), lambda b, qi: (layer, 0, 0)),
            fm_blk(B_W, QB_T, False), tm_blk(B_W, KB),
            pl.BlockSpec((1, seq, LANES), lambda b, qi: (b, 0, 0)),
            fm_blk(B_W, VB_T, True),
        ],
        out_specs=(out_blk(A_W), out_blk(B_W)),
        scratch_shapes=state(n_a) + state(n_b),
        compiler_params=pltpu.CompilerParams(
            dimension_semantics=("parallel", "parallel"),
            vmem_limit_bytes=VMEM_LIMIT),
        name="attn_ab",
    )(proj_fm, proj_tm3, proj_fm, bias_a, lam_vecs, lam_init, sub_g,
      proj_fm, proj_tm3, ckp, proj_fm)


def _attn_c_kernel(q_ref, k_ref, v_ref, bias_ref, o_ref, qd_ref, kd_ref, vtd_ref, onat_ref,
                   lnat_ref, tq_ref, tk_ref, tv_ref, *, seq):
    g = pl.program_id(2)
    n_chunks = seq // C_TILE
    lane = lax.broadcasted_iota(jnp.int32, (1, LANES), 1)
    first = lane < HEAD_DIM

    def run_group(group, dil):
        n_tiles = seq // dil // C_TILE

        def out_rows(i):
            r, t = i // n_tiles, i % n_tiles
            return pl.ds(r + t * (C_TILE * dil), C_TILE, stride=dil)

        if dil > C_SPLIT:
            per = seq // C_SPLIT // C_TILE

            def presplit(j, carry):
                for u in range(C_UNROLL):
                    i = j * C_UNROLL + u
                    rows = pl.ds(i // per + (i % per) * (C_TILE * C_SPLIT), C_TILE, stride=C_SPLIT)
                    dst = pl.ds(pl.multiple_of(i * C_TILE, C_TILE), C_TILE)
                    for src, tmp in ((q_ref, tq_ref), (k_ref, tk_ref), (v_ref, tv_ref)):
                        tmp[dst, :] = src[rows, :]
                return carry

            lax.fori_loop(0, n_chunks // C_UNROLL, presplit, 0)
            srcs, step = (tq_ref, tk_ref, tv_ref), dil // C_SPLIT

            def src_rows(i):
                r, t = i // n_tiles, i % n_tiles
                base = (r % C_SPLIT) * (seq // C_SPLIT) + r // C_SPLIT + t * (C_TILE * step)
                return pl.ds(base, C_TILE, stride=step)
        else:
            srcs, src_rows = (q_ref, k_ref, v_ref), out_rows

        kd_ref[0:C_TILE, :] = jnp.zeros((C_TILE, LANES), jnp.bfloat16)
        vtd_ref[:, 0:C_TILE] = jnp.zeros((LANES, C_TILE), jnp.bfloat16)

        def gather(j, carry):
            ids = [j * C_UNROLL + u for u in range(C_UNROLL)]
            vts = [jnp.transpose(srcs[2][src_rows(i), :]) for i in ids]
            for i, vt in zip(ids, vts):
                rows = src_rows(i)
                dst = pl.multiple_of(i * C_TILE, C_TILE)
                qd_ref[pl.ds(dst, C_TILE), :] = srcs[0][rows, :].astype(jnp.bfloat16)
                kd_ref[pl.ds(dst + C_TILE, C_TILE), :] = srcs[1][rows, :].astype(jnp.bfloat16)
                vtd_ref[:, pl.ds(dst + C_TILE, C_TILE)] = vt.astype(jnp.bfloat16)
            return carry

        lax.fori_loop(0, n_chunks // C_UNROLL, gather, 0)

        def tiles(j, carry):
            ids = [j * C_UNROLL + u for u in range(C_UNROLL)]
            row0s = [pl.multiple_of(i * C_TILE, C_TILE) for i in ids]

            def scores(u):
                q = qd_ref[pl.ds(row0s[u], C_TILE), :]
                q2 = jnp.concatenate([jnp.where(first, q, jnp.zeros_like(q)),
                                      jnp.where(first, jnp.zeros_like(q), q)], axis=0)
                return _dot_nt(kd_ref[pl.ds(row0s[u], 2 * C_TILE), :], q2)

            def softmax_pv(u, st):
                ver = jnp.minimum(ids[u] % n_tiles, 1)
                res = []
                for h in range(2):
                    x = st[:, h * C_TILE:(h + 1) * C_TILE] + bias_ref[h, ver]
                    m = _colmax(x)
                    p = jnp.exp2(x - m).astype(jnp.bfloat16)
                    v_aug = jnp.concatenate(
                        [vtd_ref[h * HEAD_DIM:(h + 1) * HEAD_DIM, pl.ds(row0s[u], 2 * C_TILE)],
                         jnp.ones((ONES_ROWS, 2 * C_TILE), jnp.bfloat16)], axis=0)
                    res.append((m, jnp.dot(v_aug, p, preferred_element_type=jnp.float32)))
                return res

            def write_back(u, res):
                outs, lses = [], []
                for m, pv in res:
                    l = pv[HEAD_DIM:HEAD_DIM + 1]
                    outs.append(pv[0:HEAD_DIM] / l)
                    lses.append(jnp.broadcast_to(m + jnp.log2(l), (HEAD_DIM, C_TILE)))
                rows = out_rows(ids[u])
                onat_ref[group, rows, :] = jnp.transpose(jnp.concatenate(outs, axis=0))
                lnat_ref[group, rows, :] = jnp.transpose(jnp.concatenate(lses, axis=0))

            sts = {u: scores(u) for u in range(C_AHEAD)}
            pending = None
            for u in range(C_UNROLL):
                if u + C_AHEAD < C_UNROLL:
                    sts[u + C_AHEAD] = scores(u + C_AHEAD)
                res = softmax_pv(u, sts.pop(u))
                if pending is not None:
                    write_back(*pending)
                pending = (u, res)
            write_back(*pending)
            return carry

        lax.fori_loop(0, n_chunks // C_UNROLL, tiles, 0)

    for group, (_, dil) in enumerate(C_PAIRS):
        pl.when(g == group)(functools.partial(run_group, group, dil))

    @pl.when(g == C_GROUPS - 1)
    def _():
        rows_per = 512

        def combine(c, carry):
            rows = pl.ds(pl.multiple_of(c * rows_per, rows_per), rows_per)
            l0, l1, l2 = lnat_ref[0, rows, :], lnat_ref[1, rows, :], lnat_ref[2, rows, :]
            m = jnp.maximum(jnp.maximum(l0, l1), l2)
            e0, e1, e2 = jnp.exp2(l0 - m), jnp.exp2(l1 - m), jnp.exp2(l2 - m)
            num = e0 * onat_ref[0, rows, :] + e1 * onat_ref[1, rows, :] + e2 * onat_ref[2, rows, :]
            o_ref[0, rows, :] = (num / (e0 + e1 + e2)).astype(o_ref.dtype)
            return carry

        lax.fori_loop(0, seq // rows_per, combine, 0)


def _attn_c(proj_c, bias_c, bsz, seq):
    assert seq % (C_TILE * C_UNROLL) == 0 and seq % (C_TILE * max(d for _, d in C_PAIRS)) == 0
    slab = lambda role: pl.BlockSpec(
        (None, seq, LANES), lambda b, hp, g: (role * 2 * C_GROUPS + g * 2 + hp, b, 0))
    return pl.pallas_call(
        functools.partial(_attn_c_kernel, seq=seq),
        out_shape=jax.ShapeDtypeStruct((bsz, seq, C_W), jnp.bfloat16),
        grid=(bsz, C_HEADS // 2, C_GROUPS),
        in_specs=[slab(0), slab(1), slab(2),
                  pl.BlockSpec((2, 2, 2 * C_TILE, C_TILE), lambda b, hp, g: (g * 2 + hp, 0, 0, 0))],
        out_specs=pl.BlockSpec((1, seq, LANES), lambda b, hp, g: (b, 0, hp)),
        scratch_shapes=[pltpu.VMEM((seq, LANES), jnp.bfloat16),
                        pltpu.VMEM((seq + C_TILE, LANES), jnp.bfloat16),
                        pltpu.VMEM((LANES, seq + C_TILE), jnp.bfloat16),
                        pltpu.VMEM((C_GROUPS, seq, LANES), jnp.float32),
                        pltpu.VMEM((C_GROUPS, seq, LANES), jnp.float32)]
        + [pltpu.VMEM((seq, LANES), jnp.float32)] * 3,
        compiler_params=pltpu.CompilerParams(
            dimension_semantics=("parallel", "parallel", "arbitrary"),
            vmem_limit_bytes=VMEM_LIMIT),
        name="attn_c",
    )(proj_c, proj_c, proj_c, bias_c)


def _mix_mlp_kernel(x_ref, oa_ref, ob_ref, oc_ref, wo_ref, g_ref, w1_ref, w2_ref, fg_ref, out_ref, *,
                    final, tf):
    mixed = jnp.concatenate([oa_ref[...], ob_ref[...], oc_ref[...]], axis=1)
    acc = x_ref[...] + jnp.dot(mixed, wo_ref[0], preferred_element_type=jnp.float32)
    h = _rms(acc, g_ref[0]).astype(jnp.bfloat16)
    for c in range(D_FF // tf):
        u = jnp.dot(h, w1_ref[0, :, c * tf:(c + 1) * tf], preferred_element_type=jnp.float32)
        u = jnp.square(jnp.maximum(u, 0.0)).astype(jnp.bfloat16)
        acc = acc + jnp.dot(u, w2_ref[0, c * tf:(c + 1) * tf, :],
                            preferred_element_type=jnp.float32)
    out_ref[...] = _rms(acc, fg_ref[...]) if final else acc


def _mix_mlp(x2, oa, ob, oc, w_o, norm_g, w_1, w_2, final_g, layer, final, *, tm=MLP_TM, tf=MLP_TF):
    n_tok = x2.shape[0]
    row = lambda w: pl.BlockSpec((tm, w), lambda i: (i, 0))
    resident = lambda shape: pl.BlockSpec(shape, lambda i: (layer, 0, 0),
                                          pipeline_mode=pl.Buffered(1))
    return pl.pallas_call(
        functools.partial(_mix_mlp_kernel, final=final, tf=tf),
        out_shape=jax.ShapeDtypeStruct((n_tok, D_MODEL), jnp.float32),
        grid=(n_tok // tm,),
        in_specs=[row(D_MODEL), row(A_W), row(B_W), row(C_W),
                  resident((1, D_MODEL, D_MODEL)),
                  pl.BlockSpec((1, 1, D_MODEL), lambda i: (layer, 0, 0)),
                  resident((1, D_MODEL, D_FF)),
                  resident((1, D_FF, D_MODEL)),
                  pl.BlockSpec((1, D_MODEL), lambda i: (0, 0))],
        out_specs=row(D_MODEL),
        compiler_params=pltpu.CompilerParams(dimension_semantics=("parallel",),
                                             vmem_limit_bytes=VMEM_LIMIT),
        name="mix_mlp",
    )(x2, oa, ob, oc, w_o, norm_g, w_1, w_2, final_g)


def _prep_w_in(w_in):
    a0, b0, f0, c0 = 0, 3 * A_W, 3 * A_W + 3 * B_W, 3 * A_W + 3 * B_W + B_HEADS
    w16 = w_in.astype(jnp.bfloat16)
    w_tm = jnp.concatenate([w16[:, :, b0 + B_W:b0 + 2 * B_W],
                            w16[:, :, a0 + A_W:a0 + 2 * A_W]], axis=-1)
    w_fm = jnp.concatenate([
        w16[:, :, b0:b0 + B_W], w16[:, :, b0 + 2 * B_W:f0],
        w16[:, :, a0:a0 + A_W], w16[:, :, a0 + 2 * A_W:b0],
        w16[:, :, f0:c0],
    ], axis=-1)
    w_fm = jnp.pad(w_fm, ((0, 0), (0, 0), (0, F_ROWS - B_HEADS)))
    return w_tm, w_fm, w16[:, :, c0:]


def kernel(x, norm1_g, w_in, b_f, lam_q1, lam_k1, lam_q2, lam_k2, diff_norm_g, w_o, norm2_g,
           w_1, w_2, rel_bias, final_g):
    bsz, seq, _ = x.shape
    depth = w_in.shape[0]
    w_tm, w_fm, w_c = _prep_w_in(w_in)
    w_o16, w_116, w_216 = (w.astype(jnp.bfloat16) for w in (w_o, w_1, w_2))
    lam_vecs = jnp.stack([lam_q1, lam_k1, lam_q2, lam_k2], axis=1)
    bias_a, bias_c = _make_bias_tiles(rel_bias)

    x2 = x.reshape(bsz * seq, D_MODEL)
    for l in range(depth):
        proj_tm, proj_fm, ft, proj_c = _in_proj(x2, norm1_g[:, None, :], w_tm, w_fm, w_c, l)
        proj3 = proj_tm.reshape(bsz, seq, N_TM)
        ckp = _decay(ft, b_f[:, :, None], l, bsz, seq)
        lam_init = jnp.full((1,), 0.8 - 0.6 * math.exp(-0.3 * l), jnp.float32)
        oa, ob = _attn_ab(proj3, proj_fm, bias_a, lam_vecs, lam_init, diff_norm_g[:, :, None],
                          ckp, l)
        oc = _attn_c(proj_c, bias_c, bsz, seq)
        x2 = _mix_mlp(x2, oa.reshape(bsz * seq, A_W), ob.reshape(bsz * seq, B_W),
                      oc.reshape(bsz * seq, C_W), w_o16, norm2_g[:, None, :], w_116, w_216,
                      final_g[None, :], l, l == depth - 1)
    return x2.reshape(bsz, seq, D_MODEL)
```

```python
import functools
import math

import jax
import jax.numpy as jnp
from jax import lax
from jax.experimental import pallas as pl
from jax.experimental.pallas import tpu as pltpu

D_MODEL = 1024
HEAD_DIM = 64
A_HEADS = 4
A_QK_DIM = 32
B_HEADS = 8
C_HEADS = 4
C_PAIRS = ((128, 1), (512, 4), (2048, 16))
C_GROUPS = 3
A_W = 256
B_W = 512
C_W = 256
D_FF = 4096
N_BUCKETS = 32
MAX_DISTANCE = 2048
NORM_EPS = 1e-6
NEG_INF = -1e30
LOG2E = 1.4426950408889634
SCALE_A = A_QK_DIM ** -0.5 * LOG2E
SCALE_BC = HEAD_DIM ** -0.5 * LOG2E

LANES = 128
VMEM_LIMIT = 56 * 1024 * 1024

KB, KA = 0, 512
N_TM = 768
QB_T, VB_T, QA_T, VA_T = 0, 512, 1024, 1280
N_FM = 1536
N_C = 2304
N_C_SLABS = N_C // 128
F_ROWS = 16
ONES_ROWS = 16
ACC_ROWS = HEAD_DIM + ONES_ROWS

C_TILE = 128
C_UNROLL = 32
C_AHEAD = 3
C_SPLIT = 4
FLASH_TILE = 256
MLP_TM = 1024
MLP_TF = 1024
A_GROUP = 1
B_GROUP = 1
FLASH_PAIRS = 2


def _bucket_thresholds():
    max_exact = N_BUCKETS // 2
    out = []
    for k in range(1, N_BUCKETS - max_exact):
        t = max_exact * (MAX_DISTANCE / max_exact) ** (k / (N_BUCKETS - max_exact))
        out.append(int(math.ceil(t)))
    return tuple(out)


_THRESHOLDS = _bucket_thresholds()


def _n_near_tiles(tile):
    return -(-(_THRESHOLDS[-1] + tile - 1) // tile)


def _t5_bucket(d):
    big = jnp.full(d.shape, N_BUCKETS // 2, jnp.int32)
    for t in _THRESHOLDS:
        big = big + (d >= t).astype(jnp.int32)
    return jnp.where(d < N_BUCKETS // 2, d, big)


def _bias_lookup(rb_ref, bucket, col):
    val = jnp.zeros(bucket.shape, jnp.float32)
    for b in range(N_BUCKETS):
        val = jnp.where(bucket == b, rb_ref[b, col], val)
    return val


def _bias_a_kernel(rb_ref, out_ref, *, tile):
    delta = pl.program_id(0)
    j = lax.broadcasted_iota(jnp.int32, (tile, tile), 0)
    i = lax.broadcasted_iota(jnp.int32, (tile, tile), 1)
    d = delta * tile + i - j
    bucket = _t5_bucket(d)
    for h in range(A_HEADS):
        val = _bias_lookup(rb_ref, bucket, h) * LOG2E
        out_ref[h, 0] = jnp.where(d >= 0, val, NEG_INF)


def _bias_c_kernel(rb_ref, out_ref):
    g = pl.program_id(0)
    dil = jnp.where(g == 0, C_PAIRS[0][1], jnp.where(g == 1, C_PAIRS[1][1], C_PAIRS[2][1]))
    c = lax.broadcasted_iota(jnp.int32, (2 * C_TILE, C_TILE), 0)
    i = lax.broadcasted_iota(jnp.int32, (2 * C_TILE, C_TILE), 1)
    steps = i + C_TILE - c
    valid = (steps >= 0) & (steps <= C_TILE)
    bucket = _t5_bucket(steps * dil)
    for h in range(C_HEADS):
        val = _bias_lookup(rb_ref, bucket, A_HEADS + g * C_HEADS + h) * LOG2E
        out_ref[h, 1] = jnp.where(valid, val, NEG_INF)
        out_ref[h, 0] = jnp.where(valid & (c >= C_TILE), val, NEG_INF)


def _make_bias_tiles(rel_bias):
    tile = FLASH_TILE
    n_bias = _n_near_tiles(tile) + 1
    smem = pl.BlockSpec(memory_space=pltpu.SMEM)
    bias_a = pl.pallas_call(
        functools.partial(_bias_a_kernel, tile=tile),
        out_shape=jax.ShapeDtypeStruct((A_HEADS, n_bias, tile, tile), jnp.float32),
        grid=(n_bias,),
        in_specs=[smem],
        out_specs=pl.BlockSpec((A_HEADS, 1, tile, tile), lambda d: (0, d, 0, 0)),
        name="bias_a_tiles",
    )(rel_bias)
    bias_c = pl.pallas_call(
        _bias_c_kernel,
        out_shape=jax.ShapeDtypeStruct((C_GROUPS * C_HEADS, 2, 2 * C_TILE, C_TILE), jnp.float32),
        grid=(C_GROUPS,),
        in_specs=[smem],
        out_specs=pl.BlockSpec((C_HEADS, 2, 2 * C_TILE, C_TILE), lambda g: (g, 0, 0, 0)),
        name="bias_c_tiles",
    )(rel_bias)
    return bias_a, bias_c


def _rms(x, g):
    return x * lax.rsqrt(jnp.mean(x * x, axis=-1, keepdims=True) + NORM_EPS) * g


def _dot_nt(a, b):
    return lax.dot_general(a, b, (((1,), (1,)), ((), ())), preferred_element_type=jnp.float32)


def _in_proj_kernel(x_ref, g_ref, wtm_ref, wfm_ref, wc_ref, tm_ref, fm_ref, ft_ref, c_ref):
    h = _rms(x_ref[...], g_ref[0]).astype(jnp.bfloat16)
    tm_ref[...] = jnp.dot(h, wtm_ref[0], preferred_element_type=jnp.float32).astype(jnp.bfloat16)
    fm = lax.dot_general(wfm_ref[0], h, (((0,), (1,)), ((), ())),
                         preferred_element_type=jnp.float32)
    for lo, hi, scale in ((QB_T, VB_T, SCALE_BC), (VB_T, QA_T, 1.0), (QA_T, VA_T, SCALE_A),
                          (VA_T, N_FM, 1.0)):
        blk = fm[lo:hi] if scale == 1.0 else fm[lo:hi] * scale
        fm_ref[lo:hi, :] = blk.astype(jnp.bfloat16)
    ft_ref[...] = fm[N_FM:]
    for c in range(N_C // 256):
        pc = jnp.dot(h, wc_ref[0, :, c * 256:(c + 1) * 256], preferred_element_type=jnp.float32)
        if c * 256 < C_GROUPS * C_W:
            pc = pc * SCALE_BC
        c_ref[2 * c] = pc[:, :LANES]
        c_ref[2 * c + 1] = pc[:, LANES:]


def _in_proj(x2, norm_g, w_tm, w_fm, w_c, layer, *, tm=512):
    n_tok = x2.shape[0]
    return pl.pallas_call(
        _in_proj_kernel,
        out_shape=(jax.ShapeDtypeStruct((n_tok, N_TM), jnp.bfloat16),
                   jax.ShapeDtypeStruct((N_FM, n_tok), jnp.bfloat16),
                   jax.ShapeDtypeStruct((F_ROWS, n_tok), jnp.float32),
                   jax.ShapeDtypeStruct((N_C_SLABS, n_tok, LANES), jnp.float32)),
        grid=(n_tok // tm,),
        in_specs=[
            pl.BlockSpec((tm, D_MODEL), lambda i: (i, 0)),
            pl.BlockSpec((1, 1, D_MODEL), lambda i: (layer, 0, 0)),
            pl.BlockSpec((1, D_MODEL, N_TM), lambda i: (layer, 0, 0)),
            pl.BlockSpec((1, D_MODEL, N_FM + F_ROWS), lambda i: (layer, 0, 0)),
            pl.BlockSpec((1, D_MODEL, N_C), lambda i: (layer, 0, 0)),
        ],
        out_specs=(pl.BlockSpec((tm, N_TM), lambda i: (i, 0)),
                   pl.BlockSpec((N_FM, tm), lambda i: (0, i)),
                   pl.BlockSpec((F_ROWS, tm), lambda i: (0, i)),
                   pl.BlockSpec((N_C_SLABS, tm, LANES), lambda i: (0, i, 0))),
        compiler_params=pltpu.CompilerParams(dimension_semantics=("parallel",),
                                             vmem_limit_bytes=VMEM_LIMIT),
        name="in_proj",
    )(x2, norm_g, w_tm, w_fm, w_c)


def _decay_kernel(ft_ref, bf_ref, ckp_ref):
    z = ft_ref[0:B_HEADS, :] + bf_ref[0]
    x = jnp.minimum(z, 0.0) - jnp.log1p(jnp.exp(-jnp.abs(z)))
    seq = x.shape[1]
    pos = lax.broadcasted_iota(jnp.int32, x.shape, 1)
    shift = 1
    while shift < seq:
        x = x + jnp.where(pos >= shift, pltpu.roll(x, shift, 1), 0.0)
        shift *= 2
    c = x * (-LOG2E)
    hi = c.astype(jnp.bfloat16).astype(jnp.float32)
    mid = (c - hi).astype(jnp.bfloat16).astype(jnp.float32)
    lo = c - hi - mid
    stacked = jnp.concatenate(
        [hi, mid, lo, jnp.zeros((LANES - 3 * B_HEADS, seq), jnp.float32)], axis=0)
    ckp_ref[0] = jnp.transpose(stacked).astype(jnp.bfloat16)


def _decay(ft, b_f, layer, bsz, seq):
    return pl.pallas_call(
        _decay_kernel,
        out_shape=jax.ShapeDtypeStruct((bsz, seq, LANES), jnp.bfloat16),
        grid=(bsz,),
        in_specs=[pl.BlockSpec((F_ROWS, seq), lambda b: (0, b)),
                  pl.BlockSpec((1, B_HEADS, 1), lambda b: (layer, 0, 0))],
        out_specs=pl.BlockSpec((1, seq, LANES), lambda b: (b, 0, 0)),
        compiler_params=pltpu.CompilerParams(dimension_semantics=("parallel",)),
        name="decay_scan",
    )(ft, b_f)


def _colmax(s):
    rows = s.shape[0]
    while rows > 8:
        rows //= 2
        s = jnp.maximum(s[:rows], s[rows:2 * rows])
    return jnp.max(s, axis=0, keepdims=True)


def _online_step(s, smax, v_aug, m_ref, acc_ref, idx):
    m_old = m_ref[idx]
    m_new = jnp.maximum(m_old, smax)
    alpha = jnp.exp2(m_old - m_new)
    p = jnp.exp2(s - m_new).astype(jnp.bfloat16)
    acc_ref[idx] = alpha * acc_ref[idx] + jnp.dot(v_aug, p, preferred_element_type=jnp.float32)
    m_ref[idx] = m_new


def _v_aug(vt_ref, head, k0, tile):
    v = vt_ref[head * HEAD_DIM:(head + 1) * HEAD_DIM, pl.ds(k0, tile)]
    return jnp.concatenate([v, jnp.ones((ONES_ROWS, tile), jnp.bfloat16)], axis=0)


def _flash_pipeline(qi, groups, produce, consume, finish):
    def overlapped(pk, pslot, ck, cslot):
        for grp in groups:
            produce(pk, pslot, grp)
            consume(ck, cslot, False, grp)

    def consume_all(k, slot):
        for grp in groups:
            consume(k, slot, True, grp)

    for grp in groups:
        produce(0, 0, grp)

    def pair(k):
        overlapped(k + 1, 1, k, 0)
        overlapped(k + 2, 0, k + 1, 1)

    def pairs(j, carry):
        for u in range(FLASH_PAIRS):
            pair(2 * (FLASH_PAIRS * j + u))
        return carry

    n_pairs = qi // 2
    lax.fori_loop(0, n_pairs // FLASH_PAIRS, pairs, 0)
    for u in range(FLASH_PAIRS - 1):
        done = (n_pairs // FLASH_PAIRS) * FLASH_PAIRS + u
        pl.when(done < n_pairs)(functools.partial(pair, 2 * done))

    @pl.when(qi % 2 == 0)
    def _():
        consume_all(qi, 0)
        finish()

    @pl.when(qi % 2 == 1)
    def _():
        overlapped(qi, 1, qi - 1, 0)
        consume_all(qi, 1)
        finish()


def _mixer_a_stages(qi, qt_ref, k_ref, vt_ref, bias_ref, lamv_ref, lami_ref, subg_ref, o_ref,
                    m_ref, acc_ref, s_ref, smax_ref, tile, n_near):
    n_sub = 2 * A_HEADS
    qt = qt_ref[...].astype(jnp.float32)
    row = lax.broadcasted_iota(jnp.int32, (LANES, tile), 0)
    qs = []
    for s in range(n_sub):
        blk = qt[(s // 4) * LANES:(s // 4 + 1) * LANES]
        lo = (s % 4) * A_QK_DIM
        qs.append(jnp.where((row >= lo) & (row < lo + A_QK_DIM), blk, 0.0).astype(jnp.bfloat16))
    m_ref[...] = jnp.full(m_ref.shape, NEG_INF, jnp.float32)
    acc_ref[...] = jnp.zeros(acc_ref.shape, jnp.float32)

    def produce(ki, slot, subs):
        k0 = pl.multiple_of(ki * tile, tile)
        delta = jnp.minimum(qi - ki, n_near)
        for s in subs:
            k = k_ref[0, pl.ds(k0, tile), (s // 4) * LANES:(s // 4 + 1) * LANES]
            x = jnp.dot(k, qs[s], preferred_element_type=jnp.float32) + bias_ref[s // 2, delta]
            s_ref[slot, s] = x
            smax_ref[slot, s] = _colmax(x)

    def consume(ki, slot, last, subs):
        k0 = pl.multiple_of(ki * tile, tile)
        for s in subs:
            _online_step(s_ref[slot, s], smax_ref[slot, s], _v_aug(vt_ref, s // 2, k0, tile),
                         m_ref, acc_ref, s)

    def finish():
        lam_init = lami_ref[0]
        lv = lamv_ref[0]
        e1 = jnp.exp(jnp.sum(lv[0:1] * lv[1:2], axis=1, keepdims=True))
        e2 = jnp.exp(jnp.sum(lv[2:3] * lv[3:4], axis=1, keepdims=True))
        lam = e1 - e2 + lam_init
        heads = []
        for h in range(A_HEADS):
            a0, a1 = acc_ref[2 * h], acc_ref[2 * h + 1]
            o = (a0[0:HEAD_DIM] / a0[HEAD_DIM:HEAD_DIM + 1]
                 - lam * (a1[0:HEAD_DIM] / a1[HEAD_DIM:HEAD_DIM + 1]))
            ms = jnp.mean(o * o, axis=0, keepdims=True)
            heads.append(o * lax.rsqrt(ms + NORM_EPS) * subg_ref[0] * (1.0 - lam_init))
        o_ref[0] = jnp.transpose(jnp.concatenate(heads, axis=0)).astype(o_ref.dtype)

    return produce, consume, finish


def _mixer_b_stages(qi, qt_ref, k_ref, ckp_ref, vt_ref, o_ref, m_ref, acc_ref, s_ref, smax_ref,
                    tile):
    n_h = B_HEADS
    qt = qt_ref[...].astype(jnp.float32)
    row = lax.broadcasted_iota(jnp.int32, (LANES, tile), 0)
    q_aug = []
    for s in range(n_h):
        blk = qt[(s // 2) * LANES:(s // 2 + 1) * LANES]
        lo = (s % 2) * HEAD_DIM
        q_s = jnp.where((row >= lo) & (row < lo + HEAD_DIM), blk, 0.0)
        pick = (row < 3 * B_HEADS) & ((row % B_HEADS) == s)
        ones = jnp.where(pick, 1.0, 0.0)
        q_aug.append(jnp.concatenate([q_s, ones], axis=0).astype(jnp.bfloat16))
    m_ref[...] = jnp.full(m_ref.shape, NEG_INF, jnp.float32)
    acc_ref[...] = jnp.zeros(acc_ref.shape, jnp.float32)

    def produce(ki, slot, subs):
        k0 = pl.multiple_of(ki * tile, tile)
        ck = ckp_ref[0, pl.ds(k0, tile), :]
        for s in subs:
            k = k_ref[0, pl.ds(k0, tile), (s // 2) * LANES:(s // 2 + 1) * LANES]
            x = jnp.dot(jnp.concatenate([k, ck], axis=1), q_aug[s],
                        preferred_element_type=jnp.float32)
            s_ref[slot, s] = x
            smax_ref[slot, s] = _colmax(x)

    def consume(ki, slot, last, subs):
        k0 = pl.multiple_of(ki * tile, tile)
        if last:
            kk = lax.broadcasted_iota(jnp.int32, (tile, tile), 0)
            qq = lax.broadcasted_iota(jnp.int32, (tile, tile), 1)
            keep = kk <= qq
        for s in subs:
            x, xmax = s_ref[slot, s], smax_ref[slot, s]
            if last:
                x = jnp.where(keep, x, NEG_INF)
                xmax = _colmax(x)
            _online_step(x, xmax, _v_aug(vt_ref, s, k0, tile), m_ref, acc_ref, s)

    def finish():
        heads = [acc_ref[s][0:HEAD_DIM] / acc_ref[s][HEAD_DIM:HEAD_DIM + 1] for s in range(n_h)]
        o_ref[0] = jnp.transpose(jnp.concatenate(heads, axis=0)).astype(o_ref.dtype)

    return produce, consume, finish


def _attn_ab_kernel(qa_ref, ka_ref, va_ref, bias_ref, lamv_ref, lami_ref, subg_ref,
                    qb_ref, kb_ref, ckp_ref, vb_ref, oa_ref, ob_ref,
                    ma_ref, acca_ref, sa_ref, smaxa_ref, mb_ref, accb_ref, sb_ref, smaxb_ref, *,
                    tile, n_near):
    qi = pl.program_id(1)
    stages = (_mixer_a_stages(qi, qa_ref, ka_ref, va_ref, bias_ref, lamv_ref, lami_ref, subg_ref,
                              oa_ref, ma_ref, acca_ref, sa_ref, smaxa_ref, tile, n_near),
              _mixer_b_stages(qi, qb_ref, kb_ref, ckp_ref, vb_ref, ob_ref, mb_ref, accb_ref,
                              sb_ref, smaxb_ref, tile))
    groups = []
    for g in range(2 * A_HEADS // A_GROUP):
        b_per = B_HEADS * A_GROUP // (2 * A_HEADS)
        for b0 in range(g * b_per, (g + 1) * b_per, B_GROUP):
            groups.append((1, range(b0, b0 + B_GROUP)))
        groups.append((0, range(g * A_GROUP, (g + 1) * A_GROUP)))

    def finish():
        stages[0][2]()
        stages[1][2]()

    _flash_pipeline(qi, groups,
                    lambda k, slot, grp: stages[grp[0]][0](k, slot, grp[1]),
                    lambda k, slot, last, grp: stages[grp[0]][1](k, slot, last, grp[1]),
                    finish)


def _attn_ab(proj_tm3, proj_fm, bias_a, lam_vecs, lam_init, sub_g, ckp, layer):
    bsz, seq, _ = proj_tm3.shape
    tile = FLASH_TILE
    n_near = _n_near_tiles(tile)
    n_q = seq // tile
    n_a, n_b = 2 * A_HEADS, B_HEADS
    fm_blk = lambda w, off, whole: pl.BlockSpec(
        (w, seq if whole else tile),
        (lambda b, qi: (off // w, b)) if whole else (lambda b, qi: (off // w, b * n_q + qi)))
    tm_blk = lambda w, off: pl.BlockSpec((1, seq, w), lambda b, qi: (b, 0, off // w))
    out_blk = lambda w: pl.BlockSpec((1, tile, w), lambda b, qi: (b, qi, 0))
    state = lambda n: [pltpu.VMEM((n, 1, tile), jnp.float32),
                       pltpu.VMEM((n, ACC_ROWS, tile), jnp.float32),
                       pltpu.VMEM((2, n, tile, tile), jnp.float32),
                       pltpu.VMEM((2, n, 1, tile), jnp.float32)]
    return pl.pallas_call(
        functools.partial(_attn_ab_kernel, tile=tile, n_near=n_near),
        out_shape=(jax.ShapeDtypeStruct((bsz, seq, A_W), jnp.bfloat16),
                   jax.ShapeDtypeStruct((bsz, seq, B_W), jnp.bfloat16)),
        grid=(bsz, n_q),
        in_specs=[
            fm_blk(A_W, QA_T, False), tm_blk(A_W, KA), fm_blk(A_W, VA_T, True),
            pl.BlockSpec((A_HEADS, n_near + 1, tile, tile), lambda b, qi: (0, 0, 0, 0),
                         pipeline_mode=pl.Buffered(1)),
            pl.BlockSpec((1, 4, A_QK_DIM), lambda b, qi: (layer, 0, 0)),
            pl.BlockSpec(memory_space=pltpu.SMEM),
            pl.BlockSpec((1, HEAD_DIM, 1), lambda b, qi: (layer, 0, 0)),
            fm_blk(B_W, QB_T, False), tm_blk(B_W, KB),
            pl.BlockSpec((1, seq, LANES), lambda b, qi: (b, 0, 0)),
            fm_blk(B_W, VB_T, True),
        ],
        out_specs=(out_blk(A_W), out_blk(B_W)),
        scratch_shapes=state(n_a) + state(n_b),
        compiler_params=pltpu.CompilerParams(
            dimension_semantics=("parallel", "parallel"),
            vmem_limit_bytes=VMEM_LIMIT),
        name="attn_ab",
    )(proj_fm, proj_tm3, proj_fm, bias_a, lam_vecs, lam_init, sub_g,
      proj_fm, proj_tm3, ckp, proj_fm)


def _attn_c_kernel(q_ref, k_ref, v_ref, bias_ref, o_ref, qd_ref, kd_ref, vtd_ref, onat_ref,
                   lnat_ref, tq_ref, tk_ref, tv_ref, *, seq):
    g = pl.program_id(2)
    n_chunks = seq // C_TILE
    lane = lax.broadcasted_iota(jnp.int32, (1, LANES), 1)
    first = lane < HEAD_DIM

    def run_group(group, dil):
        n_tiles = seq // dil // C_TILE

        def out_rows(i):
            r, t = i // n_tiles, i % n_tiles
            return pl.ds(r + t * (C_TILE * dil), C_TILE, stride=dil)

        if dil > C_SPLIT:
            per = seq // C_SPLIT // C_TILE

            def presplit(j, carry):
                for u in range(C_UNROLL):
                    i = j * C_UNROLL + u
                    rows = pl.ds(i // per + (i % per) * (C_TILE * C_SPLIT), C_TILE, stride=C_SPLIT)
                    dst = pl.ds(pl.multiple_of(i * C_TILE, C_TILE), C_TILE)
                    for src, tmp in ((q_ref, tq_ref), (k_ref, tk_ref), (v_ref, tv_ref)):
                        tmp[dst, :] = src[rows, :]
                return carry

            lax.fori_loop(0, n_chunks // C_UNROLL, presplit, 0)
            srcs, step = (tq_ref, tk_ref, tv_ref), dil // C_SPLIT

            def src_rows(i):
                r, t = i // n_tiles, i % n_tiles
                base = (r % C_SPLIT) * (seq // C_SPLIT) + r // C_SPLIT + t * (C_TILE * step)
                return pl.ds(base, C_TILE, stride=step)
        else:
            srcs, src_rows = (q_ref, k_ref, v_ref), out_rows

        kd_ref[0:C_TILE, :] = jnp.zeros((C_TILE, LANES), jnp.bfloat16)
        vtd_ref[:, 0:C_TILE] = jnp.zeros((LANES, C_TILE), jnp.bfloat16)

        def gather(j, carry):
            ids = [j * C_UNROLL + u for u in range(C_UNROLL)]
            vts = [jnp.transpose(srcs[2][src_rows(i), :]) for i in ids]
            for i, vt in zip(ids, vts):
                rows = src_rows(i)
                dst = pl.multiple_of(i * C_TILE, C_TILE)
                qd_ref[pl.ds(dst, C_TILE), :] = srcs[0][rows, :].astype(jnp.bfloat16)
                kd_ref[pl.ds(dst + C_TILE, C_TILE), :] = srcs[1][rows, :].astype(jnp.bfloat16)
                vtd_ref[:, pl.ds(dst + C_TILE, C_TILE)] = vt.astype(jnp.bfloat16)
            return carry

        lax.fori_loop(0, n_chunks // C_UNROLL, gather, 0)

        def tiles(j, carry):
            ids = [j * C_UNROLL + u for u in range(C_UNROLL)]
            row0s = [pl.multiple_of(i * C_TILE, C_TILE) for i in ids]

            def scores(u):
                q = qd_ref[pl.ds(row0s[u], C_TILE), :]
                q2 = jnp.concatenate([jnp.where(first, q, jnp.zeros_like(q)),
                                      jnp.where(first, jnp.zeros_like(q), q)], axis=0)
                return _dot_nt(kd_ref[pl.ds(row0s[u], 2 * C_TILE), :], q2)

            def softmax_pv(u, st):
                ver = jnp.minimum(ids[u] % n_tiles, 1)
                res = []
                for h in range(2):
                    x = st[:, h * C_TILE:(h + 1) * C_TILE] + bias_ref[h, ver]
                    m = _colmax(x)
                    p = jnp.exp2(x - m).astype(jnp.bfloat16)
                    v_aug = jnp.concatenate(
                        [vtd_ref[h * HEAD_DIM:(h + 1) * HEAD_DIM, pl.ds(row0s[u], 2 * C_TILE)],
                         jnp.ones((ONES_ROWS, 2 * C_TILE), jnp.bfloat16)], axis=0)
                    res.append((m, jnp.dot(v_aug, p, preferred_element_type=jnp.float32)))
                return res

            def write_back(u, res):
                outs, lses = [], []
                for m, pv in res:
                    l = pv[HEAD_DIM:HEAD_DIM + 1]
                    outs.append(pv[0:HEAD_DIM] / l)
                    lses.append(jnp.broadcast_to(m + jnp.log2(l), (HEAD_DIM, C_TILE)))
                rows = out_rows(ids[u])
                onat_ref[group, rows, :] = jnp.transpose(jnp.concatenate(outs, axis=0))
                lnat_ref[group, rows, :] = jnp.transpose(jnp.concatenate(lses, axis=0))

            sts = {u: scores(u) for u in range(C_AHEAD)}
            pending = None
            for u in range(C_UNROLL):
                if u + C_AHEAD < C_UNROLL:
                    sts[u + C_AHEAD] = scores(u + C_AHEAD)
                res = softmax_pv(u, sts.pop(u))
                if pending is not None:
                    write_back(*pending)
                pending = (u, res)
            write_back(*pending)
            return carry

        lax.fori_loop(0, n_chunks // C_UNROLL, tiles, 0)

    for group, (_, dil) in enumerate(C_PAIRS):
        pl.when(g == group)(functools.partial(run_group, group, dil))

    @pl.when(g == C_GROUPS - 1)
    def _():
        rows_per = 512

        def combine(c, carry):
            rows = pl.ds(pl.multiple_of(c * rows_per, rows_per), rows_per)
            l0, l1, l2 = lnat_ref[0, rows, :], lnat_ref[1, rows, :], lnat_ref[2, rows, :]
            m = jnp.maximum(jnp.maximum(l0, l1), l2)
            e0, e1, e2 = jnp.exp2(l0 - m), jnp.exp2(l1 - m), jnp.exp2(l2 - m)
            num = e0 * onat_ref[0, rows, :] + e1 * onat_ref[1, rows, :] + e2 * onat_ref[2, rows, :]
            o_ref[0, rows, :] = (num / (e0 + e1 + e2)).astype(o_ref.dtype)
            return carry

        lax.fori_loop(0, seq // rows_per, combine, 0)


def _attn_c(proj_c, bias_c, bsz, seq):
    assert seq % (C_TILE * C_UNROLL) == 0 and seq % (C_TILE * max(d for _, d in C_PAIRS)) == 0
    slab = lambda role: pl.BlockSpec(
        (None, seq, LANES), lambda b, hp, g: (role * 2 * C_GROUPS + g * 2 + hp, b, 0))
    return pl.pallas_call(
        functools.partial(_attn_c_kernel, seq=seq),
        out_shape=jax.ShapeDtypeStruct((bsz, seq, C_W), jnp.bfloat16),
        grid=(bsz, C_HEADS // 2, C_GROUPS),
        in_specs=[slab(0), slab(1), slab(2),
                  pl.BlockSpec((2, 2, 2 * C_TILE, C_TILE), lambda b, hp, g: (g * 2 + hp, 0, 0, 0))],
        out_specs=pl.BlockSpec((1, seq, LANES), lambda b, hp, g: (b, 0, hp)),
        scratch_shapes=[pltpu.VMEM((seq, LANES), jnp.bfloat16),
                        pltpu.VMEM((seq + C_TILE, LANES), jnp.bfloat16),
                        pltpu.VMEM((LANES, seq + C_TILE), jnp.bfloat16),
                        pltpu.VMEM((C_GROUPS, seq, LANES), jnp.float32),
                        pltpu.VMEM((C_GROUPS, seq, LANES), jnp.float32)]
        + [pltpu.VMEM((seq, LANES), jnp.float32)] * 3,
        compiler_params=pltpu.CompilerParams(
            dimension_semantics=("parallel", "parallel", "arbitrary"),
            vmem_limit_bytes=VMEM_LIMIT),
        name="attn_c",
    )(proj_c, proj_c, proj_c, bias_c)


def _mix_mlp_kernel(x_ref, oa_ref, ob_ref, oc_ref, wo_ref, g_ref, w1_ref, w2_ref, fg_ref, out_ref, *,
                    final, tf):
    mixed = jnp.concatenate([oa_ref[...], ob_ref[...], oc_ref[...]], axis=1)
    acc = x_ref[...] + jnp.dot(mixed, wo_ref[0], preferred_element_type=jnp.float32)
    h = _rms(acc, g_ref[0]).astype(jnp.bfloat16)
    for c in range(D_FF // tf):
        u = jnp.dot(h, w1_ref[0, :, c * tf:(c + 1) * tf], preferred_element_type=jnp.float32)
        u = jnp.square(jnp.maximum(u, 0.0)).astype(jnp.bfloat16)
        acc = acc + jnp.dot(u, w2_ref[0, c * tf:(c + 1) * tf, :],
                            preferred_element_type=jnp.float32)
    out_ref[...] = _rms(acc, fg_ref[...]) if final else acc


def _mix_mlp(x2, oa, ob, oc, w_o, norm_g, w_1, w_2, final_g, layer, final, *, tm=MLP_TM, tf=MLP_TF):
    n_tok = x2.shape[0]
    row = lambda w: pl.BlockSpec((tm, w), lambda i: (i, 0))
    resident = lambda shape: pl.BlockSpec(shape, lambda i: (layer, 0, 0),
                                          pipeline_mode=pl.Buffered(1))
    return pl.pallas_call(
        functools.partial(_mix_mlp_kernel, final=final, tf=tf),
        out_shape=jax.ShapeDtypeStruct((n_tok, D_MODEL), jnp.float32),
        grid=(n_tok // tm,),
        in_specs=[row(D_MODEL), row(A_W), row(B_W), row(C_W),
                  resident((1, D_MODEL, D_MODEL)),
                  pl.BlockSpec((1, 1, D_MODEL), lambda i: (layer, 0, 0)),
                  resident((1, D_MODEL, D_FF)),
                  resident((1, D_FF, D_MODEL)),
                  pl.BlockSpec((1, D_MODEL), lambda i: (0, 0))],
        out_specs=row(D_MODEL),
        compiler_params=pltpu.CompilerParams(dimension_semantics=("parallel",),
                                             vmem_limit_bytes=VMEM_LIMIT),
        name="mix_mlp",
    )(x2, oa, ob, oc, w_o, norm_g, w_1, w_2, final_g)


def _prep_w_in(w_in):
    a0, b0, f0, c0 = 0, 3 * A_W, 3 * A_W + 3 * B_W, 3 * A_W + 3 * B_W + B_HEADS
    w16 = w_in.astype(jnp.bfloat16)
    w_tm = jnp.concatenate([w16[:, :, b0 + B_W:b0 + 2 * B_W],
                            w16[:, :, a0 + A_W:a0 + 2 * A_W]], axis=-1)
    w_fm = jnp.concatenate([
        w16[:, :, b0:b0 + B_W], w16[:, :, b0 + 2 * B_W:f0],
        w16[:, :, a0:a0 + A_W], w16[:, :, a0 + 2 * A_W:b0],
        w16[:, :, f0:c0],
    ], axis=-1)
    w_fm = jnp.pad(w_fm, ((0, 0), (0, 0), (0, F_ROWS - B_HEADS)))
    return w_tm, w_fm, w16[:, :, c0:]


def kernel(x, norm1_g, w_in, b_f, lam_q1, lam_k1, lam_q2, lam_k2, diff_norm_g, w_o, norm2_g,
           w_1, w_2, rel_bias, final_g):
    bsz, seq, _ = x.shape
    depth = w_in.shape[0]
    w_tm, w_fm, w_c = _prep_w_in(w_in)
    w_o16, w_116, w_216 = (w.astype(jnp.bfloat16) for w in (w_o, w_1, w_2))
    lam_vecs = jnp.stack([lam_q1, lam_k1, lam_q2, lam_k2], axis=1)
    bias_a, bias_c = _make_bias_tiles(rel_bias)

    x2 = x.reshape(bsz * seq, D_MODEL)
    for l in range(depth):
        proj_tm, proj_fm, ft, proj_c = _in_proj(x2, norm1_g[:, None, :], w_tm, w_fm, w_c, l)
        proj3 = proj_tm.reshape(bsz, seq, N_TM)
        ckp = _decay(ft, b_f[:, :, None], l, bsz, seq)
        lam_init = jnp.full((1,), 0.8 - 0.6 * math.exp(-0.3 * l), jnp.float32)
        oa, ob = _attn_ab(proj3, proj_fm, bias_a, lam_vecs, lam_init, diff_norm_g[:, :, None],
                          ckp, l)
        oc = _attn_c(proj_c, bias_c, bsz, seq)
        x2 = _mix_mlp(x2, oa.reshape(bsz * seq, A_W), ob.reshape(bsz * seq, B_W),
                      oc.reshape(bsz * seq, C_W), w_o16, norm2_g[:, None, :], w_116, w_216,
                      final_g[None, :], l, l == depth - 1)
    return x2.reshape(bsz, seq, D_MODEL)
```

```python
import functools
import math

import jax
import jax.numpy as jnp
from jax import lax
from jax.experimental import pallas as pl
from jax.experimental.pallas import tpu as pltpu

D_MODEL = 1024
HEAD_DIM = 64
A_HEADS = 4
A_QK_DIM = 32
B_HEADS = 8
C_HEADS = 4
C_PAIRS = ((128, 1), (512, 4), (2048, 16))
C_GROUPS = 3
A_W = 256
B_W = 512
C_W = 256
D_FF = 4096
N_BUCKETS = 32
MAX_DISTANCE = 2048
NORM_EPS = 1e-6
NEG_INF = -1e30
LOG2E = 1.4426950408889634
SCALE_A = A_QK_DIM ** -0.5 * LOG2E
SCALE_BC = HEAD_DIM ** -0.5 * LOG2E

LANES = 128
VMEM_LIMIT = 56 * 1024 * 1024

KB, KA = 0, 512
N_TM = 768
QB_T, VB_T, QA_T, VA_T = 0, 512, 1024, 1280
N_FM = 1536
N_C = 2304
N_C_SLABS = N_C // 128
F_ROWS = 16
ONES_ROWS = 16
ACC_ROWS = HEAD_DIM + ONES_ROWS

C_TILE = 128
C_UNROLL = 32
C_AHEAD = 3
C_SPLIT = 4
FLASH_TILE = 256
MLP_TM = 1024
MLP_TF = 1024
A_GROUP = 1
B_GROUP = 1
FLASH_PAIRS = 2


def _bucket_thresholds():
    max_exact = N_BUCKETS // 2
    out = []
    for k in range(1, N_BUCKETS - max_exact):
        t = max_exact * (MAX_DISTANCE / max_exact) ** (k / (N_BUCKETS - max_exact))
        out.append(int(math.ceil(t)))
    return tuple(out)


_THRESHOLDS = _bucket_thresholds()


def _n_near_tiles(tile):
    return -(-(_THRESHOLDS[-1] + tile - 1) // tile)


def _t5_bucket(d):
    big = jnp.full(d.shape, N_BUCKETS // 2, jnp.int32)
    for t in _THRESHOLDS:
        big = big + (d >= t).astype(jnp.int32)
    return jnp.where(d < N_BUCKETS // 2, d, big)


def _bias_lookup(rb_ref, bucket, col):
    val = jnp.zeros(bucket.shape, jnp.float32)
    for b in range(N_BUCKETS):
        val = jnp.where(bucket == b, rb_ref[b, col], val)
    return val


def _bias_a_kernel(rb_ref, out_ref, *, tile):
    delta = pl.program_id(0)
    j = lax.broadcasted_iota(jnp.int32, (tile, tile), 0)
    i = lax.broadcasted_iota(jnp.int32, (tile, tile), 1)
    d = delta * tile + i - j
    bucket = _t5_bucket(d)
    for h in range(A_HEADS):
        val = _bias_lookup(rb_ref, bucket, h) * LOG2E
        out_ref[h, 0] = jnp.where(d >= 0, val, NEG_INF)


def _bias_c_kernel(rb_ref, out_ref):
    g = pl.program_id(0)
    dil = jnp.where(g == 0, C_PAIRS[0][1], jnp.where(g == 1, C_PAIRS[1][1], C_PAIRS[2][1]))
    c = lax.broadcasted_iota(jnp.int32, (2 * C_TILE, C_TILE), 0)
    i = lax.broadcasted_iota(jnp.int32, (2 * C_TILE, C_TILE), 1)
    steps = i + C_TILE - c
    valid = (steps >= 0) & (steps <= C_TILE)
    bucket = _t5_bucket(steps * dil)
    for h in range(C_HEADS):
        val = _bias_lookup(rb_ref, bucket, A_HEADS + g * C_HEADS + h) * LOG2E
        out_ref[h, 1] = jnp.where(valid, val, NEG_INF)
        out_ref[h, 0] = jnp.where(valid & (c >= C_TILE), val, NEG_INF)


def _make_bias_tiles(rel_bias):
    tile = FLASH_TILE
    n_bias = _n_near_tiles(tile) + 1
    smem = pl.BlockSpec(memory_space=pltpu.SMEM)
    bias_a = pl.pallas_call(
        functools.partial(_bias_a_kernel, tile=tile),
        out_shape=jax.ShapeDtypeStruct((A_HEADS, n_bias, tile, tile), jnp.float32),
        grid=(n_bias,),
        in_specs=[smem],
        out_specs=pl.BlockSpec((A_HEADS, 1, tile, tile), lambda d: (0, d, 0, 0)),
        name="bias_a_tiles",
    )(rel_bias)
    bias_c = pl.pallas_call(
        _bias_c_kernel,
        out_shape=jax.ShapeDtypeStruct((C_GROUPS * C_HEADS, 2, 2 * C_TILE, C_TILE), jnp.float32),
        grid=(C_GROUPS,),
        in_specs=[smem],
        out_specs=pl.BlockSpec((C_HEADS, 2, 2 * C_TILE, C_TILE), lambda g: (g, 0, 0, 0)),
        name="bias_c_tiles",
    )(rel_bias)
    return bias_a, bias_c


def _rms(x, g):
    return x * lax.rsqrt(jnp.mean(x * x, axis=-1, keepdims=True) + NORM_EPS) * g


def _dot_nt(a, b):
    return lax.dot_general(a, b, (((1,), (1,)), ((), ())), preferred_element_type=jnp.float32)


def _in_proj_kernel(x_ref, g_ref, wtm_ref, wfm_ref, wc_ref, tm_ref, fm_ref, ft_ref, c_ref):
    h = _rms(x_ref[...], g_ref[0]).astype(jnp.bfloat16)
    tm_ref[...] = jnp.dot(h, wtm_ref[0], preferred_element_type=jnp.float32).astype(jnp.bfloat16)
    fm = lax.dot_general(wfm_ref[0], h, (((0,), (1,)), ((), ())),
                         preferred_element_type=jnp.float32)
    for lo, hi, scale in ((QB_T, VB_T, SCALE_BC), (VB_T, QA_T, 1.0), (QA_T, VA_T, SCALE_A),
                          (VA_T, N_FM, 1.0)):
        blk = fm[lo:hi] if scale == 1.0 else fm[lo:hi] * scale
        fm_ref[lo:hi, :] = blk.astype(jnp.bfloat16)
    ft_ref[...] = fm[N_FM:]
    for c in range(N_C // 256):
        pc = jnp.dot(h, wc_ref[0, :, c * 256:(c + 1) * 256], preferred_element_type=jnp.float32)
        if c * 256 < C_GROUPS * C_W:
            pc = pc * SCALE_BC
        c_ref[2 * c] = pc[:, :LANES]
        c_ref[2 * c + 1] = pc[:, LANES:]


def _in_proj(x2, norm_g, w_tm, w_fm, w_c, layer, *, tm=1024):
    n_tok = x2.shape[0]
    return pl.pallas_call(
        _in_proj_kernel,
        out_shape=(jax.ShapeDtypeStruct((n_tok, N_TM), jnp.bfloat16),
                   jax.ShapeDtypeStruct((N_FM, n_tok), jnp.bfloat16),
                   jax.ShapeDtypeStruct((F_ROWS, n_tok), jnp.float32),
                   jax.ShapeDtypeStruct((N_C_SLABS, n_tok, LANES), jnp.float32)),
        grid=(n_tok // tm,),
        in_specs=[
            pl.BlockSpec((tm, D_MODEL), lambda i: (i, 0)),
            pl.BlockSpec((1, 1, D_MODEL), lambda i: (layer, 0, 0)),
            pl.BlockSpec((1, D_MODEL, N_TM), lambda i: (layer, 0, 0), pipeline_mode=pl.Buffered(1)),
            pl.BlockSpec((1, D_MODEL, N_FM + F_ROWS), lambda i: (layer, 0, 0),
                         pipeline_mode=pl.Buffered(1)),
            pl.BlockSpec((1, D_MODEL, N_C), lambda i: (layer, 0, 0), pipeline_mode=pl.Buffered(1)),
        ],
        out_specs=(pl.BlockSpec((tm, N_TM), lambda i: (i, 0)),
                   pl.BlockSpec((N_FM, tm), lambda i: (0, i)),
                   pl.BlockSpec((F_ROWS, tm), lambda i: (0, i)),
                   pl.BlockSpec((N_C_SLABS, tm, LANES), lambda i: (0, i, 0))),
        compiler_params=pltpu.CompilerParams(dimension_semantics=("parallel",),
                                             vmem_limit_bytes=VMEM_LIMIT),
        name="in_proj",
    )(x2, norm_g, w_tm, w_fm, w_c)


def _decay_kernel(ft_ref, bf_ref, ckp_ref):
    z = ft_ref[0:B_HEADS, :] + bf_ref[0]
    x = jnp.minimum(z, 0.0) - jnp.log1p(jnp.exp(-jnp.abs(z)))
    seq = x.shape[1]
    pos = lax.broadcasted_iota(jnp.int32, x.shape, 1)
    shift = 1
    while shift < seq:
        x = x + jnp.where(pos >= shift, pltpu.roll(x, shift, 1), 0.0)
        shift *= 2
    c = x * (-LOG2E)
    hi = c.astype(jnp.bfloat16).astype(jnp.float32)
    mid = (c - hi).astype(jnp.bfloat16).astype(jnp.float32)
    lo = c - hi - mid
    stacked = jnp.concatenate(
        [hi, mid, lo, jnp.zeros((LANES - 3 * B_HEADS, seq), jnp.float32)], axis=0)
    ckp_ref[0] = jnp.transpose(stacked).astype(jnp.bfloat16)


def _decay(ft, b_f, layer, bsz, seq):
    return pl.pallas_call(
        _decay_kernel,
        out_shape=jax.ShapeDtypeStruct((bsz, seq, LANES), jnp.bfloat16),
        grid=(bsz,),
        in_specs=[pl.BlockSpec((F_ROWS, seq), lambda b: (0, b)),
                  pl.BlockSpec((1, B_HEADS, 1), lambda b: (layer, 0, 0))],
        out_specs=pl.BlockSpec((1, seq, LANES), lambda b: (b, 0, 0)),
        compiler_params=pltpu.CompilerParams(dimension_semantics=("parallel",)),
        name="decay_scan",
    )(ft, b_f)


def _colmax(s):
    rows = s.shape[0]
    while rows > 8:
        rows //= 2
        s = jnp.maximum(s[:rows], s[rows:2 * rows])
    return jnp.max(s, axis=0, keepdims=True)


def _online_step(s, smax, v_aug, m_ref, acc_ref, idx):
    m_old = m_ref[idx]
    m_new = jnp.maximum(m_old, smax)
    alpha = jnp.exp2(m_old - m_new)
    p = jnp.exp2(s - m_new).astype(jnp.bfloat16)
    acc_ref[idx] = alpha * acc_ref[idx] + jnp.dot(v_aug, p, preferred_element_type=jnp.float32)
    m_ref[idx] = m_new


def _v_aug(vt_ref, head, k0, tile):
    v = vt_ref[head * HEAD_DIM:(head + 1) * HEAD_DIM, pl.ds(k0, tile)]
    return jnp.concatenate([v, jnp.ones((ONES_ROWS, tile), jnp.bfloat16)], axis=0)


def _flash_pipeline(qi, groups, produce, consume, finish):
    def overlapped(pk, pslot, ck, cslot):
        for grp in groups:
            produce(pk, pslot, grp)
            consume(ck, cslot, False, grp)

    def consume_all(k, slot):
        for grp in groups:
            consume(k, slot, True, grp)

    for grp in groups:
        produce(0, 0, grp)

    def pair(k):
        overlapped(k + 1, 1, k, 0)
        overlapped(k + 2, 0, k + 1, 1)

    def pairs(j, carry):
        for u in range(FLASH_PAIRS):
            pair(2 * (FLASH_PAIRS * j + u))
        return carry

    n_pairs = qi // 2
    lax.fori_loop(0, n_pairs // FLASH_PAIRS, pairs, 0)
    for u in range(FLASH_PAIRS - 1):
        done = (n_pairs // FLASH_PAIRS) * FLASH_PAIRS + u
        pl.when(done < n_pairs)(functools.partial(pair, 2 * done))

    @pl.when(qi % 2 == 0)
    def _():
        consume_all(qi, 0)
        finish()

    @pl.when(qi % 2 == 1)
    def _():
        overlapped(qi, 1, qi - 1, 0)
        consume_all(qi, 1)
        finish()


def _mixer_a_stages(qi, qt_ref, k_ref, vt_ref, bias_ref, lamv_ref, lami_ref, subg_ref, o_ref,
                    m_ref, acc_ref, s_ref, smax_ref, tile, n_near):
    n_sub = 2 * A_HEADS
    qt = qt_ref[...].astype(jnp.float32)
    row = lax.broadcasted_iota(jnp.int32, (LANES, tile), 0)
    qs = []
    for s in range(n_sub):
        blk = qt[(s // 4) * LANES:(s // 4 + 1) * LANES]
        lo = (s % 4) * A_QK_DIM
        qs.append(jnp.where((row >= lo) & (row < lo + A_QK_DIM), blk, 0.0).astype(jnp.bfloat16))
    m_ref[...] = jnp.full(m_ref.shape, NEG_INF, jnp.float32)
    acc_ref[...] = jnp.zeros(acc_ref.shape, jnp.float32)

    def produce(ki, slot, subs):
        k0 = pl.multiple_of(ki * tile, tile)
        delta = jnp.minimum(qi - ki, n_near)
        for s in subs:
            k = k_ref[0, pl.ds(k0, tile), (s // 4) * LANES:(s // 4 + 1) * LANES]
            x = jnp.dot(k, qs[s], preferred_element_type=jnp.float32) + bias_ref[s // 2, delta]
            s_ref[slot, s] = x
            smax_ref[slot, s] = _colmax(x)

    def consume(ki, slot, last, subs):
        k0 = pl.multiple_of(ki * tile, tile)
        for s in subs:
            _online_step(s_ref[slot, s], smax_ref[slot, s], _v_aug(vt_ref, s // 2, k0, tile),
                         m_ref, acc_ref, s)

    def finish():
        lam_init = lami_ref[0]
        lv = lamv_ref[0]
        e1 = jnp.exp(jnp.sum(lv[0:1] * lv[1:2], axis=1, keepdims=True))
        e2 = jnp.exp(jnp.sum(lv[2:3] * lv[3:4], axis=1, keepdims=True))
        lam = e1 - e2 + lam_init
        heads = []
        for h in range(A_HEADS):
            a0, a1 = acc_ref[2 * h], acc_ref[2 * h + 1]
            o = (a0[0:HEAD_DIM] / a0[HEAD_DIM:HEAD_DIM + 1]
                 - lam * (a1[0:HEAD_DIM] / a1[HEAD_DIM:HEAD_DIM + 1]))
            ms = jnp.mean(o * o, axis=0, keepdims=True)
            heads.append(o * lax.rsqrt(ms + NORM_EPS) * subg_ref[0] * (1.0 - lam_init))
        o_ref[0] = jnp.transpose(jnp.concatenate(heads, axis=0)).astype(o_ref.dtype)

    return produce, consume, finish


def _mixer_b_stages(qi, qt_ref, k_ref, ckp_ref, vt_ref, o_ref, m_ref, acc_ref, s_ref, smax_ref,
                    tile):
    n_h = B_HEADS
    qt = qt_ref[...].astype(jnp.float32)
    row = lax.broadcasted_iota(jnp.int32, (LANES, tile), 0)
    q_aug = []
    for s in range(n_h):
        blk = qt[(s // 2) * LANES:(s // 2 + 1) * LANES]
        lo = (s % 2) * HEAD_DIM
        q_s = jnp.where((row >= lo) & (row < lo + HEAD_DIM), blk, 0.0)
        pick = (row < 3 * B_HEADS) & ((row % B_HEADS) == s)
        ones = jnp.where(pick, 1.0, 0.0)
        q_aug.append(jnp.concatenate([q_s, ones], axis=0).astype(jnp.bfloat16))
    m_ref[...] = jnp.full(m_ref.shape, NEG_INF, jnp.float32)
    acc_ref[...] = jnp.zeros(acc_ref.shape, jnp.float32)

    def produce(ki, slot, subs):
        k0 = pl.multiple_of(ki * tile, tile)
        ck = ckp_ref[0, pl.ds(k0, tile), :]
        for s in subs:
            k = k_ref[0, pl.ds(k0, tile), (s // 2) * LANES:(s // 2 + 1) * LANES]
            x = jnp.dot(jnp.concatenate([k, ck], axis=1), q_aug[s],
                        preferred_element_type=jnp.float32)
            s_ref[slot, s] = x
            smax_ref[slot, s] = _colmax(x)

    def consume(ki, slot, last, subs):
        k0 = pl.multiple_of(ki * tile, tile)
        if last:
            kk = lax.broadcasted_iota(jnp.int32, (tile, tile), 0)
            qq = lax.broadcasted_iota(jnp.int32, (tile, tile), 1)
            keep = kk <= qq
        for s in subs:
            x, xmax = s_ref[slot, s], smax_ref[slot, s]
            if last:
                x = jnp.where(keep, x, NEG_INF)
                xmax = _colmax(x)
            _online_step(x, xmax, _v_aug(vt_ref, s, k0, tile), m_ref, acc_ref, s)

    def finish():
        heads = [acc_ref[s][0:HEAD_DIM] / acc_ref[s][HEAD_DIM:HEAD_DIM + 1] for s in range(n_h)]
        o_ref[0] = jnp.transpose(jnp.concatenate(heads, axis=0)).astype(o_ref.dtype)

    return produce, consume, finish


def _attn_ab_kernel(qa_ref, ka_ref, va_ref, bias_ref, lamv_ref, lami_ref, subg_ref,
                    qb_ref, kb_ref, ckp_ref, vb_ref, oa_ref, ob_ref,
                    ma_ref, acca_ref, sa_ref, smaxa_ref, mb_ref, accb_ref, sb_ref, smaxb_ref, *,
                    tile, n_near):
    qi = pl.program_id(1)
    stages = (_mixer_a_stages(qi, qa_ref, ka_ref, va_ref, bias_ref, lamv_ref, lami_ref, subg_ref,
                              oa_ref, ma_ref, acca_ref, sa_ref, smaxa_ref, tile, n_near),
              _mixer_b_stages(qi, qb_ref, kb_ref, ckp_ref, vb_ref, ob_ref, mb_ref, accb_ref,
                              sb_ref, smaxb_ref, tile))
    groups = []
    for g in range(2 * A_HEADS // A_GROUP):
        b_per = B_HEADS * A_GROUP // (2 * A_HEADS)
        for b0 in range(g * b_per, (g + 1) * b_per, B_GROUP):
            groups.append((1, range(b0, b0 + B_GROUP)))
        groups.append((0, range(g * A_GROUP, (g + 1) * A_GROUP)))

    def finish():
        stages[0][2]()
        stages[1][2]()

    _flash_pipeline(qi, groups,
                    lambda k, slot, grp: stages[grp[0]][0](k, slot, grp[1]),
                    lambda k, slot, last, grp: stages[grp[0]][1](k, slot, last, grp[1]),
                    finish)


def _attn_ab(proj_tm3, proj_fm, bias_a, lam_vecs, lam_init, sub_g, ckp, layer):
    bsz, seq, _ = proj_tm3.shape
    tile = FLASH_TILE
    n_near = _n_near_tiles(tile)
    n_q = seq // tile
    n_a, n_b = 2 * A_HEADS, B_HEADS
    fm_blk = lambda w, off, whole: pl.BlockSpec(
        (w, seq if whole else tile),
        (lambda b, qi: (off // w, b)) if whole else (lambda b, qi: (off // w, b * n_q + qi)))
    tm_blk = lambda w, off: pl.BlockSpec((1, seq, w), lambda b, qi: (b, 0, off // w))
    out_blk = lambda w: pl.BlockSpec((1, tile, w), lambda b, qi: (b, qi, 0))
    state = lambda n: [pltpu.VMEM((n, 1, tile), jnp.float32),
                       pltpu.VMEM((n, ACC_ROWS, tile), jnp.float32),
                       pltpu.VMEM((2, n, tile, tile), jnp.float32),
                       pltpu.VMEM((2, n, 1, tile), jnp.float32)]
    return pl.pallas_call(
        functools.partial(_attn_ab_kernel, tile=tile, n_near=n_near),
        out_shape=(jax.ShapeDtypeStruct((bsz, seq, A_W), jnp.bfloat16),
                   jax.ShapeDtypeStruct((bsz, seq, B_W), jnp.bfloat16)),
        grid=(bsz, n_q),
        in_specs=[
            fm_blk(A_W, QA_T, False), tm_blk(A_W, KA), fm_blk(A_W, VA_T, True),
            pl.BlockSpec((A_HEADS, n_near + 1, tile, tile), lambda b, qi: (0, 0, 0, 0),
                         pipeline_mode=pl.Buffered(1)),
            pl.BlockSpec((1, 4, A_QK_DIM), lambda b, qi: (layer, 0, 0)),
            pl.BlockSpec(memory_space=pltpu.SMEM),
            pl.BlockSpec((1, HEAD_DIM, 1), lambda b, qi: (layer, 0, 0)),
            fm_blk(B_W, QB_T, False), tm_blk(B_W, KB),
            pl.BlockSpec((1, seq, LANES), lambda b, qi: (b, 0, 0)),
            fm_blk(B_W, VB_T, True),
        ],
        out_specs=(out_blk(A_W), out_blk(B_W)),
        scratch_shapes=state(n_a) + state(n_b),
        compiler_params=pltpu.CompilerParams(
            dimension_semantics=("parallel", "parallel"),
            vmem_limit_bytes=VMEM_LIMIT),
        name="attn_ab",
    )(proj_fm, proj_tm3, proj_fm, bias_a, lam_vecs, lam_init, sub_g,
      proj_fm, proj_tm3, ckp, proj_fm)


def _attn_c_kernel(q_ref, k_ref, v_ref, bias_ref, o_ref, qd_ref, kd_ref, vtd_ref, onat_ref,
                   lnat_ref, tq_ref, tk_ref, tv_ref, *, seq):
    g = pl.program_id(2)
    n_chunks = seq // C_TILE
    lane = lax.broadcasted_iota(jnp.int32, (1, LANES), 1)
    first = lane < HEAD_DIM

    def run_group(group, dil):
        n_tiles = seq // dil // C_TILE

        def out_rows(i):
            r, t = i // n_tiles, i % n_tiles
            return pl.ds(r + t * (C_TILE * dil), C_TILE, stride=dil)

        if dil > C_SPLIT:
            per = seq // C_SPLIT // C_TILE

            def presplit(j, carry):
                for u in range(C_UNROLL):
                    i = j * C_UNROLL + u
                    rows = pl.ds(i // per + (i % per) * (C_TILE * C_SPLIT), C_TILE, stride=C_SPLIT)
                    dst = pl.ds(pl.multiple_of(i * C_TILE, C_TILE), C_TILE)
                    for src, tmp in ((q_ref, tq_ref), (k_ref, tk_ref), (v_ref, tv_ref)):
                        tmp[dst, :] = src[rows, :]
                return carry

            lax.fori_loop(0, n_chunks // C_UNROLL, presplit, 0)
            srcs, step = (tq_ref, tk_ref, tv_ref), dil // C_SPLIT

            def src_rows(i):
                r, t = i // n_tiles, i % n_tiles
                base = (r % C_SPLIT) * (seq // C_SPLIT) + r // C_SPLIT + t * (C_TILE * step)
                return pl.ds(base, C_TILE, stride=step)
        else:
            srcs, src_rows = (q_ref, k_ref, v_ref), out_rows

        kd_ref[0:C_TILE, :] = jnp.zeros((C_TILE, LANES), jnp.bfloat16)
        vtd_ref[:, 0:C_TILE] = jnp.zeros((LANES, C_TILE), jnp.bfloat16)

        def gather(j, carry):
            ids = [j * C_UNROLL + u for u in range(C_UNROLL)]
            vts = [jnp.transpose(srcs[2][src_rows(i), :]) for i in ids]
            for i, vt in zip(ids, vts):
                rows = src_rows(i)
                dst = pl.multiple_of(i * C_TILE, C_TILE)
                qd_ref[pl.ds(dst, C_TILE), :] = srcs[0][rows, :].astype(jnp.bfloat16)
                kd_ref[pl.ds(dst + C_TILE, C_TILE), :] = srcs[1][rows, :].astype(jnp.bfloat16)
                vtd_ref[:, pl.ds(dst + C_TILE, C_TILE)] = vt.astype(jnp.bfloat16)
            return carry

        lax.fori_loop(0, n_chunks // C_UNROLL, gather, 0)

        def tiles(j, carry):
            ids = [j * C_UNROLL + u for u in range(C_UNROLL)]
            row0s = [pl.multiple_of(i * C_TILE, C_TILE) for i in ids]

            def scores(u):
                q = qd_ref[pl.ds(row0s[u], C_TILE), :]
                q2 = jnp.concatenate([jnp.where(first, q, jnp.zeros_like(q)),
                                      jnp.where(first, jnp.zeros_like(q), q)], axis=0)
                return _dot_nt(kd_ref[pl.ds(row0s[u], 2 * C_TILE), :], q2)

            def softmax_pv(u, st):
                ver = jnp.minimum(ids[u] % n_tiles, 1)
                res = []
                for h in range(2):
                    x = st[:, h * C_TILE:(h + 1) * C_TILE] + bias_ref[h, ver]
                    m = _colmax(x)
                    p = jnp.exp2(x - m).astype(jnp.bfloat16)
                    v_aug = jnp.concatenate(
                        [vtd_ref[h * HEAD_DIM:(h + 1) * HEAD_DIM, pl.ds(row0s[u], 2 * C_TILE)],
                         jnp.ones((ONES_ROWS, 2 * C_TILE), jnp.bfloat16)], axis=0)
                    res.append((m, jnp.dot(v_aug, p, preferred_element_type=jnp.float32)))
                return res

            def write_back(u, res):
                outs, lses = [], []
                for m, pv in res:
                    l = pv[HEAD_DIM:HEAD_DIM + 1]
                    outs.append(pv[0:HEAD_DIM] / l)
                    lses.append(jnp.broadcast_to(m + jnp.log2(l), (HEAD_DIM, C_TILE)))
                rows = out_rows(ids[u])
                onat_ref[group, rows, :] = jnp.transpose(jnp.concatenate(outs, axis=0))
                lnat_ref[group, rows, :] = jnp.transpose(jnp.concatenate(lses, axis=0))

            sts = {u: scores(u) for u in range(C_AHEAD)}
            pending = None
            for u in range(C_UNROLL):
                if u + C_AHEAD < C_UNROLL:
                    sts[u + C_AHEAD] = scores(u + C_AHEAD)
                res = softmax_pv(u, sts.pop(u))
                if pending is not None:
                    write_back(*pending)
                pending = (u, res)
            write_back(*pending)
            return carry

        lax.fori_loop(0, n_chunks // C_UNROLL, tiles, 0)

    for group, (_, dil) in enumerate(C_PAIRS):
        pl.when(g == group)(functools.partial(run_group, group, dil))

    @pl.when(g == C_GROUPS - 1)
    def _():
        rows_per = 512

        def combine(c, carry):
            rows = pl.ds(pl.multiple_of(c * rows_per, rows_per), rows_per)
            l0, l1, l2 = lnat_ref[0, rows, :], lnat_ref[1, rows, :], lnat_ref[2, rows, :]
            m = jnp.maximum(jnp.maximum(l0, l1), l2)
            e0, e1, e2 = jnp.exp2(l0 - m), jnp.exp2(l1 - m), jnp.exp2(l2 - m)
            num = e0 * onat_ref[0, rows, :] + e1 * onat_ref[1, rows, :] + e2 * onat_ref[2, rows, :]
            o_ref[0, rows, :] = (num / (e0 + e1 + e2)).astype(o_ref.dtype)
            return carry

        lax.fori_loop(0, seq // rows_per, combine, 0)


def _attn_c(proj_c, bias_c, bsz, seq):
    assert seq % (C_TILE * C_UNROLL) == 0 and seq % (C_TILE * max(d for _, d in C_PAIRS)) == 0
    slab = lambda role: pl.BlockSpec(
        (None, seq, LANES), lambda b, hp, g: (role * 2 * C_GROUPS + g * 2 + hp, b, 0))
    return pl.pallas_call(
        functools.partial(_attn_c_kernel, seq=seq),
        out_shape=jax.ShapeDtypeStruct((bsz, seq, C_W), jnp.bfloat16),
        grid=(bsz, C_HEADS // 2, C_GROUPS),
        in_specs=[slab(0), slab(1), slab(2),
                  pl.BlockSpec((2, 2, 2 * C_TILE, C_TILE), lambda b, hp, g: (g * 2 + hp, 0, 0, 0))],
        out_specs=pl.BlockSpec((1, seq, LANES), lambda b, hp, g: (b, 0, hp)),
        scratch_shapes=[pltpu.VMEM((seq, LANES), jnp.bfloat16),
                        pltpu.VMEM((seq + C_TILE, LANES), jnp.bfloat16),
                        pltpu.VMEM((LANES, seq + C_TILE), jnp.bfloat16),
                        pltpu.VMEM((C_GROUPS, seq, LANES), jnp.float32),
                        pltpu.VMEM((C_GROUPS, seq, LANES), jnp.float32)]
        + [pltpu.VMEM((seq, LANES), jnp.float32)] * 3,
        compiler_params=pltpu.CompilerParams(
            dimension_semantics=("parallel", "parallel", "arbitrary"),
            vmem_limit_bytes=VMEM_LIMIT),
        name="attn_c",
    )(proj_c, proj_c, proj_c, bias_c)


def _mix_mlp_kernel(x_ref, oa_ref, ob_ref, oc_ref, wo_ref, g_ref, w1_ref, w2_ref, fg_ref, out_ref, *,
                    final, tf):
    mixed = jnp.concatenate([oa_ref[...], ob_ref[...], oc_ref[...]], axis=1)
    acc = x_ref[...] + jnp.dot(mixed, wo_ref[0], preferred_element_type=jnp.float32)
    h = _rms(acc, g_ref[0]).astype(jnp.bfloat16)
    for c in range(D_FF // tf):
        u = jnp.dot(h, w1_ref[0, :, c * tf:(c + 1) * tf], preferred_element_type=jnp.float32)
        u = jnp.square(jnp.maximum(u, 0.0)).astype(jnp.bfloat16)
        acc = acc + jnp.dot(u, w2_ref[0, c * tf:(c + 1) * tf, :],
                            preferred_element_type=jnp.float32)
    out_ref[...] = _rms(acc, fg_ref[...]) if final else acc


def _mix_mlp(x2, oa, ob, oc, w_o, norm_g, w_1, w_2, final_g, layer, final, *, tm=MLP_TM, tf=MLP_TF):
    n_tok = x2.shape[0]
    row = lambda w: pl.BlockSpec((tm, w), lambda i: (i, 0))
    resident = lambda shape: pl.BlockSpec(shape, lambda i: (layer, 0, 0),
                                          pipeline_mode=pl.Buffered(1))
    return pl.pallas_call(
        functools.partial(_mix_mlp_kernel, final=final, tf=tf),
        out_shape=jax.ShapeDtypeStruct((n_tok, D_MODEL), jnp.float32),
        grid=(n_tok // tm,),
        in_specs=[row(D_MODEL), row(A_W), row(B_W), row(C_W),
                  resident((1, D_MODEL, D_MODEL)),
                  pl.BlockSpec((1, 1, D_MODEL), lambda i: (layer, 0, 0)),
                  resident((1, D_MODEL, D_FF)),
                  resident((1, D_FF, D_MODEL)),
                  pl.BlockSpec((1, D_MODEL), lambda i: (0, 0))],
        out_specs=row(D_MODEL),
        compiler_params=pltpu.CompilerParams(dimension_semantics=("parallel",),
                                             vmem_limit_bytes=VMEM_LIMIT),
        name="mix_mlp",
    )(x2, oa, ob, oc, w_o, norm_g, w_1, w_2, final_g)


def _prep_w_in(w_in):
    a0, b0, f0, c0 = 0, 3 * A_W, 3 * A_W + 3 * B_W, 3 * A_W + 3 * B_W + B_HEADS
    w16 = w_in.astype(jnp.bfloat16)
    w_tm = jnp.concatenate([w16[:, :, b0 + B_W:b0 + 2 * B_W],
                            w16[:, :, a0 + A_W:a0 + 2 * A_W]], axis=-1)
    w_fm = jnp.concatenate([
        w16[:, :, b0:b0 + B_W], w16[:, :, b0 + 2 * B_W:f0],
        w16[:, :, a0:a0 + A_W], w16[:, :, a0 + 2 * A_W:b0],
        w16[:, :, f0:c0],
    ], axis=-1)
    w_fm = jnp.pad(w_fm, ((0, 0), (0, 0), (0, F_ROWS - B_HEADS)))
    return w_tm, w_fm, w16[:, :, c0:]


def kernel(x, norm1_g, w_in, b_f, lam_q1, lam_k1, lam_q2, lam_k2, diff_norm_g, w_o, norm2_g,
           w_1, w_2, rel_bias, final_g):
    bsz, seq, _ = x.shape
    depth = w_in.shape[0]
    w_tm, w_fm, w_c = _prep_w_in(w_in)
    w_o16, w_116, w_216 = (w.astype(jnp.bfloat16) for w in (w_o, w_1, w_2))
    lam_vecs = jnp.stack([lam_q1, lam_k1, lam_q2, lam_k2], axis=1)
    bias_a, bias_c = _make_bias_tiles(rel_bias)

    x2 = x.reshape(bsz * seq, D_MODEL)
    for l in range(depth):
        proj_tm, proj_fm, ft, proj_c = _in_proj(x2, norm1_g[:, None, :], w_tm, w_fm, w_c, l)
        proj3 = proj_tm.reshape(bsz, seq, N_TM)
        ckp = _decay(ft, b_f[:, :, None], l, bsz, seq)
        lam_init = jnp.full((1,), 0.8 - 0.6 * math.exp(-0.3 * l), jnp.float32)
        oa, ob = _attn_ab(proj3, proj_fm, bias_a, lam_vecs, lam_init, diff_norm_g[:, :, None],
                          ckp, l)
        oc = _attn_c(proj_c, bias_c, bsz, seq)
        x2 = _mix_mlp(x2, oa.reshape(bsz * seq, A_W), ob.reshape(bsz * seq, B_W),
                      oc.reshape(bsz * seq, C_W), w_o16, norm2_g[:, None, :], w_116, w_216,
                      final_g[None, :], l, l == depth - 1)
    return x2.reshape(bsz, seq, D_MODEL)
```

```python
import functools
import math

import jax
import jax.numpy as jnp
from jax import lax
from jax.experimental import pallas as pl
from jax.experimental.pallas import tpu as pltpu

D_MODEL = 1024
HEAD_DIM = 64
A_HEADS = 4
A_QK_DIM = 32
B_HEADS = 8
C_HEADS = 4
C_PAIRS = ((128, 1), (512, 4), (2048, 16))
C_GROUPS = 3
A_W = 256
B_W = 512
C_W = 256
D_FF = 4096
N_BUCKETS = 32
MAX_DISTANCE = 2048
NORM_EPS = 1e-6
NEG_INF = -1e30
LOG2E = 1.4426950408889634
SCALE_A = A_QK_DIM ** -0.5 * LOG2E
SCALE_BC = HEAD_DIM ** -0.5 * LOG2E

LANES = 128
VMEM_LIMIT = 56 * 1024 * 1024

KB, KA = 0, 512
N_TM = 768
QB_T, VB_T, QA_T, VA_T = 0, 512, 1024, 1280
N_FM = 1536
N_C = 2304
N_C_SLABS = N_C // 128
F_ROWS = 16
ONES_ROWS = 16
ACC_ROWS = HEAD_DIM + ONES_ROWS

C_TILE = 128
C_UNROLL = 32
C_AHEAD = 3
C_SPLIT = 4
FLASH_TILE = 256
MLP_TM = 1024
MLP_TF = 1024
A_GROUP = 1
B_GROUP = 1
FLASH_PAIRS = 2


def _bucket_thresholds():
    max_exact = N_BUCKETS // 2
    out = []
    for k in range(1, N_BUCKETS - max_exact):
        t = max_exact * (MAX_DISTANCE / max_exact) ** (k / (N_BUCKETS - max_exact))
        out.append(int(math.ceil(t)))
    return tuple(out)


_THRESHOLDS = _bucket_thresholds()


def _n_near_tiles(tile):
    return -(-(_THRESHOLDS[-1] + tile - 1) // tile)


def _t5_bucket(d):
    big = jnp.full(d.shape, N_BUCKETS // 2, jnp.int32)
    for t in _THRESHOLDS:
        big = big + (d >= t).astype(jnp.int32)
    return jnp.where(d < N_BUCKETS // 2, d, big)


def _bias_lookup(rb_ref, bucket, col):
    val = jnp.zeros(bucket.shape, jnp.float32)
    for b in range(N_BUCKETS):
        val = jnp.where(bucket == b, rb_ref[b, col], val)
    return val


def _bias_a_kernel(rb_ref, out_ref, *, tile):
    delta = pl.program_id(0)
    j = lax.broadcasted_iota(jnp.int32, (tile, tile), 0)
    i = lax.broadcasted_iota(jnp.int32, (tile, tile), 1)
    d = delta * tile + i - j
    bucket = _t5_bucket(d)
    for h in range(A_HEADS):
        val = _bias_lookup(rb_ref, bucket, h) * LOG2E
        out_ref[h, 0] = jnp.where(d >= 0, val, NEG_INF)


def _bias_c_kernel(rb_ref, out_ref):
    g = pl.program_id(0)
    dil = jnp.where(g == 0, C_PAIRS[0][1], jnp.where(g == 1, C_PAIRS[1][1], C_PAIRS[2][1]))
    c = lax.broadcasted_iota(jnp.int32, (2 * C_TILE, C_TILE), 0)
    i = lax.broadcasted_iota(jnp.int32, (2 * C_TILE, C_TILE), 1)
    steps = i + C_TILE - c
    valid = (steps >= 0) & (steps <= C_TILE)
    bucket = _t5_bucket(steps * dil)
    for h in range(C_HEADS):
        val = _bias_lookup(rb_ref, bucket, A_HEADS + g * C_HEADS + h) * LOG2E
        out_ref[h, 1] = jnp.where(valid, val, NEG_INF)
        out_ref[h, 0] = jnp.where(valid & (c >= C_TILE), val, NEG_INF)


def _make_bias_tiles(rel_bias):
    tile = FLASH_TILE
    n_bias = _n_near_tiles(tile) + 1
    smem = pl.BlockSpec(memory_space=pltpu.SMEM)
    bias_a = pl.pallas_call(
        functools.partial(_bias_a_kernel, tile=tile),
        out_shape=jax.ShapeDtypeStruct((A_HEADS, n_bias, tile, tile), jnp.float32),
        grid=(n_bias,),
        in_specs=[smem],
        out_specs=pl.BlockSpec((A_HEADS, 1, tile, tile), lambda d: (0, d, 0, 0)),
        name="bias_a_tiles",
    )(rel_bias)
    bias_c = pl.pallas_call(
        _bias_c_kernel,
        out_shape=jax.ShapeDtypeStruct((C_GROUPS * C_HEADS, 2, 2 * C_TILE, C_TILE), jnp.float32),
        grid=(C_GROUPS,),
        in_specs=[smem],
        out_specs=pl.BlockSpec((C_HEADS, 2, 2 * C_TILE, C_TILE), lambda g: (g, 0, 0, 0)),
        name="bias_c_tiles",
    )(rel_bias)
    return bias_a, bias_c


def _rms(x, g):
    return x * lax.rsqrt(jnp.mean(x * x, axis=-1, keepdims=True) + NORM_EPS) * g


def _dot_nt(a, b):
    return lax.dot_general(a, b, (((1,), (1,)), ((), ())), preferred_element_type=jnp.float32)


def _in_proj_kernel(x_ref, g_ref, wtm_ref, wfm_ref, wc_ref, tm_ref, fm_ref, ft_ref, c_ref):
    h = _rms(x_ref[...], g_ref[0]).astype(jnp.bfloat16)
    tm_ref[...] = jnp.dot(h, wtm_ref[0], preferred_element_type=jnp.float32).astype(jnp.bfloat16)
    fm = lax.dot_general(wfm_ref[0], h, (((0,), (1,)), ((), ())),
                         preferred_element_type=jnp.float32)
    for lo, hi, scale in ((QB_T, VB_T, SCALE_BC), (VB_T, QA_T, 1.0), (QA_T, VA_T, SCALE_A),
                          (VA_T, N_FM, 1.0)):
        blk = fm[lo:hi] if scale == 1.0 else fm[lo:hi] * scale
        fm_ref[lo:hi, :] = blk.astype(jnp.bfloat16)
    ft_ref[...] = fm[N_FM:]
    for c in range(N_C // 256):
        pc = jnp.dot(h, wc_ref[0, :, c * 256:(c + 1) * 256], preferred_element_type=jnp.float32)
        if c * 256 < C_GROUPS * C_W:
            pc = pc * SCALE_BC
        c_ref[2 * c] = pc[:, :LANES]
        c_ref[2 * c + 1] = pc[:, LANES:]


def _in_proj(x2, norm_g, w_tm, w_fm, w_c, layer, *, tm=1024):
    n_tok = x2.shape[0]
    return pl.pallas_call(
        _in_proj_kernel,
        out_shape=(jax.ShapeDtypeStruct((n_tok, N_TM), jnp.bfloat16),
                   jax.ShapeDtypeStruct((N_FM, n_tok), jnp.bfloat16),
                   jax.ShapeDtypeStruct((F_ROWS, n_tok), jnp.float32),
                   jax.ShapeDtypeStruct((N_C_SLABS, n_tok, LANES), jnp.float32)),
        grid=(n_tok // tm,),
        in_specs=[
            pl.BlockSpec((tm, D_MODEL), lambda i: (i, 0)),
            pl.BlockSpec((1, 1, D_MODEL), lambda i: (layer, 0, 0)),
            pl.BlockSpec((1, D_MODEL, N_TM), lambda i: (layer, 0, 0), pipeline_mode=pl.Buffered(1)),
            pl.BlockSpec((1, D_MODEL, N_FM + F_ROWS), lambda i: (layer, 0, 0),
                         pipeline_mode=pl.Buffered(1)),
            pl.BlockSpec((1, D_MODEL, N_C), lambda i: (layer, 0, 0), pipeline_mode=pl.Buffered(1)),
        ],
        out_specs=(pl.BlockSpec((tm, N_TM), lambda i: (i, 0)),
                   pl.BlockSpec((N_FM, tm), lambda i: (0, i)),
                   pl.BlockSpec((F_ROWS, tm), lambda i: (0, i)),
                   pl.BlockSpec((N_C_SLABS, tm, LANES), lambda i: (0, i, 0))),
        compiler_params=pltpu.CompilerParams(dimension_semantics=("parallel",),
                                             vmem_limit_bytes=VMEM_LIMIT),
        name="in_proj",
    )(x2, norm_g, w_tm, w_fm, w_c)


def _decay_kernel(ft_ref, bf_ref, ckp_ref):
    z = ft_ref[0:B_HEADS, :] + bf_ref[0]
    x = jnp.minimum(z, 0.0) - jnp.log1p(jnp.exp(-jnp.abs(z)))
    seq = x.shape[1]
    pos = lax.broadcasted_iota(jnp.int32, x.shape, 1)
    shift = 1
    while shift < seq:
        x = x + jnp.where(pos >= shift, pltpu.roll(x, shift, 1), 0.0)
        shift *= 2
    c = x * (-LOG2E)
    hi = c.astype(jnp.bfloat16).astype(jnp.float32)
    mid = (c - hi).astype(jnp.bfloat16).astype(jnp.float32)
    lo = c - hi - mid
    stacked = jnp.concatenate(
        [hi, mid, lo, jnp.zeros((LANES - 3 * B_HEADS, seq), jnp.float32)], axis=0)
    ckp_ref[0] = jnp.transpose(stacked).astype(jnp.bfloat16)


def _decay(ft, b_f, layer, bsz, seq):
    return pl.pallas_call(
        _decay_kernel,
        out_shape=jax.ShapeDtypeStruct((bsz, seq, LANES), jnp.bfloat16),
        grid=(bsz,),
        in_specs=[pl.BlockSpec((F_ROWS, seq), lambda b: (0, b)),
                  pl.BlockSpec((1, B_HEADS, 1), lambda b: (layer, 0, 0))],
        out_specs=pl.BlockSpec((1, seq, LANES), lambda b: (b, 0, 0)),
        compiler_params=pltpu.CompilerParams(dimension_semantics=("parallel",)),
        name="decay_scan",
    )(ft, b_f)


def _colmax(s):
    rows = s.shape[0]
    while rows > 8:
        rows //= 2
        s = jnp.maximum(s[:rows], s[rows:2 * rows])
    return jnp.max(s, axis=0, keepdims=True)


def _online_step(s, smax, v_aug, m_ref, acc_ref, idx):
    m_old = m_ref[idx]
    m_new = jnp.maximum(m_old, smax)
    alpha = jnp.exp2(m_old - m_new)
    p = jnp.exp2(s - m_new)
    part, rows = p, p.shape[0]
    while rows > 8:
        rows //= 2
        part = part[:rows] + part[rows:2 * rows]
    acc_ref[idx, 0:HEAD_DIM] = alpha * acc_ref[idx, 0:HEAD_DIM] + jnp.dot(
        v_aug, p.astype(jnp.bfloat16), preferred_element_type=jnp.float32)
    acc_ref[idx, HEAD_DIM:HEAD_DIM + 8] = alpha * acc_ref[idx, HEAD_DIM:HEAD_DIM + 8] + part
    m_ref[idx] = m_new


def _row_sum(acc):
    return jnp.sum(acc[HEAD_DIM:HEAD_DIM + 8], axis=0, keepdims=True)


def _v_aug(vt_ref, head, k0, tile):
    return vt_ref[head * HEAD_DIM:(head + 1) * HEAD_DIM, pl.ds(k0, tile)]


def _flash_pipeline(qi, groups, produce, consume, finish):
    def overlapped(pk, pslot, ck, cslot):
        for grp in groups:
            produce(pk, pslot, grp)
            consume(ck, cslot, False, grp)

    def consume_all(k, slot):
        for grp in groups:
            consume(k, slot, True, grp)

    for grp in groups:
        produce(0, 0, grp)

    def pair(k):
        overlapped(k + 1, 1, k, 0)
        overlapped(k + 2, 0, k + 1, 1)

    def pairs(j, carry):
        for u in range(FLASH_PAIRS):
            pair(2 * (FLASH_PAIRS * j + u))
        return carry

    n_pairs = qi // 2
    lax.fori_loop(0, n_pairs // FLASH_PAIRS, pairs, 0)
    for u in range(FLASH_PAIRS - 1):
        done = (n_pairs // FLASH_PAIRS) * FLASH_PAIRS + u
        pl.when(done < n_pairs)(functools.partial(pair, 2 * done))

    @pl.when(qi % 2 == 0)
    def _():
        consume_all(qi, 0)
        finish()

    @pl.when(qi % 2 == 1)
    def _():
        overlapped(qi, 1, qi - 1, 0)
        consume_all(qi, 1)
        finish()


def _mixer_a_stages(qi, qt_ref, k_ref, vt_ref, bias_ref, lamv_ref, lami_ref, subg_ref, o_ref,
                    m_ref, acc_ref, s_ref, smax_ref, tile, n_near):
    n_sub = 2 * A_HEADS
    qt = qt_ref[...].astype(jnp.float32)
    row = lax.broadcasted_iota(jnp.int32, (LANES, tile), 0)
    qs = []
    for s in range(n_sub):
        blk = qt[(s // 4) * LANES:(s // 4 + 1) * LANES]
        lo = (s % 4) * A_QK_DIM
        qs.append(jnp.where((row >= lo) & (row < lo + A_QK_DIM), blk, 0.0).astype(jnp.bfloat16))
    m_ref[...] = jnp.full(m_ref.shape, NEG_INF, jnp.float32)
    acc_ref[...] = jnp.zeros(acc_ref.shape, jnp.float32)

    def produce(ki, slot, subs):
        k0 = pl.multiple_of(ki * tile, tile)
        delta = jnp.minimum(qi - ki, n_near)
        for s in subs:
            k = k_ref[0, pl.ds(k0, tile), (s // 4) * LANES:(s // 4 + 1) * LANES]
            x = jnp.dot(k, qs[s], preferred_element_type=jnp.float32) + bias_ref[s // 2, delta]
            s_ref[slot, s] = x
            smax_ref[slot, s] = _colmax(x)

    def consume(ki, slot, last, subs):
        k0 = pl.multiple_of(ki * tile, tile)
        for s in subs:
            _online_step(s_ref[slot, s], smax_ref[slot, s], _v_aug(vt_ref, s // 2, k0, tile),
                         m_ref, acc_ref, s)

    def finish():
        lam_init = lami_ref[0]
        lv = lamv_ref[0]
        e1 = jnp.exp(jnp.sum(lv[0:1] * lv[1:2], axis=1, keepdims=True))
        e2 = jnp.exp(jnp.sum(lv[2:3] * lv[3:4], axis=1, keepdims=True))
        lam = e1 - e2 + lam_init
        heads = []
        for h in range(A_HEADS):
            a0, a1 = acc_ref[2 * h], acc_ref[2 * h + 1]
            o = (a0[0:HEAD_DIM] / _row_sum(a0)
                 - lam * (a1[0:HEAD_DIM] / _row_sum(a1)))
            ms = jnp.mean(o * o, axis=0, keepdims=True)
            heads.append(o * lax.rsqrt(ms + NORM_EPS) * subg_ref[0] * (1.0 - lam_init))
        o_ref[0] = jnp.transpose(jnp.concatenate(heads, axis=0)).astype(o_ref.dtype)

    return produce, consume, finish


def _mixer_b_stages(qi, qt_ref, k_ref, ckp_ref, vt_ref, o_ref, m_ref, acc_ref, s_ref, smax_ref,
                    tile):
    n_h = B_HEADS
    qt = qt_ref[...].astype(jnp.float32)
    row = lax.broadcasted_iota(jnp.int32, (LANES, tile), 0)
    q_aug = []
    for s in range(n_h):
        blk = qt[(s // 2) * LANES:(s // 2 + 1) * LANES]
        lo = (s % 2) * HEAD_DIM
        q_s = jnp.where((row >= lo) & (row < lo + HEAD_DIM), blk, 0.0)
        pick = (row < 3 * B_HEADS) & ((row % B_HEADS) == s)
        ones = jnp.where(pick, 1.0, 0.0)
        q_aug.append(jnp.concatenate([q_s, ones], axis=0).astype(jnp.bfloat16))
    m_ref[...] = jnp.full(m_ref.shape, NEG_INF, jnp.float32)
    acc_ref[...] = jnp.zeros(acc_ref.shape, jnp.float32)

    def produce(ki, slot, subs):
        k0 = pl.multiple_of(ki * tile, tile)
        ck = ckp_ref[0, pl.ds(k0, tile), :]
        for s in subs:
            k = k_ref[0, pl.ds(k0, tile), (s // 2) * LANES:(s // 2 + 1) * LANES]
            x = jnp.dot(jnp.concatenate([k, ck], axis=1), q_aug[s],
                        preferred_element_type=jnp.float32)
            s_ref[slot, s] = x
            smax_ref[slot, s] = _colmax(x)

    def consume(ki, slot, last, subs):
        k0 = pl.multiple_of(ki * tile, tile)
        if last:
            kk = lax.broadcasted_iota(jnp.int32, (tile, tile), 0)
            qq = lax.broadcasted_iota(jnp.int32, (tile, tile), 1)
            keep = kk <= qq
        for s in subs:
            x, xmax = s_ref[slot, s], smax_ref[slot, s]
            if last:
                x = jnp.where(keep, x, NEG_INF)
                xmax = _colmax(x)
            _online_step(x, xmax, _v_aug(vt_ref, s, k0, tile), m_ref, acc_ref, s)

    def finish():
        heads = [acc_ref[s][0:HEAD_DIM] / _row_sum(acc_ref[s]) for s in range(n_h)]
        o_ref[0] = jnp.transpose(jnp.concatenate(heads, axis=0)).astype(o_ref.dtype)

    return produce, consume, finish


def _attn_ab_kernel(qa_ref, ka_ref, va_ref, bias_ref, lamv_ref, lami_ref, subg_ref,
                    qb_ref, kb_ref, ckp_ref, vb_ref, oa_ref, ob_ref,
                    ma_ref, acca_ref, sa_ref, smaxa_ref, mb_ref, accb_ref, sb_ref, smaxb_ref, *,
                    tile, n_near):
    qi = pl.program_id(1)
    stages = (_mixer_a_stages(qi, qa_ref, ka_ref, va_ref, bias_ref, lamv_ref, lami_ref, subg_ref,
                              oa_ref, ma_ref, acca_ref, sa_ref, smaxa_ref, tile, n_near),
              _mixer_b_stages(qi, qb_ref, kb_ref, ckp_ref, vb_ref, ob_ref, mb_ref, accb_ref,
                              sb_ref, smaxb_ref, tile))
    groups = []
    for g in range(2 * A_HEADS // A_GROUP):
        b_per = B_HEADS * A_GROUP // (2 * A_HEADS)
        for b0 in range(g * b_per, (g + 1) * b_per, B_GROUP):
            groups.append((1, range(b0, b0 + B_GROUP)))
        groups.append((0, range(g * A_GROUP, (g + 1) * A_GROUP)))

    def finish():
        stages[0][2]()
        stages[1][2]()

    _flash_pipeline(qi, groups,
                    lambda k, slot, grp: stages[grp[0]][0](k, slot, grp[1]),
                    lambda k, slot, last, grp: stages[grp[0]][1](k, slot, last, grp[1]),
                    finish)


def _attn_ab(proj_tm3, proj_fm, bias_a, lam_vecs, lam_init, sub_g, ckp, layer):
    bsz, seq, _ = proj_tm3.shape
    tile = FLASH_TILE
    n_near = _n_near_tiles(tile)
    n_q = seq // tile
    n_a, n_b = 2 * A_HEADS, B_HEADS
    fm_blk = lambda w, off, whole: pl.BlockSpec(
        (w, seq if whole else tile),
        (lambda b, qi: (off // w, b)) if whole else (lambda b, qi: (off // w, b * n_q + qi)))
    tm_blk = lambda w, off: pl.BlockSpec((1, seq, w), lambda b, qi: (b, 0, off // w))
    out_blk = lambda w: pl.BlockSpec((1, tile, w), lambda b, qi: (b, qi, 0))
    state = lambda n: [pltpu.VMEM((n, 1, tile), jnp.float32),
                       pltpu.VMEM((n, ACC_ROWS, tile), jnp.float32),
                       pltpu.VMEM((2, n, tile, tile), jnp.float32),
                       pltpu.VMEM((2, n, 1, tile), jnp.float32)]
    return pl.pallas_call(
        functools.partial(_attn_ab_kernel, tile=tile, n_near=n_near),
        out_shape=(jax.ShapeDtypeStruct((bsz, seq, A_W), jnp.bfloat16),
                   jax.ShapeDtypeStruct((bsz, seq, B_W), jnp.bfloat16)),
        grid=(bsz, n_q),
        in_specs=[
            fm_blk(A_W, QA_T, False), tm_blk(A_W, KA), fm_blk(A_W, VA_T, True),
            pl.BlockSpec((A_HEADS, n_near + 1, tile, tile), lambda b, qi: (0, 0, 0, 0),
                         pipeline_mode=pl.Buffered(1)),
            pl.BlockSpec((1, 4, A_QK_DIM), lambda b, qi: (layer, 0, 0)),
            pl.BlockSpec(memory_space=pltpu.SMEM),
            pl.BlockSpec((1, HEAD_DIM, 1), lambda b, qi: (layer, 0, 0)),
            fm_blk(B_W, QB_T, False), tm_blk(B_W, KB),
            pl.BlockSpec((1, seq, LANES), lambda b, qi: (b, 0, 0)),
            fm_blk(B_W, VB_T, True),
        ],
        out_specs=(out_blk(A_W), out_blk(B_W)),
        scratch_shapes=state(n_a) + state(n_b),
        compiler_params=pltpu.CompilerParams(
            dimension_semantics=("parallel", "parallel"),
            vmem_limit_bytes=VMEM_LIMIT),
        name="attn_ab",
    )(proj_fm, proj_tm3, proj_fm, bias_a, lam_vecs, lam_init, sub_g,
      proj_fm, proj_tm3, ckp, proj_fm)


def _attn_c_kernel(q_ref, k_ref, v_ref, bias_ref, o_ref, qd_ref, kd_ref, vtd_ref, onat_ref,
                   lnat_ref, tq_ref, tk_ref, tv_ref, *, seq):
    g = pl.program_id(2)
    n_chunks = seq // C_TILE
    lane = lax.broadcasted_iota(jnp.int32, (1, LANES), 1)
    first = lane < HEAD_DIM

    def run_group(group, dil):
        n_tiles = seq // dil // C_TILE

        def out_rows(i):
            r, t = i // n_tiles, i % n_tiles
            return pl.ds(r + t * (C_TILE * dil), C_TILE, stride=dil)

        if dil > C_SPLIT:
            per = seq // C_SPLIT // C_TILE

            def presplit(j, carry):
                for u in range(C_UNROLL):
                    i = j * C_UNROLL + u
                    rows = pl.ds(i // per + (i % per) * (C_TILE * C_SPLIT), C_TILE, stride=C_SPLIT)
                    dst = pl.ds(pl.multiple_of(i * C_TILE, C_TILE), C_TILE)
                    for src, tmp in ((q_ref, tq_ref), (k_ref, tk_ref), (v_ref, tv_ref)):
                        tmp[dst, :] = src[rows, :]
                return carry

            lax.fori_loop(0, n_chunks // C_UNROLL, presplit, 0)
            srcs, step = (tq_ref, tk_ref, tv_ref), dil // C_SPLIT

            def src_rows(i):
                r, t = i // n_tiles, i % n_tiles
                base = (r % C_SPLIT) * (seq // C_SPLIT) + r // C_SPLIT + t * (C_TILE * step)
                return pl.ds(base, C_TILE, stride=step)
        else:
            srcs, src_rows = (q_ref, k_ref, v_ref), out_rows

        kd_ref[0:C_TILE, :] = jnp.zeros((C_TILE, LANES), jnp.bfloat16)
        vtd_ref[:, 0:C_TILE] = jnp.zeros((LANES, C_TILE), jnp.bfloat16)

        def gather(j, carry):
            ids = [j * C_UNROLL + u for u in range(C_UNROLL)]
            vts = [jnp.transpose(srcs[2][src_rows(i), :]) for i in ids]
            for i, vt in zip(ids, vts):
                rows = src_rows(i)
                dst = pl.multiple_of(i * C_TILE, C_TILE)
                qd_ref[pl.ds(dst, C_TILE), :] = srcs[0][rows, :].astype(jnp.bfloat16)
                kd_ref[pl.ds(dst + C_TILE, C_TILE), :] = srcs[1][rows, :].astype(jnp.bfloat16)
                vtd_ref[:, pl.ds(dst + C_TILE, C_TILE)] = vt.astype(jnp.bfloat16)
            return carry

        lax.fori_loop(0, n_chunks // C_UNROLL, gather, 0)

        def tiles(j, carry):
            ids = [j * C_UNROLL + u for u in range(C_UNROLL)]
            row0s = [pl.multiple_of(i * C_TILE, C_TILE) for i in ids]

            def scores(u):
                q = qd_ref[pl.ds(row0s[u], C_TILE), :]
                q2 = jnp.concatenate([jnp.where(first, q, jnp.zeros_like(q)),
                                      jnp.where(first, jnp.zeros_like(q), q)], axis=0)
                return _dot_nt(kd_ref[pl.ds(row0s[u], 2 * C_TILE), :], q2)

            def softmax_pv(u, st):
                ver = jnp.minimum(ids[u] % n_tiles, 1)
                res = []
                for h in range(2):
                    x = st[:, h * C_TILE:(h + 1) * C_TILE] + bias_ref[h, ver]
                    m = _colmax(x)
                    p = jnp.exp2(x - m).astype(jnp.bfloat16)
                    v_aug = jnp.concatenate(
                        [vtd_ref[h * HEAD_DIM:(h + 1) * HEAD_DIM, pl.ds(row0s[u], 2 * C_TILE)],
                         jnp.ones((ONES_ROWS, 2 * C_TILE), jnp.bfloat16)], axis=0)
                    res.append((m, jnp.dot(v_aug, p, preferred_element_type=jnp.float32)))
                return res

            def write_back(u, res):
                outs, lses = [], []
                for m, pv in res:
                    l = pv[HEAD_DIM:HEAD_DIM + 1]
                    outs.append(pv[0:HEAD_DIM] / l)
                    lses.append(jnp.broadcast_to(m + jnp.log2(l), (HEAD_DIM, C_TILE)))
                rows = out_rows(ids[u])
                onat_ref[group, rows, :] = jnp.transpose(jnp.concatenate(outs, axis=0))
                lnat_ref[group, rows, :] = jnp.transpose(jnp.concatenate(lses, axis=0))

            sts = {u: scores(u) for u in range(C_AHEAD)}
            pending = None
            for u in range(C_UNROLL):
                if u + C_AHEAD < C_UNROLL:
                    sts[u + C_AHEAD] = scores(u + C_AHEAD)
                res = softmax_pv(u, sts.pop(u))
                if pending is not None:
                    write_back(*pending)
                pending = (u, res)
            write_back(*pending)
            return carry

        lax.fori_loop(0, n_chunks // C_UNROLL, tiles, 0)

    for group, (_, dil) in enumerate(C_PAIRS):
        pl.when(g == group)(functools.partial(run_group, group, dil))

    @pl.when(g == C_GROUPS - 1)
    def _():
        rows_per = 512

        def combine(c, carry):
            rows = pl.ds(pl.multiple_of(c * rows_per, rows_per), rows_per)
            l0, l1, l2 = lnat_ref[0, rows, :], lnat_ref[1, rows, :], lnat_ref[2, rows, :]
            m = jnp.maximum(jnp.maximum(l0, l1), l2)
            e0, e1, e2 = jnp.exp2(l0 - m), jnp.exp2(l1 - m), jnp.exp2(l2 - m)
            num = e0 * onat_ref[0, rows, :] + e1 * onat_ref[1, rows, :] + e2 * onat_ref[2, rows, :]
            o_ref[0, rows, :] = (num / (e0 + e1 + e2)).astype(o_ref.dtype)
            return carry

        lax.fori_loop(0, seq // rows_per, combine, 0)


def _attn_c(proj_c, bias_c, bsz, seq):
    assert seq % (C_TILE * C_UNROLL) == 0 and seq % (C_TILE * max(d for _, d in C_PAIRS)) == 0
    slab = lambda role: pl.BlockSpec(
        (None, seq, LANES), lambda b, hp, g: (role * 2 * C_GROUPS + g * 2 + hp, b, 0))
    return pl.pallas_call(
        functools.partial(_attn_c_kernel, seq=seq),
        out_shape=jax.ShapeDtypeStruct((bsz, seq, C_W), jnp.bfloat16),
        grid=(bsz, C_HEADS // 2, C_GROUPS),
        in_specs=[slab(0), slab(1), slab(2),
                  pl.BlockSpec((2, 2, 2 * C_TILE, C_TILE), lambda b, hp, g: (g * 2 + hp, 0, 0, 0))],
        out_specs=pl.BlockSpec((1, seq, LANES), lambda b, hp, g: (b, 0, hp)),
        scratch_shapes=[pltpu.VMEM((seq, LANES), jnp.bfloat16),
                        pltpu.VMEM((seq + C_TILE, LANES), jnp.bfloat16),
                        pltpu.VMEM((LANES, seq + C_TILE), jnp.bfloat16),
                        pltpu.VMEM((C_GROUPS, seq, LANES), jnp.float32),
                        pltpu.VMEM((C_GROUPS, seq, LANES), jnp.float32)]
        + [pltpu.VMEM((seq, LANES), jnp.float32)] * 3,
        compiler_params=pltpu.CompilerParams(
            dimension_semantics=("parallel", "parallel", "arbitrary"),
            vmem_limit_bytes=VMEM_LIMIT),
        name="attn_c",
    )(proj_c, proj_c, proj_c, bias_c)


def _mix_mlp_kernel(x_ref, oa_ref, ob_ref, oc_ref, wo_ref, g_ref, w1_ref, w2_ref, fg_ref, out_ref, *,
                    final, tf):
    mixed = jnp.concatenate([oa_ref[...], ob_ref[...], oc_ref[...]], axis=1)
    acc = x_ref[...] + jnp.dot(mixed, wo_ref[0], preferred_element_type=jnp.float32)
    h = _rms(acc, g_ref[0]).astype(jnp.bfloat16)
    for c in range(D_FF // tf):
        u = jnp.dot(h, w1_ref[0, :, c * tf:(c + 1) * tf], preferred_element_type=jnp.float32)
        u = jnp.square(jnp.maximum(u, 0.0)).astype(jnp.bfloat16)
        acc = acc + jnp.dot(u, w2_ref[0, c * tf:(c + 1) * tf, :],
                            preferred_element_type=jnp.float32)
    out_ref[...] = _rms(acc, fg_ref[...]) if final else acc


def _mix_mlp(x2, oa, ob, oc, w_o, norm_g, w_1, w_2, final_g, layer, final, *, tm=MLP_TM, tf=MLP_TF):
    n_tok = x2.shape[0]
    row = lambda w: pl.BlockSpec((tm, w), lambda i: (i, 0))
    resident = lambda shape: pl.BlockSpec(shape, lambda i: (layer, 0, 0),
                                          pipeline_mode=pl.Buffered(1))
    return pl.pallas_call(
        functools.partial(_mix_mlp_kernel, final=final, tf=tf),
        out_shape=jax.ShapeDtypeStruct((n_tok, D_MODEL), jnp.float32),
        grid=(n_tok // tm,),
        in_specs=[row(D_MODEL), row(A_W), row(B_W), row(C_W),
                  resident((1, D_MODEL, D_MODEL)),
                  pl.BlockSpec((1, 1, D_MODEL), lambda i: (layer, 0, 0)),
                  resident((1, D_MODEL, D_FF)),
                  resident((1, D_FF, D_MODEL)),
                  pl.BlockSpec((1, D_MODEL), lambda i: (0, 0))],
        out_specs=row(D_MODEL),
        compiler_params=pltpu.CompilerParams(dimension_semantics=("parallel",),
                                             vmem_limit_bytes=VMEM_LIMIT),
        name="mix_mlp",
    )(x2, oa, ob, oc, w_o, norm_g, w_1, w_2, final_g)


def _prep_w_in(w_in):
    a0, b0, f0, c0 = 0, 3 * A_W, 3 * A_W + 3 * B_W, 3 * A_W + 3 * B_W + B_HEADS
    w16 = w_in.astype(jnp.bfloat16)
    w_tm = jnp.concatenate([w16[:, :, b0 + B_W:b0 + 2 * B_W],
                            w16[:, :, a0 + A_W:a0 + 2 * A_W]], axis=-1)
    w_fm = jnp.concatenate([
        w16[:, :, b0:b0 + B_W], w16[:, :, b0 + 2 * B_W:f0],
        w16[:, :, a0:a0 + A_W], w16[:, :, a0 + 2 * A_W:b0],
        w16[:, :, f0:c0],
    ], axis=-1)
    w_fm = jnp.pad(w_fm, ((0, 0), (0, 0), (0, F_ROWS - B_HEADS)))
    return w_tm, w_fm, w16[:, :, c0:]


def kernel(x, norm1_g, w_in, b_f, lam_q1, lam_k1, lam_q2, lam_k2, diff_norm_g, w_o, norm2_g,
           w_1, w_2, rel_bias, final_g):
    bsz, seq, _ = x.shape
    depth = w_in.shape[0]
    w_tm, w_fm, w_c = _prep_w_in(w_in)
    w_o16, w_116, w_216 = (w.astype(jnp.bfloat16) for w in (w_o, w_1, w_2))
    lam_vecs = jnp.stack([lam_q1, lam_k1, lam_q2, lam_k2], axis=1)
    bias_a, bias_c = _make_bias_tiles(rel_bias)

    x2 = x.reshape(bsz * seq, D_MODEL)
    for l in range(depth):
        proj_tm, proj_fm, ft, proj_c = _in_proj(x2, norm1_g[:, None, :], w_tm, w_fm, w_c, l)
        proj3 = proj_tm.reshape(bsz, seq, N_TM)
        ckp = _decay(ft, b_f[:, :, None], l, bsz, seq)
        lam_init = jnp.full((1,), 0.8 - 0.6 * math.exp(-0.3 * l), jnp.float32)
        oa, ob = _attn_ab(proj3, proj_fm, bias_a, lam_vecs, lam_init, diff_norm_g[:, :, None],
                          ckp, l)
        oc = _attn_c(proj_c, bias_c, bsz, seq)
        x2 = _mix_mlp(x2, oa.reshape(bsz * seq, A_W), ob.reshape(bsz * seq, B_W),
                      oc.reshape(bsz * seq, C_W), w_o16, norm2_g[:, None, :], w_116, w_216,
                      final_g[None, :], l, l == depth - 1)
    return x2.reshape(bsz, seq, D_MODEL)
```
